```python
import jax, jax.numpy as jnp
from jax import lax
import numpy as np

D_MODEL = 1024
BATCH = 8
SEQ = 2048
DEPTH = 1
DEC_BATCH = 128
DEC_SEQ = 4
PAST_LEN = 2048
PAGE_SIZE = 128

N_HEADS = 8
HEAD_DIM = 64
N_KV_HEADS = 2
N_IDX_HEADS = 8
IDX_DIM = 64
TOPK_MAX = 256
Q_BLOCK = 128
C_CONV = D_MODEL // 2
CONV_WIDTH = 31
D_FF = 2816
FFN_CONV_WIDTH = 3
LN_EPS = 1e-5
ALPHA = (2.0 * DEPTH) ** 0.25
BETA = (8.0 * DEPTH) ** -0.25

Q_W = N_HEADS * HEAD_DIM
KV_W = N_KV_HEADS * HEAD_DIM
IQ_W = N_IDX_HEADS * IDX_DIM
IN_SPLITS = (Q_W, KV_W, KV_W, IQ_W, IDX_DIM, N_IDX_HEADS, 2 * C_CONV, D_MODEL, D_MODEL)
IN_W = sum(IN_SPLITS)

kernel_name = 'dsa_conformer_convffn_hybrid_step'


def layer_norm(x, g, b):
    xf = x.astype(jnp.float32)
    mu = jnp.mean(xf, axis=-1, keepdims=True)
    var = jnp.mean(jnp.square(xf - mu), axis=-1, keepdims=True)
    y = (xf - mu) * lax.rsqrt(var + LN_EPS) * g.astype(jnp.float32) + b.astype(jnp.float32)
    return y.astype(x.dtype)


def causal_dwconv(prev, x, w, b):
    xp = jnp.concatenate([prev.astype(x.dtype), x], axis=1)
    y = lax.conv_general_dilated(xp, w[:, None, :].astype(x.dtype), window_strides=(1,), padding='VALID',
                                 dimension_numbers=('NWC', 'WIO', 'NWC'), feature_group_count=w.shape[1])
    return y + b, xp[:, -(w.shape[0] - 1):]


def sparse_attend_block(q, qi, wi, qpos, k_all, v_all, ki_all, topk):
    B, T = q.shape[0], q.shape[1]
    L = k_all.shape[1]
    s_idx = jnp.einsum('bthd,bsd->bths', qi.astype(jnp.float32), ki_all.astype(jnp.float32)) * (IDX_DIM ** -0.5)
    scores = jnp.einsum('bths,bth->bts', jax.nn.relu(s_idx), wi.astype(jnp.float32) * (N_IDX_HEADS ** -0.5))
    key_pos = jnp.arange(L, dtype=jnp.int32)
    admissible = key_pos[None, :] <= qpos[:, None]
    scores = jnp.where(admissible[None], scores, -jnp.inf)
    _, idx = lax.top_k(scores, topk)
    valid = idx <= qpos[None, :, None]
    gather = jax.vmap(lambda rows, ids: rows[ids])
    k_sel = gather(k_all, idx)
    v_sel = gather(v_all, idx)
    qg = q.reshape(B, T, N_KV_HEADS, N_HEADS // N_KV_HEADS, HEAD_DIM)
    logits = jnp.einsum('btngd,btknd->btngk', qg.astype(jnp.float32), k_sel.astype(jnp.float32)) * (HEAD_DIM ** -0.5)
    logits = jnp.where(valid[:, :, None, None, :], logits, -jnp.inf)
    p = jax.nn.softmax(logits, axis=-1)
    o = jnp.einsum('btngk,btknd->btngd', p, v_sel.astype(jnp.float32))
    return o.reshape(B, T, Q_W).astype(q.dtype)


def trunk_layer(x, past_k, past_v, past_ki, conv_prev, ffn_prev, w_in, w_attn_o, w_conv_dw, b_conv_dw,
                ln_conv_g, ln_conv_b, w_conv_o, w_out, ln1_g, ln1_b, w_ffn_up, w_ffn_gate, w_ffn_dw,
                b_ffn_dw, w_ffn_down, ln2_g, ln2_b):
    B, T, _ = x.shape
    P = past_k.shape[1]
    topk = min(TOPK_MAX, (P + T) // 4)
    proj = x @ w_in
    q, k, v, qi, ki, wi, glu_in, g_a, g_b = jnp.split(proj, np.cumsum(IN_SPLITS)[:-1].tolist(), axis=-1)
    q = q.reshape(B, T, N_HEADS, HEAD_DIM)
    k = k.reshape(B, T, N_KV_HEADS, HEAD_DIM)
    v = v.reshape(B, T, N_KV_HEADS, HEAD_DIM)
    qi = qi.reshape(B, T, N_IDX_HEADS, IDX_DIM)
    k_all = jnp.concatenate([past_k.astype(x.dtype), k], axis=1)
    v_all = jnp.concatenate([past_v.astype(x.dtype), v], axis=1)
    ki_all = jnp.concatenate([past_ki.astype(x.dtype), ki], axis=1)
    qpos = P + jnp.arange(T, dtype=jnp.int32)
    if T > Q_BLOCK and T % Q_BLOCK == 0:
        nb = T // Q_BLOCK
        blk = lambda a: jnp.moveaxis(a.reshape(B, nb, Q_BLOCK, *a.shape[2:]), 1, 0)
        out = lax.map(lambda xs: sparse_attend_block(xs[0], xs[1], xs[2], xs[3], k_all, v_all, ki_all, topk),
                      (blk(q), blk(qi), blk(wi), qpos.reshape(nb, Q_BLOCK)))
        attn = jnp.moveaxis(out, 0, 1).reshape(B, T, Q_W)
    else:
        attn = sparse_attend_block(q, qi, wi, qpos, k_all, v_all, ki_all, topk)
    a_branch = attn @ w_attn_o
    glu_a, glu_b = jnp.split(glu_in, 2, axis=-1)
    glu = glu_a * jax.nn.sigmoid(glu_b)
    c, conv_state = causal_dwconv(conv_prev, glu, w_conv_dw, b_conv_dw)
    c_branch = jax.nn.silu(layer_norm(c, ln_conv_g, ln_conv_b)) @ w_conv_o
    merged = jax.nn.sigmoid(g_a) * a_branch + jax.nn.sigmoid(g_b) * c_branch
    x1 = layer_norm(ALPHA * x + merged @ w_out, ln1_g, ln1_b)
    u = x1 @ w_ffn_up
    uc, ffn_state = causal_dwconv(ffn_prev, u, w_ffn_dw, b_ffn_dw)
    f = (jax.nn.gelu(uc) * (x1 @ w_ffn_gate)) @ w_ffn_down
    y = layer_norm(ALPHA * x1 + f, ln2_g, ln2_b)
    return y, k, v, ki, conv_state, ffn_state


def setup_inputs(seed: int = 0) -> dict:
    key = jax.random.key(seed)
    ks = jax.random.split(key, 32)
    n_pages = PAST_LEN // PAGE_SIZE
    n_phys = (DEC_BATCH * n_pages * 5) // 4
    nrm = lambda k, shape, s: jax.random.normal(k, shape, jnp.float32) * s
    v_lo = Q_W + KV_W
    w_in = nrm(ks[0], (D_MODEL, IN_W), D_MODEL ** -0.5)
    w_in = w_in.at[:, v_lo:v_lo + KV_W].multiply(BETA)
    page_table = jax.random.permutation(ks[1], n_phys)[:DEC_BATCH * n_pages].reshape(DEC_BATCH, n_pages).astype(jnp.int32)
    return {
        'x_prompt': nrm(ks[2], (BATCH, SEQ, D_MODEL), 1.0),
        'x_sample': nrm(ks[3], (DEC_BATCH, DEC_SEQ, D_MODEL), 1.0),
        'cache_k': nrm(ks[4], (n_phys, PAGE_SIZE, N_KV_HEADS, HEAD_DIM), 1.0),
        'cache_v': nrm(ks[5], (n_phys, PAGE_SIZE, N_KV_HEADS, HEAD_DIM), BETA),
        'cache_kidx': nrm(ks[6], (n_phys, PAGE_SIZE, IDX_DIM), 1.0),
        'state_conv': nrm(ks[7], (DEC_BATCH, CONV_WIDTH - 1, C_CONV), 0.5),
        'state_ffn': nrm(ks[8], (DEC_BATCH, FFN_CONV_WIDTH - 1, D_FF), BETA),
        'page_table': page_table,
        'w_in': w_in,
        'w_attn_o': nrm(ks[9], (Q_W, D_MODEL), Q_W ** -0.5),
        'w_conv_dw': nrm(ks[10], (CONV_WIDTH, C_CONV), CONV_WIDTH ** -0.5),
        'b_conv_dw': nrm(ks[11], (C_CONV,), 0.01),
        'ln_conv_g': 1.0 + nrm(ks[12], (C_CONV,), 0.01),
        'ln_conv_b': nrm(ks[13], (C_CONV,), 0.01),
        'w_conv_o': nrm(ks[14], (C_CONV, D_MODEL), C_CONV ** -0.5),
        'w_out': nrm(ks[15], (D_MODEL, D_MODEL), BETA * D_MODEL ** -0.5),
        'ln1_g': 1.0 + nrm(ks[16], (D_MODEL,), 0.01),
        'ln1_b': nrm(ks[17], (D_MODEL,), 0.01),
        'w_ffn_up': nrm(ks[18], (D_MODEL, D_FF), BETA * D_MODEL ** -0.5),
        'w_ffn_gate': nrm(ks[19], (D_MODEL, D_FF), D_MODEL ** -0.5),
        'w_ffn_dw': nrm(ks[20], (FFN_CONV_WIDTH, D_FF), FFN_CONV_WIDTH ** -0.5),
        'b_ffn_dw': nrm(ks[21], (D_FF,), 0.01),
        'w_ffn_down': nrm(ks[22], (D_FF, D_MODEL), BETA * D_FF ** -0.5),
        'ln2_g': 1.0 + nrm(ks[23], (D_MODEL,), 0.01),
        'ln2_b': nrm(ks[24], (D_MODEL,), 0.01),
    }


def reference(x_prompt, x_sample, cache_k, cache_v, cache_kidx, state_conv, state_ffn, page_table,
              w_in, w_attn_o, w_conv_dw, b_conv_dw, ln_conv_g, ln_conv_b, w_conv_o, w_out, ln1_g, ln1_b,
              w_ffn_up, w_ffn_gate, w_ffn_dw, b_ffn_dw, w_ffn_down, ln2_g, ln2_b):
    weights = (w_in, w_attn_o, w_conv_dw, b_conv_dw, ln_conv_g, ln_conv_b, w_conv_o, w_out, ln1_g, ln1_b,
               w_ffn_up, w_ffn_gate, w_ffn_dw, b_ffn_dw, w_ffn_down, ln2_g, ln2_b)
    y_prompt, y_sample = x_prompt, x_sample
    for _ in range(DEPTH):
        B = x_prompt.shape[0]
        dt = x_prompt.dtype
        y_prompt, k_p, v_p, ki_p, conv_p, ffn_p = trunk_layer(
            y_prompt,
            jnp.zeros((B, 0, N_KV_HEADS, HEAD_DIM), dt), jnp.zeros((B, 0, N_KV_HEADS, HEAD_DIM), dt),
            jnp.zeros((B, 0, IDX_DIM), dt),
            jnp.zeros((B, CONV_WIDTH - 1, C_CONV), dt), jnp.zeros((B, FFN_CONV_WIDTH - 1, D_FF), dt),
            *weights)
        DB, NP = page_table.shape
        past_k = cache_k[page_table].reshape(DB, NP * PAGE_SIZE, N_KV_HEADS, HEAD_DIM)
        past_v = cache_v[page_table].reshape(DB, NP * PAGE_SIZE, N_KV_HEADS, HEAD_DIM)
        past_ki = cache_kidx[page_table].reshape(DB, NP * PAGE_SIZE, IDX_DIM)
        y_sample, k_s, v_s, ki_s, conv_s, ffn_s = trunk_layer(
            y_sample, past_k, past_v, past_ki, state_conv, state_ffn, *weights)
    return (y_prompt, y_sample, k_p, v_p, ki_p, conv_p, ffn_p, k_s, v_s, ki_s, conv_s, ffn_s)
```

```python
import functools

import jax
import jax.numpy as jnp
from jax import lax
from jax.experimental import pallas as pl
from jax.experimental.pallas import tpu as pltpu

D_MODEL = 1024
N_HEADS = 8
HEAD_DIM = 64
N_KV_HEADS = 2
N_IDX_HEADS = 8
IDX_DIM = 64
TOPK_MAX = 256
C_CONV = D_MODEL // 2
CONV_WIDTH = 31
D_FF = 2816
FFN_CONV_WIDTH = 3
LN_EPS = 1e-5
DEPTH = 1
ALPHA = (2.0 * DEPTH) ** 0.25
PAGE_SIZE = 128

Q_W = N_HEADS * HEAD_DIM
KV_W = N_KV_HEADS * HEAD_DIM
IQ_W = N_IDX_HEADS * IDX_DIM
GROUP = N_HEADS // N_KV_HEADS

LANES = 128
SUBLANES = 8
QBD_W = N_HEADS * KV_W
PROJ_W = QBD_W + 2 * KV_W + IQ_W + LANES
PROJ_T = KV_W + SUBLANES
TQ = 256
TQ_S = 256
CONV_HALO = 32
FFN_HALO = 8
N_BISECT = 20
VMEM_LIMIT = 56 * 1024 * 1024

F32 = jnp.float32
BF16 = jnp.bfloat16
NEG_INF = float("-inf")
POS_INF = float("inf")


def _dot(a, b):
    return jnp.dot(a, b, preferred_element_type=F32)


def _dot_nt(a, b):
    return lax.dot_general(a, b, (((1,), (1,)), ((), ())), preferred_element_type=F32)


def _layer_norm(x, g, b):
    mu = jnp.mean(x, axis=-1, keepdims=True)
    xc = x - mu
    var = jnp.mean(xc * xc, axis=-1, keepdims=True)
    return xc * lax.rsqrt(var + LN_EPS) * g + b


def _params(sem):
    return pltpu.CompilerParams(dimension_semantics=sem, vmem_limit_bytes=VMEM_LIMIT)


def _whole_spec(shape):
    nd = len(shape)
    return pl.BlockSpec(shape, lambda *_: (0,) * nd)


def _const_spec(shape):
    nd = len(shape)
    return pl.BlockSpec(shape, lambda *_: (0,) * nd, pipeline_mode=pl.Buffered(1))


def _proj_kernel(x_ref, w_ref, wt_ref, qbd_ref, k_ref, v_ref, qi_ref, ki_ref, kb_ref, kib_ref, vt_ref, wit_ref):
    xb = x_ref[...].astype(BF16)
    o = 0
    qbd = _dot(xb, w_ref[:, o:o + QBD_W]); o += QBD_W
    k = _dot(xb, w_ref[:, o:o + KV_W]); o += KV_W
    v_ref[...] = _dot(xb, w_ref[:, o:o + KV_W]); o += KV_W
    qi = _dot(xb, w_ref[:, o:o + IQ_W]); o += IQ_W
    ki = _dot(xb, w_ref[:, o:o + LANES])[:, :IDX_DIM]
    qbd_ref[...] = (qbd * (HEAD_DIM ** -0.5)).astype(BF16)
    k_ref[...] = k
    qi_ref[...] = qi.astype(BF16)
    ki_ref[...] = ki
    kb_ref[...] = k.astype(BF16)
    kib_ref[...] = ki.astype(BF16)
    t = _dot_nt(wt_ref[...], xb)
    vt = t[:KV_W, :].astype(BF16)
    for j in range(vt_ref.shape[0]):
        vt_ref[j] = vt[:, j * TQ:(j + 1) * TQ]
    wit_ref[...] = t[KV_W:, :] * ((IDX_DIM ** -0.5) * (N_IDX_HEADS ** -0.5))


def _proj(x2d, w_att, w_t, tm):
    n = x2d.shape[0]
    row = lambda w: pl.BlockSpec((tm, w), lambda i: (i, 0))
    outs = [(QBD_W, BF16), (KV_W, F32), (KV_W, F32), (IQ_W, BF16), (IDX_DIM, F32), (KV_W, BF16), (IDX_DIM, BF16)]
    return pl.pallas_call(
        _proj_kernel,
        grid=(n // tm,),
        in_specs=[row(D_MODEL), _const_spec((D_MODEL, PROJ_W)), _const_spec((PROJ_T, D_MODEL))],
        out_specs=[row(w) for w, _ in outs]
        + [pl.BlockSpec((tm // TQ, KV_W, TQ), lambda i: (i, 0, 0)), pl.BlockSpec((SUBLANES, tm), lambda i: (0, i))],
        out_shape=[jax.ShapeDtypeStruct((n, w), dt) for w, dt in outs]
        + [jax.ShapeDtypeStruct((n // TQ, KV_W, TQ), BF16), jax.ShapeDtypeStruct((SUBLANES, n), F32)],
        compiler_params=_params(("parallel",)),
        name="proj",
    )(x2d, w_att, w_t)


def _chunk(ref, c, ch):
    return ref[pl.ds(pl.multiple_of(c * ch, ch), ch), :]


def _fold_rows(x, op):
    parts = [x[r:r + SUBLANES, :] for r in range(0, x.shape[0], SUBLANES)]
    while len(parts) > 1:
        nxt = [op(parts[a], parts[a + 1]) for a in range(0, len(parts) - 1, 2)]
        if len(parts) % 2:
            nxt.append(parts[-1])
        parts = nxt
    return parts[0]


def _reduce_keys(s_ref, nk, ch, init, f, op, red):
    w = s_ref.shape[1]

    def body(c, acc):
        return op(acc, _fold_rows(f(_chunk(s_ref, c, ch), c), op))

    acc = lax.fori_loop(0, nk, body, jnp.full((SUBLANES, w), init, F32))
    return red(acc, axis=0, keepdims=True)


def _select_bias(s_ref, bias_ref, nk, ch, topk):
    w = s_ref.shape[1]
    kf = float(topk)
    count = lambda pred: _reduce_keys(s_ref, nk, ch, 0.0, lambda x, c: jnp.where(pred(x, c), 1.0, 0.0),
                                      jnp.add, jnp.sum)
    min_above = lambda t: _reduce_keys(s_ref, nk, ch, POS_INF, lambda x, c: jnp.where(x > t, x, POS_INF),
                                       jnp.minimum, jnp.min)

    neg = jnp.full((1, w), NEG_INF, F32)

    hi = _reduce_keys(s_ref, nk, ch, NEG_INF, lambda x, c: x, jnp.maximum, jnp.max)
    lo_fin = min_above(neg)
    n_adm = count(lambda x, _: x > neg)

    def bisect(st):
        it, lo, lo_fin, hi, n_lo = st
        mid = 0.5 * lo_fin + 0.5 * hi
        c = count(lambda x, _: x > mid)
        ok = c >= kf
        return (it + 1, jnp.where(ok, mid, lo), jnp.where(ok, mid, lo_fin), jnp.where(ok, hi, mid),
                jnp.where(ok, c, n_lo))

    _, lo, _, _, n_lo = lax.while_loop(
        lambda st: jnp.logical_and(st[0] < N_BISECT, jnp.max(st[4]) > kf), bisect,
        (jnp.int32(0), neg, lo_fin, hi, n_adm))

    def peel(st):
        lo, thr, n_gt, done = st
        v = min_above(lo)
        c = count(lambda x, _: x > v)
        found = c < kf
        newly = jnp.logical_and(done < 0.5, found)
        return (jnp.where(jnp.logical_or(found, done > 0.5), lo, v), jnp.where(newly, v, thr),
                jnp.where(newly, c, n_gt), jnp.where(found, 1.0, done))

    _, thr, n_gt, _ = lax.while_loop(lambda st: jnp.min(st[3]) < 0.5, peel,
                                     (lo, lo, n_lo, jnp.where(n_lo <= kf, 1.0, 0.0)))

    need = jnp.where(thr == neg, 0.0, kf - n_gt)
    tri = (lax.broadcasted_iota(jnp.int32, (ch, ch), 1) <= lax.broadcasted_iota(jnp.int32, (ch, ch), 0)).astype(BF16)

    def write(c, seen):
        x = _chunk(s_ref, c, ch)
        eq = x == thr
        rank = seen + _dot(tri, jnp.where(eq, 1.0, 0.0).astype(BF16))
        sel = jnp.logical_or(x > thr, jnp.logical_and(eq, rank <= need))
        bias_ref[pl.ds(pl.multiple_of(c * ch, ch), ch), :] = jnp.where(sel, 0.0, NEG_INF)
        return rank[ch - 1:ch, :]

    lax.fori_loop(0, nk, write, jnp.zeros((1, w), F32))


def _prompt_attn_kernel(qbd_ref, qi_ref, wit_ref, kib_ref, kb_ref, vt_ref, o_ref, s_ref, bias_ref, m_ref, l_ref,
                        acc_ref, lg_ref, *, topk):
    i = pl.program_id(1)
    nk = i + 1
    qi = qi_ref[...]
    wit = wit_ref[...]
    q_pos = lax.broadcasted_iota(jnp.int32, (TQ, TQ), 1) + i * TQ
    k_off = lax.broadcasted_iota(jnp.int32, (TQ, TQ), 0)

    def score(c, carry):
        kc = _chunk(kib_ref, c, TQ)
        s = jnp.zeros((TQ, TQ), F32)
        for h in range(N_IDX_HEADS):
            z = _dot_nt(kc, qi[:, h * IDX_DIM:(h + 1) * IDX_DIM])
            s = s + jnp.maximum(z, 0.0) * wit[h:h + 1, :]
        s_ref[pl.ds(pl.multiple_of(c * TQ, TQ), TQ), :] = jnp.where(k_off + c * TQ <= q_pos, s, NEG_INF)
        return carry

    lax.fori_loop(0, nk, score, 0)
    _select_bias(s_ref, bias_ref, nk, TQ, topk)

    m_ref[...] = jnp.full(m_ref.shape, NEG_INF, F32)
    l_ref[...] = jnp.zeros(l_ref.shape, F32)
    acc_ref[...] = jnp.zeros(acc_ref.shape, F32)

    def attend(c, carry):
        rows = pl.ds(pl.multiple_of(c * TQ, TQ), TQ)
        mx = []
        for h in range(N_HEADS):
            lg = _dot_nt(kb_ref[rows, :], qbd_ref[:, h * KV_W:(h + 1) * KV_W]) + bias_ref[rows, :]
            lg_ref[h] = lg
            mx.append(_fold_rows(lg, jnp.maximum))
        m_old = m_ref[...]
        m_new = jnp.maximum(m_old, jnp.concatenate([jnp.max(x, axis=0, keepdims=True) for x in mx], axis=0))
        m_use = jnp.where(m_new == NEG_INF, 0.0, m_new)
        alpha = jnp.exp(m_old - m_use)
        m_ref[...] = m_new
        sums = []
        for h in range(N_HEADS):
            n = h // GROUP
            out = slice(h * HEAD_DIM, (h + 1) * HEAD_DIM)
            p = jnp.exp(lg_ref[h] - m_use[h:h + 1, :])
            sums.append(jnp.sum(_fold_rows(p, jnp.add), axis=0, keepdims=True))
            vt = vt_ref[c, pl.ds(n * HEAD_DIM, HEAD_DIM), :]
            acc_ref[out, :] = alpha[h:h + 1, :] * acc_ref[out, :] + _dot(vt, p.astype(BF16))
        l_ref[...] = alpha * l_ref[...] + jnp.concatenate(sums, axis=0)
        return carry

    lax.fori_loop(0, nk, attend, 0)
    outs = [acc_ref[h * HEAD_DIM:(h + 1) * HEAD_DIM, :] / l_ref[h:h + 1, :] for h in range(N_HEADS)]
    o_ref[...] = jnp.concatenate(outs, axis=0).T.astype(o_ref.dtype)


def _prompt_attention(qbd, qi, wit, kib, kb, vt3, topk):
    b, t, _ = qbd.shape
    nblk = t // TQ
    qblk = lambda w: pl.BlockSpec((None, TQ, w), lambda bi, i: (bi, i, 0))
    full = lambda w: pl.BlockSpec((None, t, w), lambda bi, i: (bi, 0, 0))
    return pl.pallas_call(
        functools.partial(_prompt_attn_kernel, topk=topk),
        grid=(b, nblk),
        in_specs=[qblk(QBD_W), qblk(IQ_W), pl.BlockSpec((SUBLANES, TQ), lambda bi, i: (0, bi * nblk + i)),
                  full(IDX_DIM), full(KV_W), pl.BlockSpec((nblk, KV_W, TQ), lambda bi, i: (bi, 0, 0))],
        out_specs=qblk(Q_W),
        out_shape=jax.ShapeDtypeStruct((b, t, Q_W), BF16),
        scratch_shapes=[pltpu.VMEM((t, TQ), F32), pltpu.VMEM((t, TQ), F32), pltpu.VMEM((N_HEADS, TQ), F32),
                        pltpu.VMEM((N_HEADS, TQ), F32), pltpu.VMEM((Q_W, TQ), F32),
                        pltpu.VMEM((N_HEADS, TQ, TQ), F32)],
        compiler_params=_params(("parallel", "arbitrary")),
        name="prompt_attn",
    )(qbd, qi, wit, kib, kb, vt3)


def _page_specs(n_pages, block):
    nd = len(block)

    def spec(j):
        return pl.BlockSpec((None,) + block, lambda bi, pt: (pt[bi * n_pages + j],) + (0,) * nd)
    return [spec(j) for j in range(n_pages)]


def _sample_score_kernel(pt_ref, qi_ref, w_ref, *refs):
    del pt_ref
    pages, o_ref = refs[:-1], refs[-1]
    qi = qi_ref[...]
    w = w_ref[...]
    t = qi.shape[0] // N_IDX_HEADS
    for j, page in enumerate(pages):
        s = _dot(qi, page[...].astype(BF16))
        s = jnp.maximum(s, 0.0) * w
        o_ref[:, j * PAGE_SIZE:(j + 1) * PAGE_SIZE] = jnp.sum(s.reshape(t, N_IDX_HEADS, PAGE_SIZE), axis=1)


def _sample_scores(page_table_flat, qi32, w32, kidx_t, ki_new_t, n_pages):
    nb, rows, _ = qi32.shape
    t = rows // N_IDX_HEADS
    lk = (n_pages + 1) * PAGE_SIZE
    per_seq = lambda r, w: pl.BlockSpec((None, r, w), lambda bi, pt: (bi, 0, 0))
    return pl.pallas_call(
        _sample_score_kernel,
        grid_spec=pltpu.PrefetchScalarGridSpec(
            num_scalar_prefetch=1,
            grid=(nb,),
            in_specs=[per_seq(rows, IDX_DIM), per_seq(rows, 1)] + _page_specs(n_pages, (IDX_DIM, PAGE_SIZE))
            + [per_seq(IDX_DIM, PAGE_SIZE)],
            out_specs=per_seq(t, lk),
        ),
        out_shape=jax.ShapeDtypeStruct((nb, t, lk), F32),
        compiler_params=_params(("parallel",)),
        name="sample_scores",
    )(page_table_flat, qi32, w32, *([kidx_t] * n_pages), ki_new_t)


def _sample_select_kernel(s_ref, bias_ref, sm_ref, *, past, t, topk):
    lk = s_ref.shape[0]
    nk = lk // PAGE_SIZE
    qcol = lax.broadcasted_iota(jnp.int32, (PAGE_SIZE, TQ_S), 1)
    qpos = past + (qcol & (t - 1))
    krow = lax.broadcasted_iota(jnp.int32, (PAGE_SIZE, TQ_S), 0)
    for c in range(nk):
        rows = slice(c * PAGE_SIZE, (c + 1) * PAGE_SIZE)
        sm_ref[rows, :] = jnp.where(krow + c * PAGE_SIZE <= qpos, s_ref[rows, :], NEG_INF)
    _select_bias(sm_ref, bias_ref, nk, PAGE_SIZE, topk)


def _sample_select(scores_t, past, t, topk):
    lk, n = scores_t.shape
    assert t & (t - 1) == 0 and TQ_S % t == 0, "token index is taken from the low bits of the query index"
    blk = pl.BlockSpec((lk, TQ_S), lambda i: (0, i))
    return pl.pallas_call(
        functools.partial(_sample_select_kernel, past=past, t=t, topk=topk),
        grid=(n // TQ_S,),
        in_specs=[blk],
        out_specs=blk,
        out_shape=jax.ShapeDtypeStruct((lk, n), F32),
        scratch_shapes=[pltpu.VMEM((lk, TQ_S), F32)],
        compiler_params=_params(("parallel",)),
        name="sample_select",
    )(scores_t)


def _sample_attn_kernel(pt_ref, q_ref, bias_ref, *refs):
    del pt_ref
    n_pg = (len(refs) - 1) // 2
    k_pages, v_pages, o_ref = refs[:n_pg], refs[n_pg:2 * n_pg], refs[-1]
    qb = q_ref[...]
    reps = qb.shape[0] // bias_ref.shape[0]
    bias = jnp.concatenate([bias_ref[...]] * reps, axis=0)
    kt = lambda ref: ref[...].reshape(KV_W, PAGE_SIZE).astype(BF16)
    lg = jnp.concatenate([_dot(qb, kt(kp)) for kp in k_pages], axis=-1) + bias
    m = jnp.max(lg, axis=-1, keepdims=True)
    p = jnp.exp(lg - m)
    l = jnp.sum(p, axis=-1, keepdims=True)
    pb = p.astype(BF16)
    acc = jnp.zeros(o_ref.shape, F32)
    for j, vp in enumerate(v_pages):
        acc = acc + _dot_nt(pb[:, j * PAGE_SIZE:(j + 1) * PAGE_SIZE], kt(vp))
    o_ref[...] = acc / l


def _sample_attend(page_table_flat, q_bd, bias, k_t, v_t, k_new_t, v_new_t, n_pages):
    nb, rows, _ = q_bd.shape
    t = bias.shape[1]
    lk = bias.shape[2]
    per_seq = lambda r, w: pl.BlockSpec((None, r, w), lambda bi, pt: (bi, 0, 0))
    page = (N_KV_HEADS, HEAD_DIM, PAGE_SIZE)
    new_page = pl.BlockSpec((None,) + page, lambda bi, pt: (bi, 0, 0, 0))
    return pl.pallas_call(
        _sample_attn_kernel,
        grid_spec=pltpu.PrefetchScalarGridSpec(
            num_scalar_prefetch=1,
            grid=(nb,),
            in_specs=[per_seq(rows, KV_W), per_seq(t, lk)]
            + _page_specs(n_pages, page) + [new_page] + _page_specs(n_pages, page) + [new_page],
            out_specs=per_seq(rows, KV_W),
        ),
        out_shape=jax.ShapeDtypeStruct((nb, rows, KV_W), F32),
        compiler_params=_params(("parallel",)),
        name="sample_attn",
    )(page_table_flat, q_bd, bias, *([k_t] * n_pages), k_new_t, *([v_t] * n_pages), v_new_t)


def _glu(xb, w_glu_ref):
    gi = _dot(xb, w_glu_ref[...])
    return gi[:, :C_CONV] * jax.nn.sigmoid(gi[:, C_CONV:])


def _mix_tail(x, xb, attn_b, c, w_g_ref, w_ao_ref, lncg_ref, lncb_ref, w_co_ref, w_out_ref, ln1g_ref, ln1b_ref):
    cn = _layer_norm(c, lncg_ref[...], lncb_ref[...])
    c_branch = _dot((cn * jax.nn.sigmoid(cn)).astype(BF16), w_co_ref[...])
    a_branch = _dot(attn_b, w_ao_ref[...])
    g = _dot(xb, w_g_ref[...])
    merged = jax.nn.sigmoid(g[:, :D_MODEL]) * a_branch + jax.nn.sigmoid(g[:, D_MODEL:]) * c_branch
    h = ALPHA * x + _dot(merged.astype(BF16), w_out_ref[...])
    return _layer_norm(h, ln1g_ref[...], ln1b_ref[...])


def _mix_prompt_kernel(x_ref, attn_ref, w_glu_ref, w_g_ref, w_ao_ref, w_dw_ref, b_dw_ref, lncg_ref, lncb_ref,
                       w_co_ref, w_out_ref, ln1g_ref, ln1b_ref, x1_ref, tail_ref, xp_ref):
    tm = x_ref.shape[0]

    @pl.when(pl.program_id(1) == 0)
    def _():
        xp_ref[0:CONV_HALO, :] = jnp.zeros((CONV_HALO, C_CONV), F32)

    x = x_ref[...]
    xb = x.astype(BF16)
    glu = _glu(xb, w_glu_ref)
    xp_ref[CONV_HALO:CONV_HALO + tm, :] = glu
    tail_ref[...] = glu[tm - CONV_HALO:, :]

    first = CONV_HALO - (CONV_WIDTH - 1)
    rb = 64
    cols = []
    for c0 in range(0, C_CONV, LANES):
        accs = [jnp.broadcast_to(b_dw_ref[:, c0:c0 + LANES], (rb, LANES))] * (tm // rb)
        for j in range(CONV_WIDTH):
            wj = jnp.broadcast_to(w_dw_ref[j:j + 1, c0:c0 + LANES], (rb, LANES))
            accs = [a + wj * xp_ref[first + j + r * rb:first + j + (r + 1) * rb, c0:c0 + LANES]
                    for r, a in enumerate(accs)]
        cols.append(jnp.concatenate(accs, axis=0))
    c = jnp.concatenate(cols, axis=-1)
    xp_ref[0:CONV_HALO, :] = xp_ref[tm:tm + CONV_HALO, :]

    x1_ref[...] = _mix_tail(x, xb, attn_ref[...], c, w_g_ref, w_ao_ref, lncg_ref, lncb_ref, w_co_ref,
                            w_out_ref, ln1g_ref, ln1b_ref)


def _mix_sample_kernel(x_ref, attn_ref, hist_ref, w_glu_ref, w_g_ref, w_ao_ref, w_dw_ref, b_dw_ref, lncg_ref,
                       lncb_ref, w_co_ref, w_out_ref, ln1g_ref, ln1b_ref, x1_ref, glu_ref):
    nb = hist_ref.shape[1]
    t = x_ref.shape[0] // nb
    n_hist = CONV_WIDTH - 1
    x = x_ref[...]
    xb = x.astype(BF16)
    glu = _glu(xb, w_glu_ref)
    glu_ref[...] = glu

    def slab(m):
        return hist_ref[m] if m < n_hist else glu[(m - n_hist) * nb:(m - n_hist + 1) * nb, :]

    outs = []
    for ti in range(t):
        acc = jnp.broadcast_to(b_dw_ref[...], (nb, C_CONV))
        for j in range(CONV_WIDTH):
            acc = acc + w_dw_ref[j:j + 1, :] * slab(ti + j)
        outs.append(acc)
    c = jnp.concatenate(outs, axis=0)
    x1_ref[...] = _mix_tail(x, xb, attn_ref[...], c, w_g_ref, w_ao_ref, lncg_ref, lncb_ref, w_co_ref,
                            w_out_ref, ln1g_ref, ln1b_ref)


def _mix_weight_specs():
    return [
        _const_spec((D_MODEL, 2 * C_CONV)), _const_spec((D_MODEL, 2 * D_MODEL)), _const_spec((Q_W, D_MODEL)),
        _const_spec((CONV_WIDTH, C_CONV)), _const_spec((1, C_CONV)), _const_spec((1, C_CONV)),
        _const_spec((1, C_CONV)), _const_spec((C_CONV, D_MODEL)), _const_spec((D_MODEL, D_MODEL)),
        _const_spec((1, D_MODEL)), _const_spec((1, D_MODEL)),
    ]


def _mix_prompt(x, attn, mix_w, tm):
    b, t, _ = x.shape
    rows = lambda w: pl.BlockSpec((None, tm, w), lambda bi, i: (bi, i, 0))
    return pl.pallas_call(
        _mix_prompt_kernel,
        grid=(b, t // tm),
        in_specs=[rows(D_MODEL), rows(Q_W)] + _mix_weight_specs(),
        out_specs=[rows(D_MODEL), pl.BlockSpec((None, CONV_HALO, C_CONV), lambda bi, i: (bi, 0, 0))],
        out_shape=[jax.ShapeDtypeStruct((b, t, D_MODEL), F32), jax.ShapeDtypeStruct((b, CONV_HALO, C_CONV), F32)],
        scratch_shapes=[pltpu.VMEM((CONV_HALO + tm, C_CONV), F32)],
        compiler_params=_params(("parallel", "arbitrary")),
        name="mix_prompt",
    )(x, attn, *mix_w)


def _mix_sample(x_tm, attn_tm, hist_tm, mix_w):
    n = x_tm.shape[0]
    return pl.pallas_call(
        _mix_sample_kernel,
        grid=(1,),
        in_specs=[_const_spec((n, D_MODEL)), _const_spec((n, Q_W)), _const_spec(hist_tm.shape)] + _mix_weight_specs(),
        out_specs=[_whole_spec((n, D_MODEL)), _whole_spec((n, C_CONV))],
        out_shape=[jax.ShapeDtypeStruct((n, D_MODEL), F32), jax.ShapeDtypeStruct((n, C_CONV), F32)],
        compiler_params=_params(("arbitrary",)),
        name="mix_sample",
    )(x_tm, attn_tm, hist_tm, *mix_w)


def _ffn_tail(x1, uc, gate, w_down_ref, ln2g_ref, ln2b_ref):
    f = _dot((jax.nn.gelu(uc) * gate).astype(BF16), w_down_ref[...])
    return _layer_norm(ALPHA * x1 + f, ln2g_ref[...], ln2b_ref[...])


def _ffn_prompt_kernel(x1_ref, w_up_ref, w_gate_ref, w_dw_ref, b_dw_ref, w_down_ref, ln2g_ref, ln2b_ref,
                       y_ref, tail_ref, up_ref):
    tm = x1_ref.shape[0]

    @pl.when(pl.program_id(1) == 0)
    def _():
        up_ref[0:FFN_HALO, :] = jnp.zeros((FFN_HALO, D_FF), F32)

    x1 = x1_ref[...]
    x1b = x1.astype(BF16)
    u = _dot(x1b, w_up_ref[...])
    gate = _dot(x1b, w_gate_ref[...])
    up_ref[FFN_HALO:FFN_HALO + tm, :] = u
    tail_ref[...] = u[tm - FFN_HALO:, :]
    first = FFN_HALO - (FFN_CONV_WIDTH - 1)
    uc = b_dw_ref[...]
    for j in range(FFN_CONV_WIDTH):
        uc = uc + w_dw_ref[j:j + 1, :] * up_ref[first + j:first + j + tm, :]
    up_ref[0:FFN_HALO, :] = up_ref[tm:tm + FFN_HALO, :]
    y_ref[...] = _ffn_tail(x1, uc, gate, w_down_ref, ln2g_ref, ln2b_ref)


def _ffn_sample_kernel(x1_ref, hist_ref, w_up_ref, w_gate_ref, w_dw_ref, b_dw_ref, w_down_ref, ln2g_ref, ln2b_ref,
                       y_ref, u_ref):
    nb = hist_ref.shape[1]
    t = x1_ref.shape[0] // nb
    n_hist = FFN_CONV_WIDTH - 1
    x1 = x1_ref[...]
    x1b = x1.astype(BF16)
    u = _dot(x1b, w_up_ref[...])
    gate = _dot(x1b, w_gate_ref[...])
    u_ref[...] = u

    def slab(m):
        return hist_ref[m] if m < n_hist else u[(m - n_hist) * nb:(m - n_hist + 1) * nb, :]

    outs = []
    for ti in range(t):
        acc = jnp.broadcast_to(b_dw_ref[...], (nb, D_FF))
        for j in range(FFN_CONV_WIDTH):
            acc = acc + w_dw_ref[j:j + 1, :] * slab(ti + j)
        outs.append(acc)
    uc = jnp.concatenate(outs, axis=0)
    y_ref[...] = _ffn_tail(x1, uc, gate, w_down_ref, ln2g_ref, ln2b_ref)


def _ffn_weight_specs():
    return [
        _const_spec((D_MODEL, D_FF)), _const_spec((D_MODEL, D_FF)), _const_spec((FFN_CONV_WIDTH, D_FF)),
        _const_spec((1, D_FF)), _const_spec((D_FF, D_MODEL)), _const_spec((1, D_MODEL)), _const_spec((1, D_MODEL)),
    ]


def _ffn_prompt(x1, ffn_w, tm):
    b, t, _ = x1.shape
    rows = pl.BlockSpec((None, tm, D_MODEL), lambda bi, i: (bi, i, 0))
    return pl.pallas_call(
        _ffn_prompt_kernel,
        grid=(b, t // tm),
        in_specs=[rows] + _ffn_weight_specs(),
        out_specs=[rows, pl.BlockSpec((None, FFN_HALO, D_FF), lambda bi, i: (bi, 0, 0))],
        out_shape=[jax.ShapeDtypeStruct((b, t, D_MODEL), F32), jax.ShapeDtypeStruct((b, FFN_HALO, D_FF), F32)],
        scratch_shapes=[pltpu.VMEM((FFN_HALO + tm, D_FF), F32)],
        compiler_params=_params(("parallel", "arbitrary")),
        name="ffn_prompt",
    )(x1, *ffn_w)


def _ffn_sample(x1_tm, hist_tm, ffn_w):
    n = x1_tm.shape[0]
    return pl.pallas_call(
        _ffn_sample_kernel,
        grid=(1,),
        in_specs=[_const_spec((n, D_MODEL)), _const_spec(hist_tm.shape)] + _ffn_weight_specs(),
        out_specs=[_whole_spec((n, D_MODEL)), _whole_spec((n, D_FF))],
        out_shape=[jax.ShapeDtypeStruct((n, D_MODEL), F32), jax.ShapeDtypeStruct((n, D_FF), F32)],
        compiler_params=_params(("arbitrary",)),
        name="ffn_sample",
    )(x1_tm, hist_tm, *ffn_w)


def _token_major(a):
    b, t, w = a.shape
    return jnp.transpose(a, (1, 0, 2)).reshape(t * b, w)


def _batch_major(a, b):
    tb, w = a.shape
    return jnp.transpose(a.reshape(tb // b, b, w), (1, 0, 2))


def _stage_weights(w_in):
    o = 0
    w_q = w_in[:, o:o + Q_W]; o += Q_W
    w_k = w_in[:, o:o + KV_W]; o += KV_W
    w_v = w_in[:, o:o + KV_W]; o += KV_W
    w_qi = w_in[:, o:o + IQ_W]; o += IQ_W
    w_ki = w_in[:, o:o + IDX_DIM]; o += IDX_DIM
    w_wi = w_in[:, o:o + N_IDX_HEADS]; o += N_IDX_HEADS
    zero = jnp.zeros((D_MODEL, HEAD_DIM), w_in.dtype)
    qbd_cols = []
    for h in range(N_HEADS):
        w_h = w_q[:, h * HEAD_DIM:(h + 1) * HEAD_DIM]
        qbd_cols += [w_h, zero] if h // GROUP == 0 else [zero, w_h]
    w_att = jnp.concatenate(qbd_cols + [w_k, w_v, w_qi, w_ki, zero], axis=1).astype(BF16)
    w_t = jnp.concatenate([w_v, w_wi], axis=1).T.astype(BF16)
    return w_att, w_t, o


def kernel(x_prompt, x_sample, cache_k, cache_v, cache_kidx, state_conv, state_ffn, page_table, w_in, w_attn_o, w_conv_dw, b_conv_dw, ln_conv_g, ln_conv_b, w_conv_o, w_out, ln1_g, ln1_b, w_ffn_up, w_ffn_gate, w_ffn_dw, b_ffn_dw, w_ffn_down, ln2_g, ln2_b):
    bp, tp, _ = x_prompt.shape
    bs, ts, _ = x_sample.shape
    n_pages = page_table.shape[1]
    past = n_pages * PAGE_SIZE
    assert N_KV_HEADS == 2 and PROJ_T == KV_W + N_IDX_HEADS

    w_att, w_t, o = _stage_weights(w_in)
    w_glu = w_in[:, o:o + 2 * C_CONV].astype(BF16); o += 2 * C_CONV
    w_g = w_in[:, o:o + 2 * D_MODEL].astype(BF16)
    row2 = lambda a: a.reshape(1, -1)
    mix_w = (w_glu, w_g, w_attn_o.astype(BF16), w_conv_dw, row2(b_conv_dw), row2(ln_conv_g), row2(ln_conv_b),
             w_conv_o.astype(BF16), w_out.astype(BF16), row2(ln1_g), row2(ln1_b))
    ffn_w = (w_ffn_up.astype(BF16), w_ffn_gate.astype(BF16), w_ffn_dw, row2(b_ffn_dw), w_ffn_down.astype(BF16),
             row2(ln2_g), row2(ln2_b))

    qbd, k, v, qi, ki, kb, kib, vt3, wit = _proj(x_prompt.reshape(bp * tp, D_MODEL), w_att, w_t, 512)
    seq = lambda a: a.reshape(bp, tp, a.shape[-1])
    attn_p = _prompt_attention(seq(qbd), seq(qi), wit, seq(kib), seq(kb), vt3, min(TOPK_MAX, tp // 4))
    x1_p, conv_tail = _mix_prompt(x_prompt, attn_p, mix_w, 256)
    y_p, ffn_tail = _ffn_prompt(x1_p, ffn_w, 256)
    k_p = k.reshape(bp, tp, N_KV_HEADS, HEAD_DIM)
    v_p = v.reshape(bp, tp, N_KV_HEADS, HEAD_DIM)
    ki_p = ki.reshape(bp, tp, IDX_DIM)
    conv_p = conv_tail[:, CONV_HALO - (CONV_WIDTH - 1):, :]
    ffn_p = ffn_tail[:, FFN_HALO - (FFN_CONV_WIDTH - 1):, :]

    n_s = bs * ts
    qbd_s, ks, vs, qis, kis, _, _, _, wit_s = _proj(x_sample.reshape(n_s, D_MODEL), w_att, w_t, n_s)
    pt_flat = page_table.reshape(-1).astype(jnp.int32)
    new_page = lambda a: jnp.pad(jnp.swapaxes(a.reshape(bs, ts, -1), 1, 2), ((0, 0), (0, 0), (0, PAGE_SIZE - ts)))
    qi32 = qis.reshape(bs, ts * N_IDX_HEADS, IDX_DIM)
    w32 = wit_s.T.reshape(bs, ts * N_IDX_HEADS, 1)
    scores = _sample_scores(pt_flat, qi32, w32, jnp.transpose(cache_kidx, (0, 2, 1)), new_page(kis), n_pages)
    lk = scores.shape[-1]
    bias_t = _sample_select(scores.reshape(n_s, lk).T, past, ts, min(TOPK_MAX, (past + ts) // 4))
    q_bd = jnp.transpose(qbd_s.reshape(bs, ts, N_HEADS, KV_W), (0, 2, 1, 3)).reshape(bs, N_HEADS * ts, KV_W)
    as_pages = lambda a: jnp.transpose(a, (0, 2, 3, 1))
    kv_new = lambda a: new_page(a).reshape(bs, N_KV_HEADS, HEAD_DIM, PAGE_SIZE)
    o_bd = _sample_attend(pt_flat, q_bd, bias_t.T.reshape(bs, ts, lk), as_pages(cache_k), as_pages(cache_v),
                          kv_new(ks), kv_new(vs), n_pages)
    o6 = o_bd.reshape(bs, N_KV_HEADS, GROUP, ts, N_KV_HEADS, HEAD_DIM)
    o_sel = jnp.stack([o6[:, n, :, :, n, :] for n in range(N_KV_HEADS)], axis=1)
    attn_s = jnp.transpose(o_sel, (0, 3, 1, 2, 4)).reshape(bs, ts, Q_W)

    x1_s, glu_s = _mix_sample(_token_major(x_sample), _token_major(attn_s).astype(BF16),
                              jnp.transpose(state_conv, (1, 0, 2)), mix_w)
    y_s, u_s = _ffn_sample(x1_s, jnp.transpose(state_ffn, (1, 0, 2)), ffn_w)
    y_s = _batch_major(y_s, bs)
    conv_s = jnp.concatenate([state_conv, _batch_major(glu_s, bs)], axis=1)[:, -(CONV_WIDTH - 1):, :]
    ffn_s = jnp.concatenate([state_ffn, _batch_major(u_s, bs)], axis=1)[:, -(FFN_CONV_WIDTH - 1):, :]
    k_s = ks.reshape(bs, ts, N_KV_HEADS, HEAD_DIM)
    v_s = vs.reshape(bs, ts, N_KV_HEADS, HEAD_DIM)
    ki_s = kis.reshape(bs, ts, IDX_DIM)

    return (y_p, y_s, k_p, v_p, ki_p, conv_p, ffn_p, k_s, v_s, ki_s, conv_s, ffn_s)
```

```python
import functools

import jax
import jax.numpy as jnp
from jax import lax
from jax.experimental import pallas as pl
from jax.experimental.pallas import tpu as pltpu

D_MODEL = 1024
N_HEADS = 8
HEAD_DIM = 64
N_KV_HEADS = 2
N_IDX_HEADS = 8
IDX_DIM = 64
TOPK_MAX = 256
C_CONV = D_MODEL // 2
CONV_WIDTH = 31
D_FF = 2816
FFN_CONV_WIDTH = 3
LN_EPS = 1e-5
DEPTH = 1
ALPHA = (2.0 * DEPTH) ** 0.25
PAGE_SIZE = 128

Q_W = N_HEADS * HEAD_DIM
KV_W = N_KV_HEADS * HEAD_DIM
IQ_W = N_IDX_HEADS * IDX_DIM
GROUP = N_HEADS // N_KV_HEADS

LANES = 128
SUBLANES = 8
QBD_W = N_HEADS * KV_W
PROJ_W = QBD_W + 2 * KV_W + IQ_W + LANES
PROJ_T = KV_W + SUBLANES
TQ = 256
TQ_S = 256
SCORE_GROUP = 4
ATTN_GROUP = 2
CONV_HALO = 32
FFN_HALO = 8
N_BISECT = 19
VT_ROWS = HEAD_DIM + 16
LOG2E = 1.4426950408889634
VMEM_LIMIT = 56 * 1024 * 1024

F32 = jnp.float32
BF16 = jnp.bfloat16
NEG_INF = float("-inf")
POS_INF = float("inf")


def _dot(a, b):
    return jnp.dot(a, b, preferred_element_type=F32)


def _dot_nt(a, b):
    return lax.dot_general(a, b, (((1,), (1,)), ((), ())), preferred_element_type=F32)


def _layer_norm(x, g, b):
    mu = jnp.mean(x, axis=-1, keepdims=True)
    xc = x - mu
    var = jnp.mean(xc * xc, axis=-1, keepdims=True)
    return xc * lax.rsqrt(var + LN_EPS) * g + b


def _params(sem):
    return pltpu.CompilerParams(dimension_semantics=sem, vmem_limit_bytes=VMEM_LIMIT)


def _whole_spec(shape):
    nd = len(shape)
    return pl.BlockSpec(shape, lambda *_: (0,) * nd)


def _const_spec(shape):
    nd = len(shape)
    return pl.BlockSpec(shape, lambda *_: (0,) * nd, pipeline_mode=pl.Buffered(1))


def _proj_kernel(x_ref, w_ref, wt_ref, qbd_ref, k_ref, v_ref, qi_ref, ki_ref, kb_ref, kib_ref, vt_ref, wit_ref):
    xb = x_ref[...].astype(BF16)
    o = 0
    qbd = _dot(xb, w_ref[:, o:o + QBD_W]); o += QBD_W
    k = _dot(xb, w_ref[:, o:o + KV_W]); o += KV_W
    v_ref[...] = _dot(xb, w_ref[:, o:o + KV_W]); o += KV_W
    qi = _dot(xb, w_ref[:, o:o + IQ_W]); o += IQ_W
    ki = _dot(xb, w_ref[:, o:o + LANES])[:, :IDX_DIM]
    qbd_ref[...] = (qbd * (HEAD_DIM ** -0.5 * LOG2E)).astype(BF16)
    k_ref[...] = k
    qi_ref[...] = qi.astype(BF16)
    ki_ref[...] = ki
    kb_ref[...] = k.astype(BF16)
    kib_ref[...] = ki.astype(BF16)
    t = _dot_nt(wt_ref[...], xb)
    vt = t[:KV_W, :].astype(BF16)
    ones = jnp.ones((VT_ROWS - HEAD_DIM, TQ), BF16)
    for j in range(vt_ref.shape[0]):
        for n in range(N_KV_HEADS):
            vt_ref[j, n * VT_ROWS:n * VT_ROWS + HEAD_DIM, :] = vt[n * HEAD_DIM:(n + 1) * HEAD_DIM, j * TQ:(j + 1) * TQ]
            vt_ref[j, n * VT_ROWS + HEAD_DIM:(n + 1) * VT_ROWS, :] = ones
    wit_ref[...] = t[KV_W:, :] * ((IDX_DIM ** -0.5) * (N_IDX_HEADS ** -0.5))


def _proj(x2d, w_att, w_t, tm):
    n = x2d.shape[0]
    row = lambda w: pl.BlockSpec((tm, w), lambda i: (i, 0))
    outs = [(QBD_W, BF16), (KV_W, F32), (KV_W, F32), (IQ_W, BF16), (IDX_DIM, F32), (KV_W, BF16), (IDX_DIM, BF16)]
    return pl.pallas_call(
        _proj_kernel,
        grid=(n // tm,),
        in_specs=[row(D_MODEL), _const_spec((D_MODEL, PROJ_W)), _const_spec((PROJ_T, D_MODEL))],
        out_specs=[row(w) for w, _ in outs]
        + [pl.BlockSpec((tm // TQ, N_KV_HEADS * VT_ROWS, TQ), lambda i: (i, 0, 0)), pl.BlockSpec((SUBLANES, tm), lambda i: (0, i))],
        out_shape=[jax.ShapeDtypeStruct((n, w), dt) for w, dt in outs]
        + [jax.ShapeDtypeStruct((n // TQ, N_KV_HEADS * VT_ROWS, TQ), BF16), jax.ShapeDtypeStruct((SUBLANES, n), F32)],
        compiler_params=_params(("parallel",)),
        name="proj",
    )(x2d, w_att, w_t)


def _chunk(ref, c, ch):
    return ref[pl.ds(pl.multiple_of(c * ch, ch), ch), :]


def _fold_rows(x, op):
    parts = [x[r:r + SUBLANES, :] for r in range(0, x.shape[0], SUBLANES)]
    accs = parts[:4]
    for i, part in enumerate(parts[4:]):
        accs[i % 4] = op(accs[i % 4], part)
    while len(accs) > 1:
        accs = [op(accs[a], accs[a + 1]) for a in range(0, len(accs) - 1, 2)] + ([accs[-1]] if len(accs) % 2 else [])
    return accs[0]


def _reduce_keys(s_ref, nk, ch, init, f, op, red):
    w = s_ref.shape[1]

    def body(c, acc):
        return op(acc, _fold_rows(f(_chunk(s_ref, c, ch), c), op))

    acc = lax.fori_loop(0, nk, body, jnp.full((SUBLANES, w), init, F32))
    return red(acc, axis=0, keepdims=True)


def _select_bias(s_ref, bias_ref, nk, ch, topk):
    w = s_ref.shape[1]
    kf = float(topk)
    count = lambda pred: _reduce_keys(s_ref, nk, ch, 0.0, lambda x, c: jnp.where(pred(x, c), 1.0, 0.0),
                                      jnp.add, jnp.sum)
    min_above = lambda t: _reduce_keys(s_ref, nk, ch, POS_INF, lambda x, c: jnp.where(x > t, x, POS_INF),
                                       jnp.minimum, jnp.min)

    neg = jnp.full((1, w), NEG_INF, F32)

    hi = _reduce_keys(s_ref, nk, ch, NEG_INF, lambda x, c: x, jnp.maximum, jnp.max)
    lo_fin = min_above(neg)
    n_adm = count(lambda x, _: x > neg)

    def bisect(_, st):
        lo, lo_fin, hi, n_lo = st
        mid = 0.5 * lo_fin + 0.5 * hi
        c = count(lambda x, _: x > mid)
        ok = c >= kf
        return jnp.where(ok, mid, lo), jnp.where(ok, mid, lo_fin), jnp.where(ok, hi, mid), jnp.where(ok, c, n_lo)

    lo, _, _, n_lo = lax.fori_loop(0, N_BISECT, bisect, (neg, lo_fin, hi, n_adm))

    def peel(st):
        lo, thr, n_gt, done = st
        v = min_above(lo)
        c = count(lambda x, _: x > v)
        found = c < kf
        newly = jnp.logical_and(done < 0.5, found)
        return (jnp.where(jnp.logical_or(found, done > 0.5), lo, v), jnp.where(newly, v, thr),
                jnp.where(newly, c, n_gt), jnp.where(found, 1.0, done))

    _, thr, n_gt, _ = lax.while_loop(lambda st: jnp.min(st[3]) < 0.5, peel,
                                     (lo, lo, n_lo, jnp.where(n_lo <= kf, 1.0, 0.0)))

    need = jnp.where(thr == neg, 0.0, kf - n_gt)
    tri = (lax.broadcasted_iota(jnp.int32, (ch, ch), 1) <= lax.broadcasted_iota(jnp.int32, (ch, ch), 0)).astype(BF16)

    def write(c, seen):
        x = _chunk(s_ref, c, ch)
        eq = x == thr
        rank = seen + _dot(tri, jnp.where(eq, 1.0, 0.0).astype(BF16))
        sel = jnp.logical_or(x > thr, jnp.logical_and(eq, rank <= need))
        bias_ref[pl.ds(pl.multiple_of(c * ch, ch), ch), :] = jnp.where(sel, 0.0, NEG_INF)
        return rank[ch - 1:ch, :]

    lax.fori_loop(0, nk, write, jnp.zeros((1, w), F32))


def _prompt_attn_kernel(qbd_ref, qi_ref, wit_ref, kib_ref, kb_ref, vt_ref, o_ref, s_ref, bias_ref, m_ref, acc_ref,
                        lg_ref, *, topk):
    i = pl.program_id(1)
    nk = i + 1
    qi = qi_ref[...]
    wit = wit_ref[...]
    q_pos = lax.broadcasted_iota(jnp.int32, (TQ, TQ), 1) + i * TQ
    k_off = lax.broadcasted_iota(jnp.int32, (TQ, TQ), 0)

    def score(c, carry):
        kc = _chunk(kib_ref, c, TQ)
        s = jnp.zeros((TQ, TQ), F32)
        for h in range(N_IDX_HEADS):
            z = _dot_nt(kc, qi[:, h * IDX_DIM:(h + 1) * IDX_DIM])
            s = s + jnp.maximum(z, 0.0) * wit[h:h + 1, :]
        s_ref[pl.ds(pl.multiple_of(c * TQ, TQ), TQ), :] = jnp.where(k_off + c * TQ <= q_pos, s, NEG_INF)
        return carry

    lax.fori_loop(0, nk, score, 0)
    _select_bias(s_ref, bias_ref, nk, TQ, topk)

    m_ref[...] = jnp.full(m_ref.shape, NEG_INF, F32)
    acc_ref[...] = jnp.zeros(acc_ref.shape, F32)

    def attend(c, carry):
        rows = pl.ds(pl.multiple_of(c * TQ, TQ), TQ)
        mx = []
        for h in range(N_HEADS):
            lg = _dot_nt(kb_ref[rows, :], qbd_ref[:, h * KV_W:(h + 1) * KV_W]) + bias_ref[rows, :]
            lg_ref[h] = lg
            mx.append(_fold_rows(lg, jnp.maximum))
        m_old = m_ref[...]
        m_new = jnp.maximum(m_old, jnp.concatenate([jnp.max(x, axis=0, keepdims=True) for x in mx], axis=0))
        m_use = jnp.where(m_new == NEG_INF, 0.0, m_new)
        alpha = jnp.exp2(m_old - m_use)
        m_ref[...] = m_new
        for h in range(N_HEADS):
            n = h // GROUP
            out = slice(h * VT_ROWS, (h + 1) * VT_ROWS)
            p = jnp.exp2(lg_ref[h] - m_use[h:h + 1, :]).astype(BF16)
            vt = vt_ref[c, pl.ds(n * VT_ROWS, VT_ROWS), :]
            acc_ref[out, :] = alpha[h:h + 1, :] * acc_ref[out, :] + _dot(vt, p)
        return carry

    lax.fori_loop(0, nk, attend, 0)
    outs = [acc_ref[h * VT_ROWS:h * VT_ROWS + HEAD_DIM, :] / acc_ref[h * VT_ROWS + HEAD_DIM:h * VT_ROWS + HEAD_DIM + 1, :]
            for h in range(N_HEADS)]
    o_ref[...] = jnp.concatenate(outs, axis=0).T.astype(o_ref.dtype)


def _prompt_attention(qbd, qi, wit, kib, kb, vt3, topk):
    b, t, _ = qbd.shape
    nblk = t // TQ
    qblk = lambda w: pl.BlockSpec((None, TQ, w), lambda bi, i: (bi, i, 0))
    full = lambda w: pl.BlockSpec((None, t, w), lambda bi, i: (bi, 0, 0))
    return pl.pallas_call(
        functools.partial(_prompt_attn_kernel, topk=topk),
        grid=(b, nblk),
        in_specs=[qblk(QBD_W), qblk(IQ_W), pl.BlockSpec((SUBLANES, TQ), lambda bi, i: (0, bi * nblk + i)),
                  full(IDX_DIM), full(KV_W), pl.BlockSpec((nblk, N_KV_HEADS * VT_ROWS, TQ), lambda bi, i: (bi, 0, 0))],
        out_specs=qblk(Q_W),
        out_shape=jax.ShapeDtypeStruct((b, t, Q_W), BF16),
        scratch_shapes=[pltpu.VMEM((t, TQ), F32), pltpu.VMEM((t, TQ), F32), pltpu.VMEM((N_HEADS, TQ), F32),
                        pltpu.VMEM((N_HEADS * VT_ROWS, TQ), F32), pltpu.VMEM((N_HEADS, TQ, TQ), F32)],
        compiler_params=_params(("parallel", "arbitrary")),
        name="prompt_attn",
    )(qbd, qi, wit, kib, kb, vt3)


def _page_specs(n_pages, block, group):
    nd = len(block)

    def spec(g, j):
        return pl.BlockSpec((None,) + block, lambda bi, pt: (pt[(bi * group + g) * n_pages + j],) + (0,) * nd)
    return [spec(g, j) for g in range(group) for j in range(n_pages)]


def _group_spec(group, *block):
    nd = len(block)
    return pl.BlockSpec((group,) + block, lambda bi, pt: (bi,) + (0,) * nd)


def _sample_score_kernel(pt_ref, qi_ref, w_ref, *refs, n_pages):
    del pt_ref
    pages, new_ref, o_ref = refs[:-2], refs[-2], refs[-1]
    t = qi_ref.shape[1] // N_IDX_HEADS
    for g in range(qi_ref.shape[0]):
        qi = qi_ref[g]
        w = w_ref[g]
        seq_pages = [p[...] for p in pages[g * n_pages:(g + 1) * n_pages]] + [new_ref[g]]
        for j, page in enumerate(seq_pages):
            s = _dot(qi, page.astype(BF16))
            s = jnp.maximum(s, 0.0) * w
            o_ref[g, :, j * PAGE_SIZE:(j + 1) * PAGE_SIZE] = jnp.sum(s.reshape(t, N_IDX_HEADS, PAGE_SIZE), axis=1)


def _sample_scores(page_table_flat, qi32, w32, kidx_t, ki_new_t, n_pages):
    nb, rows, _ = qi32.shape
    t = rows // N_IDX_HEADS
    lk = (n_pages + 1) * PAGE_SIZE
    g = SCORE_GROUP
    return pl.pallas_call(
        functools.partial(_sample_score_kernel, n_pages=n_pages),
        grid_spec=pltpu.PrefetchScalarGridSpec(
            num_scalar_prefetch=1,
            grid=(nb // g,),
            in_specs=[_group_spec(g, rows, IDX_DIM), _group_spec(g, rows, 1)]
            + _page_specs(n_pages, (IDX_DIM, PAGE_SIZE), g) + [_group_spec(g, IDX_DIM, PAGE_SIZE)],
            out_specs=_group_spec(g, t, lk),
        ),
        out_shape=jax.ShapeDtypeStruct((nb, t, lk), F32),
        compiler_params=_params(("parallel",)),
        name="sample_scores",
    )(page_table_flat, qi32, w32, *([kidx_t] * (g * n_pages)), ki_new_t)


def _sample_select_kernel(s_ref, bias_ref, sm_ref, *, past, t, topk):
    lk = s_ref.shape[0]
    nk = lk // PAGE_SIZE
    qcol = lax.broadcasted_iota(jnp.int32, (PAGE_SIZE, TQ_S), 1)
    qpos = past + (qcol & (t - 1))
    krow = lax.broadcasted_iota(jnp.int32, (PAGE_SIZE, TQ_S), 0)
    for c in range(nk):
        rows = slice(c * PAGE_SIZE, (c + 1) * PAGE_SIZE)
        sm_ref[rows, :] = jnp.where(krow + c * PAGE_SIZE <= qpos, s_ref[rows, :], NEG_INF)
    _select_bias(sm_ref, bias_ref, nk, PAGE_SIZE, topk)


def _sample_select(scores_t, past, t, topk):
    lk, n = scores_t.shape
    assert t & (t - 1) == 0 and TQ_S % t == 0, "token index is taken from the low bits of the query index"
    blk = pl.BlockSpec((lk, TQ_S), lambda i: (0, i))
    return pl.pallas_call(
        functools.partial(_sample_select_kernel, past=past, t=t, topk=topk),
        grid=(n // TQ_S,),
        in_specs=[blk],
        out_specs=blk,
        out_shape=jax.ShapeDtypeStruct((lk, n), F32),
        scratch_shapes=[pltpu.VMEM((lk, TQ_S), F32)],
        compiler_params=_params(("parallel",)),
        name="sample_select",
    )(scores_t)


def _sample_attn_kernel(pt_ref, q_ref, bias_ref, *refs, n_pages):
    del pt_ref
    group = q_ref.shape[0]
    n_pg = group * n_pages
    k_pages, k_new, v_pages, v_new, o_ref = refs[:n_pg], refs[n_pg], refs[n_pg + 1:2 * n_pg + 1], refs[-2], refs[-1]
    kt = lambda page: page.reshape(KV_W, PAGE_SIZE).astype(BF16)
    for g in range(group):
        qb = q_ref[g]
        reps = qb.shape[0] // bias_ref.shape[1]
        bias = jnp.concatenate([bias_ref[g]] * reps, axis=0)
        ks = [p[...] for p in k_pages[g * n_pages:(g + 1) * n_pages]] + [k_new[g]]
        vs = [p[...] for p in v_pages[g * n_pages:(g + 1) * n_pages]] + [v_new[g]]
        lg = jnp.concatenate([_dot(qb, kt(kp)) for kp in ks], axis=-1) + bias
        m = jnp.max(lg, axis=-1, keepdims=True)
        p = jnp.exp2(lg - m)
        l = jnp.sum(p, axis=-1, keepdims=True)
        pb = p.astype(BF16)
        acc = jnp.zeros(o_ref.shape[1:], F32)
        for j, vp in enumerate(vs):
            acc = acc + _dot_nt(pb[:, j * PAGE_SIZE:(j + 1) * PAGE_SIZE], kt(vp))
        o_ref[g] = acc / l


def _sample_attend(page_table_flat, q_bd, bias, k_t, v_t, k_new_t, v_new_t, n_pages):
    nb, rows, _ = q_bd.shape
    t = bias.shape[1]
    lk = bias.shape[2]
    g = ATTN_GROUP
    page = (N_KV_HEADS, HEAD_DIM, PAGE_SIZE)
    return pl.pallas_call(
        functools.partial(_sample_attn_kernel, n_pages=n_pages),
        grid_spec=pltpu.PrefetchScalarGridSpec(
            num_scalar_prefetch=1,
            grid=(nb // g,),
            in_specs=[_group_spec(g, rows, KV_W), _group_spec(g, t, lk)]
            + _page_specs(n_pages, page, g) + [_group_spec(g, *page)]
            + _page_specs(n_pages, page, g) + [_group_spec(g, *page)],
            out_specs=_group_spec(g, rows, KV_W),
        ),
        out_shape=jax.ShapeDtypeStruct((nb, rows, KV_W), F32),
        compiler_params=_params(("parallel",)),
        name="sample_attn",
    )(page_table_flat, q_bd, bias, *([k_t] * (g * n_pages)), k_new_t, *([v_t] * (g * n_pages)), v_new_t)


def _glu(xb, w_glu_ref):
    gi = _dot(xb, w_glu_ref[...])
    return gi[:, :C_CONV] * jax.nn.sigmoid(gi[:, C_CONV:])


def _mix_tail(x, xb, attn_b, c, w_g_ref, w_ao_ref, lncg_ref, lncb_ref, w_co_ref, w_out_ref, ln1g_ref, ln1b_ref):
    cn = _layer_norm(c, lncg_ref[...], lncb_ref[...])
    c_branch = _dot((cn * jax.nn.sigmoid(cn)).astype(BF16), w_co_ref[...])
    a_branch = _dot(attn_b, w_ao_ref[...])
    g = _dot(xb, w_g_ref[...])
    merged = jax.nn.sigmoid(g[:, :D_MODEL]) * a_branch + jax.nn.sigmoid(g[:, D_MODEL:]) * c_branch
    h = ALPHA * x + _dot(merged.astype(BF16), w_out_ref[...])
    return _layer_norm(h, ln1g_ref[...], ln1b_ref[...])


def _mix_prompt_kernel(x_ref, attn_ref, w_glu_ref, w_g_ref, w_ao_ref, w_dw_ref, b_dw_ref, lncg_ref, lncb_ref,
                       w_co_ref, w_out_ref, ln1g_ref, ln1b_ref, x1_ref, tail_ref, xp_ref):
    tm = x_ref.shape[0]

    @pl.when(pl.program_id(1) == 0)
    def _():
        xp_ref[0:CONV_HALO, :] = jnp.zeros((CONV_HALO, C_CONV), F32)

    x = x_ref[...]
    xb = x.astype(BF16)
    glu = _glu(xb, w_glu_ref)
    xp_ref[CONV_HALO:CONV_HALO + tm, :] = glu
    tail_ref[...] = glu[tm - CONV_HALO:, :]

    first = CONV_HALO - (CONV_WIDTH - 1)
    rb = 128
    cols = []
    for c0 in range(0, C_CONV, LANES):
        lanes = slice(c0, c0 + LANES)
        blocks = []
        for r0 in range(0, tm, rb):
            y = jnp.broadcast_to(b_dw_ref[:, lanes], (rb, LANES))
            for b in range(SUBLANES):
                rows = rb + (SUBLANES if b else 0)
                part = None
                for j in range(CONV_WIDTH):
                    if (first + j) % SUBLANES == b:
                        a0 = first + j - b + r0
                        term = w_dw_ref[j:j + 1, lanes] * xp_ref[a0:a0 + rows, lanes]
                        part = term if part is None else part + term
                y = y + part[b:b + rb, :]
            blocks.append(y)
        cols.append(jnp.concatenate(blocks, axis=0))
    c = jnp.concatenate(cols, axis=-1)
    xp_ref[0:CONV_HALO, :] = xp_ref[tm:tm + CONV_HALO, :]

    x1_ref[...] = _mix_tail(x, xb, attn_ref[...], c, w_g_ref, w_ao_ref, lncg_ref, lncb_ref, w_co_ref,
                            w_out_ref, ln1g_ref, ln1b_ref)


def _mix_sample_kernel(x_ref, attn_ref, hist_ref, w_glu_ref, w_g_ref, w_ao_ref, w_dw_ref, b_dw_ref, lncg_ref,
                       lncb_ref, w_co_ref, w_out_ref, ln1g_ref, ln1b_ref, x1_ref, glu_ref):
    nb = hist_ref.shape[1]
    t = x_ref.shape[0] // nb
    n_hist = CONV_WIDTH - 1
    x = x_ref[...]
    xb = x.astype(BF16)
    glu = _glu(xb, w_glu_ref)
    glu_ref[...] = glu

    def slab(m):
        return hist_ref[m] if m < n_hist else glu[(m - n_hist) * nb:(m - n_hist + 1) * nb, :]

    outs = []
    for ti in range(t):
        acc = jnp.broadcast_to(b_dw_ref[...], (nb, C_CONV))
        for j in range(CONV_WIDTH):
            acc = acc + w_dw_ref[j:j + 1, :] * slab(ti + j)
        outs.append(acc)
    c = jnp.concatenate(outs, axis=0)
    x1_ref[...] = _mix_tail(x, xb, attn_ref[...], c, w_g_ref, w_ao_ref, lncg_ref, lncb_ref, w_co_ref,
                            w_out_ref, ln1g_ref, ln1b_ref)


def _mix_weight_specs():
    return [
        _const_spec((D_MODEL, 2 * C_CONV)), _const_spec((D_MODEL, 2 * D_MODEL)), _const_spec((Q_W, D_MODEL)),
        _const_spec((CONV_WIDTH, C_CONV)), _const_spec((1, C_CONV)), _const_spec((1, C_CONV)),
        _const_spec((1, C_CONV)), _const_spec((C_CONV, D_MODEL)), _const_spec((D_MODEL, D_MODEL)),
        _const_spec((1, D_MODEL)), _const_spec((1, D_MODEL)),
    ]


def _mix_prompt(x, attn, mix_w, tm):
    b, t, _ = x.shape
    rows = lambda w: pl.BlockSpec((None, tm, w), lambda bi, i: (bi, i, 0))
    return pl.pallas_call(
        _mix_prompt_kernel,
        grid=(b, t // tm),
        in_specs=[rows(D_MODEL), rows(Q_W)] + _mix_weight_specs(),
        out_specs=[rows(D_MODEL), pl.BlockSpec((None, CONV_HALO, C_CONV), lambda bi, i: (bi, 0, 0))],
        out_shape=[jax.ShapeDtypeStruct((b, t, D_MODEL), F32), jax.ShapeDtypeStruct((b, CONV_HALO, C_CONV), F32)],
        scratch_shapes=[pltpu.VMEM((CONV_HALO + tm, C_CONV), F32)],
        compiler_params=_params(("parallel", "arbitrary")),
        name="mix_prompt",
    )(x, attn, *mix_w)


def _mix_sample(x_tm, attn_tm, hist_tm, mix_w):
    n = x_tm.shape[0]
    return pl.pallas_call(
        _mix_sample_kernel,
        grid=(1,),
        in_specs=[_const_spec((n, D_MODEL)), _const_spec((n, Q_W)), _const_spec(hist_tm.shape)] + _mix_weight_specs(),
        out_specs=[_whole_spec((n, D_MODEL)), _whole_spec((n, C_CONV))],
        out_shape=[jax.ShapeDtypeStruct((n, D_MODEL), F32), jax.ShapeDtypeStruct((n, C_CONV), F32)],
        compiler_params=_params(("arbitrary",)),
        name="mix_sample",
    )(x_tm, attn_tm, hist_tm, *mix_w)


def _ffn_tail(x1, uc, gate, w_down_ref, ln2g_ref, ln2b_ref):
    f = _dot((jax.nn.gelu(uc) * gate).astype(BF16), w_down_ref[...])
    return _layer_norm(ALPHA * x1 + f, ln2g_ref[...], ln2b_ref[...])


def _ffn_prompt_kernel(x1_ref, w_up_ref, w_gate_ref, w_dw_ref, b_dw_ref, w_down_ref, ln2g_ref, ln2b_ref,
                       y_ref, tail_ref, up_ref):
    tm = x1_ref.shape[0]

    @pl.when(pl.program_id(1) == 0)
    def _():
        up_ref[0:FFN_HALO, :] = jnp.zeros((FFN_HALO, D_FF), F32)

    x1 = x1_ref[...]
    x1b = x1.astype(BF16)
    u = _dot(x1b, w_up_ref[...])
    gate = _dot(x1b, w_gate_ref[...])
    up_ref[FFN_HALO:FFN_HALO + tm, :] = u
    tail_ref[...] = u[tm - FFN_HALO:, :]
    first = FFN_HALO - (FFN_CONV_WIDTH - 1)
    uc = b_dw_ref[...]
    for j in range(FFN_CONV_WIDTH):
        uc = uc + w_dw_ref[j:j + 1, :] * up_ref[first + j:first + j + tm, :]
    up_ref[0:FFN_HALO, :] = up_ref[tm:tm + FFN_HALO, :]
    y_ref[...] = _ffn_tail(x1, uc, gate, w_down_ref, ln2g_ref, ln2b_ref)


def _ffn_sample_kernel(x1_ref, hist_ref, w_up_ref, w_gate_ref, w_dw_ref, b_dw_ref, w_down_ref, ln2g_ref, ln2b_ref,
                       y_ref, u_ref):
    nb = hist_ref.shape[1]
    t = x1_ref.shape[0] // nb
    n_hist = FFN_CONV_WIDTH - 1
    x1 = x1_ref[...]
    x1b = x1.astype(BF16)
    u = _dot(x1b, w_up_ref[...])
    gate = _dot(x1b, w_gate_ref[...])
    u_ref[...] = u

    def slab(m):
        return hist_ref[m] if m < n_hist else u[(m - n_hist) * nb:(m - n_hist + 1) * nb, :]

    outs = []
    for ti in range(t):
        acc = jnp.broadcast_to(b_dw_ref[...], (nb, D_FF))
        for j in range(FFN_CONV_WIDTH):
            acc = acc + w_dw_ref[j:j + 1, :] * slab(ti + j)
        outs.append(acc)
    uc = jnp.concatenate(outs, axis=0)
    y_ref[...] = _ffn_tail(x1, uc, gate, w_down_ref, ln2g_ref, ln2b_ref)


def _ffn_weight_specs():
    return [
        _const_spec((D_MODEL, D_FF)), _const_spec((D_MODEL, D_FF)), _const_spec((FFN_CONV_WIDTH, D_FF)),
        _const_spec((1, D_FF)), _const_spec((D_FF, D_MODEL)), _const_spec((1, D_MODEL)), _const_spec((1, D_MODEL)),
    ]


def _ffn_prompt(x1, ffn_w, tm):
    b, t, _ = x1.shape
    rows = pl.BlockSpec((None, tm, D_MODEL), lambda bi, i: (bi, i, 0))
    return pl.pallas_call(
        _ffn_prompt_kernel,
        grid=(b, t // tm),
        in_specs=[rows] + _ffn_weight_specs(),
        out_specs=[rows, pl.BlockSpec((None, FFN_HALO, D_FF), lambda bi, i: (bi, 0, 0))],
        out_shape=[jax.ShapeDtypeStruct((b, t, D_MODEL), F32), jax.ShapeDtypeStruct((b, FFN_HALO, D_FF), F32)],
        scratch_shapes=[pltpu.VMEM((FFN_HALO + tm, D_FF), F32)],
        compiler_params=_params(("parallel", "arbitrary")),
        name="ffn_prompt",
    )(x1, *ffn_w)


def _ffn_sample(x1_tm, hist_tm, ffn_w):
    n = x1_tm.shape[0]
    return pl.pallas_call(
        _ffn_sample_kernel,
        grid=(1,),
        in_specs=[_const_spec((n, D_MODEL)), _const_spec(hist_tm.shape)] + _ffn_weight_specs(),
        out_specs=[_whole_spec((n, D_MODEL)), _whole_spec((n, D_FF))],
        out_shape=[jax.ShapeDtypeStruct((n, D_MODEL), F32), jax.ShapeDtypeStruct((n, D_FF), F32)],
        compiler_params=_params(("arbitrary",)),
        name="ffn_sample",
    )(x1_tm, hist_tm, *ffn_w)


def _token_major(a):
    b, t, w = a.shape
    return jnp.transpose(a, (1, 0, 2)).reshape(t * b, w)


def _batch_major(a, b):
    tb, w = a.shape
    return jnp.transpose(a.reshape(tb // b, b, w), (1, 0, 2))


def _stage_weights(w_in):
    o = 0
    w_q = w_in[:, o:o + Q_W]; o += Q_W
    w_k = w_in[:, o:o + KV_W]; o += KV_W
    w_v = w_in[:, o:o + KV_W]; o += KV_W
    w_qi = w_in[:, o:o + IQ_W]; o += IQ_W
    w_ki = w_in[:, o:o + IDX_DIM]; o += IDX_DIM
    w_wi = w_in[:, o:o + N_IDX_HEADS]; o += N_IDX_HEADS
    zero = jnp.zeros((D_MODEL, HEAD_DIM), w_in.dtype)
    qbd_cols = []
    for h in range(N_HEADS):
        w_h = w_q[:, h * HEAD_DIM:(h + 1) * HEAD_DIM]
        qbd_cols += [w_h, zero] if h // GROUP == 0 else [zero, w_h]
    w_att = jnp.concatenate(qbd_cols + [w_k, w_v, w_qi, w_ki, zero], axis=1).astype(BF16)
    w_t = jnp.concatenate([w_v, w_wi], axis=1).T.astype(BF16)
    return w_att, w_t, o


def kernel(x_prompt, x_sample, cache_k, cache_v, cache_kidx, state_conv, state_ffn, page_table, w_in, w_attn_o, w_conv_dw, b_conv_dw, ln_conv_g, ln_conv_b, w_conv_o, w_out, ln1_g, ln1_b, w_ffn_up, w_ffn_gate, w_ffn_dw, b_ffn_dw, w_ffn_down, ln2_g, ln2_b):
    bp, tp, _ = x_prompt.shape
    bs, ts, _ = x_sample.shape
    n_pages = page_table.shape[1]
    past = n_pages * PAGE_SIZE
    assert N_KV_HEADS == 2 and PROJ_T == KV_W + N_IDX_HEADS

    w_att, w_t, o = _stage_weights(w_in)
    w_glu = w_in[:, o:o + 2 * C_CONV].astype(BF16); o += 2 * C_CONV
    w_g = w_in[:, o:o + 2 * D_MODEL].astype(BF16)
    row2 = lambda a: a.reshape(1, -1)
    mix_w = (w_glu, w_g, w_attn_o.astype(BF16), w_conv_dw, row2(b_conv_dw), row2(ln_conv_g), row2(ln_conv_b),
             w_conv_o.astype(BF16), w_out.astype(BF16), row2(ln1_g), row2(ln1_b))
    ffn_w = (w_ffn_up.astype(BF16), w_ffn_gate.astype(BF16), w_ffn_dw, row2(b_ffn_dw), w_ffn_down.astype(BF16),
             row2(ln2_g), row2(ln2_b))

    qbd, k, v, qi, ki, kb, kib, vt3, wit = _proj(x_prompt.reshape(bp * tp, D_MODEL), w_att, w_t, 512)
    seq = lambda a: a.reshape(bp, tp, a.shape[-1])
    attn_p = _prompt_attention(seq(qbd), seq(qi), wit, seq(kib), seq(kb), vt3, min(TOPK_MAX, tp // 4))
    x1_p, conv_tail = _mix_prompt(x_prompt, attn_p, mix_w, 256)
    y_p, ffn_tail = _ffn_prompt(x1_p, ffn_w, 256)
    k_p = k.reshape(bp, tp, N_KV_HEADS, HEAD_DIM)
    v_p = v.reshape(bp, tp, N_KV_HEADS, HEAD_DIM)
    ki_p = ki.reshape(bp, tp, IDX_DIM)
    conv_p = conv_tail[:, CONV_HALO - (CONV_WIDTH - 1):, :]
    ffn_p = ffn_tail[:, FFN_HALO - (FFN_CONV_WIDTH - 1):, :]

    n_s = bs * ts
    qbd_s, ks, vs, qis, kis, _, _, _, wit_s = _proj(x_sample.reshape(n_s, D_MODEL), w_att, w_t, n_s)
    pt_flat = page_table.reshape(-1).astype(jnp.int32)
    new_page = lambda a: jnp.pad(jnp.swapaxes(a.reshape(bs, ts, -1), 1, 2), ((0, 0), (0, 0), (0, PAGE_SIZE - ts)))
    qi32 = qis.reshape(bs, ts * N_IDX_HEADS, IDX_DIM)
    w32 = wit_s.T.reshape(bs, ts * N_IDX_HEADS, 1)
    scores = _sample_scores(pt_flat, qi32, w32, jnp.transpose(cache_kidx, (0, 2, 1)), new_page(kis), n_pages)
    lk = scores.shape[-1]
    bias_t = _sample_select(scores.reshape(n_s, lk).T, past, ts, min(TOPK_MAX, (past + ts) // 4))
    q_bd = jnp.transpose(qbd_s.reshape(bs, ts, N_HEADS, KV_W), (0, 2, 1, 3)).reshape(bs, N_HEADS * ts, KV_W)
    as_pages = lambda a: jnp.transpose(a, (0, 2, 3, 1))
    kv_new = lambda a: new_page(a).reshape(bs, N_KV_HEADS, HEAD_DIM, PAGE_SIZE)
    o_bd = _sample_attend(pt_flat, q_bd, bias_t.T.reshape(bs, ts, lk), as_pages(cache_k), as_pages(cache_v),
                          kv_new(ks), kv_new(vs), n_pages)
    o6 = o_bd.reshape(bs, N_KV_HEADS, GROUP, ts, N_KV_HEADS, HEAD_DIM)
    o_sel = jnp.stack([o6[:, n, :, :, n, :] for n in range(N_KV_HEADS)], axis=1)
    attn_s = jnp.transpose(o_sel, (0, 3, 1, 2, 4)).reshape(bs, ts, Q_W)

    x1_s, glu_s = _mix_sample(_token_major(x_sample), _token_major(attn_s).astype(BF16),
                              jnp.transpose(state_conv, (1, 0, 2)), mix_w)
    y_s, u_s = _ffn_sample(x1_s, jnp.transpose(state_ffn, (1, 0, 2)), ffn_w)
    y_s = _batch_major(y_s, bs)
    conv_s = jnp.concatenate([state_conv, _batch_major(glu_s, bs)], axis=1)[:, -(CONV_WIDTH - 1):, :]
    ffn_s = jnp.concatenate([state_ffn, _batch_major(u_s, bs)], axis=1)[:, -(FFN_CONV_WIDTH - 1):, :]
    k_s = ks.reshape(bs, ts, N_KV_HEADS, HEAD_DIM)
    v_s = vs.reshape(bs, ts, N_KV_HEADS, HEAD_DIM)
    ki_s = kis.reshape(bs, ts, IDX_DIM)

    return (y_p, y_s, k_p, v_p, ki_p, conv_p, ffn_p, k_s, v_s, ki_s, conv_s, ffn_s)
```

```python
import functools

import jax
import jax.numpy as jnp
from jax import lax
from jax.experimental import pallas as pl
from jax.experimental.pallas import tpu as pltpu

D_MODEL = 1024
N_HEADS = 8
HEAD_DIM = 64
N_KV_HEADS = 2
N_IDX_HEADS = 8
IDX_DIM = 64
TOPK_MAX = 256
C_CONV = D_MODEL // 2
CONV_WIDTH = 31
D_FF = 2816
FFN_CONV_WIDTH = 3
LN_EPS = 1e-5
DEPTH = 1
ALPHA = (2.0 * DEPTH) ** 0.25
PAGE_SIZE = 128

Q_W = N_HEADS * HEAD_DIM
KV_W = N_KV_HEADS * HEAD_DIM
IQ_W = N_IDX_HEADS * IDX_DIM
GROUP = N_HEADS // N_KV_HEADS

LANES = 128
SUBLANES = 8
QBD_W = N_HEADS * KV_W
PROJ_W = Q_W + KV_W + IQ_W + LANES
PROJ_T = 2 * KV_W + IDX_DIM + N_IDX_HEADS
TQ = 256
TQ_S = 256
SCORE_GROUP = 8
ATTN_GROUP = 4
CONV_HALO = 32
FFN_HALO = 8
N_BISECT = 19
VT_ROWS = HEAD_DIM + 16
LOG2E = 1.4426950408889634
VMEM_LIMIT = 56 * 1024 * 1024

F32 = jnp.float32
BF16 = jnp.bfloat16
NEG_INF = float("-inf")
POS_INF = float("inf")


def _dot(a, b):
    return jnp.dot(a, b, preferred_element_type=F32)


def _dot_nt(a, b):
    return lax.dot_general(a, b, (((1,), (1,)), ((), ())), preferred_element_type=F32)


def _layer_norm(x, g, b):
    mu = jnp.mean(x, axis=-1, keepdims=True)
    xc = x - mu
    var = jnp.mean(xc * xc, axis=-1, keepdims=True)
    return xc * lax.rsqrt(var + LN_EPS) * g + b


def _params(sem):
    return pltpu.CompilerParams(dimension_semantics=sem, vmem_limit_bytes=VMEM_LIMIT)


def _whole_spec(shape):
    nd = len(shape)
    return pl.BlockSpec(shape, lambda *_: (0,) * nd)


def _const_spec(shape):
    nd = len(shape)
    return pl.BlockSpec(shape, lambda *_: (0,) * nd, pipeline_mode=pl.Buffered(1))


def _proj_kernel(x_ref, w_ref, wt_ref, qbd_ref, qi_ref, kb_ref, kib_ref, vt_ref, kt_ref, vtf_ref, kit_ref, wit_ref):
    xb = x_ref[...].astype(BF16)
    o = 0
    qp = _dot(xb, w_ref[:, o:o + Q_W]); o += Q_W
    k = _dot(xb, w_ref[:, o:o + KV_W]); o += KV_W
    qi = _dot(xb, w_ref[:, o:o + IQ_W]); o += IQ_W
    ki = _dot(xb, w_ref[:, o:o + LANES])[:, :IDX_DIM]
    qp = (qp * (HEAD_DIM ** -0.5 * LOG2E)).astype(BF16)
    low = lax.broadcasted_iota(jnp.int32, (qp.shape[0], KV_W), 1) < HEAD_DIM
    for h in range(N_HEADS):
        pair = qp[:, (h % GROUP) * KV_W:(h % GROUP + 1) * KV_W]
        qbd_ref[:, h * KV_W:(h + 1) * KV_W] = jnp.where(low if h < GROUP else jnp.logical_not(low), pair, 0.0)
    qi_ref[...] = qi.astype(BF16)
    kb_ref[...] = k.astype(BF16)
    kib_ref[...] = ki.astype(BF16)
    t = _dot_nt(wt_ref[...], xb)
    kt_ref[...] = t[:KV_W, :]
    vtf_ref[...] = t[KV_W:2 * KV_W, :]
    kit_ref[...] = t[2 * KV_W:2 * KV_W + IDX_DIM, :]
    wit_ref[...] = t[2 * KV_W + IDX_DIM:, :] * ((IDX_DIM ** -0.5) * (N_IDX_HEADS ** -0.5))
    vt = t[KV_W:2 * KV_W, :].astype(BF16)
    ones = jnp.ones((VT_ROWS - HEAD_DIM, TQ), BF16)
    for j in range(vt_ref.shape[0]):
        for n in range(N_KV_HEADS):
            vt_ref[j, n * VT_ROWS:n * VT_ROWS + HEAD_DIM, :] = vt[n * HEAD_DIM:(n + 1) * HEAD_DIM, j * TQ:(j + 1) * TQ]
            vt_ref[j, n * VT_ROWS + HEAD_DIM:(n + 1) * VT_ROWS, :] = ones


def _proj(x, w_att, w_t, tm):
    b, t, _ = x.shape
    n = b * t
    tiles = t // tm
    row = lambda w: pl.BlockSpec((tm, w), lambda i: (i, 0))
    col = lambda r: pl.BlockSpec((None, r, tm), lambda i: (i // tiles, 0, i % tiles))
    outs = [(QBD_W, BF16), (IQ_W, BF16), (KV_W, BF16), (IDX_DIM, BF16)]
    return pl.pallas_call(
        _proj_kernel,
        grid=(n // tm,),
        in_specs=[row(D_MODEL), _const_spec((D_MODEL, PROJ_W)), _const_spec((PROJ_T, D_MODEL))],
        out_specs=[row(w) for w, _ in outs]
        + [pl.BlockSpec((tm // TQ, N_KV_HEADS * VT_ROWS, TQ), lambda i: (i, 0, 0)), col(KV_W), col(KV_W), col(IDX_DIM),
           pl.BlockSpec((SUBLANES, tm), lambda i: (0, i))],
        out_shape=[jax.ShapeDtypeStruct((n, w), dt) for w, dt in outs]
        + [jax.ShapeDtypeStruct((n // TQ, N_KV_HEADS * VT_ROWS, TQ), BF16), jax.ShapeDtypeStruct((b, KV_W, t), F32),
           jax.ShapeDtypeStruct((b, KV_W, t), F32), jax.ShapeDtypeStruct((b, IDX_DIM, t), F32),
           jax.ShapeDtypeStruct((SUBLANES, n), F32)],
        compiler_params=_params(("parallel",)),
        name="proj",
    )(x.reshape(n, D_MODEL), w_att, w_t)


def _chunk(ref, c, ch):
    return ref[pl.ds(pl.multiple_of(c * ch, ch), ch), :]


def _fold_rows(x, op):
    parts = [x[r:r + SUBLANES, :] for r in range(0, x.shape[0], SUBLANES)]
    accs = parts[:4]
    for i, part in enumerate(parts[4:]):
        accs[i % 4] = op(accs[i % 4], part)
    while len(accs) > 1:
        accs = [op(accs[a], accs[a + 1]) for a in range(0, len(accs) - 1, 2)] + ([accs[-1]] if len(accs) % 2 else [])
    return accs[0]


def _reduce_keys(s_ref, nk, ch, init, f, op, red):
    w = s_ref.shape[1]

    def body(c, acc):
        return op(acc, _fold_rows(f(_chunk(s_ref, c, ch), c), op))

    acc = lax.fori_loop(0, nk, body, jnp.full((SUBLANES, w), init, F32))
    return red(acc, axis=0, keepdims=True)


def _select_bias(s_ref, bias_ref, nk, ch, topk):
    w = s_ref.shape[1]
    kf = float(topk)
    count = lambda pred: _reduce_keys(s_ref, nk, ch, 0.0, lambda x, c: jnp.where(pred(x, c), 1.0, 0.0),
                                      jnp.add, jnp.sum)
    min_above = lambda t: _reduce_keys(s_ref, nk, ch, POS_INF, lambda x, c: jnp.where(x > t, x, POS_INF),
                                       jnp.minimum, jnp.min)

    neg = jnp.full((1, w), NEG_INF, F32)

    hi = _reduce_keys(s_ref, nk, ch, NEG_INF, lambda x, c: x, jnp.maximum, jnp.max)
    lo_fin = min_above(neg)
    n_adm = count(lambda x, _: x > neg)

    def bisect(_, st):
        lo, lo_fin, hi, n_lo = st
        mid = 0.5 * lo_fin + 0.5 * hi
        c = count(lambda x, _: x > mid)
        ok = c >= kf
        return jnp.where(ok, mid, lo), jnp.where(ok, mid, lo_fin), jnp.where(ok, hi, mid), jnp.where(ok, c, n_lo)

    lo, _, _, n_lo = lax.fori_loop(0, N_BISECT, bisect, (neg, lo_fin, hi, n_adm))

    def peel(st):
        lo, thr, n_gt, done = st
        v = min_above(lo)
        c = count(lambda x, _: x > v)
        found = c < kf
        newly = jnp.logical_and(done < 0.5, found)
        return (jnp.where(jnp.logical_or(found, done > 0.5), lo, v), jnp.where(newly, v, thr),
                jnp.where(newly, c, n_gt), jnp.where(found, 1.0, done))

    _, thr, n_gt, _ = lax.while_loop(lambda st: jnp.min(st[3]) < 0.5, peel,
                                     (lo, lo, n_lo, jnp.where(n_lo <= kf, 1.0, 0.0)))

    need = jnp.where(thr == neg, 0.0, kf - n_gt)
    tri = (lax.broadcasted_iota(jnp.int32, (ch, ch), 1) <= lax.broadcasted_iota(jnp.int32, (ch, ch), 0)).astype(BF16)

    def write(c, seen):
        x = _chunk(s_ref, c, ch)
        eq = x == thr
        rank = seen + _dot(tri, jnp.where(eq, 1.0, 0.0).astype(BF16))
        sel = jnp.logical_or(x > thr, jnp.logical_and(eq, rank <= need))
        bias_ref[pl.ds(pl.multiple_of(c * ch, ch), ch), :] = jnp.where(sel, 0.0, NEG_INF)
        return rank[ch - 1:ch, :]

    lax.fori_loop(0, nk, write, jnp.zeros((1, w), F32))


def _prompt_attn_kernel(qbd_ref, qi_ref, wit_ref, kib_ref, kb_ref, vt_ref, o_ref, s_ref, bias_ref, m_ref, acc_ref,
                        lg_ref, *, topk):
    i = pl.program_id(1)
    nk = i + 1
    qi = qi_ref[...]
    wit = wit_ref[...]
    q_pos = lax.broadcasted_iota(jnp.int32, (TQ, TQ), 1) + i * TQ
    k_off = lax.broadcasted_iota(jnp.int32, (TQ, TQ), 0)

    def score(c, carry):
        kc = _chunk(kib_ref, c, TQ)
        s = jnp.zeros((TQ, TQ), F32)
        for h in range(N_IDX_HEADS):
            z = _dot_nt(kc, qi[:, h * IDX_DIM:(h + 1) * IDX_DIM])
            s = s + jnp.maximum(z, 0.0) * wit[h:h + 1, :]
        s_ref[pl.ds(pl.multiple_of(c * TQ, TQ), TQ), :] = jnp.where(k_off + c * TQ <= q_pos, s, NEG_INF)
        return carry

    lax.fori_loop(0, nk, score, 0)
    _select_bias(s_ref, bias_ref, nk, TQ, topk)

    m_ref[...] = jnp.full(m_ref.shape, NEG_INF, F32)
    acc_ref[...] = jnp.zeros(acc_ref.shape, F32)

    def attend(c, carry):
        rows = pl.ds(pl.multiple_of(c * TQ, TQ), TQ)
        mx = []
        for h in range(N_HEADS):
            lg = _dot_nt(kb_ref[rows, :], qbd_ref[:, h * KV_W:(h + 1) * KV_W]) + bias_ref[rows, :]
            lg_ref[h] = lg
            mx.append(_fold_rows(lg, jnp.maximum))
        m_old = m_ref[...]
        m_new = jnp.maximum(m_old, jnp.concatenate([jnp.max(x, axis=0, keepdims=True) for x in mx], axis=0))
        m_use = jnp.where(m_new == NEG_INF, 0.0, m_new)
        alpha = jnp.exp2(m_old - m_use)
        m_ref[...] = m_new
        for h in range(N_HEADS):
            n = h // GROUP
            out = slice(h * VT_ROWS, (h + 1) * VT_ROWS)
            p = jnp.exp2(lg_ref[h] - m_use[h:h + 1, :]).astype(BF16)
            vt = vt_ref[c, pl.ds(n * VT_ROWS, VT_ROWS), :]
            acc_ref[out, :] = alpha[h:h + 1, :] * acc_ref[out, :] + _dot(vt, p)
        return carry

    lax.fori_loop(0, nk, attend, 0)
    outs = [acc_ref[h * VT_ROWS:h * VT_ROWS + HEAD_DIM, :] / acc_ref[h * VT_ROWS + HEAD_DIM:h * VT_ROWS + HEAD_DIM + 1, :]
            for h in range(N_HEADS)]
    o_ref[...] = jnp.concatenate(outs, axis=0).T.astype(o_ref.dtype)


def _prompt_attention(qbd, qi, wit, kib, kb, vt3, topk):
    b, t, _ = qbd.shape
    nblk = t // TQ
    qblk = lambda w: pl.BlockSpec((None, TQ, w), lambda bi, i: (bi, i, 0))
    full = lambda w: pl.BlockSpec((None, t, w), lambda bi, i: (bi, 0, 0))
    return pl.pallas_call(
        functools.partial(_prompt_attn_kernel, topk=topk),
        grid=(b, nblk),
        in_specs=[qblk(QBD_W), qblk(IQ_W), pl.BlockSpec((SUBLANES, TQ), lambda bi, i: (0, bi * nblk + i)),
                  full(IDX_DIM), full(KV_W), pl.BlockSpec((nblk, N_KV_HEADS * VT_ROWS, TQ), lambda bi, i: (bi, 0, 0))],
        out_specs=qblk(Q_W),
        out_shape=jax.ShapeDtypeStruct((b, t, Q_W), BF16),
        scratch_shapes=[pltpu.VMEM((t, TQ), F32), pltpu.VMEM((t, TQ), F32), pltpu.VMEM((N_HEADS, TQ), F32),
                        pltpu.VMEM((N_HEADS * VT_ROWS, TQ), F32), pltpu.VMEM((N_HEADS, TQ, TQ), F32)],
        compiler_params=_params(("parallel", "arbitrary")),
        name="prompt_attn",
    )(qbd, qi, wit, kib, kb, vt3)


def _page_specs(n_pages, block, group):
    nd = len(block)

    def spec(g, j):
        return pl.BlockSpec((None,) + block, lambda bi, pt: (pt[(bi * group + g) * n_pages + j],) + (0,) * nd)
    return [spec(g, j) for g in range(group) for j in range(n_pages)]


def _group_spec(group, *block):
    nd = len(block)
    return pl.BlockSpec((group,) + block, lambda bi, pt: (bi,) + (0,) * nd)


def _sample_score_kernel(pt_ref, qi_ref, w_ref, *refs, n_pages):
    del pt_ref
    pages, new_ref, o_ref = refs[:-2], refs[-2], refs[-1]
    t = qi_ref.shape[1] // N_IDX_HEADS
    for g in range(qi_ref.shape[0]):
        qi = qi_ref[g]
        w = w_ref[g]
        seq_pages = [p[...] for p in pages[g * n_pages:(g + 1) * n_pages]] + [new_ref[g]]
        for j, page in enumerate(seq_pages):
            s = _dot(qi, page.astype(BF16))
            s = jnp.maximum(s, 0.0) * w
            o_ref[g, :, j * PAGE_SIZE:(j + 1) * PAGE_SIZE] = jnp.sum(s.reshape(t, N_IDX_HEADS, PAGE_SIZE), axis=1)


def _sample_scores(page_table_flat, qi32, w32, kidx_t, ki_new_t, n_pages):
    nb, rows, _ = qi32.shape
    t = rows // N_IDX_HEADS
    lk = (n_pages + 1) * PAGE_SIZE
    g = SCORE_GROUP
    return pl.pallas_call(
        functools.partial(_sample_score_kernel, n_pages=n_pages),
        grid_spec=pltpu.PrefetchScalarGridSpec(
            num_scalar_prefetch=1,
            grid=(nb // g,),
            in_specs=[_group_spec(g, rows, IDX_DIM), _group_spec(g, rows, 1)]
            + _page_specs(n_pages, (IDX_DIM, PAGE_SIZE), g) + [_group_spec(g, IDX_DIM, PAGE_SIZE)],
            out_specs=_group_spec(g, t, lk),
        ),
        out_shape=jax.ShapeDtypeStruct((nb, t, lk), F32),
        compiler_params=_params(("parallel",)),
        name="sample_scores",
    )(page_table_flat, qi32, w32, *([kidx_t] * (g * n_pages)), ki_new_t)


def _sample_select_kernel(s_ref, bias_ref, sm_ref, *, past, t, topk):
    lk = s_ref.shape[0]
    nk = lk // PAGE_SIZE
    qcol = lax.broadcasted_iota(jnp.int32, (PAGE_SIZE, TQ_S), 1)
    qpos = past + (qcol & (t - 1))
    krow = lax.broadcasted_iota(jnp.int32, (PAGE_SIZE, TQ_S), 0)
    for c in range(nk):
        rows = slice(c * PAGE_SIZE, (c + 1) * PAGE_SIZE)
        sm_ref[rows, :] = jnp.where(krow + c * PAGE_SIZE <= qpos, s_ref[rows, :], NEG_INF)
    _select_bias(sm_ref, bias_ref, nk, PAGE_SIZE, topk)


def _sample_select(scores_t, past, t, topk):
    lk, n = scores_t.shape
    assert t & (t - 1) == 0 and TQ_S % t == 0, "token index is taken from the low bits of the query index"
    blk = pl.BlockSpec((lk, TQ_S), lambda i: (0, i))
    return pl.pallas_call(
        functools.partial(_sample_select_kernel, past=past, t=t, topk=topk),
        grid=(n // TQ_S,),
        in_specs=[blk],
        out_specs=blk,
        out_shape=jax.ShapeDtypeStruct((lk, n), F32),
        scratch_shapes=[pltpu.VMEM((lk, TQ_S), F32)],
        compiler_params=_params(("parallel",)),
        name="sample_select",
    )(scores_t)


def _sample_attn_kernel(pt_ref, q_ref, bias_ref, *refs, n_pages):
    del pt_ref
    group = q_ref.shape[0]
    n_pg = group * n_pages
    k_pages, k_new, v_pages, v_new, o_ref = refs[:n_pg], refs[n_pg], refs[n_pg + 1:2 * n_pg + 1], refs[-2], refs[-1]
    kt = lambda page: page.reshape(KV_W, PAGE_SIZE).astype(BF16)
    for g in range(group):
        qb = q_ref[g]
        reps = qb.shape[0] // bias_ref.shape[1]
        bias = jnp.concatenate([bias_ref[g]] * reps, axis=0)
        ks = [p[...] for p in k_pages[g * n_pages:(g + 1) * n_pages]] + [k_new[g]]
        vs = [p[...] for p in v_pages[g * n_pages:(g + 1) * n_pages]] + [v_new[g]]
        lg = jnp.concatenate([_dot(qb, kt(kp)) for kp in ks], axis=-1) + bias
        m = jnp.max(lg, axis=-1, keepdims=True)
        p = jnp.exp2(lg - m)
        l = jnp.sum(p, axis=-1, keepdims=True)
        pb = p.astype(BF16)
        acc = jnp.zeros(o_ref.shape[1:], F32)
        for j, vp in enumerate(vs):
            acc = acc + _dot_nt(pb[:, j * PAGE_SIZE:(j + 1) * PAGE_SIZE], kt(vp))
        o_ref[g] = acc / l


def _sample_attend(page_table_flat, q_bd, bias, k_t, v_t, k_new_t, v_new_t, n_pages):
    nb, rows, _ = q_bd.shape
    t = bias.shape[1]
    lk = bias.shape[2]
    g = ATTN_GROUP
    page = (N_KV_HEADS, HEAD_DIM, PAGE_SIZE)
    return pl.pallas_call(
        functools.partial(_sample_attn_kernel, n_pages=n_pages),
        grid_spec=pltpu.PrefetchScalarGridSpec(
            num_scalar_prefetch=1,
            grid=(nb // g,),
            in_specs=[_group_spec(g, rows, KV_W), _group_spec(g, t, lk)]
            + _page_specs(n_pages, page, g) + [_group_spec(g, *page)]
            + _page_specs(n_pages, page, g) + [_group_spec(g, *page)],
            out_specs=_group_spec(g, rows, KV_W),
        ),
        out_shape=jax.ShapeDtypeStruct((nb, rows, KV_W), F32),
        compiler_params=_params(("parallel",)),
        name="sample_attn",
    )(page_table_flat, q_bd, bias, *([k_t] * (g * n_pages)), k_new_t, *([v_t] * (g * n_pages)), v_new_t)


def _glu(xb, w_glu_ref):
    gi = _dot(xb, w_glu_ref[...])
    return gi[:, :C_CONV] * jax.nn.sigmoid(gi[:, C_CONV:])


def _mix_tail(x, xb, attn_b, c, w_g_ref, w_ao_ref, lncg_ref, lncb_ref, w_co_ref, w_out_ref, ln1g_ref, ln1b_ref):
    cn = _layer_norm(c, lncg_ref[...], lncb_ref[...])
    c_branch = _dot((cn * jax.nn.sigmoid(cn)).astype(BF16), w_co_ref[...])
    a_branch = _dot(attn_b, w_ao_ref[...])
    g = _dot(xb, w_g_ref[...])
    merged = jax.nn.sigmoid(g[:, :D_MODEL]) * a_branch + jax.nn.sigmoid(g[:, D_MODEL:]) * c_branch
    h = ALPHA * x + _dot(merged.astype(BF16), w_out_ref[...])
    return _layer_norm(h, ln1g_ref[...], ln1b_ref[...])


def _mix_prompt_kernel(x_ref, attn_ref, w_glu_ref, w_g_ref, w_ao_ref, w_dw_ref, b_dw_ref, lncg_ref, lncb_ref,
                       w_co_ref, w_out_ref, ln1g_ref, ln1b_ref, x1_ref, tail_ref, xp_ref):
    tm = x_ref.shape[0]

    @pl.when(pl.program_id(1) == 0)
    def _():
        xp_ref[0:CONV_HALO, :] = jnp.zeros((CONV_HALO, C_CONV), F32)

    x = x_ref[...]
    xb = x.astype(BF16)
    glu = _glu(xb, w_glu_ref)
    xp_ref[CONV_HALO:CONV_HALO + tm, :] = glu
    tail_ref[...] = glu[tm - CONV_HALO:, :]

    first = CONV_HALO - (CONV_WIDTH - 1)
    rb = 128
    cols = []
    for c0 in range(0, C_CONV, LANES):
        lanes = slice(c0, c0 + LANES)
        blocks = []
        for r0 in range(0, tm, rb):
            y = jnp.broadcast_to(b_dw_ref[:, lanes], (rb, LANES))
            for b in range(SUBLANES):
                rows = rb + (SUBLANES if b else 0)
                part = None
                for j in range(CONV_WIDTH):
                    if (first + j) % SUBLANES == b:
                        a0 = first + j - b + r0
                        term = w_dw_ref[j:j + 1, lanes] * xp_ref[a0:a0 + rows, lanes]
                        part = term if part is None else part + term
                y = y + part[b:b + rb, :]
            blocks.append(y)
        cols.append(jnp.concatenate(blocks, axis=0))
    c = jnp.concatenate(cols, axis=-1)
    xp_ref[0:CONV_HALO, :] = xp_ref[tm:tm + CONV_HALO, :]

    x1_ref[...] = _mix_tail(x, xb, attn_ref[...], c, w_g_ref, w_ao_ref, lncg_ref, lncb_ref, w_co_ref,
                            w_out_ref, ln1g_ref, ln1b_ref)


def _mix_sample_kernel(x_ref, attn_ref, hist_ref, w_glu_ref, w_g_ref, w_ao_ref, w_dw_ref, b_dw_ref, lncg_ref,
                       lncb_ref, w_co_ref, w_out_ref, ln1g_ref, ln1b_ref, x1_ref, glu_ref):
    nb = hist_ref.shape[1]
    t = x_ref.shape[0] // nb
    n_hist = CONV_WIDTH - 1
    x = x_ref[...]
    xb = x.astype(BF16)
    glu = _glu(xb, w_glu_ref)
    glu_ref[...] = glu

    def slab(m):
        return hist_ref[m] if m < n_hist else glu[(m - n_hist) * nb:(m - n_hist + 1) * nb, :]

    outs = []
    for ti in range(t):
        acc = jnp.broadcast_to(b_dw_ref[...], (nb, C_CONV))
        for j in range(CONV_WIDTH):
            acc = acc + w_dw_ref[j:j + 1, :] * slab(ti + j)
        outs.append(acc)
    c = jnp.concatenate(outs, axis=0)
    x1_ref[...] = _mix_tail(x, xb, attn_ref[...], c, w_g_ref, w_ao_ref, lncg_ref, lncb_ref, w_co_ref,
                            w_out_ref, ln1g_ref, ln1b_ref)


def _mix_weight_specs():
    return [
        _const_spec((D_MODEL, 2 * C_CONV)), _const_spec((D_MODEL, 2 * D_MODEL)), _const_spec((Q_W, D_MODEL)),
        _const_spec((CONV_WIDTH, C_CONV)), _const_spec((1, C_CONV)), _const_spec((1, C_CONV)),
        _const_spec((1, C_CONV)), _const_spec((C_CONV, D_MODEL)), _const_spec((D_MODEL, D_MODEL)),
        _const_spec((1, D_MODEL)), _const_spec((1, D_MODEL)),
    ]


def _mix_prompt(x, attn, mix_w, tm):
    b, t, _ = x.shape
    rows = lambda w: pl.BlockSpec((None, tm, w), lambda bi, i: (bi, i, 0))
    return pl.pallas_call(
        _mix_prompt_kernel,
        grid=(b, t // tm),
        in_specs=[rows(D_MODEL), rows(Q_W)] + _mix_weight_specs(),
        out_specs=[rows(D_MODEL), pl.BlockSpec((None, CONV_HALO, C_CONV), lambda bi, i: (bi, 0, 0))],
        out_shape=[jax.ShapeDtypeStruct((b, t, D_MODEL), F32), jax.ShapeDtypeStruct((b, CONV_HALO, C_CONV), F32)],
        scratch_shapes=[pltpu.VMEM((CONV_HALO + tm, C_CONV), F32)],
        compiler_params=_params(("parallel", "arbitrary")),
        name="mix_prompt",
    )(x, attn, *mix_w)


def _mix_sample(x_tm, attn_tm, hist_tm, mix_w):
    n = x_tm.shape[0]
    return pl.pallas_call(
        _mix_sample_kernel,
        grid=(1,),
        in_specs=[_const_spec((n, D_MODEL)), _const_spec((n, Q_W)), _const_spec(hist_tm.shape)] + _mix_weight_specs(),
        out_specs=[_whole_spec((n, D_MODEL)), _whole_spec((n, C_CONV))],
        out_shape=[jax.ShapeDtypeStruct((n, D_MODEL), F32), jax.ShapeDtypeStruct((n, C_CONV), F32)],
        compiler_params=_params(("arbitrary",)),
        name="mix_sample",
    )(x_tm, attn_tm, hist_tm, *mix_w)


def _ffn_tail(x1, uc, gate, w_down_ref, ln2g_ref, ln2b_ref):
    f = _dot((jax.nn.gelu(uc) * gate).astype(BF16), w_down_ref[...])
    return _layer_norm(ALPHA * x1 + f, ln2g_ref[...], ln2b_ref[...])


def _ffn_prompt_kernel(x1_ref, w_up_ref, w_gate_ref, w_dw_ref, b_dw_ref, w_down_ref, ln2g_ref, ln2b_ref,
                       y_ref, tail_ref, up_ref):
    tm = x1_ref.shape[0]

    @pl.when(pl.program_id(1) == 0)
    def _():
        up_ref[0:FFN_HALO, :] = jnp.zeros((FFN_HALO, D_FF), F32)

    x1 = x1_ref[...]
    x1b = x1.astype(BF16)
    u = _dot(x1b, w_up_ref[...])
    gate = _dot(x1b, w_gate_ref[...])
    up_ref[FFN_HALO:FFN_HALO + tm, :] = u
    tail_ref[...] = u[tm - FFN_HALO:, :]
    first = FFN_HALO - (FFN_CONV_WIDTH - 1)
    uc = b_dw_ref[...]
    for j in range(FFN_CONV_WIDTH):
        uc = uc + w_dw_ref[j:j + 1, :] * up_ref[first + j:first + j + tm, :]
    up_ref[0:FFN_HALO, :] = up_ref[tm:tm + FFN_HALO, :]
    y_ref[...] = _ffn_tail(x1, uc, gate, w_down_ref, ln2g_ref, ln2b_ref)


def _ffn_sample_kernel(x1_ref, hist_ref, w_up_ref, w_gate_ref, w_dw_ref, b_dw_ref, w_down_ref, ln2g_ref, ln2b_ref,
                       y_ref, u_ref):
    nb = hist_ref.shape[1]
    t = x1_ref.shape[0] // nb
    n_hist = FFN_CONV_WIDTH - 1
    x1 = x1_ref[...]
    x1b = x1.astype(BF16)
    u = _dot(x1b, w_up_ref[...])
    gate = _dot(x1b, w_gate_ref[...])
    u_ref[...] = u

    def slab(m):
        return hist_ref[m] if m < n_hist else u[(m - n_hist) * nb:(m - n_hist + 1) * nb, :]

    outs = []
    for ti in range(t):
        acc = jnp.broadcast_to(b_dw_ref[...], (nb, D_FF))
        for j in range(FFN_CONV_WIDTH):
            acc = acc + w_dw_ref[j:j + 1, :] * slab(ti + j)
        outs.append(acc)
    uc = jnp.concatenate(outs, axis=0)
    y_ref[...] = _ffn_tail(x1, uc, gate, w_down_ref, ln2g_ref, ln2b_ref)


def _ffn_weight_specs():
    return [
        _const_spec((D_MODEL, D_FF)), _const_spec((D_MODEL, D_FF)), _const_spec((FFN_CONV_WIDTH, D_FF)),
        _const_spec((1, D_FF)), _const_spec((D_FF, D_MODEL)), _const_spec((1, D_MODEL)), _const_spec((1, D_MODEL)),
    ]


def _ffn_prompt(x1, ffn_w, tm):
    b, t, _ = x1.shape
    rows = pl.BlockSpec((None, tm, D_MODEL), lambda bi, i: (bi, i, 0))
    return pl.pallas_call(
        _ffn_prompt_kernel,
        grid=(b, t // tm),
        in_specs=[rows] + _ffn_weight_specs(),
        out_specs=[rows, pl.BlockSpec((None, FFN_HALO, D_FF), lambda bi, i: (bi, 0, 0))],
        out_shape=[jax.ShapeDtypeStruct((b, t, D_MODEL), F32), jax.ShapeDtypeStruct((b, FFN_HALO, D_FF), F32)],
        scratch_shapes=[pltpu.VMEM((FFN_HALO + tm, D_FF), F32)],
        compiler_params=_params(("parallel", "arbitrary")),
        name="ffn_prompt",
    )(x1, *ffn_w)


def _ffn_sample(x1_tm, hist_tm, ffn_w):
    n = x1_tm.shape[0]
    return pl.pallas_call(
        _ffn_sample_kernel,
        grid=(1,),
        in_specs=[_const_spec((n, D_MODEL)), _const_spec(hist_tm.shape)] + _ffn_weight_specs(),
        out_specs=[_whole_spec((n, D_MODEL)), _whole_spec((n, D_FF))],
        out_shape=[jax.ShapeDtypeStruct((n, D_MODEL), F32), jax.ShapeDtypeStruct((n, D_FF), F32)],
        compiler_params=_params(("arbitrary",)),
        name="ffn_sample",
    )(x1_tm, hist_tm, *ffn_w)


def _token_major(a):
    b, t, w = a.shape
    return jnp.transpose(a, (1, 0, 2)).reshape(t * b, w)


def _batch_major(a, b):
    tb, w = a.shape
    return jnp.transpose(a.reshape(tb // b, b, w), (1, 0, 2))


def _stage_weights(w_in):
    o = 0
    w_q = w_in[:, o:o + Q_W]; o += Q_W
    w_k = w_in[:, o:o + KV_W]; o += KV_W
    w_v = w_in[:, o:o + KV_W]; o += KV_W
    w_qi = w_in[:, o:o + IQ_W]; o += IQ_W
    w_ki = w_in[:, o:o + IDX_DIM]; o += IDX_DIM
    w_wi = w_in[:, o:o + N_IDX_HEADS]; o += N_IDX_HEADS
    head = lambda h: w_q[:, h * HEAD_DIM:(h + 1) * HEAD_DIM]
    q_pairs = [w for i in range(GROUP) for w in (head(i), head(GROUP + i))]
    zero = jnp.zeros((D_MODEL, LANES - IDX_DIM), w_in.dtype)
    w_att = jnp.concatenate(q_pairs + [w_k, w_qi, w_ki, zero], axis=1).astype(BF16)
    w_t = jnp.concatenate([w_k, w_v, w_ki, w_wi], axis=1).T.astype(BF16)
    return w_att, w_t, o


def kernel(x_prompt, x_sample, cache_k, cache_v, cache_kidx, state_conv, state_ffn, page_table, w_in, w_attn_o, w_conv_dw, b_conv_dw, ln_conv_g, ln_conv_b, w_conv_o, w_out, ln1_g, ln1_b, w_ffn_up, w_ffn_gate, w_ffn_dw, b_ffn_dw, w_ffn_down, ln2_g, ln2_b):
    bp, tp, _ = x_prompt.shape
    bs, ts, _ = x_sample.shape
    n_pages = page_table.shape[1]
    past = n_pages * PAGE_SIZE
    assert N_KV_HEADS == 2 and N_IDX_HEADS == SUBLANES

    w_att, w_t, o = _stage_weights(w_in)
    w_glu = w_in[:, o:o + 2 * C_CONV].astype(BF16); o += 2 * C_CONV
    w_g = w_in[:, o:o + 2 * D_MODEL].astype(BF16)
    row2 = lambda a: a.reshape(1, -1)
    mix_w = (w_glu, w_g, w_attn_o.astype(BF16), w_conv_dw, row2(b_conv_dw), row2(ln_conv_g), row2(ln_conv_b),
             w_conv_o.astype(BF16), w_out.astype(BF16), row2(ln1_g), row2(ln1_b))
    ffn_w = (w_ffn_up.astype(BF16), w_ffn_gate.astype(BF16), w_ffn_dw, row2(b_ffn_dw), w_ffn_down.astype(BF16),
             row2(ln2_g), row2(ln2_b))

    qbd, qi, kb, kib, vt3, k_t, v_t, ki_t, wit = _proj(x_prompt, w_att, w_t, 512)
    seq = lambda a: a.reshape(bp, tp, a.shape[-1])
    attn_p = _prompt_attention(seq(qbd), seq(qi), wit, seq(kib), seq(kb), vt3, min(TOPK_MAX, tp // 4))
    x1_p, conv_tail = _mix_prompt(x_prompt, attn_p, mix_w, 256)
    y_p, ffn_tail = _ffn_prompt(x1_p, ffn_w, 256)
    heads_last = lambda a_t: jnp.transpose(a_t.reshape(bp, N_KV_HEADS, HEAD_DIM, tp), (0, 3, 1, 2))
    k_p = heads_last(k_t)
    v_p = heads_last(v_t)
    ki_p = jnp.transpose(ki_t, (0, 2, 1))
    conv_p = conv_tail[:, CONV_HALO - (CONV_WIDTH - 1):, :]
    ffn_p = ffn_tail[:, FFN_HALO - (FFN_CONV_WIDTH - 1):, :]

    n_s = bs * ts
    qbd_s, qis, _, _, _, ks_t, vs_t, kis_t, wit_s = _proj(x_sample.reshape(1, n_s, D_MODEL), w_att, w_t, n_s)
    ks_t, vs_t, kis_t = ks_t[0], vs_t[0], kis_t[0]
    pt_flat = page_table.reshape(-1).astype(jnp.int32)
    new_page = lambda a_t: jnp.pad(jnp.transpose(a_t.reshape(-1, bs, ts), (1, 0, 2)),
                                   ((0, 0), (0, 0), (0, PAGE_SIZE - ts)))
    qi32 = qis.reshape(bs, ts * N_IDX_HEADS, IDX_DIM)
    w32 = wit_s.T.reshape(bs, ts * N_IDX_HEADS, 1)
    scores = _sample_scores(pt_flat, qi32, w32, jnp.transpose(cache_kidx, (0, 2, 1)), new_page(kis_t), n_pages)
    lk = scores.shape[-1]
    bias_t = _sample_select(scores.reshape(n_s, lk).T, past, ts, min(TOPK_MAX, (past + ts) // 4))
    q_bd = jnp.transpose(qbd_s.reshape(bs, ts, N_HEADS, KV_W), (0, 2, 1, 3)).reshape(bs, N_HEADS * ts, KV_W)
    as_pages = lambda a: jnp.transpose(a, (0, 2, 3, 1))
    kv_new = lambda a: new_page(a).reshape(bs, N_KV_HEADS, HEAD_DIM, PAGE_SIZE)
    o_bd = _sample_attend(pt_flat, q_bd, bias_t.T.reshape(bs, ts, lk), as_pages(cache_k), as_pages(cache_v),
                          kv_new(ks_t), kv_new(vs_t), n_pages)
    o6 = o_bd.reshape(bs, N_KV_HEADS, GROUP, ts, N_KV_HEADS, HEAD_DIM)
    o_sel = jnp.stack([o6[:, n, :, :, n, :] for n in range(N_KV_HEADS)], axis=1)
    attn_s = jnp.transpose(o_sel, (0, 3, 1, 2, 4)).reshape(bs, ts, Q_W)

    x1_s, glu_s = _mix_sample(_token_major(x_sample), _token_major(attn_s).astype(BF16),
                              jnp.transpose(state_conv, (1, 0, 2)), mix_w)
    y_s, u_s = _ffn_sample(x1_s, jnp.transpose(state_ffn, (1, 0, 2)), ffn_w)
    y_s = _batch_major(y_s, bs)
    conv_s = jnp.concatenate([state_conv, _batch_major(glu_s, bs)], axis=1)[:, -(CONV_WIDTH - 1):, :]
    ffn_s = jnp.concatenate([state_ffn, _batch_major(u_s, bs)], axis=1)[:, -(FFN_CONV_WIDTH - 1):, :]
    k_s = ks_t.T.reshape(bs, ts, N_KV_HEADS, HEAD_DIM)
    v_s = vs_t.T.reshape(bs, ts, N_KV_HEADS, HEAD_DIM)
    ki_s = kis_t.T.reshape(bs, ts, IDX_DIM)

    return (y_p, y_s, k_p, v_p, ki_p, conv_p, ffn_p, k_s, v_s, ki_s, conv_s, ffn_s)
```

```python
import functools

import jax
import jax.numpy as jnp
from jax import lax
from jax.experimental import pallas as pl
from jax.experimental.pallas import tpu as pltpu

D_MODEL = 1024
N_HEADS = 8
HEAD_DIM = 64
N_KV_HEADS = 2
N_IDX_HEADS = 8
IDX_DIM = 64
TOPK_MAX = 256
C_CONV = D_MODEL // 2
CONV_WIDTH = 31
D_FF = 2816
FFN_CONV_WIDTH = 3
LN_EPS = 1e-5
DEPTH = 1
ALPHA = (2.0 * DEPTH) ** 0.25
PAGE_SIZE = 128

Q_W = N_HEADS * HEAD_DIM
KV_W = N_KV_HEADS * HEAD_DIM
IQ_W = N_IDX_HEADS * IDX_DIM
GROUP = N_HEADS // N_KV_HEADS

LANES = 128
SUBLANES = 8
QBD_W = N_HEADS * KV_W
PROJ_W = Q_W + KV_W + IQ_W + LANES
PROJ_T = 2 * KV_W + IDX_DIM + N_IDX_HEADS
TQ = 256
TQ_S = 256
SCORE_GROUP = 8
ATTN_GROUP = 4
CONV_HALO = 32
FFN_HALO = 8
N_BISECT = 19
VT_ROWS = HEAD_DIM + 16
LOG2E = 1.4426950408889634
VMEM_LIMIT = 56 * 1024 * 1024

F32 = jnp.float32
BF16 = jnp.bfloat16
NEG_INF = float("-inf")
POS_INF = float("inf")


def _dot(a, b):
    return jnp.dot(a, b, preferred_element_type=F32)


def _dot_nt(a, b):
    return lax.dot_general(a, b, (((1,), (1,)), ((), ())), preferred_element_type=F32)


def _layer_norm(x, g, b):
    mu = jnp.mean(x, axis=-1, keepdims=True)
    xc = x - mu
    var = jnp.mean(xc * xc, axis=-1, keepdims=True)
    return xc * lax.rsqrt(var + LN_EPS) * g + b


def _params(sem):
    return pltpu.CompilerParams(dimension_semantics=sem, vmem_limit_bytes=VMEM_LIMIT)


def _whole_spec(shape):
    nd = len(shape)
    return pl.BlockSpec(shape, lambda *_: (0,) * nd)


def _const_spec(shape):
    nd = len(shape)
    return pl.BlockSpec(shape, lambda *_: (0,) * nd, pipeline_mode=pl.Buffered(1))


def _proj_kernel(x_ref, w_ref, wt_ref, qbd_ref, qi_ref, kb_ref, kib_ref, vt_ref, kt_ref, vtf_ref, kit_ref, wit_ref):
    xb = x_ref[...].astype(BF16)
    o = 0
    qp = _dot(xb, w_ref[:, o:o + Q_W]); o += Q_W
    k = _dot(xb, w_ref[:, o:o + KV_W]); o += KV_W
    qi = _dot(xb, w_ref[:, o:o + IQ_W]); o += IQ_W
    ki = _dot(xb, w_ref[:, o:o + LANES])[:, :IDX_DIM]
    qp = (qp * (HEAD_DIM ** -0.5 * LOG2E)).astype(BF16)
    low = lax.broadcasted_iota(jnp.int32, (qp.shape[0], KV_W), 1) < HEAD_DIM
    for h in range(N_HEADS):
        pair = qp[:, (h % GROUP) * KV_W:(h % GROUP + 1) * KV_W]
        qbd_ref[:, h * KV_W:(h + 1) * KV_W] = jnp.where(low if h < GROUP else jnp.logical_not(low), pair, 0.0)
    qi_ref[...] = qi.astype(BF16)
    kb_ref[...] = k.astype(BF16)
    kib_ref[...] = ki.astype(BF16)
    t = _dot_nt(wt_ref[...], xb)
    kt_ref[...] = t[:KV_W, :]
    vtf_ref[...] = t[KV_W:2 * KV_W, :]
    kit_ref[...] = t[2 * KV_W:2 * KV_W + IDX_DIM, :]
    wit_ref[...] = t[2 * KV_W + IDX_DIM:, :] * ((IDX_DIM ** -0.5) * (N_IDX_HEADS ** -0.5))
    vt = t[KV_W:2 * KV_W, :].astype(BF16)
    ones = jnp.ones((VT_ROWS - HEAD_DIM, TQ), BF16)
    for j in range(vt_ref.shape[0]):
        for n in range(N_KV_HEADS):
            vt_ref[j, n * VT_ROWS:n * VT_ROWS + HEAD_DIM, :] = vt[n * HEAD_DIM:(n + 1) * HEAD_DIM, j * TQ:(j + 1) * TQ]
            vt_ref[j, n * VT_ROWS + HEAD_DIM:(n + 1) * VT_ROWS, :] = ones


def _proj(x, w_att, w_t, tm):
    b, t, _ = x.shape
    n = b * t
    tiles = t // tm
    row = lambda w: pl.BlockSpec((tm, w), lambda i: (i, 0))
    col = lambda r: pl.BlockSpec((None, r, tm), lambda i: (i // tiles, 0, i % tiles))
    outs = [(QBD_W, BF16), (IQ_W, BF16), (KV_W, BF16), (IDX_DIM, BF16)]
    return pl.pallas_call(
        _proj_kernel,
        grid=(n // tm,),
        in_specs=[row(D_MODEL), _const_spec((D_MODEL, PROJ_W)), _const_spec((PROJ_T, D_MODEL))],
        out_specs=[row(w) for w, _ in outs]
        + [pl.BlockSpec((tm // TQ, N_KV_HEADS * VT_ROWS, TQ), lambda i: (i, 0, 0)), col(KV_W), col(KV_W), col(IDX_DIM),
           pl.BlockSpec((SUBLANES, tm), lambda i: (0, i))],
        out_shape=[jax.ShapeDtypeStruct((n, w), dt) for w, dt in outs]
        + [jax.ShapeDtypeStruct((n // TQ, N_KV_HEADS * VT_ROWS, TQ), BF16), jax.ShapeDtypeStruct((b, KV_W, t), F32),
           jax.ShapeDtypeStruct((b, KV_W, t), F32), jax.ShapeDtypeStruct((b, IDX_DIM, t), F32),
           jax.ShapeDtypeStruct((SUBLANES, n), F32)],
        compiler_params=_params(("parallel",)),
        name="proj",
    )(x.reshape(n, D_MODEL), w_att, w_t)


def _chunk(ref, c, ch):
    return ref[pl.ds(pl.multiple_of(c * ch, ch), ch), :]


def _fold_rows(x, op):
    parts = [x[r:r + SUBLANES, :] for r in range(0, x.shape[0], SUBLANES)]
    accs = parts[:4]
    for i, part in enumerate(parts[4:]):
        accs[i % 4] = op(accs[i % 4], part)
    while len(accs) > 1:
        accs = [op(accs[a], accs[a + 1]) for a in range(0, len(accs) - 1, 2)] + ([accs[-1]] if len(accs) % 2 else [])
    return accs[0]


def _reduce_keys(s_ref, nk, ch, init, f, op, red):
    w = s_ref.shape[1]

    def body(c, acc):
        return op(acc, _fold_rows(f(_chunk(s_ref, c, ch), c), op))

    acc = lax.fori_loop(0, nk, body, jnp.full((SUBLANES, w), init, F32))
    return red(acc, axis=0, keepdims=True)


def _select_bias(s_ref, bias_ref, nk, ch, topk):
    w = s_ref.shape[1]
    kf = float(topk)
    count = lambda pred: _reduce_keys(s_ref, nk, ch, 0.0, lambda x, c: jnp.where(pred(x, c), 1.0, 0.0),
                                      jnp.add, jnp.sum)
    min_above = lambda t: _reduce_keys(s_ref, nk, ch, POS_INF, lambda x, c: jnp.where(x > t, x, POS_INF),
                                       jnp.minimum, jnp.min)

    neg = jnp.full((1, w), NEG_INF, F32)

    hi = _reduce_keys(s_ref, nk, ch, NEG_INF, lambda x, c: x, jnp.maximum, jnp.max)
    lo_fin = min_above(neg)
    n_adm = count(lambda x, _: x > neg)

    def bisect(_, st):
        lo, lo_fin, hi, n_lo = st
        mid = 0.5 * lo_fin + 0.5 * hi
        c = count(lambda x, _: x > mid)
        ok = c >= kf
        return jnp.where(ok, mid, lo), jnp.where(ok, mid, lo_fin), jnp.where(ok, hi, mid), jnp.where(ok, c, n_lo)

    lo, _, _, n_lo = lax.fori_loop(0, N_BISECT, bisect, (neg, lo_fin, hi, n_adm))

    def peel(st):
        lo, thr, n_gt, done = st
        v = min_above(lo)
        c = count(lambda x, _: x > v)
        found = c < kf
        newly = jnp.logical_and(done < 0.5, found)
        return (jnp.where(jnp.logical_or(found, done > 0.5), lo, v), jnp.where(newly, v, thr),
                jnp.where(newly, c, n_gt), jnp.where(found, 1.0, done))

    _, thr, n_gt, _ = lax.while_loop(lambda st: jnp.min(st[3]) < 0.5, peel,
                                     (lo, lo, n_lo, jnp.where(n_lo <= kf, 1.0, 0.0)))

    need = jnp.where(thr == neg, 0.0, kf - n_gt)
    tri = (lax.broadcasted_iota(jnp.int32, (ch, ch), 1) <= lax.broadcasted_iota(jnp.int32, (ch, ch), 0)).astype(BF16)

    def write(c, seen):
        x = _chunk(s_ref, c, ch)
        eq = x == thr
        rank = seen + _dot(tri, jnp.where(eq, 1.0, 0.0).astype(BF16))
        sel = jnp.logical_or(x > thr, jnp.logical_and(eq, rank <= need))
        bias_ref[pl.ds(pl.multiple_of(c * ch, ch), ch), :] = jnp.where(sel, 0.0, NEG_INF)
        return rank[ch - 1:ch, :]

    lax.fori_loop(0, nk, write, jnp.zeros((1, w), F32))


def _prompt_attn_kernel(qbd_ref, qi_ref, wit_ref, kib_ref, kb_ref, vt_ref, o_ref, s_ref, bias_ref, m_ref, acc_ref,
                        lg_ref, *, topk):
    i = pl.program_id(1)
    nk = i + 1
    qi = qi_ref[...]
    wit = wit_ref[...]
    q_pos = lax.broadcasted_iota(jnp.int32, (TQ, TQ), 1) + i * TQ
    k_off = lax.broadcasted_iota(jnp.int32, (TQ, TQ), 0)

    def score(c, carry):
        kc = _chunk(kib_ref, c, TQ)
        s = jnp.zeros((TQ, TQ), F32)
        for h in range(N_IDX_HEADS):
            z = _dot_nt(kc, qi[:, h * IDX_DIM:(h + 1) * IDX_DIM])
            s = s + jnp.maximum(z, 0.0) * wit[h:h + 1, :]
        s_ref[pl.ds(pl.multiple_of(c * TQ, TQ), TQ), :] = jnp.where(k_off + c * TQ <= q_pos, s, NEG_INF)
        return carry

    lax.fori_loop(0, nk, score, 0)
    _select_bias(s_ref, bias_ref, nk, TQ, topk)

    m_ref[...] = jnp.full(m_ref.shape, NEG_INF, F32)
    acc_ref[...] = jnp.zeros(acc_ref.shape, F32)

    def attend(c, carry):
        rows = pl.ds(pl.multiple_of(c * TQ, TQ), TQ)
        mx = []
        for h in range(N_HEADS):
            lg = _dot_nt(kb_ref[rows, :], qbd_ref[:, h * KV_W:(h + 1) * KV_W]) + bias_ref[rows, :]
            lg_ref[h] = lg
            mx.append(_fold_rows(lg, jnp.maximum))
        m_old = m_ref[...]
        m_new = jnp.maximum(m_old, jnp.concatenate([jnp.max(x, axis=0, keepdims=True) for x in mx], axis=0))
        m_use = jnp.where(m_new == NEG_INF, 0.0, m_new)
        alpha = jnp.exp2(m_old - m_use)
        m_ref[...] = m_new
        for h in range(N_HEADS):
            n = h // GROUP
            out = slice(h * VT_ROWS, (h + 1) * VT_ROWS)
            p = jnp.exp2(lg_ref[h] - m_use[h:h + 1, :]).astype(BF16)
            vt = vt_ref[c, pl.ds(n * VT_ROWS, VT_ROWS), :]
            acc_ref[out, :] = alpha[h:h + 1, :] * acc_ref[out, :] + _dot(vt, p)
        return carry

    lax.fori_loop(0, nk, attend, 0)
    outs = [acc_ref[h * VT_ROWS:h * VT_ROWS + HEAD_DIM, :] / acc_ref[h * VT_ROWS + HEAD_DIM:h * VT_ROWS + HEAD_DIM + 1, :]
            for h in range(N_HEADS)]
    o_ref[...] = jnp.concatenate(outs, axis=0).T.astype(o_ref.dtype)


def _prompt_attention(qbd, qi, wit, kib, kb, vt3, topk):
    b, t, _ = qbd.shape
    nblk = t // TQ
    qblk = lambda w: pl.BlockSpec((None, TQ, w), lambda bi, i: (bi, i, 0))
    full = lambda w: pl.BlockSpec((None, t, w), lambda bi, i: (bi, 0, 0))
    return pl.pallas_call(
        functools.partial(_prompt_attn_kernel, topk=topk),
        grid=(b, nblk),
        in_specs=[qblk(QBD_W), qblk(IQ_W), pl.BlockSpec((SUBLANES, TQ), lambda bi, i: (0, bi * nblk + i)),
                  full(IDX_DIM), full(KV_W), pl.BlockSpec((nblk, N_KV_HEADS * VT_ROWS, TQ), lambda bi, i: (bi, 0, 0))],
        out_specs=qblk(Q_W),
        out_shape=jax.ShapeDtypeStruct((b, t, Q_W), BF16),
        scratch_shapes=[pltpu.VMEM((t, TQ), F32), pltpu.VMEM((t, TQ), F32), pltpu.VMEM((N_HEADS, TQ), F32),
                        pltpu.VMEM((N_HEADS * VT_ROWS, TQ), F32), pltpu.VMEM((N_HEADS, TQ, TQ), F32)],
        compiler_params=_params(("parallel", "arbitrary")),
        name="prompt_attn",
    )(qbd, qi, wit, kib, kb, vt3)


def _group_spec(group, *block):
    nd = len(block)
    return pl.BlockSpec((group,) + block, lambda bi, pt: (bi,) + (0,) * nd)


def _sample_score_kernel(pt_ref, qi_ref, w_ref, new_ref, kidx_hbm, o_ref, buf, sem, *, n_pages):
    group = qi_ref.shape[0]
    slot = _prefetch_pages(*_page_fetcher(pt_ref, (kidx_hbm,), (buf,), sem, group * n_pages))
    t = qi_ref.shape[1] // N_IDX_HEADS
    for g in range(group):
        qi = qi_ref[g]
        w = w_ref[g]
        seq_pages = [buf[slot, g * n_pages + j] for j in range(n_pages)] + [new_ref[g]]
        for j, page in enumerate(seq_pages):
            s = _dot(qi, page.astype(BF16))
            s = jnp.maximum(s, 0.0) * w
            o_ref[g, :, j * PAGE_SIZE:(j + 1) * PAGE_SIZE] = jnp.sum(s.reshape(t, N_IDX_HEADS, PAGE_SIZE), axis=1)


def _sample_scores(page_table_flat, qi32, w32, kidx_t, ki_new_t, n_pages):
    nb, rows, _ = qi32.shape
    t = rows // N_IDX_HEADS
    lk = (n_pages + 1) * PAGE_SIZE
    g = SCORE_GROUP
    return pl.pallas_call(
        functools.partial(_sample_score_kernel, n_pages=n_pages),
        grid_spec=pltpu.PrefetchScalarGridSpec(
            num_scalar_prefetch=1,
            grid=(nb // g,),
            in_specs=[_group_spec(g, rows, IDX_DIM), _group_spec(g, rows, 1), _group_spec(g, IDX_DIM, PAGE_SIZE),
                      pl.BlockSpec(memory_space=pl.ANY)],
            out_specs=_group_spec(g, t, lk),
            scratch_shapes=[pltpu.VMEM((2, g * n_pages, IDX_DIM, PAGE_SIZE), F32), pltpu.SemaphoreType.DMA((1, 2))],
        ),
        out_shape=jax.ShapeDtypeStruct((nb, t, lk), F32),
        compiler_params=_params(("arbitrary",)),
        name="sample_scores",
    )(page_table_flat, qi32, w32, ki_new_t, kidx_t)


def _sample_select_kernel(s_ref, bias_ref, sm_ref, *, past, t, topk):
    lk = s_ref.shape[0]
    nk = lk // PAGE_SIZE
    qcol = lax.broadcasted_iota(jnp.int32, (PAGE_SIZE, TQ_S), 1)
    qpos = past + (qcol & (t - 1))
    krow = lax.broadcasted_iota(jnp.int32, (PAGE_SIZE, TQ_S), 0)
    for c in range(nk):
        rows = slice(c * PAGE_SIZE, (c + 1) * PAGE_SIZE)
        sm_ref[rows, :] = jnp.where(krow + c * PAGE_SIZE <= qpos, s_ref[rows, :], NEG_INF)
    _select_bias(sm_ref, bias_ref, nk, PAGE_SIZE, topk)


def _sample_select(scores_t, past, t, topk):
    lk, n = scores_t.shape
    assert t & (t - 1) == 0 and TQ_S % t == 0, "token index is taken from the low bits of the query index"
    blk = pl.BlockSpec((lk, TQ_S), lambda i: (0, i))
    return pl.pallas_call(
        functools.partial(_sample_select_kernel, past=past, t=t, topk=topk),
        grid=(n // TQ_S,),
        in_specs=[blk],
        out_specs=blk,
        out_shape=jax.ShapeDtypeStruct((lk, n), F32),
        scratch_shapes=[pltpu.VMEM((lk, TQ_S), F32)],
        compiler_params=_params(("parallel",)),
        name="sample_select",
    )(scores_t)


def _page_fetcher(pt_ref, srcs, bufs, sem, n_copy):
    def copy(a, slot, i, page):
        return pltpu.make_async_copy(srcs[a].at[page], bufs[a].at[slot, i], sem.at[a, slot])

    def start(step, slot):
        def body(i, carry):
            page = pt_ref[step * n_copy + i]
            for a in range(len(srcs)):
                copy(a, slot, i, page).start()
            return carry
        lax.fori_loop(0, n_copy, body, 0)

    def wait(slot):
        def body(i, carry):
            for a in range(len(srcs)):
                copy(a, slot, i, 0).wait()
            return carry
        lax.fori_loop(0, n_copy, body, 0)

    return start, wait


def _prefetch_pages(start, wait):
    s = pl.program_id(0)
    slot = s % 2

    @pl.when(s == 0)
    def _():
        start(0, 0)

    @pl.when(s + 1 < pl.num_programs(0))
    def _():
        start(s + 1, 1 - slot)

    wait(slot)
    return slot


def _sample_attn_kernel(pt_ref, q_ref, bias_ref, k_new, v_new, k_hbm, v_hbm, o_ref, kbuf, vbuf, sem, *, n_pages):
    group = q_ref.shape[0]
    slot = _prefetch_pages(*_page_fetcher(pt_ref, (k_hbm, v_hbm), (kbuf, vbuf), sem, group * n_pages))
    kt = lambda page: page.reshape(KV_W, PAGE_SIZE).astype(BF16)
    for g in range(group):
        qb = q_ref[g]
        reps = qb.shape[0] // bias_ref.shape[1]
        bias = jnp.concatenate([bias_ref[g]] * reps, axis=0)
        ks = [kbuf[slot, g * n_pages + j] for j in range(n_pages)] + [k_new[g]]
        vs = [vbuf[slot, g * n_pages + j] for j in range(n_pages)] + [v_new[g]]
        lg = jnp.concatenate([_dot(qb, kt(kp)) for kp in ks], axis=-1) + bias
        m = jnp.max(lg, axis=-1, keepdims=True)
        p = jnp.exp2(lg - m)
        l = jnp.sum(p, axis=-1, keepdims=True)
        pb = p.astype(BF16)
        acc = jnp.zeros(o_ref.shape[1:], F32)
        for j, vp in enumerate(vs):
            acc = acc + _dot_nt(pb[:, j * PAGE_SIZE:(j + 1) * PAGE_SIZE], kt(vp))
        o_ref[g] = acc / l


def _sample_attend(page_table_flat, q_bd, bias, k_t, v_t, k_new_t, v_new_t, n_pages):
    nb, rows, _ = q_bd.shape
    t = bias.shape[1]
    lk = bias.shape[2]
    g = ATTN_GROUP
    page = (N_KV_HEADS, HEAD_DIM, PAGE_SIZE)
    hbm = pl.BlockSpec(memory_space=pl.ANY)
    page_buf = pltpu.VMEM((2, g * n_pages) + page, F32)
    return pl.pallas_call(
        functools.partial(_sample_attn_kernel, n_pages=n_pages),
        grid_spec=pltpu.PrefetchScalarGridSpec(
            num_scalar_prefetch=1,
            grid=(nb // g,),
            in_specs=[_group_spec(g, rows, KV_W), _group_spec(g, t, lk), _group_spec(g, *page), _group_spec(g, *page),
                      hbm, hbm],
            out_specs=_group_spec(g, rows, KV_W),
            scratch_shapes=[page_buf, page_buf, pltpu.SemaphoreType.DMA((2, 2))],
        ),
        out_shape=jax.ShapeDtypeStruct((nb, rows, KV_W), F32),
        compiler_params=_params(("arbitrary",)),
        name="sample_attn",
    )(page_table_flat, q_bd, bias, k_new_t, v_new_t, k_t, v_t)


def _glu(xb, w_glu_ref):
    gi = _dot(xb, w_glu_ref[...])
    return gi[:, :C_CONV] * jax.nn.sigmoid(gi[:, C_CONV:])


def _gates(xb, attn_b, w_g_ref, w_ao_ref, cols=slice(0, D_MODEL)):
    gate_cols = slice(D_MODEL + cols.start, D_MODEL + cols.stop)
    a_term = jax.nn.sigmoid(_dot(xb, w_g_ref[:, cols])) * _dot(attn_b, w_ao_ref[:, cols])
    return a_term, jax.nn.sigmoid(_dot(xb, w_g_ref[:, gate_cols]))


def _mix_tail(x, c, a_term, c_gate, lncg_ref, lncb_ref, w_co_ref, w_out_ref, ln1g_ref, ln1b_ref):
    cn = _layer_norm(c, lncg_ref[...], lncb_ref[...])
    c_branch = _dot((cn * jax.nn.sigmoid(cn)).astype(BF16), w_co_ref[...])
    merged = a_term + c_gate * c_branch
    h = ALPHA * x + _dot(merged.astype(BF16), w_out_ref[...])
    return _layer_norm(h, ln1g_ref[...], ln1b_ref[...])


def _mix_prompt_kernel(x_ref, attn_ref, w_glu_ref, w_g_ref, w_ao_ref, w_dw_ref, b_dw_ref, lncg_ref, lncb_ref,
                       w_co_ref, w_out_ref, ln1g_ref, ln1b_ref, x1_ref, tail_ref, xp_ref):
    tm = x_ref.shape[0]

    @pl.when(pl.program_id(1) == 0)
    def _():
        xp_ref[0:CONV_HALO, :] = jnp.zeros((CONV_HALO, C_CONV), F32)

    x = x_ref[...]
    xb = x.astype(BF16)
    glu = _glu(xb, w_glu_ref)
    xp_ref[CONV_HALO:CONV_HALO + tm, :] = glu
    tail_ref[...] = glu[tm - CONV_HALO:, :]
    attn_b = attn_ref[...]

    first = CONV_HALO - (CONV_WIDTH - 1)
    rb = 128
    n_blk = C_CONV // LANES
    cols, gate_parts = [], []
    for c0 in range(0, C_CONV, LANES):
        lanes = slice(c0, c0 + LANES)
        q = c0 // LANES
        gate_parts.append(_gates(xb, attn_b, w_g_ref, w_ao_ref, slice(q * D_MODEL // n_blk, (q + 1) * D_MODEL // n_blk)))
        blocks = []
        for r0 in range(0, tm, rb):
            y = jnp.broadcast_to(b_dw_ref[:, lanes], (rb, LANES))
            for b in range(SUBLANES):
                rows = rb + (SUBLANES if b else 0)
                part = None
                for j in range(CONV_WIDTH):
                    if (first + j) % SUBLANES == b:
                        a0 = first + j - b + r0
                        term = w_dw_ref[j:j + 1, lanes] * xp_ref[a0:a0 + rows, lanes]
                        part = term if part is None else part + term
                y = y + part[b:b + rb, :]
            blocks.append(y)
        cols.append(jnp.concatenate(blocks, axis=0))
    c = jnp.concatenate(cols, axis=-1)
    a_term = jnp.concatenate([p[0] for p in gate_parts], axis=-1)
    c_gate = jnp.concatenate([p[1] for p in gate_parts], axis=-1)
    xp_ref[0:CONV_HALO, :] = xp_ref[tm:tm + CONV_HALO, :]

    x1_ref[...] = _mix_tail(x, c, a_term, c_gate, lncg_ref, lncb_ref, w_co_ref, w_out_ref, ln1g_ref, ln1b_ref)


def _mix_sample_kernel(x_ref, attn_ref, hist_ref, w_glu_ref, w_g_ref, w_ao_ref, w_dw_ref, b_dw_ref, lncg_ref,
                       lncb_ref, w_co_ref, w_out_ref, ln1g_ref, ln1b_ref, x1_ref, glu_ref):
    nb = hist_ref.shape[1]
    t = x_ref.shape[0] // nb
    n_hist = CONV_WIDTH - 1
    x = x_ref[...]
    xb = x.astype(BF16)
    glu = _glu(xb, w_glu_ref)
    glu_ref[...] = glu

    def slab(m):
        return hist_ref[m] if m < n_hist else glu[(m - n_hist) * nb:(m - n_hist + 1) * nb, :]

    outs = []
    for ti in range(t):
        acc = jnp.broadcast_to(b_dw_ref[...], (nb, C_CONV))
        for j in range(CONV_WIDTH):
            acc = acc + w_dw_ref[j:j + 1, :] * slab(ti + j)
        outs.append(acc)
    c = jnp.concatenate(outs, axis=0)
    a_term, c_gate = _gates(xb, attn_ref[...], w_g_ref, w_ao_ref)
    x1_ref[...] = _mix_tail(x, c, a_term, c_gate, lncg_ref, lncb_ref, w_co_ref, w_out_ref, ln1g_ref, ln1b_ref)


def _mix_weight_specs():
    return [
        _const_spec((D_MODEL, 2 * C_CONV)), _const_spec((D_MODEL, 2 * D_MODEL)), _const_spec((Q_W, D_MODEL)),
        _const_spec((CONV_WIDTH, C_CONV)), _const_spec((1, C_CONV)), _const_spec((1, C_CONV)),
        _const_spec((1, C_CONV)), _const_spec((C_CONV, D_MODEL)), _const_spec((D_MODEL, D_MODEL)),
        _const_spec((1, D_MODEL)), _const_spec((1, D_MODEL)),
    ]


def _mix_prompt(x, attn, mix_w, tm):
    b, t, _ = x.shape
    rows = lambda w: pl.BlockSpec((None, tm, w), lambda bi, i: (bi, i, 0))
    return pl.pallas_call(
        _mix_prompt_kernel,
        grid=(b, t // tm),
        in_specs=[rows(D_MODEL), rows(Q_W)] + _mix_weight_specs(),
        out_specs=[rows(D_MODEL), pl.BlockSpec((None, CONV_HALO, C_CONV), lambda bi, i: (bi, 0, 0))],
        out_shape=[jax.ShapeDtypeStruct((b, t, D_MODEL), F32), jax.ShapeDtypeStruct((b, CONV_HALO, C_CONV), F32)],
        scratch_shapes=[pltpu.VMEM((CONV_HALO + tm, C_CONV), F32)],
        compiler_params=_params(("parallel", "arbitrary")),
        name="mix_prompt",
    )(x, attn, *mix_w)


def _mix_sample(x_tm, attn_tm, hist_tm, mix_w):
    n = x_tm.shape[0]
    return pl.pallas_call(
        _mix_sample_kernel,
        grid=(1,),
        in_specs=[_const_spec((n, D_MODEL)), _const_spec((n, Q_W)), _const_spec(hist_tm.shape)] + _mix_weight_specs(),
        out_specs=[_whole_spec((n, D_MODEL)), _whole_spec((n, C_CONV))],
        out_shape=[jax.ShapeDtypeStruct((n, D_MODEL), F32), jax.ShapeDtypeStruct((n, C_CONV), F32)],
        compiler_params=_params(("arbitrary",)),
        name="mix_sample",
    )(x_tm, attn_tm, hist_tm, *mix_w)


def _ffn_tail(x1, uc, gate, w_down_ref, ln2g_ref, ln2b_ref):
    f = _dot((jax.nn.gelu(uc) * gate).astype(BF16), w_down_ref[...])
    return _layer_norm(ALPHA * x1 + f, ln2g_ref[...], ln2b_ref[...])


def _ffn_prompt_kernel(x1_ref, w_up_ref, w_gate_ref, w_dw_ref, b_dw_ref, w_down_ref, ln2g_ref, ln2b_ref,
                       y_ref, tail_ref, up_ref):
    tm = x1_ref.shape[0]

    @pl.when(pl.program_id(1) == 0)
    def _():
        up_ref[0:FFN_HALO, :] = jnp.zeros((FFN_HALO, D_FF), F32)

    x1 = x1_ref[...]
    x1b = x1.astype(BF16)
    u = _dot(x1b, w_up_ref[...])
    gate = _dot(x1b, w_gate_ref[...])
    up_ref[FFN_HALO:FFN_HALO + tm, :] = u
    tail_ref[...] = u[tm - FFN_HALO:, :]
    first = FFN_HALO - (FFN_CONV_WIDTH - 1)
    uc = b_dw_ref[...]
    for j in range(FFN_CONV_WIDTH):
        uc = uc + w_dw_ref[j:j + 1, :] * up_ref[first + j:first + j + tm, :]
    up_ref[0:FFN_HALO, :] = up_ref[tm:tm + FFN_HALO, :]
    y_ref[...] = _ffn_tail(x1, uc, gate, w_down_ref, ln2g_ref, ln2b_ref)


def _ffn_sample_kernel(x1_ref, hist_ref, w_up_ref, w_gate_ref, w_dw_ref, b_dw_ref, w_down_ref, ln2g_ref, ln2b_ref,
                       y_ref, u_ref):
    nb = hist_ref.shape[1]
    t = x1_ref.shape[0] // nb
    n_hist = FFN_CONV_WIDTH - 1
    x1 = x1_ref[...]
    x1b = x1.astype(BF16)
    u = _dot(x1b, w_up_ref[...])
    gate = _dot(x1b, w_gate_ref[...])
    u_ref[...] = u

    def slab(m):
        return hist_ref[m] if m < n_hist else u[(m - n_hist) * nb:(m - n_hist + 1) * nb, :]

    outs = []
    for ti in range(t):
        acc = jnp.broadcast_to(b_dw_ref[...], (nb, D_FF))
        for j in range(FFN_CONV_WIDTH):
            acc = acc + w_dw_ref[j:j + 1, :] * slab(ti + j)
        outs.append(acc)
    uc = jnp.concatenate(outs, axis=0)
    y_ref[...] = _ffn_tail(x1, uc, gate, w_down_ref, ln2g_ref, ln2b_ref)


def _ffn_weight_specs():
    return [
        _const_spec((D_MODEL, D_FF)), _const_spec((D_MODEL, D_FF)), _const_spec((FFN_CONV_WIDTH, D_FF)),
        _const_spec((1, D_FF)), _const_spec((D_FF, D_MODEL)), _const_spec((1, D_MODEL)), _const_spec((1, D_MODEL)),
    ]


def _ffn_prompt(x1, ffn_w, tm):
    b, t, _ = x1.shape
    rows = pl.BlockSpec((None, tm, D_MODEL), lambda bi, i: (bi, i, 0))
    return pl.pallas_call(
        _ffn_prompt_kernel,
        grid=(b, t // tm),
        in_specs=[rows] + _ffn_weight_specs(),
        out_specs=[rows, pl.BlockSpec((None, FFN_HALO, D_FF), lambda bi, i: (bi, 0, 0))],
        out_shape=[jax.ShapeDtypeStruct((b, t, D_MODEL), F32), jax.ShapeDtypeStruct((b, FFN_HALO, D_FF), F32)],
        scratch_shapes=[pltpu.VMEM((FFN_HALO + tm, D_FF), F32)],
        compiler_params=_params(("parallel", "arbitrary")),
        name="ffn_prompt",
    )(x1, *ffn_w)


def _ffn_sample(x1_tm, hist_tm, ffn_w):
    n = x1_tm.shape[0]
    return pl.pallas_call(
        _ffn_sample_kernel,
        grid=(1,),
        in_specs=[_const_spec((n, D_MODEL)), _const_spec(hist_tm.shape)] + _ffn_weight_specs(),
        out_specs=[_whole_spec((n, D_MODEL)), _whole_spec((n, D_FF))],
        out_shape=[jax.ShapeDtypeStruct((n, D_MODEL), F32), jax.ShapeDtypeStruct((n, D_FF), F32)],
        compiler_params=_params(("arbitrary",)),
        name="ffn_sample",
    )(x1_tm, hist_tm, *ffn_w)


def _token_major(a):
    b, t, w = a.shape
    return jnp.transpose(a, (1, 0, 2)).reshape(t * b, w)


def _batch_major(a, b):
    tb, w = a.shape
    return jnp.transpose(a.reshape(tb // b, b, w), (1, 0, 2))


def _stage_weights(w_in):
    o = 0
    w_q = w_in[:, o:o + Q_W]; o += Q_W
    w_k = w_in[:, o:o + KV_W]; o += KV_W
    w_v = w_in[:, o:o + KV_W]; o += KV_W
    w_qi = w_in[:, o:o + IQ_W]; o += IQ_W
    w_ki = w_in[:, o:o + IDX_DIM]; o += IDX_DIM
    w_wi = w_in[:, o:o + N_IDX_HEADS]; o += N_IDX_HEADS
    head = lambda h: w_q[:, h * HEAD_DIM:(h + 1) * HEAD_DIM]
    q_pairs = [w for i in range(GROUP) for w in (head(i), head(GROUP + i))]
    zero = jnp.zeros((D_MODEL, LANES - IDX_DIM), w_in.dtype)
    w_att = jnp.concatenate(q_pairs + [w_k, w_qi, w_ki, zero], axis=1).astype(BF16)
    w_t = jnp.concatenate([w_k, w_v, w_ki, w_wi], axis=1).T.astype(BF16)
    return w_att, w_t, o


def kernel(x_prompt, x_sample, cache_k, cache_v, cache_kidx, state_conv, state_ffn, page_table, w_in, w_attn_o, w_conv_dw, b_conv_dw, ln_conv_g, ln_conv_b, w_conv_o, w_out, ln1_g, ln1_b, w_ffn_up, w_ffn_gate, w_ffn_dw, b_ffn_dw, w_ffn_down, ln2_g, ln2_b):
    bp, tp, _ = x_prompt.shape
    bs, ts, _ = x_sample.shape
    n_pages = page_table.shape[1]
    past = n_pages * PAGE_SIZE
    assert N_KV_HEADS == 2 and N_IDX_HEADS == SUBLANES

    w_att, w_t, o = _stage_weights(w_in)
    w_glu = w_in[:, o:o + 2 * C_CONV].astype(BF16); o += 2 * C_CONV
    w_g = w_in[:, o:o + 2 * D_MODEL].astype(BF16)
    row2 = lambda a: a.reshape(1, -1)
    mix_w = (w_glu, w_g, w_attn_o.astype(BF16), w_conv_dw, row2(b_conv_dw), row2(ln_conv_g), row2(ln_conv_b),
             w_conv_o.astype(BF16), w_out.astype(BF16), row2(ln1_g), row2(ln1_b))
    ffn_w = (w_ffn_up.astype(BF16), w_ffn_gate.astype(BF16), w_ffn_dw, row2(b_ffn_dw), w_ffn_down.astype(BF16),
             row2(ln2_g), row2(ln2_b))

    qbd, qi, kb, kib, vt3, k_t, v_t, ki_t, wit = _proj(x_prompt, w_att, w_t, 512)
    seq = lambda a: a.reshape(bp, tp, a.shape[-1])
    attn_p = _prompt_attention(seq(qbd), seq(qi), wit, seq(kib), seq(kb), vt3, min(TOPK_MAX, tp // 4))
    x1_p, conv_tail = _mix_prompt(x_prompt, attn_p, mix_w, 512)
    y_p, ffn_tail = _ffn_prompt(x1_p, ffn_w, 256)
    heads_last = lambda a_t: jnp.transpose(a_t.reshape(bp, N_KV_HEADS, HEAD_DIM, tp), (0, 3, 1, 2))
    k_p = heads_last(k_t)
    v_p = heads_last(v_t)
    ki_p = jnp.transpose(ki_t, (0, 2, 1))
    conv_p = conv_tail[:, CONV_HALO - (CONV_WIDTH - 1):, :]
    ffn_p = ffn_tail[:, FFN_HALO - (FFN_CONV_WIDTH - 1):, :]

    n_s = bs * ts
    qbd_s, qis, _, _, _, ks_t, vs_t, kis_t, wit_s = _proj(x_sample.reshape(1, n_s, D_MODEL), w_att, w_t, n_s)
    ks_t, vs_t, kis_t = ks_t[0], vs_t[0], kis_t[0]
    pt_flat = page_table.reshape(-1).astype(jnp.int32)
    new_page = lambda a_t: jnp.pad(jnp.transpose(a_t.reshape(-1, bs, ts), (1, 0, 2)),
                                   ((0, 0), (0, 0), (0, PAGE_SIZE - ts)))
    qi32 = qis.reshape(bs, ts * N_IDX_HEADS, IDX_DIM)
    w32 = wit_s.T.reshape(bs, ts * N_IDX_HEADS, 1)
    scores = _sample_scores(pt_flat, qi32, w32, jnp.transpose(cache_kidx, (0, 2, 1)), new_page(kis_t), n_pages)
    lk = scores.shape[-1]
    bias_t = _sample_select(scores.reshape(n_s, lk).T, past, ts, min(TOPK_MAX, (past + ts) // 4))
    q_bd = jnp.transpose(qbd_s.reshape(bs, ts, N_HEADS, KV_W), (0, 2, 1, 3)).reshape(bs, N_HEADS * ts, KV_W)
    as_pages = lambda a: jnp.transpose(a, (0, 2, 3, 1))
    kv_new = lambda a: new_page(a).reshape(bs, N_KV_HEADS, HEAD_DIM, PAGE_SIZE)
    o_bd = _sample_attend(pt_flat, q_bd, bias_t.T.reshape(bs, ts, lk), as_pages(cache_k), as_pages(cache_v),
                          kv_new(ks_t), kv_new(vs_t), n_pages)
    o6 = o_bd.reshape(bs, N_KV_HEADS, GROUP, ts, N_KV_HEADS, HEAD_DIM)
    o_sel = jnp.stack([o6[:, n, :, :, n, :] for n in range(N_KV_HEADS)], axis=1)
    attn_s = jnp.transpose(o_sel, (0, 3, 1, 2, 4)).reshape(bs, ts, Q_W)

    x1_s, glu_s = _mix_sample(_token_major(x_sample), _token_major(attn_s).astype(BF16),
                              jnp.transpose(state_conv, (1, 0, 2)), mix_w)
    y_s, u_s = _ffn_sample(x1_s, jnp.transpose(state_ffn, (1, 0, 2)), ffn_w)
    y_s = _batch_major(y_s, bs)
    conv_s = jnp.concatenate([state_conv, _batch_major(glu_s, bs)], axis=1)[:, -(CONV_WIDTH - 1):, :]
    ffn_s = jnp.concatenate([state_ffn, _batch_major(u_s, bs)], axis=1)[:, -(FFN_CONV_WIDTH - 1):, :]
    k_s = ks_t.T.reshape(bs, ts, N_KV_HEADS, HEAD_DIM)
    v_s = vs_t.T.reshape(bs, ts, N_KV_HEADS, HEAD_DIM)
    ki_s = kis_t.T.reshape(bs, ts, IDX_DIM)

    return (y_p, y_s, k_p, v_p, ki_p, conv_p, ffn_p, k_s, v_s, ki_s, conv_s, ffn_s)
```

```python
import functools

import jax
import jax.numpy as jnp
from jax import lax
from jax.experimental import pallas as pl
from jax.experimental.pallas import tpu as pltpu

D_MODEL = 1024
N_HEADS = 8
HEAD_DIM = 64
N_KV_HEADS = 2
N_IDX_HEADS = 8
IDX_DIM = 64
TOPK_MAX = 256
C_CONV = D_MODEL // 2
CONV_WIDTH = 31
D_FF = 2816
FFN_CONV_WIDTH = 3
LN_EPS = 1e-5
DEPTH = 1
ALPHA = (2.0 * DEPTH) ** 0.25
PAGE_SIZE = 128

Q_W = N_HEADS * HEAD_DIM
KV_W = N_KV_HEADS * HEAD_DIM
IQ_W = N_IDX_HEADS * IDX_DIM
GROUP = N_HEADS // N_KV_HEADS

LANES = 128
SUBLANES = 8
QBD_W = N_HEADS * KV_W
PROJ_W = Q_W + KV_W + IQ_W + LANES
PROJ_T = 2 * KV_W + IDX_DIM + N_IDX_HEADS
TQ = 256
TQ_S = 256
SCORE_GROUP = 8
ATTN_GROUP = 4
CONV_HALO = 32
FFN_HALO = 8
N_BISECT = 19
VT_ROWS = HEAD_DIM + 16
LOG2E = 1.4426950408889634
VMEM_LIMIT = 56 * 1024 * 1024

F32 = jnp.float32
BF16 = jnp.bfloat16
NEG_INF = float("-inf")
POS_INF = float("inf")


def _dot(a, b):
    return jnp.dot(a, b, preferred_element_type=F32)


def _dot_nt(a, b):
    return lax.dot_general(a, b, (((1,), (1,)), ((), ())), preferred_element_type=F32)


def _layer_norm(x, g, b):
    mu = jnp.mean(x, axis=-1, keepdims=True)
    xc = x - mu
    var = jnp.mean(xc * xc, axis=-1, keepdims=True)
    return xc * lax.rsqrt(var + LN_EPS) * g + b


def _params(sem):
    return pltpu.CompilerParams(dimension_semantics=sem, vmem_limit_bytes=VMEM_LIMIT)


def _whole_spec(shape):
    nd = len(shape)
    return pl.BlockSpec(shape, lambda *_: (0,) * nd)


def _const_spec(shape):
    nd = len(shape)
    return pl.BlockSpec(shape, lambda *_: (0,) * nd, pipeline_mode=pl.Buffered(1))


def _proj_kernel(x_ref, w_ref, wt_ref, qbd_ref, qi_ref, kb_ref, kib_ref, vt_ref, kt_ref, vtf_ref, kit_ref, wit_ref):
    xb = x_ref[...].astype(BF16)
    o = 0
    qp = _dot_nt(xb, w_ref[o:o + Q_W, :]); o += Q_W
    k = _dot_nt(xb, w_ref[o:o + KV_W, :]); o += KV_W
    qi = _dot_nt(xb, w_ref[o:o + IQ_W, :]); o += IQ_W
    ki = _dot_nt(xb, w_ref[o:o + LANES, :])[:, :IDX_DIM]
    qp = (qp * (HEAD_DIM ** -0.5 * LOG2E)).astype(BF16)
    low = lax.broadcasted_iota(jnp.int32, (qp.shape[0], KV_W), 1) < HEAD_DIM
    for h in range(N_HEADS):
        pair = qp[:, (h % GROUP) * KV_W:(h % GROUP + 1) * KV_W]
        qbd_ref[:, h * KV_W:(h + 1) * KV_W] = jnp.where(low if h < GROUP else jnp.logical_not(low), pair, 0.0)
    qi_ref[...] = qi.astype(BF16)
    kb_ref[...] = k.astype(BF16)
    kib_ref[...] = ki.astype(BF16)
    t = _dot_nt(wt_ref[...], xb)
    kt_ref[...] = t[:KV_W, :]
    vtf_ref[...] = t[KV_W:2 * KV_W, :]
    kit_ref[...] = t[2 * KV_W:2 * KV_W + IDX_DIM, :]
    wit_ref[...] = t[2 * KV_W + IDX_DIM:, :] * ((IDX_DIM ** -0.5) * (N_IDX_HEADS ** -0.5))
    vt = t[KV_W:2 * KV_W, :].astype(BF16)
    ones = jnp.ones((VT_ROWS - HEAD_DIM, TQ), BF16)
    for j in range(vt_ref.shape[0]):
        for n in range(N_KV_HEADS):
            vt_ref[j, n * VT_ROWS:n * VT_ROWS + HEAD_DIM, :] = vt[n * HEAD_DIM:(n + 1) * HEAD_DIM, j * TQ:(j + 1) * TQ]
            vt_ref[j, n * VT_ROWS + HEAD_DIM:(n + 1) * VT_ROWS, :] = ones


def _proj(x, w_att, w_t, tm):
    b, t, _ = x.shape
    n = b * t
    tiles = t // tm
    row = lambda w: pl.BlockSpec((tm, w), lambda i: (i, 0))
    col = lambda r: pl.BlockSpec((None, r, tm), lambda i: (i // tiles, 0, i % tiles))
    outs = [(QBD_W, BF16), (IQ_W, BF16), (KV_W, BF16), (IDX_DIM, BF16)]
    return pl.pallas_call(
        _proj_kernel,
        grid=(n // tm,),
        in_specs=[row(D_MODEL), _const_spec((PROJ_W, D_MODEL)), _const_spec((PROJ_T, D_MODEL))],
        out_specs=[row(w) for w, _ in outs]
        + [pl.BlockSpec((tm // TQ, N_KV_HEADS * VT_ROWS, TQ), lambda i: (i, 0, 0)), col(KV_W), col(KV_W), col(IDX_DIM),
           pl.BlockSpec((SUBLANES, tm), lambda i: (0, i))],
        out_shape=[jax.ShapeDtypeStruct((n, w), dt) for w, dt in outs]
        + [jax.ShapeDtypeStruct((n // TQ, N_KV_HEADS * VT_ROWS, TQ), BF16), jax.ShapeDtypeStruct((b, KV_W, t), F32),
           jax.ShapeDtypeStruct((b, KV_W, t), F32), jax.ShapeDtypeStruct((b, IDX_DIM, t), F32),
           jax.ShapeDtypeStruct((SUBLANES, n), F32)],
        compiler_params=_params(("parallel",)),
        name="proj",
    )(x.reshape(n, D_MODEL), w_att, w_t)


def _chunk(ref, c, ch):
    return ref[pl.ds(pl.multiple_of(c * ch, ch), ch), :]


def _fold_rows(x, op):
    parts = [x[r:r + SUBLANES, :] for r in range(0, x.shape[0], SUBLANES)]
    accs = parts[:4]
    for i, part in enumerate(parts[4:]):
        accs[i % 4] = op(accs[i % 4], part)
    while len(accs) > 1:
        accs = [op(accs[a], accs[a + 1]) for a in range(0, len(accs) - 1, 2)] + ([accs[-1]] if len(accs) % 2 else [])
    return accs[0]


def _reduce_keys(s_ref, nk, ch, init, f, op, red):
    w = s_ref.shape[1]

    def body(c, acc):
        return op(acc, _fold_rows(f(_chunk(s_ref, c, ch), c), op))

    acc = lax.fori_loop(0, nk, body, jnp.full((SUBLANES, w), init, F32))
    return red(acc, axis=0, keepdims=True)


def _select_bias(s_ref, bias_ref, nk, ch, topk):
    w = s_ref.shape[1]
    kf = float(topk)
    count = lambda pred: _reduce_keys(s_ref, nk, ch, 0.0, lambda x, c: jnp.where(pred(x, c), 1.0, 0.0),
                                      jnp.add, jnp.sum)
    min_above = lambda t: _reduce_keys(s_ref, nk, ch, POS_INF, lambda x, c: jnp.where(x > t, x, POS_INF),
                                       jnp.minimum, jnp.min)

    neg = jnp.full((1, w), NEG_INF, F32)

    hi = _reduce_keys(s_ref, nk, ch, NEG_INF, lambda x, c: x, jnp.maximum, jnp.max)
    lo_fin = min_above(neg)
    n_adm = count(lambda x, _: x > neg)

    def bisect(_, st):
        lo, lo_fin, hi, n_lo = st
        mid = 0.5 * lo_fin + 0.5 * hi
        c = count(lambda x, _: x > mid)
        ok = c >= kf
        return jnp.where(ok, mid, lo), jnp.where(ok, mid, lo_fin), jnp.where(ok, hi, mid), jnp.where(ok, c, n_lo)

    lo, _, _, n_lo = lax.fori_loop(0, N_BISECT, bisect, (neg, lo_fin, hi, n_adm))

    def peel(st):
        lo, thr, n_gt, done = st
        v = min_above(lo)
        c = count(lambda x, _: x > v)
        found = c < kf
        newly = jnp.logical_and(done < 0.5, found)
        return (jnp.where(jnp.logical_or(found, done > 0.5), lo, v), jnp.where(newly, v, thr),
                jnp.where(newly, c, n_gt), jnp.where(found, 1.0, done))

    _, thr, n_gt, _ = lax.while_loop(lambda st: jnp.min(st[3]) < 0.5, peel,
                                     (lo, lo, n_lo, jnp.where(n_lo <= kf, 1.0, 0.0)))

    need = jnp.where(thr == neg, 0.0, kf - n_gt)
    any_copies = jnp.max(need) > 0.0

    @pl.when(any_copies)
    def _():
        tri = (lax.broadcasted_iota(jnp.int32, (ch, ch), 1)
               <= lax.broadcasted_iota(jnp.int32, (ch, ch), 0)).astype(BF16)

        def write(c, seen):
            x = _chunk(s_ref, c, ch)
            eq = x == thr
            rank = seen + _dot(tri, jnp.where(eq, 1.0, 0.0).astype(BF16))
            sel = jnp.logical_or(x > thr, jnp.logical_and(eq, rank <= need))
            bias_ref[pl.ds(pl.multiple_of(c * ch, ch), ch), :] = jnp.where(sel, 0.0, NEG_INF)
            return rank[ch - 1:ch, :]

        lax.fori_loop(0, nk, write, jnp.zeros((1, w), F32))

    @pl.when(jnp.logical_not(any_copies))
    def _():
        def write(c, carry):
            sel = _chunk(s_ref, c, ch) > thr
            bias_ref[pl.ds(pl.multiple_of(c * ch, ch), ch), :] = jnp.where(sel, 0.0, NEG_INF)
            return carry

        lax.fori_loop(0, nk, write, 0)


def _prompt_attn_kernel(qbd_ref, qi_ref, wit_ref, kib_ref, kb_ref, vt_ref, o_ref, s_ref, bias_ref, m_ref, acc_ref,
                        lg_ref, *, topk):
    i = pl.program_id(1)
    nk = i + 1
    qi = qi_ref[...]
    wit = wit_ref[...]
    q_pos = lax.broadcasted_iota(jnp.int32, (TQ, TQ), 1) + i * TQ
    k_off = lax.broadcasted_iota(jnp.int32, (TQ, TQ), 0)

    def score(c, carry):
        kc = _chunk(kib_ref, c, TQ)
        s = jnp.zeros((TQ, TQ), F32)
        for h in range(N_IDX_HEADS):
            z = _dot_nt(kc, qi[:, h * IDX_DIM:(h + 1) * IDX_DIM])
            s = s + jnp.maximum(z, 0.0) * wit[h:h + 1, :]
        s_ref[pl.ds(pl.multiple_of(c * TQ, TQ), TQ), :] = jnp.where(k_off + c * TQ <= q_pos, s, NEG_INF)
        return carry

    lax.fori_loop(0, nk, score, 0)
    _select_bias(s_ref, bias_ref, nk, TQ, topk)

    m_ref[...] = jnp.full(m_ref.shape, NEG_INF, F32)
    acc_ref[...] = jnp.zeros(acc_ref.shape, F32)

    def attend(c, carry):
        rows = pl.ds(pl.multiple_of(c * TQ, TQ), TQ)
        mx = []
        for h in range(N_HEADS):
            lg = _dot_nt(kb_ref[rows, :], qbd_ref[:, h * KV_W:(h + 1) * KV_W]) + bias_ref[rows, :]
            lg_ref[h] = lg
            mx.append(_fold_rows(lg, jnp.maximum))
        m_all = m_ref[...]
        m_rows = []
        for h in range(N_HEADS):
            n = h // GROUP
            out = slice(h * VT_ROWS, (h + 1) * VT_ROWS)
            m_old = m_all[h:h + 1, :]
            m_new = jnp.maximum(m_old, jnp.max(mx[h], axis=0, keepdims=True))
            m_rows.append(m_new)
            m_use = jnp.where(m_new == NEG_INF, 0.0, m_new)
            p = jnp.exp2(lg_ref[h] - m_use).astype(BF16)
            vt = vt_ref[c, pl.ds(n * VT_ROWS, VT_ROWS), :]
            acc_ref[out, :] = jnp.exp2(m_old - m_use) * acc_ref[out, :] + _dot(vt, p)
        m_ref[...] = jnp.concatenate(m_rows, axis=0)
        return carry

    lax.fori_loop(0, nk, attend, 0)
    outs = [acc_ref[h * VT_ROWS:h * VT_ROWS + HEAD_DIM, :] / acc_ref[h * VT_ROWS + HEAD_DIM:h * VT_ROWS + HEAD_DIM + 1, :]
            for h in range(N_HEADS)]
    o_ref[...] = jnp.concatenate(outs, axis=0).T.astype(o_ref.dtype)


def _prompt_attention(qbd, qi, wit, kib, kb, vt3, topk):
    b, t, _ = qbd.shape
    nblk = t // TQ
    qblk = lambda w: pl.BlockSpec((None, TQ, w), lambda bi, i: (bi, i, 0))
    full = lambda w: pl.BlockSpec((None, t, w), lambda bi, i: (bi, 0, 0))
    return pl.pallas_call(
        functools.partial(_prompt_attn_kernel, topk=topk),
        grid=(b, nblk),
        in_specs=[qblk(QBD_W), qblk(IQ_W), pl.BlockSpec((SUBLANES, TQ), lambda bi, i: (0, bi * nblk + i)),
                  full(IDX_DIM), full(KV_W), pl.BlockSpec((nblk, N_KV_HEADS * VT_ROWS, TQ), lambda bi, i: (bi, 0, 0))],
        out_specs=qblk(Q_W),
        out_shape=jax.ShapeDtypeStruct((b, t, Q_W), BF16),
        scratch_shapes=[pltpu.VMEM((t, TQ), F32), pltpu.VMEM((t, TQ), F32), pltpu.VMEM((N_HEADS, TQ), F32),
                        pltpu.VMEM((N_HEADS * VT_ROWS, TQ), F32), pltpu.VMEM((N_HEADS, TQ, TQ), F32)],
        compiler_params=_params(("parallel", "arbitrary")),
        name="prompt_attn",
    )(qbd, qi, wit, kib, kb, vt3)


def _group_spec(group, *block):
    nd = len(block)
    return pl.BlockSpec((group,) + block, lambda bi, pt: (bi,) + (0,) * nd)


def _sample_score_kernel(pt_ref, qi_ref, w_ref, new_ref, kidx_hbm, o_ref, buf, sem, *, n_pages):
    group = qi_ref.shape[0]
    slot = _prefetch_pages(*_page_fetcher(pt_ref, (kidx_hbm,), (buf,), sem, group * n_pages))
    t = qi_ref.shape[1] // N_IDX_HEADS
    for g in range(group):
        qi = qi_ref[g]
        w = w_ref[g]
        seq_pages = [buf[slot, g * n_pages + j] for j in range(n_pages)] + [new_ref[g]]
        for j, page in enumerate(seq_pages):
            s = _dot(qi, page.astype(BF16))
            s = jnp.maximum(s, 0.0) * w
            o_ref[g, :, j * PAGE_SIZE:(j + 1) * PAGE_SIZE] = jnp.sum(s.reshape(t, N_IDX_HEADS, PAGE_SIZE), axis=1)


def _sample_scores(page_table_flat, qi32, w32, kidx_t, ki_new_t, n_pages):
    nb, rows, _ = qi32.shape
    t = rows // N_IDX_HEADS
    lk = (n_pages + 1) * PAGE_SIZE
    g = SCORE_GROUP
    return pl.pallas_call(
        functools.partial(_sample_score_kernel, n_pages=n_pages),
        grid_spec=pltpu.PrefetchScalarGridSpec(
            num_scalar_prefetch=1,
            grid=(nb // g,),
            in_specs=[_group_spec(g, rows, IDX_DIM), _group_spec(g, rows, 1), _group_spec(g, IDX_DIM, PAGE_SIZE),
                      pl.BlockSpec(memory_space=pl.ANY)],
            out_specs=_group_spec(g, t, lk),
            scratch_shapes=[pltpu.VMEM((2, g * n_pages, IDX_DIM, PAGE_SIZE), F32), pltpu.SemaphoreType.DMA((1, 2))],
        ),
        out_shape=jax.ShapeDtypeStruct((nb, t, lk), F32),
        compiler_params=_params(("arbitrary",)),
        name="sample_scores",
    )(page_table_flat, qi32, w32, ki_new_t, kidx_t)


def _sample_select_kernel(s_ref, bias_ref, sm_ref, *, past, t, topk):
    lk = s_ref.shape[0]
    nk = lk // PAGE_SIZE
    qcol = lax.broadcasted_iota(jnp.int32, (PAGE_SIZE, TQ_S), 1)
    qpos = past + (qcol & (t - 1))
    krow = lax.broadcasted_iota(jnp.int32, (PAGE_SIZE, TQ_S), 0)
    for c in range(nk):
        rows = slice(c * PAGE_SIZE, (c + 1) * PAGE_SIZE)
        sm_ref[rows, :] = jnp.where(krow + c * PAGE_SIZE <= qpos, s_ref[rows, :], NEG_INF)
    _select_bias(sm_ref, bias_ref, nk, PAGE_SIZE, topk)


def _sample_select(scores_t, past, t, topk):
    lk, n = scores_t.shape
    assert t & (t - 1) == 0 and TQ_S % t == 0, "token index is taken from the low bits of the query index"
    blk = pl.BlockSpec((lk, TQ_S), lambda i: (0, i))
    return pl.pallas_call(
        functools.partial(_sample_select_kernel, past=past, t=t, topk=topk),
        grid=(n // TQ_S,),
        in_specs=[blk],
        out_specs=blk,
        out_shape=jax.ShapeDtypeStruct((lk, n), F32),
        scratch_shapes=[pltpu.VMEM((lk, TQ_S), F32)],
        compiler_params=_params(("parallel",)),
        name="sample_select",
    )(scores_t)


def _page_fetcher(pt_ref, srcs, bufs, sem, n_copy):
    def copy(a, slot, i, page):
        return pltpu.make_async_copy(srcs[a].at[page], bufs[a].at[slot, i], sem.at[a, slot])

    def start(step, slot):
        def body(i, carry):
            page = pt_ref[step * n_copy + i]
            for a in range(len(srcs)):
                copy(a, slot, i, page).start()
            return carry
        lax.fori_loop(0, n_copy, body, 0)

    def wait(slot):
        def body(i, carry):
            for a in range(len(srcs)):
                copy(a, slot, i, 0).wait()
            return carry
        lax.fori_loop(0, n_copy, body, 0)

    return start, wait


def _prefetch_pages(start, wait):
    s = pl.program_id(0)
    slot = s % 2

    @pl.when(s == 0)
    def _():
        start(0, 0)

    @pl.when(s + 1 < pl.num_programs(0))
    def _():
        start(s + 1, 1 - slot)

    wait(slot)
    return slot


def _sample_attn_kernel(pt_ref, q_ref, bias_ref, k_new, v_new, k_hbm, v_hbm, o_ref, kbuf, vbuf, sem, *, n_pages):
    group = q_ref.shape[0]
    slot = _prefetch_pages(*_page_fetcher(pt_ref, (k_hbm, v_hbm), (kbuf, vbuf), sem, group * n_pages))
    kt = lambda page: page.reshape(KV_W, PAGE_SIZE).astype(BF16)
    for g in range(group):
        qb = q_ref[g]
        reps = qb.shape[0] // bias_ref.shape[1]
        bias = jnp.concatenate([bias_ref[g]] * reps, axis=0)
        ks = [kbuf[slot, g * n_pages + j] for j in range(n_pages)] + [k_new[g]]
        vs = [vbuf[slot, g * n_pages + j] for j in range(n_pages)] + [v_new[g]]
        lg = jnp.concatenate([_dot(qb, kt(kp)) for kp in ks], axis=-1) + bias
        m = jnp.max(lg, axis=-1, keepdims=True)
        p = jnp.exp2(lg - m)
        l = jnp.sum(p, axis=-1, keepdims=True)
        pb = p.astype(BF16)
        acc = jnp.zeros(o_ref.shape[1:], F32)
        for j, vp in enumerate(vs):
            acc = acc + _dot_nt(pb[:, j * PAGE_SIZE:(j + 1) * PAGE_SIZE], kt(vp))
        o_ref[g] = acc / l


def _sample_attend(page_table_flat, q_bd, bias, k_t, v_t, k_new_t, v_new_t, n_pages):
    nb, rows, _ = q_bd.shape
    t = bias.shape[1]
    lk = bias.shape[2]
    g = ATTN_GROUP
    page = (N_KV_HEADS, HEAD_DIM, PAGE_SIZE)
    hbm = pl.BlockSpec(memory_space=pl.ANY)
    page_buf = pltpu.VMEM((2, g * n_pages) + page, F32)
    return pl.pallas_call(
        functools.partial(_sample_attn_kernel, n_pages=n_pages),
        grid_spec=pltpu.PrefetchScalarGridSpec(
            num_scalar_prefetch=1,
            grid=(nb // g,),
            in_specs=[_group_spec(g, rows, KV_W), _group_spec(g, t, lk), _group_spec(g, *page), _group_spec(g, *page),
                      hbm, hbm],
            out_specs=_group_spec(g, rows, KV_W),
            scratch_shapes=[page_buf, page_buf, pltpu.SemaphoreType.DMA((2, 2))],
        ),
        out_shape=jax.ShapeDtypeStruct((nb, rows, KV_W), F32),
        compiler_params=_params(("arbitrary",)),
        name="sample_attn",
    )(page_table_flat, q_bd, bias, k_new_t, v_new_t, k_t, v_t)


def _glu(xb, w_glu_ref):
    gi = _dot_nt(xb, w_glu_ref[...])
    return gi[:, :C_CONV] * jax.nn.sigmoid(gi[:, C_CONV:])


def _gates(xb, attn_b, w_g_ref, w_ao_ref, cols=slice(0, D_MODEL)):
    gate_cols = slice(D_MODEL + cols.start, D_MODEL + cols.stop)
    a_term = jax.nn.sigmoid(_dot_nt(xb, w_g_ref[cols, :])) * _dot(attn_b, w_ao_ref[:, cols])
    return a_term, jax.nn.sigmoid(_dot_nt(xb, w_g_ref[gate_cols, :]))


def _mix_tail(x, c, a_term, c_gate, lncg_ref, lncb_ref, w_co_ref, w_out_ref, ln1g_ref, ln1b_ref):
    cn = _layer_norm(c, lncg_ref[...], lncb_ref[...])
    c_branch = _dot((cn * jax.nn.sigmoid(cn)).astype(BF16), w_co_ref[...])
    merged = a_term + c_gate * c_branch
    h = ALPHA * x + _dot(merged.astype(BF16), w_out_ref[...])
    return _layer_norm(h, ln1g_ref[...], ln1b_ref[...])


def _mix_prompt_kernel(x_ref, attn_ref, w_glu_ref, w_g_ref, w_ao_ref, w_dw_ref, b_dw_ref, lncg_ref, lncb_ref,
                       w_co_ref, w_out_ref, ln1g_ref, ln1b_ref, x1_ref, tail_ref, xp_ref):
    tm = x_ref.shape[0]

    @pl.when(pl.program_id(1) == 0)
    def _():
        xp_ref[0:CONV_HALO, :] = jnp.zeros((CONV_HALO, C_CONV), F32)

    x = x_ref[...]
    xb = x.astype(BF16)
    glu = _glu(xb, w_glu_ref)
    xp_ref[CONV_HALO:CONV_HALO + tm, :] = glu
    tail_ref[...] = glu[tm - CONV_HALO:, :]
    attn_b = attn_ref[...]

    first = CONV_HALO - (CONV_WIDTH - 1)
    rb = 128
    n_blk = C_CONV // LANES
    cols, gate_parts = [], []
    for c0 in range(0, C_CONV, LANES):
        lanes = slice(c0, c0 + LANES)
        q = c0 // LANES
        gate_parts.append(_gates(xb, attn_b, w_g_ref, w_ao_ref, slice(q * D_MODEL // n_blk, (q + 1) * D_MODEL // n_blk)))
        blocks = []
        for r0 in range(0, tm, rb):
            y = jnp.broadcast_to(b_dw_ref[:, lanes], (rb, LANES))
            for b in range(SUBLANES):
                rows = rb + (SUBLANES if b else 0)
                part = None
                for j in range(CONV_WIDTH):
                    if (first + j) % SUBLANES == b:
                        a0 = first + j - b + r0
                        term = w_dw_ref[j:j + 1, lanes] * xp_ref[a0:a0 + rows, lanes]
                        part = term if part is None else part + term
                y = y + part[b:b + rb, :]
            blocks.append(y)
        cols.append(jnp.concatenate(blocks, axis=0))
    c = jnp.concatenate(cols, axis=-1)
    a_term = jnp.concatenate([p[0] for p in gate_parts], axis=-1)
    c_gate = jnp.concatenate([p[1] for p in gate_parts], axis=-1)
    xp_ref[0:CONV_HALO, :] = xp_ref[tm:tm + CONV_HALO, :]

    x1_ref[...] = _mix_tail(x, c, a_term, c_gate, lncg_ref, lncb_ref, w_co_ref, w_out_ref, ln1g_ref, ln1b_ref)


def _mix_sample_kernel(x_ref, attn_ref, hist_ref, w_glu_ref, w_g_ref, w_ao_ref, w_dw_ref, b_dw_ref, lncg_ref,
                       lncb_ref, w_co_ref, w_out_ref, ln1g_ref, ln1b_ref, x1_ref, glu_ref):
    nb = hist_ref.shape[1]
    t = x_ref.shape[0] // nb
    n_hist = CONV_WIDTH - 1
    x = x_ref[...]
    xb = x.astype(BF16)
    glu = _glu(xb, w_glu_ref)
    glu_ref[...] = glu

    def slab(m):
        return hist_ref[m] if m < n_hist else glu[(m - n_hist) * nb:(m - n_hist + 1) * nb, :]

    outs = []
    for ti in range(t):
        acc = jnp.broadcast_to(b_dw_ref[...], (nb, C_CONV))
        for j in range(CONV_WIDTH):
            acc = acc + w_dw_ref[j:j + 1, :] * slab(ti + j)
        outs.append(acc)
    c = jnp.concatenate(outs, axis=0)
    a_term, c_gate = _gates(xb, attn_ref[...], w_g_ref, w_ao_ref)
    x1_ref[...] = _mix_tail(x, c, a_term, c_gate, lncg_ref, lncb_ref, w_co_ref, w_out_ref, ln1g_ref, ln1b_ref)


def _mix_weight_specs():
    return [
        _const_spec((2 * C_CONV, D_MODEL)), _const_spec((2 * D_MODEL, D_MODEL)), _const_spec((Q_W, D_MODEL)),
        _const_spec((CONV_WIDTH, C_CONV)), _const_spec((1, C_CONV)), _const_spec((1, C_CONV)),
        _const_spec((1, C_CONV)), _const_spec((C_CONV, D_MODEL)), _const_spec((D_MODEL, D_MODEL)),
        _const_spec((1, D_MODEL)), _const_spec((1, D_MODEL)),
    ]


def _mix_prompt(x, attn, mix_w, tm):
    b, t, _ = x.shape
    rows = lambda w: pl.BlockSpec((None, tm, w), lambda bi, i: (bi, i, 0))
    return pl.pallas_call(
        _mix_prompt_kernel,
        grid=(b, t // tm),
        in_specs=[rows(D_MODEL), rows(Q_W)] + _mix_weight_specs(),
        out_specs=[rows(D_MODEL), pl.BlockSpec((None, CONV_HALO, C_CONV), lambda bi, i: (bi, 0, 0))],
        out_shape=[jax.ShapeDtypeStruct((b, t, D_MODEL), F32), jax.ShapeDtypeStruct((b, CONV_HALO, C_CONV), F32)],
        scratch_shapes=[pltpu.VMEM((CONV_HALO + tm, C_CONV), F32)],
        compiler_params=_params(("parallel", "arbitrary")),
        name="mix_prompt",
    )(x, attn, *mix_w)


def _mix_sample(x_tm, attn_tm, hist_tm, mix_w):
    n = x_tm.shape[0]
    return pl.pallas_call(
        _mix_sample_kernel,
        grid=(1,),
        in_specs=[_const_spec((n, D_MODEL)), _const_spec((n, Q_W)), _const_spec(hist_tm.shape)] + _mix_weight_specs(),
        out_specs=[_whole_spec((n, D_MODEL)), _whole_spec((n, C_CONV))],
        out_shape=[jax.ShapeDtypeStruct((n, D_MODEL), F32), jax.ShapeDtypeStruct((n, C_CONV), F32)],
        compiler_params=_params(("arbitrary",)),
        name="mix_sample",
    )(x_tm, attn_tm, hist_tm, *mix_w)


def _ffn_tail(x1, uc, gate, w_down_ref, ln2g_ref, ln2b_ref):
    f = _dot((jax.nn.gelu(uc) * gate).astype(BF16), w_down_ref[...])
    return _layer_norm(ALPHA * x1 + f, ln2g_ref[...], ln2b_ref[...])


def _ffn_prompt_kernel(x1_ref, w_up_ref, w_gate_ref, w_dw_ref, b_dw_ref, w_down_ref, ln2g_ref, ln2b_ref,
                       y_ref, tail_ref, up_ref):
    tm = x1_ref.shape[0]

    @pl.when(pl.program_id(1) == 0)
    def _():
        up_ref[0:FFN_HALO, :] = jnp.zeros((FFN_HALO, D_FF), F32)

    x1 = x1_ref[...]
    x1b = x1.astype(BF16)
    u = _dot(x1b, w_up_ref[...])
    gate = _dot(x1b, w_gate_ref[...])
    up_ref[FFN_HALO:FFN_HALO + tm, :] = u
    tail_ref[...] = u[tm - FFN_HALO:, :]
    first = FFN_HALO - (FFN_CONV_WIDTH - 1)
    uc = b_dw_ref[...]
    for j in range(FFN_CONV_WIDTH):
        uc = uc + w_dw_ref[j:j + 1, :] * up_ref[first + j:first + j + tm, :]
    up_ref[0:FFN_HALO, :] = up_ref[tm:tm + FFN_HALO, :]
    y_ref[...] = _ffn_tail(x1, uc, gate, w_down_ref, ln2g_ref, ln2b_ref)


def _ffn_sample_kernel(x1_ref, hist_ref, w_up_ref, w_gate_ref, w_dw_ref, b_dw_ref, w_down_ref, ln2g_ref, ln2b_ref,
                       y_ref, u_ref):
    nb = hist_ref.shape[1]
    t = x1_ref.shape[0] // nb
    n_hist = FFN_CONV_WIDTH - 1
    x1 = x1_ref[...]
    x1b = x1.astype(BF16)
    u = _dot(x1b, w_up_ref[...])
    gate = _dot(x1b, w_gate_ref[...])
    u_ref[...] = u

    def slab(m):
        return hist_ref[m] if m < n_hist else u[(m - n_hist) * nb:(m - n_hist + 1) * nb, :]

    outs = []
    for ti in range(t):
        acc = jnp.broadcast_to(b_dw_ref[...], (nb, D_FF))
        for j in range(FFN_CONV_WIDTH):
            acc = acc + w_dw_ref[j:j + 1, :] * slab(ti + j)
        outs.append(acc)
    uc = jnp.concatenate(outs, axis=0)
    y_ref[...] = _ffn_tail(x1, uc, gate, w_down_ref, ln2g_ref, ln2b_ref)


def _ffn_weight_specs():
    return [
        _const_spec((D_MODEL, D_FF)), _const_spec((D_MODEL, D_FF)), _const_spec((FFN_CONV_WIDTH, D_FF)),
        _const_spec((1, D_FF)), _const_spec((D_FF, D_MODEL)), _const_spec((1, D_MODEL)), _const_spec((1, D_MODEL)),
    ]


def _ffn_prompt(x1, ffn_w, tm):
    b, t, _ = x1.shape
    rows = pl.BlockSpec((None, tm, D_MODEL), lambda bi, i: (bi, i, 0))
    return pl.pallas_call(
        _ffn_prompt_kernel,
        grid=(b, t // tm),
        in_specs=[rows] + _ffn_weight_specs(),
        out_specs=[rows, pl.BlockSpec((None, FFN_HALO, D_FF), lambda bi, i: (bi, 0, 0))],
        out_shape=[jax.ShapeDtypeStruct((b, t, D_MODEL), F32), jax.ShapeDtypeStruct((b, FFN_HALO, D_FF), F32)],
        scratch_shapes=[pltpu.VMEM((FFN_HALO + tm, D_FF), F32)],
        compiler_params=_params(("parallel", "arbitrary")),
        name="ffn_prompt",
    )(x1, *ffn_w)


def _ffn_sample(x1_tm, hist_tm, ffn_w):
    n = x1_tm.shape[0]
    return pl.pallas_call(
        _ffn_sample_kernel,
        grid=(1,),
        in_specs=[_const_spec((n, D_MODEL)), _const_spec(hist_tm.shape)] + _ffn_weight_specs(),
        out_specs=[_whole_spec((n, D_MODEL)), _whole_spec((n, D_FF))],
        out_shape=[jax.ShapeDtypeStruct((n, D_MODEL), F32), jax.ShapeDtypeStruct((n, D_FF), F32)],
        compiler_params=_params(("arbitrary",)),
        name="ffn_sample",
    )(x1_tm, hist_tm, *ffn_w)


def _token_major(a):
    b, t, w = a.shape
    return jnp.transpose(a, (1, 0, 2)).reshape(t * b, w)


def _batch_major(a, b):
    tb, w = a.shape
    return jnp.transpose(a.reshape(tb // b, b, w), (1, 0, 2))


def _stage_weights(w_in):
    wt = w_in.T
    o = 0
    w_q = wt[o:o + Q_W]; o += Q_W
    w_k = wt[o:o + KV_W]; o += KV_W
    w_v = wt[o:o + KV_W]; o += KV_W
    w_qi = wt[o:o + IQ_W]; o += IQ_W
    w_ki = wt[o:o + IDX_DIM]; o += IDX_DIM
    w_wi = wt[o:o + N_IDX_HEADS]; o += N_IDX_HEADS
    w_glu = wt[o:o + 2 * C_CONV]; o += 2 * C_CONV
    w_g = wt[o:o + 2 * D_MODEL]
    head = lambda h: w_q[h * HEAD_DIM:(h + 1) * HEAD_DIM]
    q_pairs = [w for i in range(GROUP) for w in (head(i), head(GROUP + i))]
    zero = jnp.zeros((LANES - IDX_DIM, D_MODEL), w_in.dtype)
    w_att = jnp.concatenate(q_pairs + [w_k, w_qi, w_ki, zero], axis=0).astype(BF16)
    w_t = jnp.concatenate([w_k, w_v, w_ki, w_wi], axis=0).astype(BF16)
    return w_att, w_t, w_glu.astype(BF16), w_g.astype(BF16)


def kernel(x_prompt, x_sample, cache_k, cache_v, cache_kidx, state_conv, state_ffn, page_table, w_in, w_attn_o, w_conv_dw, b_conv_dw, ln_conv_g, ln_conv_b, w_conv_o, w_out, ln1_g, ln1_b, w_ffn_up, w_ffn_gate, w_ffn_dw, b_ffn_dw, w_ffn_down, ln2_g, ln2_b):
    bp, tp, _ = x_prompt.shape
    bs, ts, _ = x_sample.shape
    n_pages = page_table.shape[1]
    past = n_pages * PAGE_SIZE
    assert N_KV_HEADS == 2 and N_IDX_HEADS == SUBLANES

    w_att, w_t, w_glu, w_g = _stage_weights(w_in)
    row2 = lambda a: a.reshape(1, -1)
    mix_w = (w_glu, w_g, w_attn_o.astype(BF16), w_conv_dw, row2(b_conv_dw), row2(ln_conv_g), row2(ln_conv_b),
             w_conv_o.astype(BF16), w_out.astype(BF16), row2(ln1_g), row2(ln1_b))
    ffn_w = (w_ffn_up.astype(BF16), w_ffn_gate.astype(BF16), w_ffn_dw, row2(b_ffn_dw), w_ffn_down.astype(BF16),
             row2(ln2_g), row2(ln2_b))

    qbd, qi, kb, kib, vt3, k_t, v_t, ki_t, wit = _proj(x_prompt, w_att, w_t, 512)
    seq = lambda a: a.reshape(bp, tp, a.shape[-1])
    attn_p = _prompt_attention(seq(qbd), seq(qi), wit, seq(kib), seq(kb), vt3, min(TOPK_MAX, tp // 4))
    x1_p, conv_tail = _mix_prompt(x_prompt, attn_p, mix_w, 512)
    y_p, ffn_tail = _ffn_prompt(x1_p, ffn_w, 256)
    heads_last = lambda a_t: jnp.transpose(a_t.reshape(bp, N_KV_HEADS, HEAD_DIM, tp), (0, 3, 1, 2))
    k_p = heads_last(k_t)
    v_p = heads_last(v_t)
    ki_p = jnp.transpose(ki_t, (0, 2, 1))
    conv_p = conv_tail[:, CONV_HALO - (CONV_WIDTH - 1):, :]
    ffn_p = ffn_tail[:, FFN_HALO - (FFN_CONV_WIDTH - 1):, :]

    n_s = bs * ts
    qbd_s, qis, _, _, _, ks_t, vs_t, kis_t, wit_s = _proj(x_sample.reshape(1, n_s, D_MODEL), w_att, w_t, n_s)
    ks_t, vs_t, kis_t = ks_t[0], vs_t[0], kis_t[0]
    pt_flat = page_table.reshape(-1).astype(jnp.int32)
    new_page = lambda a_t: jnp.pad(jnp.transpose(a_t.reshape(-1, bs, ts), (1, 0, 2)),
                                   ((0, 0), (0, 0), (0, PAGE_SIZE - ts)))
    qi32 = qis.reshape(bs, ts * N_IDX_HEADS, IDX_DIM)
    w32 = wit_s.T.reshape(bs, ts * N_IDX_HEADS, 1)
    scores = _sample_scores(pt_flat, qi32, w32, jnp.transpose(cache_kidx, (0, 2, 1)), new_page(kis_t), n_pages)
    lk = scores.shape[-1]
    bias_t = _sample_select(scores.reshape(n_s, lk).T, past, ts, min(TOPK_MAX, (past + ts) // 4))
    q_bd = jnp.transpose(qbd_s.reshape(bs, ts, N_HEADS, KV_W), (0, 2, 1, 3)).reshape(bs, N_HEADS * ts, KV_W)
    as_pages = lambda a: jnp.transpose(a, (0, 2, 3, 1))
    kv_new = lambda a: new_page(a).reshape(bs, N_KV_HEADS, HEAD_DIM, PAGE_SIZE)
    o_bd = _sample_attend(pt_flat, q_bd, bias_t.T.reshape(bs, ts, lk), as_pages(cache_k), as_pages(cache_v),
                          kv_new(ks_t), kv_new(vs_t), n_pages)
    o6 = o_bd.reshape(bs, N_KV_HEADS, GROUP, ts, N_KV_HEADS, HEAD_DIM)
    o_sel = jnp.stack([o6[:, n, :, :, n, :] for n in range(N_KV_HEADS)], axis=1)
    attn_s = jnp.transpose(o_sel, (0, 3, 1, 2, 4)).reshape(bs, ts, Q_W)

    x1_s, glu_s = _mix_sample(_token_major(x_sample), _token_major(attn_s).astype(BF16),
                              jnp.transpose(state_conv, (1, 0, 2)), mix_w)
    y_s, u_s = _ffn_sample(x1_s, jnp.transpose(state_ffn, (1, 0, 2)), ffn_w)
    y_s = _batch_major(y_s, bs)
    conv_s = jnp.concatenate([state_conv, _batch_major(glu_s, bs)], axis=1)[:, -(CONV_WIDTH - 1):, :]
    ffn_s = jnp.concatenate([state_ffn, _batch_major(u_s, bs)], axis=1)[:, -(FFN_CONV_WIDTH - 1):, :]
    k_s = ks_t.T.reshape(bs, ts, N_KV_HEADS, HEAD_DIM)
    v_s = vs_t.T.reshape(bs, ts, N_KV_HEADS, HEAD_DIM)
    ki_s = kis_t.T.reshape(bs, ts, IDX_DIM)

    return (y_p, y_s, k_p, v_p, ki_p, conv_p, ffn_p, k_s, v_s, ki_s, conv_s, ffn_s)
```

```python
import functools

import jax
import jax.numpy as jnp
from jax import lax
from jax.experimental import pallas as pl
from jax.experimental.pallas import tpu as pltpu

D_MODEL = 1024
N_HEADS = 8
HEAD_DIM = 64
N_KV_HEADS = 2
N_IDX_HEADS = 8
IDX_DIM = 64
TOPK_MAX = 256
C_CONV = D_MODEL // 2
CONV_WIDTH = 31
D_FF = 2816
FFN_CONV_WIDTH = 3
LN_EPS = 1e-5
DEPTH = 1
ALPHA = (2.0 * DEPTH) ** 0.25
PAGE_SIZE = 128

Q_W = N_HEADS * HEAD_DIM
KV_W = N_KV_HEADS * HEAD_DIM
IQ_W = N_IDX_HEADS * IDX_DIM
GROUP = N_HEADS // N_KV_HEADS

LANES = 128
SUBLANES = 8
QBD_W = N_HEADS * KV_W
PROJ_W = Q_W + KV_W + IQ_W + LANES
PROJ_T = 2 * KV_W + IDX_DIM + N_IDX_HEADS
TQ = 256
TQ_S = 256
SCORE_GROUP = 8
ATTN_GROUP = 4
CONV_HALO = 32
FFN_HALO = 8
N_BISECT = 19
VT_ROWS = HEAD_DIM + 16
LOG2E = 1.4426950408889634
VMEM_LIMIT = 56 * 1024 * 1024

F32 = jnp.float32
BF16 = jnp.bfloat16
NEG_INF = float("-inf")
POS_INF = float("inf")


def _dot(a, b):
    return jnp.dot(a, b, preferred_element_type=F32)


def _dot_nt(a, b):
    return lax.dot_general(a, b, (((1,), (1,)), ((), ())), preferred_element_type=F32)


def _layer_norm(x, g, b):
    mu = jnp.mean(x, axis=-1, keepdims=True)
    xc = x - mu
    var = jnp.mean(xc * xc, axis=-1, keepdims=True)
    return xc * lax.rsqrt(var + LN_EPS) * g + b


def _params(sem):
    return pltpu.CompilerParams(dimension_semantics=sem, vmem_limit_bytes=VMEM_LIMIT)


def _whole_spec(shape):
    nd = len(shape)
    return pl.BlockSpec(shape, lambda *_: (0,) * nd)


def _const_spec(shape):
    nd = len(shape)
    return pl.BlockSpec(shape, lambda *_: (0,) * nd, pipeline_mode=pl.Buffered(1))


def _proj_kernel(x_ref, w_ref, wt_ref, qbd_ref, qi_ref, kb_ref, kib_ref, vt_ref, kt_ref, vtf_ref, kit_ref, wit_ref):
    xb = x_ref[...].astype(BF16)
    o = 0
    qp = _dot_nt(xb, w_ref[o:o + Q_W, :]); o += Q_W
    k = _dot_nt(xb, w_ref[o:o + KV_W, :]); o += KV_W
    qi = _dot_nt(xb, w_ref[o:o + IQ_W, :]); o += IQ_W
    ki = _dot_nt(xb, w_ref[o:o + LANES, :])[:, :IDX_DIM]
    qp = (qp * (HEAD_DIM ** -0.5 * LOG2E)).astype(BF16)
    low = lax.broadcasted_iota(jnp.int32, (qp.shape[0], KV_W), 1) < HEAD_DIM
    for h in range(N_HEADS):
        pair = qp[:, (h % GROUP) * KV_W:(h % GROUP + 1) * KV_W]
        qbd_ref[:, h * KV_W:(h + 1) * KV_W] = jnp.where(low if h < GROUP else jnp.logical_not(low), pair, 0.0)
    qi_ref[...] = qi.astype(BF16)
    kb_ref[...] = k.astype(BF16)
    kib_ref[...] = ki.astype(BF16)
    t = _dot_nt(wt_ref[...], xb)
    kt_ref[...] = t[:KV_W, :]
    vtf_ref[...] = t[KV_W:2 * KV_W, :]
    kit_ref[...] = t[2 * KV_W:2 * KV_W + IDX_DIM, :]
    wit_ref[...] = t[2 * KV_W + IDX_DIM:, :] * ((IDX_DIM ** -0.5) * (N_IDX_HEADS ** -0.5))
    vt = t[KV_W:2 * KV_W, :].astype(BF16)
    ones = jnp.ones((VT_ROWS - HEAD_DIM, TQ), BF16)
    for j in range(vt_ref.shape[0]):
        for n in range(N_KV_HEADS):
            vt_ref[j, n * VT_ROWS:n * VT_ROWS + HEAD_DIM, :] = vt[n * HEAD_DIM:(n + 1) * HEAD_DIM, j * TQ:(j + 1) * TQ]
            vt_ref[j, n * VT_ROWS + HEAD_DIM:(n + 1) * VT_ROWS, :] = ones


def _proj(x, w_att, w_t, tm):
    b, t, _ = x.shape
    n = b * t
    tiles = t // tm
    row = lambda w: pl.BlockSpec((tm, w), lambda i: (i, 0))
    col = lambda r: pl.BlockSpec((None, r, tm), lambda i: (i // tiles, 0, i % tiles))
    outs = [(QBD_W, BF16), (IQ_W, BF16), (KV_W, BF16), (IDX_DIM, BF16)]
    return pl.pallas_call(
        _proj_kernel,
        grid=(n // tm,),
        in_specs=[row(D_MODEL), _const_spec((PROJ_W, D_MODEL)), _const_spec((PROJ_T, D_MODEL))],
        out_specs=[row(w) for w, _ in outs]
        + [pl.BlockSpec((tm // TQ, N_KV_HEADS * VT_ROWS, TQ), lambda i: (i, 0, 0)), col(KV_W), col(KV_W), col(IDX_DIM),
           pl.BlockSpec((SUBLANES, tm), lambda i: (0, i))],
        out_shape=[jax.ShapeDtypeStruct((n, w), dt) for w, dt in outs]
        + [jax.ShapeDtypeStruct((n // TQ, N_KV_HEADS * VT_ROWS, TQ), BF16), jax.ShapeDtypeStruct((b, KV_W, t), F32),
           jax.ShapeDtypeStruct((b, KV_W, t), F32), jax.ShapeDtypeStruct((b, IDX_DIM, t), F32),
           jax.ShapeDtypeStruct((SUBLANES, n), F32)],
        compiler_params=_params(("parallel",)),
        name="proj",
    )(x.reshape(n, D_MODEL), w_att, w_t)


def _chunk(ref, c, ch):
    return ref[pl.ds(pl.multiple_of(c * ch, ch), ch), :]


def _fold_rows(x, op):
    parts = [x[r:r + SUBLANES, :] for r in range(0, x.shape[0], SUBLANES)]
    accs = parts[:4]
    for i, part in enumerate(parts[4:]):
        accs[i % 4] = op(accs[i % 4], part)
    while len(accs) > 1:
        accs = [op(accs[a], accs[a + 1]) for a in range(0, len(accs) - 1, 2)] + ([accs[-1]] if len(accs) % 2 else [])
    return accs[0]


def _reduce_keys(s_ref, nk, ch, init, f, op, red):
    w = s_ref.shape[1]

    def body(c, acc):
        return op(acc, _fold_rows(f(_chunk(s_ref, c, ch), c), op))

    acc = lax.fori_loop(0, nk, body, jnp.full((SUBLANES, w), init, F32))
    return red(acc, axis=0, keepdims=True)


def _select_bias(s_ref, bias_ref, nk, ch, topk):
    w = s_ref.shape[1]
    kf = float(topk)
    count = lambda pred: _reduce_keys(s_ref, nk, ch, 0.0, lambda x, c: jnp.where(pred(x, c), 1.0, 0.0),
                                      jnp.add, jnp.sum)
    min_above = lambda t: _reduce_keys(s_ref, nk, ch, POS_INF, lambda x, c: jnp.where(x > t, x, POS_INF),
                                       jnp.minimum, jnp.min)

    neg = jnp.full((1, w), NEG_INF, F32)

    hi = _reduce_keys(s_ref, nk, ch, NEG_INF, lambda x, c: x, jnp.maximum, jnp.max)
    lo_fin = min_above(neg)
    n_adm = count(lambda x, _: x > neg)

    def bisect(_, st):
        lo, lo_fin, hi, n_lo = st
        mid = 0.5 * lo_fin + 0.5 * hi
        c = count(lambda x, _: x > mid)
        ok = c >= kf
        return jnp.where(ok, mid, lo), jnp.where(ok, mid, lo_fin), jnp.where(ok, hi, mid), jnp.where(ok, c, n_lo)

    lo, _, _, n_lo = lax.fori_loop(0, N_BISECT, bisect, (neg, lo_fin, hi, n_adm))

    def peel(st):
        lo, thr, n_gt, done = st
        v = min_above(lo)
        c = count(lambda x, _: x > v)
        found = c < kf
        newly = jnp.logical_and(done < 0.5, found)
        return (jnp.where(jnp.logical_or(found, done > 0.5), lo, v), jnp.where(newly, v, thr),
                jnp.where(newly, c, n_gt), jnp.where(found, 1.0, done))

    _, thr, n_gt, _ = lax.while_loop(lambda st: jnp.min(st[3]) < 0.5, peel,
                                     (lo, lo, n_lo, jnp.where(n_lo <= kf, 1.0, 0.0)))

    need = jnp.where(thr == neg, 0.0, kf - n_gt)
    any_copies = jnp.max(need) > 0.0

    @pl.when(any_copies)
    def _():
        tri = (lax.broadcasted_iota(jnp.int32, (ch, ch), 1)
               <= lax.broadcasted_iota(jnp.int32, (ch, ch), 0)).astype(BF16)

        def write(c, seen):
            x = _chunk(s_ref, c, ch)
            eq = x == thr
            rank = seen + _dot(tri, jnp.where(eq, 1.0, 0.0).astype(BF16))
            sel = jnp.logical_or(x > thr, jnp.logical_and(eq, rank <= need))
            bias_ref[pl.ds(pl.multiple_of(c * ch, ch), ch), :] = jnp.where(sel, 0.0, NEG_INF)
            return rank[ch - 1:ch, :]

        lax.fori_loop(0, nk, write, jnp.zeros((1, w), F32))

    @pl.when(jnp.logical_not(any_copies))
    def _():
        def write(c, carry):
            sel = _chunk(s_ref, c, ch) > thr
            bias_ref[pl.ds(pl.multiple_of(c * ch, ch), ch), :] = jnp.where(sel, 0.0, NEG_INF)
            return carry

        lax.fori_loop(0, nk, write, 0)


def _prompt_attn_kernel(qbd_ref, qi_ref, wit_ref, kib_ref, kb_ref, vt_ref, o_ref, s_ref, bias_ref, m_ref, acc_ref,
                        lg_ref, *, topk):
    i = pl.program_id(1)
    nk = i + 1
    qi = qi_ref[...]
    wit = wit_ref[...]
    q_pos = lax.broadcasted_iota(jnp.int32, (TQ, TQ), 1) + i * TQ
    k_off = lax.broadcasted_iota(jnp.int32, (TQ, TQ), 0)

    def score(c, carry):
        kc = _chunk(kib_ref, c, TQ)
        s = jnp.zeros((TQ, TQ), F32)
        for h in range(N_IDX_HEADS):
            z = _dot_nt(kc, qi[:, h * IDX_DIM:(h + 1) * IDX_DIM])
            s = s + jnp.maximum(z, 0.0) * wit[h:h + 1, :]
        s_ref[pl.ds(pl.multiple_of(c * TQ, TQ), TQ), :] = jnp.where(k_off + c * TQ <= q_pos, s, NEG_INF)
        return carry

    lax.fori_loop(0, nk, score, 0)
    _select_bias(s_ref, bias_ref, nk, TQ, topk)

    m_ref[...] = jnp.full(m_ref.shape, NEG_INF, F32)
    acc_ref[...] = jnp.zeros(acc_ref.shape, F32)

    def attend(c, carry):
        rows = pl.ds(pl.multiple_of(c * TQ, TQ), TQ)
        mx = []
        for h in range(N_HEADS):
            lg = _dot_nt(kb_ref[rows, :], qbd_ref[:, h * KV_W:(h + 1) * KV_W]) + bias_ref[rows, :]
            lg_ref[h] = lg
            mx.append(_fold_rows(lg, jnp.maximum))
        m_all = m_ref[...]
        m_rows = []
        for h in range(N_HEADS):
            n = h // GROUP
            out = slice(h * VT_ROWS, (h + 1) * VT_ROWS)
            m_old = m_all[h:h + 1, :]
            m_new = jnp.maximum(m_old, jnp.max(mx[h], axis=0, keepdims=True))
            m_rows.append(m_new)
            m_use = jnp.where(m_new == NEG_INF, 0.0, m_new)
            p = jnp.exp2(lg_ref[h] - m_use).astype(BF16)
            vt = vt_ref[c, pl.ds(n * VT_ROWS, VT_ROWS), :]
            acc_ref[out, :] = jnp.exp2(m_old - m_use) * acc_ref[out, :] + _dot(vt, p)
        m_ref[...] = jnp.concatenate(m_rows, axis=0)
        return carry

    lax.fori_loop(0, nk, attend, 0)
    outs = [acc_ref[h * VT_ROWS:h * VT_ROWS + HEAD_DIM, :] / acc_ref[h * VT_ROWS + HEAD_DIM:h * VT_ROWS + HEAD_DIM + 1, :]
            for h in range(N_HEADS)]
    o_ref[...] = jnp.concatenate(outs, axis=0).T.astype(o_ref.dtype)


def _prompt_attention(qbd, qi, wit, kib, kb, vt3, topk):
    b, t, _ = qbd.shape
    nblk = t // TQ
    qblk = lambda w: pl.BlockSpec((None, TQ, w), lambda bi, i: (bi, i, 0))
    full = lambda w: pl.BlockSpec((None, t, w), lambda bi, i: (bi, 0, 0))
    return pl.pallas_call(
        functools.partial(_prompt_attn_kernel, topk=topk),
        grid=(b, nblk),
        in_specs=[qblk(QBD_W), qblk(IQ_W), pl.BlockSpec((SUBLANES, TQ), lambda bi, i: (0, bi * nblk + i)),
                  full(IDX_DIM), full(KV_W), pl.BlockSpec((nblk, N_KV_HEADS * VT_ROWS, TQ), lambda bi, i: (bi, 0, 0))],
        out_specs=qblk(Q_W),
        out_shape=jax.ShapeDtypeStruct((b, t, Q_W), BF16),
        scratch_shapes=[pltpu.VMEM((t, TQ), F32), pltpu.VMEM((t, TQ), F32), pltpu.VMEM((N_HEADS, TQ), F32),
                        pltpu.VMEM((N_HEADS * VT_ROWS, TQ), F32), pltpu.VMEM((N_HEADS, TQ, TQ), F32)],
        compiler_params=_params(("parallel", "arbitrary")),
        name="prompt_attn",
    )(qbd, qi, wit, kib, kb, vt3)


def _group_spec(group, *block):
    nd = len(block)
    return pl.BlockSpec((group,) + block, lambda bi, pt: (bi,) + (0,) * nd)


def _new_key_block(q, k_new):
    qf = q.astype(F32)
    kf = k_new.astype(BF16).astype(F32)
    lane = lax.broadcasted_iota(jnp.int32, (q.shape[0], PAGE_SIZE), 1)
    blk = jnp.zeros((q.shape[0], PAGE_SIZE), F32)
    for j in range(k_new.shape[0]):
        blk = jnp.where(lane == j, jnp.sum(qf * kf[j:j + 1, :], axis=-1, keepdims=True), blk)
    return blk


def _sample_score_kernel(pt_ref, qi_ref, w_ref, new_ref, kidx_hbm, o_ref, buf, sem, *, n_pages):
    group = qi_ref.shape[0]
    slot = _prefetch_pages(*_page_fetcher(pt_ref, (kidx_hbm,), (buf,), sem, group * n_pages))
    t = qi_ref.shape[1] // N_IDX_HEADS
    for g in range(group):
        qi = qi_ref[g]
        w = w_ref[g]
        blocks = [_dot(qi, buf[slot, g * n_pages + j].astype(BF16)) for j in range(n_pages)]
        blocks.append(_new_key_block(qi, new_ref[g]))
        for j, s in enumerate(blocks):
            s = jnp.maximum(s, 0.0) * w
            o_ref[g, :, j * PAGE_SIZE:(j + 1) * PAGE_SIZE] = jnp.sum(s.reshape(t, N_IDX_HEADS, PAGE_SIZE), axis=1)


def _sample_scores(page_table_flat, qi32, w32, kidx_t, ki_new_t, n_pages):
    nb, rows, _ = qi32.shape
    t = rows // N_IDX_HEADS
    lk = (n_pages + 1) * PAGE_SIZE
    g = SCORE_GROUP
    return pl.pallas_call(
        functools.partial(_sample_score_kernel, n_pages=n_pages),
        grid_spec=pltpu.PrefetchScalarGridSpec(
            num_scalar_prefetch=1,
            grid=(nb // g,),
            in_specs=[_group_spec(g, rows, IDX_DIM), _group_spec(g, rows, 1), _group_spec(g, t, IDX_DIM),
                      pl.BlockSpec(memory_space=pl.ANY)],
            out_specs=_group_spec(g, t, lk),
            scratch_shapes=[pltpu.VMEM((2, g * n_pages, IDX_DIM, PAGE_SIZE), F32), pltpu.SemaphoreType.DMA((1, 2))],
        ),
        out_shape=jax.ShapeDtypeStruct((nb, t, lk), F32),
        compiler_params=_params(("arbitrary",)),
        name="sample_scores",
    )(page_table_flat, qi32, w32, ki_new_t, kidx_t)


def _sample_select_kernel(s_ref, bias_ref, sm_ref, bt_ref, *, past, t, topk):
    lk = s_ref.shape[1]
    nk = lk // PAGE_SIZE
    qcol = lax.broadcasted_iota(jnp.int32, (PAGE_SIZE, TQ_S), 1)
    qpos = past + (qcol & (t - 1))
    krow = lax.broadcasted_iota(jnp.int32, (PAGE_SIZE, TQ_S), 0)
    for c in range(nk):
        cols = slice(c * PAGE_SIZE, (c + 1) * PAGE_SIZE)
        sm_ref[cols, :] = jnp.where(krow + c * PAGE_SIZE <= qpos, s_ref[:, cols].T, NEG_INF)
    _select_bias(sm_ref, bt_ref, nk, PAGE_SIZE, topk)
    for c in range(nk):
        cols = slice(c * PAGE_SIZE, (c + 1) * PAGE_SIZE)
        bias_ref[:, cols] = bt_ref[cols, :].T


def _sample_select(scores, past, t, topk):
    n, lk = scores.shape
    assert t & (t - 1) == 0 and TQ_S % t == 0, "token index is taken from the low bits of the query index"
    blk = pl.BlockSpec((TQ_S, lk), lambda i: (i, 0))
    return pl.pallas_call(
        functools.partial(_sample_select_kernel, past=past, t=t, topk=topk),
        grid=(n // TQ_S,),
        in_specs=[blk],
        out_specs=blk,
        out_shape=jax.ShapeDtypeStruct((n, lk), F32),
        scratch_shapes=[pltpu.VMEM((lk, TQ_S), F32), pltpu.VMEM((lk, TQ_S), F32)],
        compiler_params=_params(("parallel",)),
        name="sample_select",
    )(scores)


def _page_fetcher(pt_ref, srcs, bufs, sem, n_copy):
    def copy(a, slot, i, page):
        return pltpu.make_async_copy(srcs[a].at[page], bufs[a].at[slot, i], sem.at[a, slot])

    def start(step, slot):
        def body(i, carry):
            page = pt_ref[step * n_copy + i]
            for a in range(len(srcs)):
                copy(a, slot, i, page).start()
            return carry
        lax.fori_loop(0, n_copy, body, 0)

    def wait(slot):
        def body(i, carry):
            for a in range(len(srcs)):
                copy(a, slot, i, 0).wait()
            return carry
        lax.fori_loop(0, n_copy, body, 0)

    return start, wait


def _prefetch_pages(start, wait):
    s = pl.program_id(0)
    slot = s % 2

    @pl.when(s == 0)
    def _():
        start(0, 0)

    @pl.when(s + 1 < pl.num_programs(0))
    def _():
        start(s + 1, 1 - slot)

    wait(slot)
    return slot


def _sample_attn_kernel(pt_ref, q_ref, bias_ref, k_new, v_new, k_hbm, v_hbm, o_ref, kbuf, vbuf, sem, *, n_pages):
    group = q_ref.shape[0]
    slot = _prefetch_pages(*_page_fetcher(pt_ref, (k_hbm, v_hbm), (kbuf, vbuf), sem, group * n_pages))
    kt = lambda page: page.reshape(KV_W, PAGE_SIZE).astype(BF16)
    for g in range(group):
        qb = q_ref[g]
        reps = qb.shape[0] // bias_ref.shape[1]
        bias = jnp.concatenate([bias_ref[g]] * reps, axis=0)
        blocks = [_dot(qb, kt(kbuf[slot, g * n_pages + j])) for j in range(n_pages)]
        blocks.append(_new_key_block(qb, k_new[g]))
        lg = jnp.concatenate(blocks, axis=-1) + bias
        m = jnp.max(lg, axis=-1, keepdims=True)
        p = jnp.exp2(lg - m)
        l = jnp.sum(p, axis=-1, keepdims=True)
        pb = p.astype(BF16)
        acc = jnp.zeros(o_ref.shape[1:], F32)
        for j in range(n_pages):
            acc = acc + _dot_nt(pb[:, j * PAGE_SIZE:(j + 1) * PAGE_SIZE], kt(vbuf[slot, g * n_pages + j]))
        p_new = pb[:, n_pages * PAGE_SIZE:].astype(F32)
        v_rows = v_new[g].astype(BF16).astype(F32)
        for j in range(v_rows.shape[0]):
            acc = acc + p_new[:, j:j + 1] * v_rows[j:j + 1, :]
        o_ref[g] = acc / l


def _sample_attend(page_table_flat, q_bd, bias, k_t, v_t, k_new_t, v_new_t, n_pages):
    nb, rows, _ = q_bd.shape
    t = bias.shape[1]
    lk = bias.shape[2]
    g = ATTN_GROUP
    page = (N_KV_HEADS, HEAD_DIM, PAGE_SIZE)
    hbm = pl.BlockSpec(memory_space=pl.ANY)
    page_buf = pltpu.VMEM((2, g * n_pages) + page, F32)
    return pl.pallas_call(
        functools.partial(_sample_attn_kernel, n_pages=n_pages),
        grid_spec=pltpu.PrefetchScalarGridSpec(
            num_scalar_prefetch=1,
            grid=(nb // g,),
            in_specs=[_group_spec(g, rows, KV_W), _group_spec(g, t, lk), _group_spec(g, t, KV_W), _group_spec(g, t, KV_W),
                      hbm, hbm],
            out_specs=_group_spec(g, rows, KV_W),
            scratch_shapes=[page_buf, page_buf, pltpu.SemaphoreType.DMA((2, 2))],
        ),
        out_shape=jax.ShapeDtypeStruct((nb, rows, KV_W), F32),
        compiler_params=_params(("arbitrary",)),
        name="sample_attn",
    )(page_table_flat, q_bd, bias, k_new_t, v_new_t, k_t, v_t)


def _glu(xb, w_glu_ref):
    gi = _dot_nt(xb, w_glu_ref[...])
    return gi[:, :C_CONV] * jax.nn.sigmoid(gi[:, C_CONV:])


def _gates(xb, attn_b, w_g_ref, w_ao_ref, cols=slice(0, D_MODEL)):
    gate_cols = slice(D_MODEL + cols.start, D_MODEL + cols.stop)
    a_term = jax.nn.sigmoid(_dot_nt(xb, w_g_ref[cols, :])) * _dot(attn_b, w_ao_ref[:, cols])
    return a_term, jax.nn.sigmoid(_dot_nt(xb, w_g_ref[gate_cols, :]))


def _mix_tail(x, c, a_term, c_gate, lncg_ref, lncb_ref, w_co_ref, w_out_ref, ln1g_ref, ln1b_ref):
    cn = _layer_norm(c, lncg_ref[...], lncb_ref[...])
    c_branch = _dot((cn * jax.nn.sigmoid(cn)).astype(BF16), w_co_ref[...])
    merged = a_term + c_gate * c_branch
    h = ALPHA * x + _dot(merged.astype(BF16), w_out_ref[...])
    return _layer_norm(h, ln1g_ref[...], ln1b_ref[...])


def _mix_prompt_kernel(x_ref, attn_ref, w_glu_ref, w_g_ref, w_ao_ref, w_dw_ref, b_dw_ref, lncg_ref, lncb_ref,
                       w_co_ref, w_out_ref, ln1g_ref, ln1b_ref, x1_ref, tail_ref, xp_ref):
    tm = x_ref.shape[0]

    @pl.when(pl.program_id(1) == 0)
    def _():
        xp_ref[0:CONV_HALO, :] = jnp.zeros((CONV_HALO, C_CONV), F32)

    x = x_ref[...]
    xb = x.astype(BF16)
    glu = _glu(xb, w_glu_ref)
    xp_ref[CONV_HALO:CONV_HALO + tm, :] = glu
    tail_ref[...] = glu[tm - CONV_HALO:, :]
    attn_b = attn_ref[...]

    first = CONV_HALO - (CONV_WIDTH - 1)
    rb = 128
    n_blk = C_CONV // LANES
    cols, gate_parts = [], []
    for c0 in range(0, C_CONV, LANES):
        lanes = slice(c0, c0 + LANES)
        q = c0 // LANES
        gate_parts.append(_gates(xb, attn_b, w_g_ref, w_ao_ref, slice(q * D_MODEL // n_blk, (q + 1) * D_MODEL // n_blk)))
        blocks = []
        for r0 in range(0, tm, rb):
            y = jnp.broadcast_to(b_dw_ref[:, lanes], (rb, LANES))
            for b in range(SUBLANES):
                rows = rb + (SUBLANES if b else 0)
                part = None
                for j in range(CONV_WIDTH):
                    if (first + j) % SUBLANES == b:
                        a0 = first + j - b + r0
                        term = w_dw_ref[j:j + 1, lanes] * xp_ref[a0:a0 + rows, lanes]
                        part = term if part is None else part + term
                y = y + part[b:b + rb, :]
            blocks.append(y)
        cols.append(jnp.concatenate(blocks, axis=0))
    c = jnp.concatenate(cols, axis=-1)
    a_term = jnp.concatenate([p[0] for p in gate_parts], axis=-1)
    c_gate = jnp.concatenate([p[1] for p in gate_parts], axis=-1)
    xp_ref[0:CONV_HALO, :] = xp_ref[tm:tm + CONV_HALO, :]

    x1_ref[...] = _mix_tail(x, c, a_term, c_gate, lncg_ref, lncb_ref, w_co_ref, w_out_ref, ln1g_ref, ln1b_ref)


def _mix_sample_kernel(x_ref, attn_ref, hist_ref, w_glu_ref, w_g_ref, w_ao_ref, w_dw_ref, b_dw_ref, lncg_ref,
                       lncb_ref, w_co_ref, w_out_ref, ln1g_ref, ln1b_ref, x1_ref, glu_ref):
    nb = hist_ref.shape[1]
    t = x_ref.shape[0] // nb
    n_hist = CONV_WIDTH - 1
    x = x_ref[...]
    xb = x.astype(BF16)
    glu = _glu(xb, w_glu_ref)
    glu_ref[...] = glu

    def slab(m):
        return hist_ref[m] if m < n_hist else glu[(m - n_hist) * nb:(m - n_hist + 1) * nb, :]

    outs = []
    for ti in range(t):
        acc = jnp.broadcast_to(b_dw_ref[...], (nb, C_CONV))
        for j in range(CONV_WIDTH):
            acc = acc + w_dw_ref[j:j + 1, :] * slab(ti + j)
        outs.append(acc)
    c = jnp.concatenate(outs, axis=0)
    a_term, c_gate = _gates(xb, attn_ref[...], w_g_ref, w_ao_ref)
    x1_ref[...] = _mix_tail(x, c, a_term, c_gate, lncg_ref, lncb_ref, w_co_ref, w_out_ref, ln1g_ref, ln1b_ref)


def _mix_weight_specs():
    return [
        _const_spec((2 * C_CONV, D_MODEL)), _const_spec((2 * D_MODEL, D_MODEL)), _const_spec((Q_W, D_MODEL)),
        _const_spec((CONV_WIDTH, C_CONV)), _const_spec((1, C_CONV)), _const_spec((1, C_CONV)),
        _const_spec((1, C_CONV)), _const_spec((C_CONV, D_MODEL)), _const_spec((D_MODEL, D_MODEL)),
        _const_spec((1, D_MODEL)), _const_spec((1, D_MODEL)),
    ]


def _mix_prompt(x, attn, mix_w, tm):
    b, t, _ = x.shape
    rows = lambda w: pl.BlockSpec((None, tm, w), lambda bi, i: (bi, i, 0))
    return pl.pallas_call(
        _mix_prompt_kernel,
        grid=(b, t // tm),
        in_specs=[rows(D_MODEL), rows(Q_W)] + _mix_weight_specs(),
        out_specs=[rows(D_MODEL), pl.BlockSpec((None, CONV_HALO, C_CONV), lambda bi, i: (bi, 0, 0))],
        out_shape=[jax.ShapeDtypeStruct((b, t, D_MODEL), F32), jax.ShapeDtypeStruct((b, CONV_HALO, C_CONV), F32)],
        scratch_shapes=[pltpu.VMEM((CONV_HALO + tm, C_CONV), F32)],
        compiler_params=_params(("parallel", "arbitrary")),
        name="mix_prompt",
    )(x, attn, *mix_w)


def _mix_sample(x_tm, attn_tm, hist_tm, mix_w):
    n = x_tm.shape[0]
    return pl.pallas_call(
        _mix_sample_kernel,
        grid=(1,),
        in_specs=[_const_spec((n, D_MODEL)), _const_spec((n, Q_W)), _const_spec(hist_tm.shape)] + _mix_weight_specs(),
        out_specs=[_whole_spec((n, D_MODEL)), _whole_spec((n, C_CONV))],
        out_shape=[jax.ShapeDtypeStruct((n, D_MODEL), F32), jax.ShapeDtypeStruct((n, C_CONV), F32)],
        compiler_params=_params(("arbitrary",)),
        name="mix_sample",
    )(x_tm, attn_tm, hist_tm, *mix_w)


def _ffn_tail(x1, uc, gate, w_down_ref, ln2g_ref, ln2b_ref):
    f = _dot((jax.nn.gelu(uc) * gate).astype(BF16), w_down_ref[...])
    return _layer_norm(ALPHA * x1 + f, ln2g_ref[...], ln2b_ref[...])


def _ffn_prompt_kernel(x1_ref, w_up_ref, w_gate_ref, w_dw_ref, b_dw_ref, w_down_ref, ln2g_ref, ln2b_ref,
                       y_ref, tail_ref, up_ref):
    tm = x1_ref.shape[0]

    @pl.when(pl.program_id(1) == 0)
    def _():
        up_ref[0:FFN_HALO, :] = jnp.zeros((FFN_HALO, D_FF), F32)

    x1 = x1_ref[...]
    x1b = x1.astype(BF16)
    u = _dot(x1b, w_up_ref[...])
    gate = _dot(x1b, w_gate_ref[...])
    up_ref[FFN_HALO:FFN_HALO + tm, :] = u
    tail_ref[...] = u[tm - FFN_HALO:, :]
    first = FFN_HALO - (FFN_CONV_WIDTH - 1)
    uc = b_dw_ref[...]
    for j in range(FFN_CONV_WIDTH):
        uc = uc + w_dw_ref[j:j + 1, :] * up_ref[first + j:first + j + tm, :]
    up_ref[0:FFN_HALO, :] = up_ref[tm:tm + FFN_HALO, :]
    y_ref[...] = _ffn_tail(x1, uc, gate, w_down_ref, ln2g_ref, ln2b_ref)


def _ffn_sample_kernel(x1_ref, hist_ref, w_up_ref, w_gate_ref, w_dw_ref, b_dw_ref, w_down_ref, ln2g_ref, ln2b_ref,
                       y_ref, u_ref):
    nb = hist_ref.shape[1]
    t = x1_ref.shape[0] // nb
    n_hist = FFN_CONV_WIDTH - 1
    x1 = x1_ref[...]
    x1b = x1.astype(BF16)
    u = _dot(x1b, w_up_ref[...])
    gate = _dot(x1b, w_gate_ref[...])
    u_ref[...] = u

    def slab(m):
        return hist_ref[m] if m < n_hist else u[(m - n_hist) * nb:(m - n_hist + 1) * nb, :]

    outs = []
    for ti in range(t):
        acc = jnp.broadcast_to(b_dw_ref[...], (nb, D_FF))
        for j in range(FFN_CONV_WIDTH):
            acc = acc + w_dw_ref[j:j + 1, :] * slab(ti + j)
        outs.append(acc)
    uc = jnp.concatenate(outs, axis=0)
    y_ref[...] = _ffn_tail(x1, uc, gate, w_down_ref, ln2g_ref, ln2b_ref)


def _ffn_weight_specs():
    return [
        _const_spec((D_MODEL, D_FF)), _const_spec((D_MODEL, D_FF)), _const_spec((FFN_CONV_WIDTH, D_FF)),
        _const_spec((1, D_FF)), _const_spec((D_FF, D_MODEL)), _const_spec((1, D_MODEL)), _const_spec((1, D_MODEL)),
    ]


def _ffn_prompt(x1, ffn_w, tm):
    b, t, _ = x1.shape
    rows = pl.BlockSpec((None, tm, D_MODEL), lambda bi, i: (bi, i, 0))
    return pl.pallas_call(
        _ffn_prompt_kernel,
        grid=(b, t // tm),
        in_specs=[rows] + _ffn_weight_specs(),
        out_specs=[rows, pl.BlockSpec((None, FFN_HALO, D_FF), lambda bi, i: (bi, 0, 0))],
        out_shape=[jax.ShapeDtypeStruct((b, t, D_MODEL), F32), jax.ShapeDtypeStruct((b, FFN_HALO, D_FF), F32)],
        scratch_shapes=[pltpu.VMEM((FFN_HALO + tm, D_FF), F32)],
        compiler_params=_params(("parallel", "arbitrary")),
        name="ffn_prompt",
    )(x1, *ffn_w)


def _ffn_sample(x1_tm, hist_tm, ffn_w):
    n = x1_tm.shape[0]
    return pl.pallas_call(
        _ffn_sample_kernel,
        grid=(1,),
        in_specs=[_const_spec((n, D_MODEL)), _const_spec(hist_tm.shape)] + _ffn_weight_specs(),
        out_specs=[_whole_spec((n, D_MODEL)), _whole_spec((n, D_FF))],
        out_shape=[jax.ShapeDtypeStruct((n, D_MODEL), F32), jax.ShapeDtypeStruct((n, D_FF), F32)],
        compiler_params=_params(("arbitrary",)),
        name="ffn_sample",
    )(x1_tm, hist_tm, *ffn_w)


def _token_major(a):
    b, t, w = a.shape
    return jnp.transpose(a, (1, 0, 2)).reshape(t * b, w)


def _batch_major(a, b):
    tb, w = a.shape
    return jnp.transpose(a.reshape(tb // b, b, w), (1, 0, 2))


def _stage_weights(w_in):
    wt = w_in.T
    o = 0
    w_q = wt[o:o + Q_W]; o += Q_W
    w_k = wt[o:o + KV_W]; o += KV_W
    w_v = wt[o:o + KV_W]; o += KV_W
    w_qi = wt[o:o + IQ_W]; o += IQ_W
    w_ki = wt[o:o + IDX_DIM]; o += IDX_DIM
    w_wi = wt[o:o + N_IDX_HEADS]; o += N_IDX_HEADS
    w_glu = wt[o:o + 2 * C_CONV]; o += 2 * C_CONV
    w_g = wt[o:o + 2 * D_MODEL]
    head = lambda h: w_q[h * HEAD_DIM:(h + 1) * HEAD_DIM]
    q_pairs = [w for i in range(GROUP) for w in (head(i), head(GROUP + i))]
    zero = jnp.zeros((LANES - IDX_DIM, D_MODEL), w_in.dtype)
    w_att = jnp.concatenate(q_pairs + [w_k, w_qi, w_ki, zero], axis=0).astype(BF16)
    w_t = jnp.concatenate([w_k, w_v, w_ki, w_wi], axis=0).astype(BF16)
    return w_att, w_t, w_glu.astype(BF16), w_g.astype(BF16)


def kernel(x_prompt, x_sample, cache_k, cache_v, cache_kidx, state_conv, state_ffn, page_table, w_in, w_attn_o, w_conv_dw, b_conv_dw, ln_conv_g, ln_conv_b, w_conv_o, w_out, ln1_g, ln1_b, w_ffn_up, w_ffn_gate, w_ffn_dw, b_ffn_dw, w_ffn_down, ln2_g, ln2_b):
    bp, tp, _ = x_prompt.shape
    bs, ts, _ = x_sample.shape
    n_pages = page_table.shape[1]
    past = n_pages * PAGE_SIZE
    assert N_KV_HEADS == 2 and N_IDX_HEADS == SUBLANES

    w_att, w_t, w_glu, w_g = _stage_weights(w_in)
    row2 = lambda a: a.reshape(1, -1)
    mix_w = (w_glu, w_g, w_attn_o.astype(BF16), w_conv_dw, row2(b_conv_dw), row2(ln_conv_g), row2(ln_conv_b),
             w_conv_o.astype(BF16), w_out.astype(BF16), row2(ln1_g), row2(ln1_b))
    ffn_w = (w_ffn_up.astype(BF16), w_ffn_gate.astype(BF16), w_ffn_dw, row2(b_ffn_dw), w_ffn_down.astype(BF16),
             row2(ln2_g), row2(ln2_b))

    qbd, qi, kb, kib, vt3, k_t, v_t, ki_t, wit = _proj(x_prompt, w_att, w_t, 512)
    seq = lambda a: a.reshape(bp, tp, a.shape[-1])
    attn_p = _prompt_attention(seq(qbd), seq(qi), wit, seq(kib), seq(kb), vt3, min(TOPK_MAX, tp // 4))
    x1_p, conv_tail = _mix_prompt(x_prompt, attn_p, mix_w, 512)
    y_p, ffn_tail = _ffn_prompt(x1_p, ffn_w, 512)
    heads_last = lambda a_t: jnp.transpose(a_t.reshape(bp, N_KV_HEADS, HEAD_DIM, tp), (0, 3, 1, 2))
    k_p = heads_last(k_t)
    v_p = heads_last(v_t)
    ki_p = jnp.transpose(ki_t, (0, 2, 1))
    conv_p = conv_tail[:, CONV_HALO - (CONV_WIDTH - 1):, :]
    ffn_p = ffn_tail[:, FFN_HALO - (FFN_CONV_WIDTH - 1):, :]

    n_s = bs * ts
    qbd_s, qis, _, _, _, ks_t, vs_t, kis_t, wit_s = _proj(x_sample.reshape(1, n_s, D_MODEL), w_att, w_t, n_s)
    ks_t, vs_t, kis_t = ks_t[0], vs_t[0], kis_t[0]
    pt_flat = page_table.reshape(-1).astype(jnp.int32)
    new_rows = lambda a_t: a_t.T.reshape(bs, ts, -1)
    qi32 = qis.reshape(bs, ts * N_IDX_HEADS, IDX_DIM)
    w32 = wit_s.T.reshape(bs, ts * N_IDX_HEADS, 1)
    scores = _sample_scores(pt_flat, qi32, w32, jnp.transpose(cache_kidx, (0, 2, 1)), new_rows(kis_t), n_pages)
    lk = scores.shape[-1]
    bias = _sample_select(scores.reshape(n_s, lk), past, ts, min(TOPK_MAX, (past + ts) // 4))
    q_bd = jnp.transpose(qbd_s.reshape(bs, ts, N_HEADS, KV_W), (0, 2, 1, 3)).reshape(bs, N_HEADS * ts, KV_W)
    as_pages = lambda a: jnp.transpose(a, (0, 2, 3, 1))
    o_bd = _sample_attend(pt_flat, q_bd, bias.reshape(bs, ts, lk), as_pages(cache_k), as_pages(cache_v),
                          new_rows(ks_t), new_rows(vs_t), n_pages)
    o6 = o_bd.reshape(bs, N_KV_HEADS, GROUP, ts, N_KV_HEADS, HEAD_DIM)
    o_sel = jnp.stack([o6[:, n, :, :, n, :] for n in range(N_KV_HEADS)], axis=1)
    attn_s = jnp.transpose(o_sel, (0, 3, 1, 2, 4)).reshape(bs, ts, Q_W)

    x1_s, glu_s = _mix_sample(_token_major(x_sample), _token_major(attn_s).astype(BF16),
                              jnp.transpose(state_conv, (1, 0, 2)), mix_w)
    y_s, u_s = _ffn_sample(x1_s, jnp.transpose(state_ffn, (1, 0, 2)), ffn_w)
    y_s = _batch_major(y_s, bs)
    conv_s = jnp.concatenate([state_conv, _batch_major(glu_s, bs)], axis=1)[:, -(CONV_WIDTH - 1):, :]
    ffn_s = jnp.concatenate([state_ffn, _batch_major(u_s, bs)], axis=1)[:, -(FFN_CONV_WIDTH - 1):, :]
    k_s = new_rows(ks_t).reshape(bs, ts, N_KV_HEADS, HEAD_DIM)
    v_s = new_rows(vs_t).reshape(bs, ts, N_KV_HEADS, HEAD_DIM)
    ki_s = new_rows(kis_t)

    return (y_p, y_s, k_p, v_p, ki_p, conv_p, ffn_p, k_s, v_s, ki_s, conv_s, ffn_s)
```

```python
import functools

import jax
import jax.numpy as jnp
from jax import lax
from jax.experimental import pallas as pl
from jax.experimental.pallas import tpu as pltpu

D_MODEL = 1024
N_HEADS = 8
HEAD_DIM = 64
N_KV_HEADS = 2
N_IDX_HEADS = 8
IDX_DIM = 64
TOPK_MAX = 256
C_CONV = D_MODEL // 2
CONV_WIDTH = 31
D_FF = 2816
FFN_CONV_WIDTH = 3
LN_EPS = 1e-5
DEPTH = 1
ALPHA = (2.0 * DEPTH) ** 0.25
PAGE_SIZE = 128

Q_W = N_HEADS * HEAD_DIM
KV_W = N_KV_HEADS * HEAD_DIM
IQ_W = N_IDX_HEADS * IDX_DIM
GROUP = N_HEADS // N_KV_HEADS

LANES = 128
SUBLANES = 8
QBD_W = N_HEADS * KV_W
PROJ_W = Q_W + KV_W + IQ_W + LANES
PROJ_T = 2 * KV_W + IDX_DIM + N_IDX_HEADS
TQ = 256
TQ_S = 256
SCORE_GROUP = 8
ATTN_GROUP = 4
CONV_HALO = 32
FFN_HALO = 8
N_BISECT = 19
VT_ROWS = HEAD_DIM + 16
LOG2E = 1.4426950408889634
VMEM_LIMIT = 56 * 1024 * 1024

F32 = jnp.float32
BF16 = jnp.bfloat16
NEG_INF = float("-inf")
POS_INF = float("inf")


def _dot(a, b):
    return jnp.dot(a, b, preferred_element_type=F32)


def _dot_nt(a, b):
    return lax.dot_general(a, b, (((1,), (1,)), ((), ())), preferred_element_type=F32)


def _sigmoid(x):
    return 0.5 * jnp.tanh(0.5 * x) + 0.5


def _layer_norm(x, g, b):
    mu = jnp.mean(x, axis=-1, keepdims=True)
    xc = x - mu
    var = jnp.mean(xc * xc, axis=-1, keepdims=True)
    return xc * lax.rsqrt(var + LN_EPS) * g + b


def _params(sem):
    return pltpu.CompilerParams(dimension_semantics=sem, vmem_limit_bytes=VMEM_LIMIT)


def _whole_spec(shape):
    nd = len(shape)
    return pl.BlockSpec(shape, lambda *_: (0,) * nd)


def _const_spec(shape):
    nd = len(shape)
    return pl.BlockSpec(shape, lambda *_: (0,) * nd, pipeline_mode=pl.Buffered(1))


def _proj_kernel(x_ref, w_ref, wt_ref, qbd_ref, qi_ref, kb_ref, kib_ref, vt_ref, kt_ref, vtf_ref, kit_ref, wit_ref):
    xb = x_ref[...].astype(BF16)
    o = 0
    qp = _dot_nt(xb, w_ref[o:o + Q_W, :]); o += Q_W
    k = _dot_nt(xb, w_ref[o:o + KV_W, :]); o += KV_W
    qi = _dot_nt(xb, w_ref[o:o + IQ_W, :]); o += IQ_W
    ki = _dot_nt(xb, w_ref[o:o + LANES, :])[:, :IDX_DIM]
    qp = (qp * (HEAD_DIM ** -0.5 * LOG2E)).astype(BF16)
    low = lax.broadcasted_iota(jnp.int32, (qp.shape[0], KV_W), 1) < HEAD_DIM
    for h in range(N_HEADS):
        pair = qp[:, (h % GROUP) * KV_W:(h % GROUP + 1) * KV_W]
        qbd_ref[:, h * KV_W:(h + 1) * KV_W] = jnp.where(low if h < GROUP else jnp.logical_not(low), pair, 0.0)
    qi_ref[...] = qi.astype(BF16)
    kb_ref[...] = k.astype(BF16)
    kib_ref[...] = ki.astype(BF16)
    t = _dot_nt(wt_ref[...], xb)
    kt_ref[...] = t[:KV_W, :]
    vtf_ref[...] = t[KV_W:2 * KV_W, :]
    kit_ref[...] = t[2 * KV_W:2 * KV_W + IDX_DIM, :]
    wit_ref[...] = t[2 * KV_W + IDX_DIM:, :] * ((IDX_DIM ** -0.5) * (N_IDX_HEADS ** -0.5))
    vt = t[KV_W:2 * KV_W, :].astype(BF16)
    ones = jnp.ones((VT_ROWS - HEAD_DIM, TQ), BF16)
    for j in range(vt_ref.shape[0]):
        for n in range(N_KV_HEADS):
            vt_ref[j, n * VT_ROWS:n * VT_ROWS + HEAD_DIM, :] = vt[n * HEAD_DIM:(n + 1) * HEAD_DIM, j * TQ:(j + 1) * TQ]
            vt_ref[j, n * VT_ROWS + HEAD_DIM:(n + 1) * VT_ROWS, :] = ones


def _proj(x, w_att, w_t, tm):
    b, t, _ = x.shape
    n = b * t
    tiles = t // tm
    row = lambda w: pl.BlockSpec((tm, w), lambda i: (i, 0))
    col = lambda r: pl.BlockSpec((None, r, tm), lambda i: (i // tiles, 0, i % tiles))
    outs = [(QBD_W, BF16), (IQ_W, BF16), (KV_W, BF16), (IDX_DIM, BF16)]
    return pl.pallas_call(
        _proj_kernel,
        grid=(n // tm,),
        in_specs=[row(D_MODEL), _const_spec((PROJ_W, D_MODEL)), _const_spec((PROJ_T, D_MODEL))],
        out_specs=[row(w) for w, _ in outs]
        + [pl.BlockSpec((tm // TQ, N_KV_HEADS * VT_ROWS, TQ), lambda i: (i, 0, 0)), col(KV_W), col(KV_W), col(IDX_DIM),
           pl.BlockSpec((SUBLANES, tm), lambda i: (0, i))],
        out_shape=[jax.ShapeDtypeStruct((n, w), dt) for w, dt in outs]
        + [jax.ShapeDtypeStruct((n // TQ, N_KV_HEADS * VT_ROWS, TQ), BF16), jax.ShapeDtypeStruct((b, KV_W, t), F32),
           jax.ShapeDtypeStruct((b, KV_W, t), F32), jax.ShapeDtypeStruct((b, IDX_DIM, t), F32),
           jax.ShapeDtypeStruct((SUBLANES, n), F32)],
        compiler_params=_params(("parallel",)),
        name="proj",
    )(x.reshape(n, D_MODEL), w_att, w_t)


def _chunk(ref, c, ch):
    return ref[pl.ds(pl.multiple_of(c * ch, ch), ch), :]


def _fold_rows(x, op):
    parts = [x[r:r + SUBLANES, :] for r in range(0, x.shape[0], SUBLANES)]
    accs = parts[:4]
    for i, part in enumerate(parts[4:]):
        accs[i % 4] = op(accs[i % 4], part)
    while len(accs) > 1:
        accs = [op(accs[a], accs[a + 1]) for a in range(0, len(accs) - 1, 2)] + ([accs[-1]] if len(accs) % 2 else [])
    return accs[0]


def _reduce_keys(s_ref, nk, ch, init, f, op, red):
    w = s_ref.shape[1]

    def body(c, acc):
        return op(acc, _fold_rows(f(_chunk(s_ref, c, ch), c), op))

    acc = lax.fori_loop(0, nk, body, jnp.full((SUBLANES, w), init, F32))
    return red(acc, axis=0, keepdims=True)


def _select_bias(s_ref, bias_ref, nk, ch, topk):
    w = s_ref.shape[1]
    kf = float(topk)
    count = lambda pred: _reduce_keys(s_ref, nk, ch, 0.0, lambda x, c: jnp.where(pred(x, c), 1.0, 0.0),
                                      jnp.add, jnp.sum)
    min_above = lambda t: _reduce_keys(s_ref, nk, ch, POS_INF, lambda x, c: jnp.where(x > t, x, POS_INF),
                                       jnp.minimum, jnp.min)

    neg = jnp.full((1, w), NEG_INF, F32)

    hi = _reduce_keys(s_ref, nk, ch, NEG_INF, lambda x, c: x, jnp.maximum, jnp.max)
    lo_fin = min_above(neg)
    n_adm = count(lambda x, _: x > neg)

    def bisect(_, st):
        lo, lo_fin, hi, n_lo = st
        mid = 0.5 * lo_fin + 0.5 * hi
        c = count(lambda x, _: x > mid)
        ok = c >= kf
        return jnp.where(ok, mid, lo), jnp.where(ok, mid, lo_fin), jnp.where(ok, hi, mid), jnp.where(ok, c, n_lo)

    lo, _, _, n_lo = lax.fori_loop(0, N_BISECT, bisect, (neg, lo_fin, hi, n_adm))

    def peel(st):
        lo, thr, n_gt, done = st
        v = min_above(lo)
        c = count(lambda x, _: x > v)
        found = c < kf
        newly = jnp.logical_and(done < 0.5, found)
        return (jnp.where(jnp.logical_or(found, done > 0.5), lo, v), jnp.where(newly, v, thr),
                jnp.where(newly, c, n_gt), jnp.where(found, 1.0, done))

    _, thr, n_gt, _ = lax.while_loop(lambda st: jnp.min(st[3]) < 0.5, peel,
                                     (lo, lo, n_lo, jnp.where(n_lo <= kf, 1.0, 0.0)))

    need = jnp.where(thr == neg, 0.0, kf - n_gt)
    any_copies = jnp.max(need) > 0.0

    @pl.when(any_copies)
    def _():
        tri = (lax.broadcasted_iota(jnp.int32, (ch, ch), 1)
               <= lax.broadcasted_iota(jnp.int32, (ch, ch), 0)).astype(BF16)

        def write(c, seen):
            x = _chunk(s_ref, c, ch)
            eq = x == thr
            rank = seen + _dot(tri, jnp.where(eq, 1.0, 0.0).astype(BF16))
            sel = jnp.logical_or(x > thr, jnp.logical_and(eq, rank <= need))
            bias_ref[pl.ds(pl.multiple_of(c * ch, ch), ch), :] = jnp.where(sel, 0.0, NEG_INF)
            return rank[ch - 1:ch, :]

        lax.fori_loop(0, nk, write, jnp.zeros((1, w), F32))

    @pl.when(jnp.logical_not(any_copies))
    def _():
        def write(c, carry):
            sel = _chunk(s_ref, c, ch) > thr
            bias_ref[pl.ds(pl.multiple_of(c * ch, ch), ch), :] = jnp.where(sel, 0.0, NEG_INF)
            return carry

        lax.fori_loop(0, nk, write, 0)


def _prompt_attn_kernel(qbd_ref, qi_ref, wit_ref, kib_ref, kb_ref, vt_ref, o_ref, s_ref, bias_ref, m_ref, acc_ref,
                        lg_ref, *, topk):
    i = pl.program_id(1)
    nk = i + 1
    qi = qi_ref[...]
    wit = wit_ref[...]
    q_pos = lax.broadcasted_iota(jnp.int32, (TQ, TQ), 1) + i * TQ
    k_off = lax.broadcasted_iota(jnp.int32, (TQ, TQ), 0)

    def score(c, carry):
        kc = _chunk(kib_ref, c, TQ)
        s = jnp.zeros((TQ, TQ), F32)
        for h in range(N_IDX_HEADS):
            z = _dot_nt(kc, qi[:, h * IDX_DIM:(h + 1) * IDX_DIM])
            s = s + jnp.maximum(z, 0.0) * wit[h:h + 1, :]
        s_ref[pl.ds(pl.multiple_of(c * TQ, TQ), TQ), :] = jnp.where(k_off + c * TQ <= q_pos, s, NEG_INF)
        return carry

    lax.fori_loop(0, nk, score, 0)
    _select_bias(s_ref, bias_ref, nk, TQ, topk)

    m_ref[...] = jnp.full(m_ref.shape, NEG_INF, F32)
    acc_ref[...] = jnp.zeros(acc_ref.shape, F32)

    def attend(c, carry):
        rows = pl.ds(pl.multiple_of(c * TQ, TQ), TQ)
        mx = []
        for h in range(N_HEADS):
            lg = _dot_nt(kb_ref[rows, :], qbd_ref[:, h * KV_W:(h + 1) * KV_W]) + bias_ref[rows, :]
            lg_ref[h] = lg
            mx.append(_fold_rows(lg, jnp.maximum))
        m_all = m_ref[...]
        m_rows = []
        for h in range(N_HEADS):
            n = h // GROUP
            out = slice(h * VT_ROWS, (h + 1) * VT_ROWS)
            m_old = m_all[h:h + 1, :]
            m_new = jnp.maximum(m_old, jnp.max(mx[h], axis=0, keepdims=True))
            m_rows.append(m_new)
            m_use = jnp.where(m_new == NEG_INF, 0.0, m_new)
            p = jnp.exp2(lg_ref[h] - m_use).astype(BF16)
            vt = vt_ref[c, pl.ds(n * VT_ROWS, VT_ROWS), :]
            acc_ref[out, :] = jnp.exp2(m_old - m_use) * acc_ref[out, :] + _dot(vt, p)
        m_ref[...] = jnp.concatenate(m_rows, axis=0)
        return carry

    lax.fori_loop(0, nk, attend, 0)
    outs = [acc_ref[h * VT_ROWS:h * VT_ROWS + HEAD_DIM, :] / acc_ref[h * VT_ROWS + HEAD_DIM:h * VT_ROWS + HEAD_DIM + 1, :]
            for h in range(N_HEADS)]
    o_ref[...] = jnp.concatenate(outs, axis=0).T.astype(o_ref.dtype)


def _prompt_attention(qbd, qi, wit, kib, kb, vt3, topk):
    b, t, _ = qbd.shape
    nblk = t // TQ
    qblk = lambda w: pl.BlockSpec((None, TQ, w), lambda bi, i: (bi, i, 0))
    full = lambda w: pl.BlockSpec((None, t, w), lambda bi, i: (bi, 0, 0))
    return pl.pallas_call(
        functools.partial(_prompt_attn_kernel, topk=topk),
        grid=(b, nblk),
        in_specs=[qblk(QBD_W), qblk(IQ_W), pl.BlockSpec((SUBLANES, TQ), lambda bi, i: (0, bi * nblk + i)),
                  full(IDX_DIM), full(KV_W), pl.BlockSpec((nblk, N_KV_HEADS * VT_ROWS, TQ), lambda bi, i: (bi, 0, 0))],
        out_specs=qblk(Q_W),
        out_shape=jax.ShapeDtypeStruct((b, t, Q_W), BF16),
        scratch_shapes=[pltpu.VMEM((t, TQ), F32), pltpu.VMEM((t, TQ), F32), pltpu.VMEM((N_HEADS, TQ), F32),
                        pltpu.VMEM((N_HEADS * VT_ROWS, TQ), F32), pltpu.VMEM((N_HEADS, TQ, TQ), F32)],
        compiler_params=_params(("parallel", "arbitrary")),
        name="prompt_attn",
    )(qbd, qi, wit, kib, kb, vt3)


def _group_spec(group, *block):
    nd = len(block)
    return pl.BlockSpec((group,) + block, lambda bi, pt: (bi,) + (0,) * nd)


def _new_key_block(q, k_new):
    qf = q.astype(F32)
    kf = k_new.astype(BF16).astype(F32)
    lane = lax.broadcasted_iota(jnp.int32, (q.shape[0], PAGE_SIZE), 1)
    blk = jnp.zeros((q.shape[0], PAGE_SIZE), F32)
    for j in range(k_new.shape[0]):
        blk = jnp.where(lane == j, jnp.sum(qf * kf[j:j + 1, :], axis=-1, keepdims=True), blk)
    return blk


def _sample_score_kernel(pt_ref, qi_ref, w_ref, new_ref, kidx_hbm, o_ref, buf, sem, *, n_pages):
    group = qi_ref.shape[0]
    slot = _prefetch_pages(*_page_fetcher(pt_ref, (kidx_hbm,), (buf,), sem, group * n_pages))
    t = qi_ref.shape[1] // N_IDX_HEADS
    for g in range(group):
        qi = qi_ref[g]
        w = w_ref[g]
        blocks = [_dot(qi, buf[slot, g * n_pages + j].astype(BF16)) for j in range(n_pages)]
        blocks.append(_new_key_block(qi, new_ref[g]))
        for j, s in enumerate(blocks):
            s = jnp.maximum(s, 0.0) * w
            o_ref[g * t:(g + 1) * t, j * PAGE_SIZE:(j + 1) * PAGE_SIZE] = jnp.sum(
                s.reshape(t, N_IDX_HEADS, PAGE_SIZE), axis=1)


def _sample_scores(page_table_flat, qi32, w32, kidx_t, ki_new_t, n_pages):
    nb, rows, _ = qi32.shape
    t = rows // N_IDX_HEADS
    lk = (n_pages + 1) * PAGE_SIZE
    g = SCORE_GROUP
    return pl.pallas_call(
        functools.partial(_sample_score_kernel, n_pages=n_pages),
        grid_spec=pltpu.PrefetchScalarGridSpec(
            num_scalar_prefetch=1,
            grid=(nb // g,),
            in_specs=[_group_spec(g, rows, IDX_DIM), _group_spec(g, rows, 1), _group_spec(g, t, IDX_DIM),
                      pl.BlockSpec(memory_space=pl.ANY)],
            out_specs=pl.BlockSpec((g * t, lk), lambda bi, pt: (bi, 0)),
            scratch_shapes=[pltpu.VMEM((2, g * n_pages, IDX_DIM, PAGE_SIZE), F32), pltpu.SemaphoreType.DMA((1, 2))],
        ),
        out_shape=jax.ShapeDtypeStruct((nb * t, lk), F32),
        compiler_params=_params(("arbitrary",)),
        name="sample_scores",
    )(page_table_flat, qi32, w32, ki_new_t, kidx_t)


def _sample_select_kernel(s_ref, bias_ref, sm_ref, bt_ref, *, past, t, topk):
    lk = s_ref.shape[1]
    nk = lk // PAGE_SIZE
    qcol = lax.broadcasted_iota(jnp.int32, (PAGE_SIZE, TQ_S), 1)
    qpos = past + (qcol & (t - 1))
    krow = lax.broadcasted_iota(jnp.int32, (PAGE_SIZE, TQ_S), 0)
    for c in range(nk):
        cols = slice(c * PAGE_SIZE, (c + 1) * PAGE_SIZE)
        sm_ref[cols, :] = jnp.where(krow + c * PAGE_SIZE <= qpos, s_ref[:, cols].T, NEG_INF)
    _select_bias(sm_ref, bt_ref, nk, PAGE_SIZE, topk)
    for c in range(nk):
        cols = slice(c * PAGE_SIZE, (c + 1) * PAGE_SIZE)
        bias_ref[:, cols] = bt_ref[cols, :].T


def _sample_select(scores, past, t, topk):
    n, lk = scores.shape
    assert t & (t - 1) == 0 and TQ_S % t == 0, "token index is taken from the low bits of the query index"
    blk = pl.BlockSpec((TQ_S, lk), lambda i: (i, 0))
    return pl.pallas_call(
        functools.partial(_sample_select_kernel, past=past, t=t, topk=topk),
        grid=(n // TQ_S,),
        in_specs=[blk],
        out_specs=blk,
        out_shape=jax.ShapeDtypeStruct((n, lk), F32),
        scratch_shapes=[pltpu.VMEM((lk, TQ_S), F32), pltpu.VMEM((lk, TQ_S), F32)],
        compiler_params=_params(("parallel",)),
        name="sample_select",
    )(scores)


def _page_fetcher(pt_ref, srcs, bufs, sem, n_copy):
    def copy(a, slot, i, page):
        return pltpu.make_async_copy(srcs[a].at[page], bufs[a].at[slot, i], sem.at[a, slot])

    def start(step, slot):
        def body(i, carry):
            page = pt_ref[step * n_copy + i]
            for a in range(len(srcs)):
                copy(a, slot, i, page).start()
            return carry
        lax.fori_loop(0, n_copy, body, 0)

    def wait(slot):
        def body(i, carry):
            for a in range(len(srcs)):
                copy(a, slot, i, 0).wait()
            return carry
        lax.fori_loop(0, n_copy, body, 0)

    return start, wait


def _prefetch_pages(start, wait):
    s = pl.program_id(0)
    slot = s % 2

    @pl.when(s == 0)
    def _():
        start(0, 0)

    @pl.when(s + 1 < pl.num_programs(0))
    def _():
        start(s + 1, 1 - slot)

    wait(slot)
    return slot


def _sample_attn_kernel(pt_ref, q_ref, bias_ref, k_new, v_new, k_hbm, v_hbm, o_ref, kbuf, vbuf, sem, *, n_pages):
    group = q_ref.shape[0]
    slot = _prefetch_pages(*_page_fetcher(pt_ref, (k_hbm, v_hbm), (kbuf, vbuf), sem, group * n_pages))
    kt = lambda page: page.reshape(KV_W, PAGE_SIZE).astype(BF16)
    t = k_new.shape[1]
    for g in range(group):
        qb = q_ref[g]
        bias = jnp.concatenate([bias_ref[g * t:(g + 1) * t, :]] * (qb.shape[0] // t), axis=0)
        blocks = [_dot(qb, kt(kbuf[slot, g * n_pages + j])) for j in range(n_pages)]
        blocks.append(_new_key_block(qb, k_new[g]))
        lg = jnp.concatenate(blocks, axis=-1) + bias
        m = jnp.max(lg, axis=-1, keepdims=True)
        p = jnp.exp2(lg - m)
        l = jnp.sum(p, axis=-1, keepdims=True)
        pb = p.astype(BF16)
        acc = jnp.zeros(o_ref.shape[1:], F32)
        for j in range(n_pages):
            acc = acc + _dot_nt(pb[:, j * PAGE_SIZE:(j + 1) * PAGE_SIZE], kt(vbuf[slot, g * n_pages + j]))
        p_new = pb[:, n_pages * PAGE_SIZE:].astype(F32)
        v_rows = v_new[g].astype(BF16).astype(F32)
        for j in range(v_rows.shape[0]):
            acc = acc + p_new[:, j:j + 1] * v_rows[j:j + 1, :]
        o_ref[g] = acc / l


def _sample_attend(page_table_flat, q_bd, bias, k_t, v_t, k_new_t, v_new_t, n_pages):
    nb, rows, _ = q_bd.shape
    lk = bias.shape[1]
    t = bias.shape[0] // nb
    g = ATTN_GROUP
    page = (N_KV_HEADS, HEAD_DIM, PAGE_SIZE)
    hbm = pl.BlockSpec(memory_space=pl.ANY)
    page_buf = pltpu.VMEM((2, g * n_pages) + page, F32)
    return pl.pallas_call(
        functools.partial(_sample_attn_kernel, n_pages=n_pages),
        grid_spec=pltpu.PrefetchScalarGridSpec(
            num_scalar_prefetch=1,
            grid=(nb // g,),
            in_specs=[_group_spec(g, rows, KV_W), pl.BlockSpec((g * t, lk), lambda bi, pt: (bi, 0)),
                      _group_spec(g, t, KV_W), _group_spec(g, t, KV_W), hbm, hbm],
            out_specs=_group_spec(g, rows, KV_W),
            scratch_shapes=[page_buf, page_buf, pltpu.SemaphoreType.DMA((2, 2))],
        ),
        out_shape=jax.ShapeDtypeStruct((nb, rows, KV_W), F32),
        compiler_params=_params(("arbitrary",)),
        name="sample_attn",
    )(page_table_flat, q_bd, bias, k_new_t, v_new_t, k_t, v_t)


def _glu(xb, w_glu_ref):
    gi = _dot_nt(xb, w_glu_ref[...])
    return gi[:, :C_CONV] * _sigmoid(gi[:, C_CONV:])


def _gates(xb, attn_b, w_g_ref, w_ao_ref, cols=slice(0, D_MODEL)):
    gate_cols = slice(D_MODEL + cols.start, D_MODEL + cols.stop)
    a_term = _sigmoid(_dot_nt(xb, w_g_ref[cols, :])) * _dot(attn_b, w_ao_ref[:, cols])
    return a_term, _sigmoid(_dot_nt(xb, w_g_ref[gate_cols, :]))


def _mix_tail(x, c, a_term, c_gate, lncg_ref, lncb_ref, w_co_ref, w_out_ref, ln1g_ref, ln1b_ref):
    cn = _layer_norm(c, lncg_ref[...], lncb_ref[...])
    c_branch = _dot((cn * _sigmoid(cn)).astype(BF16), w_co_ref[...])
    merged = a_term + c_gate * c_branch
    h = ALPHA * x + _dot(merged.astype(BF16), w_out_ref[...])
    return _layer_norm(h, ln1g_ref[...], ln1b_ref[...])


def _mix_prompt_kernel(x_ref, attn_ref, w_glu_ref, w_g_ref, w_ao_ref, w_dw_ref, b_dw_ref, lncg_ref, lncb_ref,
                       w_co_ref, w_out_ref, ln1g_ref, ln1b_ref, x1_ref, tail_ref, xp_ref):
    tm = x_ref.shape[0]

    @pl.when(pl.program_id(1) == 0)
    def _():
        xp_ref[0:CONV_HALO, :] = jnp.zeros((CONV_HALO, C_CONV), F32)

    x = x_ref[...]
    xb = x.astype(BF16)
    glu = _glu(xb, w_glu_ref)
    xp_ref[CONV_HALO:CONV_HALO + tm, :] = glu
    tail_ref[...] = glu[tm - CONV_HALO:, :]
    attn_b = attn_ref[...]

    first = CONV_HALO - (CONV_WIDTH - 1)
    rb = 128
    n_blk = C_CONV // LANES
    cols, gate_parts = [], []
    for c0 in range(0, C_CONV, LANES):
        lanes = slice(c0, c0 + LANES)
        q = c0 // LANES
        gate_parts.append(_gates(xb, attn_b, w_g_ref, w_ao_ref, slice(q * D_MODEL // n_blk, (q + 1) * D_MODEL // n_blk)))
        blocks = []
        for r0 in range(0, tm, rb):
            y = jnp.broadcast_to(b_dw_ref[:, lanes], (rb, LANES))
            for b in range(SUBLANES):
                rows = rb + (SUBLANES if b else 0)
                part = None
                for j in range(CONV_WIDTH):
                    if (first + j) % SUBLANES == b:
                        a0 = first + j - b + r0
                        term = w_dw_ref[j:j + 1, lanes] * xp_ref[a0:a0 + rows, lanes]
                        part = term if part is None else part + term
                y = y + part[b:b + rb, :]
            blocks.append(y)
        cols.append(jnp.concatenate(blocks, axis=0))
    c = jnp.concatenate(cols, axis=-1)
    a_term = jnp.concatenate([p[0] for p in gate_parts], axis=-1)
    c_gate = jnp.concatenate([p[1] for p in gate_parts], axis=-1)
    xp_ref[0:CONV_HALO, :] = xp_ref[tm:tm + CONV_HALO, :]

    x1_ref[...] = _mix_tail(x, c, a_term, c_gate, lncg_ref, lncb_ref, w_co_ref, w_out_ref, ln1g_ref, ln1b_ref)


def _mix_sample_kernel(x_ref, attn_ref, hist_ref, w_glu_ref, w_g_ref, w_ao_ref, w_dw_ref, b_dw_ref, lncg_ref,
                       lncb_ref, w_co_ref, w_out_ref, ln1g_ref, ln1b_ref, x1_ref, glu_ref):
    nb = hist_ref.shape[1]
    t = x_ref.shape[0] // nb
    n_hist = CONV_WIDTH - 1
    x = x_ref[...]
    xb = x.astype(BF16)
    glu = _glu(xb, w_glu_ref)
    glu_ref[...] = glu

    def slab(m):
        return hist_ref[m] if m < n_hist else glu[(m - n_hist) * nb:(m - n_hist + 1) * nb, :]

    outs = []
    for ti in range(t):
        acc = jnp.broadcast_to(b_dw_ref[...], (nb, C_CONV))
        for j in range(CONV_WIDTH):
            acc = acc + w_dw_ref[j:j + 1, :] * slab(ti + j)
        outs.append(acc)
    c = jnp.concatenate(outs, axis=0)
    a_term, c_gate = _gates(xb, attn_ref[...], w_g_ref, w_ao_ref)
    x1_ref[...] = _mix_tail(x, c, a_term, c_gate, lncg_ref, lncb_ref, w_co_ref, w_out_ref, ln1g_ref, ln1b_ref)


def _mix_weight_specs():
    return [
        _const_spec((2 * C_CONV, D_MODEL)), _const_spec((2 * D_MODEL, D_MODEL)), _const_spec((Q_W, D_MODEL)),
        _const_spec((CONV_WIDTH, C_CONV)), _const_spec((1, C_CONV)), _const_spec((1, C_CONV)),
        _const_spec((1, C_CONV)), _const_spec((C_CONV, D_MODEL)), _const_spec((D_MODEL, D_MODEL)),
        _const_spec((1, D_MODEL)), _const_spec((1, D_MODEL)),
    ]


def _mix_prompt(x, attn, mix_w, tm):
    b, t, _ = x.shape
    rows = lambda w: pl.BlockSpec((None, tm, w), lambda bi, i: (bi, i, 0))
    return pl.pallas_call(
        _mix_prompt_kernel,
        grid=(b, t // tm),
        in_specs=[rows(D_MODEL), rows(Q_W)] + _mix_weight_specs(),
        out_specs=[rows(D_MODEL), pl.BlockSpec((None, CONV_HALO, C_CONV), lambda bi, i: (bi, 0, 0))],
        out_shape=[jax.ShapeDtypeStruct((b, t, D_MODEL), F32), jax.ShapeDtypeStruct((b, CONV_HALO, C_CONV), F32)],
        scratch_shapes=[pltpu.VMEM((CONV_HALO + tm, C_CONV), F32)],
        compiler_params=_params(("parallel", "arbitrary")),
        name="mix_prompt",
    )(x, attn, *mix_w)


def _mix_sample(x_tm, attn_tm, hist_tm, mix_w):
    n = x_tm.shape[0]
    return pl.pallas_call(
        _mix_sample_kernel,
        grid=(1,),
        in_specs=[_const_spec((n, D_MODEL)), _const_spec((n, Q_W)), _const_spec(hist_tm.shape)] + _mix_weight_specs(),
        out_specs=[_whole_spec((n, D_MODEL)), _whole_spec((n, C_CONV))],
        out_shape=[jax.ShapeDtypeStruct((n, D_MODEL), F32), jax.ShapeDtypeStruct((n, C_CONV), F32)],
        compiler_params=_params(("arbitrary",)),
        name="mix_sample",
    )(x_tm, attn_tm, hist_tm, *mix_w)


def _ffn_tail(x1, uc, gate, w_down_ref, ln2g_ref, ln2b_ref):
    f = _dot((jax.nn.gelu(uc) * gate).astype(BF16), w_down_ref[...])
    return _layer_norm(ALPHA * x1 + f, ln2g_ref[...], ln2b_ref[...])


def _ffn_prompt_kernel(x1_ref, w_up_ref, w_gate_ref, w_dw_ref, b_dw_ref, w_down_ref, ln2g_ref, ln2b_ref,
                       y_ref, tail_ref, up_ref):
    tm = x1_ref.shape[0]

    @pl.when(pl.program_id(1) == 0)
    def _():
        up_ref[0:FFN_HALO, :] = jnp.zeros((FFN_HALO, D_FF), F32)

    x1 = x1_ref[...]
    x1b = x1.astype(BF16)
    u = _dot(x1b, w_up_ref[...])
    gate = _dot(x1b, w_gate_ref[...])
    up_ref[FFN_HALO:FFN_HALO + tm, :] = u
    tail_ref[...] = u[tm - FFN_HALO:, :]
    first = FFN_HALO - (FFN_CONV_WIDTH - 1)
    uc = b_dw_ref[...]
    for j in range(FFN_CONV_WIDTH):
        uc = uc + w_dw_ref[j:j + 1, :] * up_ref[first + j:first + j + tm, :]
    up_ref[0:FFN_HALO, :] = up_ref[tm:tm + FFN_HALO, :]
    y_ref[...] = _ffn_tail(x1, uc, gate, w_down_ref, ln2g_ref, ln2b_ref)


def _ffn_sample_kernel(x1_ref, hist_ref, w_up_ref, w_gate_ref, w_dw_ref, b_dw_ref, w_down_ref, ln2g_ref, ln2b_ref,
                       y_ref, u_ref):
    nb = hist_ref.shape[1]
    t = x1_ref.shape[0] // nb
    n_hist = FFN_CONV_WIDTH - 1
    x1 = x1_ref[...]
    x1b = x1.astype(BF16)
    u = _dot(x1b, w_up_ref[...])
    gate = _dot(x1b, w_gate_ref[...])
    u_ref[...] = u

    def slab(m):
        return hist_ref[m] if m < n_hist else u[(m - n_hist) * nb:(m - n_hist + 1) * nb, :]

    outs = []
    for ti in range(t):
        acc = jnp.broadcast_to(b_dw_ref[...], (nb, D_FF))
        for j in range(FFN_CONV_WIDTH):
            acc = acc + w_dw_ref[j:j + 1, :] * slab(ti + j)
        outs.append(acc)
    uc = jnp.concatenate(outs, axis=0)
    y_ref[...] = _ffn_tail(x1, uc, gate, w_down_ref, ln2g_ref, ln2b_ref)


def _ffn_weight_specs():
    return [
        _const_spec((D_MODEL, D_FF)), _const_spec((D_MODEL, D_FF)), _const_spec((FFN_CONV_WIDTH, D_FF)),
        _const_spec((1, D_FF)), _const_spec((D_FF, D_MODEL)), _const_spec((1, D_MODEL)), _const_spec((1, D_MODEL)),
    ]


def _ffn_prompt(x1, ffn_w, tm):
    b, t, _ = x1.shape
    rows = pl.BlockSpec((None, tm, D_MODEL), lambda bi, i: (bi, i, 0))
    return pl.pallas_call(
        _ffn_prompt_kernel,
        grid=(b, t // tm),
        in_specs=[rows] + _ffn_weight_specs(),
        out_specs=[rows, pl.BlockSpec((None, FFN_HALO, D_FF), lambda bi, i: (bi, 0, 0))],
        out_shape=[jax.ShapeDtypeStruct((b, t, D_MODEL), F32), jax.ShapeDtypeStruct((b, FFN_HALO, D_FF), F32)],
        scratch_shapes=[pltpu.VMEM((FFN_HALO + tm, D_FF), F32)],
        compiler_params=_params(("parallel", "arbitrary")),
        name="ffn_prompt",
    )(x1, *ffn_w)


def _ffn_sample(x1_tm, hist_tm, ffn_w):
    n = x1_tm.shape[0]
    return pl.pallas_call(
        _ffn_sample_kernel,
        grid=(1,),
        in_specs=[_const_spec((n, D_MODEL)), _const_spec(hist_tm.shape)] + _ffn_weight_specs(),
        out_specs=[_whole_spec((n, D_MODEL)), _whole_spec((n, D_FF))],
        out_shape=[jax.ShapeDtypeStruct((n, D_MODEL), F32), jax.ShapeDtypeStruct((n, D_FF), F32)],
        compiler_params=_params(("arbitrary",)),
        name="ffn_sample",
    )(x1_tm, hist_tm, *ffn_w)


def _token_major(a):
    b, t, w = a.shape
    return jnp.transpose(a, (1, 0, 2)).reshape(t * b, w)


def _batch_major(a, b):
    tb, w = a.shape
    return jnp.transpose(a.reshape(tb // b, b, w), (1, 0, 2))


def _stage_weights(w_in):
    wt = w_in.T
    o = 0
    w_q = wt[o:o + Q_W]; o += Q_W
    w_k = wt[o:o + KV_W]; o += KV_W
    w_v = wt[o:o + KV_W]; o += KV_W
    w_qi = wt[o:o + IQ_W]; o += IQ_W
    w_ki = wt[o:o + IDX_DIM]; o += IDX_DIM
    w_wi = wt[o:o + N_IDX_HEADS]; o += N_IDX_HEADS
    w_glu = wt[o:o + 2 * C_CONV]; o += 2 * C_CONV
    w_g = wt[o:o + 2 * D_MODEL]
    head = lambda h: w_q[h * HEAD_DIM:(h + 1) * HEAD_DIM]
    q_pairs = [w for i in range(GROUP) for w in (head(i), head(GROUP + i))]
    zero = jnp.zeros((LANES - IDX_DIM, D_MODEL), w_in.dtype)
    w_att = jnp.concatenate(q_pairs + [w_k, w_qi, w_ki, zero], axis=0).astype(BF16)
    w_t = jnp.concatenate([w_k, w_v, w_ki, w_wi], axis=0).astype(BF16)
    return w_att, w_t, w_glu.astype(BF16), w_g.astype(BF16)


def kernel(x_prompt, x_sample, cache_k, cache_v, cache_kidx, state_conv, state_ffn, page_table, w_in, w_attn_o, w_conv_dw, b_conv_dw, ln_conv_g, ln_conv_b, w_conv_o, w_out, ln1_g, ln1_b, w_ffn_up, w_ffn_gate, w_ffn_dw, b_ffn_dw, w_ffn_down, ln2_g, ln2_b):
    bp, tp, _ = x_prompt.shape
    bs, ts, _ = x_sample.shape
    n_pages = page_table.shape[1]
    past = n_pages * PAGE_SIZE
    assert N_KV_HEADS == 2 and N_IDX_HEADS == SUBLANES

    w_att, w_t, w_glu, w_g = _stage_weights(w_in)
    row2 = lambda a: a.reshape(1, -1)
    mix_w = (w_glu, w_g, w_attn_o.astype(BF16), w_conv_dw, row2(b_conv_dw), row2(ln_conv_g), row2(ln_conv_b),
             w_conv_o.astype(BF16), w_out.astype(BF16), row2(ln1_g), row2(ln1_b))
    ffn_w = (w_ffn_up.astype(BF16), w_ffn_gate.astype(BF16), w_ffn_dw, row2(b_ffn_dw), w_ffn_down.astype(BF16),
             row2(ln2_g), row2(ln2_b))

    qbd, qi, kb, kib, vt3, k_t, v_t, ki_t, wit = _proj(x_prompt, w_att, w_t, 512)
    seq = lambda a: a.reshape(bp, tp, a.shape[-1])
    attn_p = _prompt_attention(seq(qbd), seq(qi), wit, seq(kib), seq(kb), vt3, min(TOPK_MAX, tp // 4))
    x1_p, conv_tail = _mix_prompt(x_prompt, attn_p, mix_w, 512)
    y_p, ffn_tail = _ffn_prompt(x1_p, ffn_w, 512)
    heads_last = lambda a_t: jnp.transpose(a_t.reshape(bp, N_KV_HEADS, HEAD_DIM, tp), (0, 3, 1, 2))
    k_p = heads_last(k_t)
    v_p = heads_last(v_t)
    ki_p = jnp.transpose(ki_t, (0, 2, 1))
    conv_p = conv_tail[:, CONV_HALO - (CONV_WIDTH - 1):, :]
    ffn_p = ffn_tail[:, FFN_HALO - (FFN_CONV_WIDTH - 1):, :]

    n_s = bs * ts
    qbd_s, qis, _, _, _, ks_t, vs_t, kis_t, wit_s = _proj(x_sample.reshape(1, n_s, D_MODEL), w_att, w_t, n_s)
    ks_t, vs_t, kis_t = ks_t[0], vs_t[0], kis_t[0]
    pt_flat = page_table.reshape(-1).astype(jnp.int32)
    new_rows = lambda a_t: a_t.T.reshape(bs, ts, -1)
    qi32 = qis.reshape(bs, ts * N_IDX_HEADS, IDX_DIM)
    w32 = wit_s.T.reshape(bs, ts * N_IDX_HEADS, 1)
    scores = _sample_scores(pt_flat, qi32, w32, jnp.transpose(cache_kidx, (0, 2, 1)), new_rows(kis_t), n_pages)
    bias = _sample_select(scores, past, ts, min(TOPK_MAX, (past + ts) // 4))
    q_bd = jnp.transpose(qbd_s.reshape(bs, ts, N_HEADS, KV_W), (0, 2, 1, 3)).reshape(bs, N_HEADS * ts, KV_W)
    as_pages = lambda a: jnp.transpose(a, (0, 2, 3, 1))
    o_bd = _sample_attend(pt_flat, q_bd, bias, as_pages(cache_k), as_pages(cache_v),
                          new_rows(ks_t), new_rows(vs_t), n_pages)
    o6 = o_bd.reshape(bs, N_KV_HEADS, GROUP, ts, N_KV_HEADS, HEAD_DIM)
    o_sel = jnp.stack([o6[:, n, :, :, n, :] for n in range(N_KV_HEADS)], axis=1)
    attn_s = jnp.transpose(o_sel, (0, 3, 1, 2, 4)).reshape(bs, ts, Q_W)

    x1_s, glu_s = _mix_sample(_token_major(x_sample), _token_major(attn_s).astype(BF16),
                              jnp.transpose(state_conv, (1, 0, 2)), mix_w)
    y_s, u_s = _ffn_sample(x1_s, jnp.transpose(state_ffn, (1, 0, 2)), ffn_w)
    y_s = _batch_major(y_s, bs)
    conv_s = jnp.concatenate([state_conv, _batch_major(glu_s, bs)], axis=1)[:, -(CONV_WIDTH - 1):, :]
    ffn_s = jnp.concatenate([state_ffn, _batch_major(u_s, bs)], axis=1)[:, -(FFN_CONV_WIDTH - 1):, :]
    k_s = new_rows(ks_t).reshape(bs, ts, N_KV_HEADS, HEAD_DIM)
    v_s = new_rows(vs_t).reshape(bs, ts, N_KV_HEADS, HEAD_DIM)
    ki_s = new_rows(kis_t)

    return (y_p, y_s, k_p, v_p, ki_p, conv_p, ffn_p, k_s, v_s, ki_s, conv_s, ffn_s)
```

```python
import functools

import jax
import jax.numpy as jnp
from jax import lax
from jax.experimental import pallas as pl
from jax.experimental.pallas import tpu as pltpu

D_MODEL = 1024
N_HEADS = 8
HEAD_DIM = 64
N_KV_HEADS = 2
N_IDX_HEADS = 8
IDX_DIM = 64
TOPK_MAX = 256
C_CONV = D_MODEL // 2
CONV_WIDTH = 31
D_FF = 2816
FFN_CONV_WIDTH = 3
LN_EPS = 1e-5
DEPTH = 1
ALPHA = (2.0 * DEPTH) ** 0.25
PAGE_SIZE = 128

Q_W = N_HEADS * HEAD_DIM
KV_W = N_KV_HEADS * HEAD_DIM
IQ_W = N_IDX_HEADS * IDX_DIM
GROUP = N_HEADS // N_KV_HEADS

LANES = 128
SUBLANES = 8
QBD_W = N_HEADS * KV_W
PROJ_W = Q_W + KV_W + IQ_W + LANES
PROJ_T = 2 * KV_W + IDX_DIM + N_IDX_HEADS
TQ = 256
TQ_S = 256
SCORE_GROUP = 8
ATTN_GROUP = 4
CONV_HALO = 32
FFN_HALO = 8
N_BISECT = 19
VT_ROWS = HEAD_DIM + 16
LOG2E = 1.4426950408889634
VMEM_LIMIT = 56 * 1024 * 1024

F32 = jnp.float32
BF16 = jnp.bfloat16
NEG_INF = float("-inf")
POS_INF = float("inf")


def _dot(a, b):
    return jnp.dot(a, b, preferred_element_type=F32)


def _dot_nt(a, b):
    return lax.dot_general(a, b, (((1,), (1,)), ((), ())), preferred_element_type=F32)


def _sigmoid(x):
    return 0.5 * jnp.tanh(0.5 * x) + 0.5


def _layer_norm(x, g, b):
    mu = jnp.mean(x, axis=-1, keepdims=True)
    xc = x - mu
    var = jnp.mean(xc * xc, axis=-1, keepdims=True)
    return xc * lax.rsqrt(var + LN_EPS) * g + b


def _params(sem):
    return pltpu.CompilerParams(dimension_semantics=sem, vmem_limit_bytes=VMEM_LIMIT)


def _whole_spec(shape):
    nd = len(shape)
    return pl.BlockSpec(shape, lambda *_: (0,) * nd)


def _const_spec(shape):
    nd = len(shape)
    return pl.BlockSpec(shape, lambda *_: (0,) * nd, pipeline_mode=pl.Buffered(1))


def _proj_kernel(x_ref, w_ref, wt_ref, qbd_ref, qi_ref, kb_ref, kib_ref, vt_ref, kt_ref, vtf_ref, kit_ref, wit_ref):
    xb = x_ref[...].astype(BF16)
    o = 0
    qp = _dot_nt(xb, w_ref[o:o + Q_W, :]); o += Q_W
    k = _dot_nt(xb, w_ref[o:o + KV_W, :]); o += KV_W
    qi = _dot_nt(xb, w_ref[o:o + IQ_W, :]); o += IQ_W
    ki = _dot_nt(xb, w_ref[o:o + LANES, :])[:, :IDX_DIM]
    qp = (qp * (HEAD_DIM ** -0.5 * LOG2E)).astype(BF16)
    low = lax.broadcasted_iota(jnp.int32, (qp.shape[0], KV_W), 1) < HEAD_DIM
    for h in range(N_HEADS):
        pair = qp[:, (h % GROUP) * KV_W:(h % GROUP + 1) * KV_W]
        qbd_ref[:, h * KV_W:(h + 1) * KV_W] = jnp.where(low if h < GROUP else jnp.logical_not(low), pair, 0.0)
    qi_ref[...] = qi.astype(BF16)
    kb_ref[...] = k.astype(BF16)
    kib_ref[...] = ki.astype(BF16)
    t = _dot_nt(wt_ref[...], xb)
    kt_ref[...] = t[:KV_W, :]
    vtf_ref[...] = t[KV_W:2 * KV_W, :]
    kit_ref[...] = t[2 * KV_W:2 * KV_W + IDX_DIM, :]
    wit_ref[...] = t[2 * KV_W + IDX_DIM:, :] * ((IDX_DIM ** -0.5) * (N_IDX_HEADS ** -0.5))
    vt = t[KV_W:2 * KV_W, :].astype(BF16)
    ones = jnp.ones((VT_ROWS - HEAD_DIM, TQ), BF16)
    for j in range(vt_ref.shape[0]):
        for n in range(N_KV_HEADS):
            vt_ref[j, n * VT_ROWS:n * VT_ROWS + HEAD_DIM, :] = vt[n * HEAD_DIM:(n + 1) * HEAD_DIM, j * TQ:(j + 1) * TQ]
            vt_ref[j, n * VT_ROWS + HEAD_DIM:(n + 1) * VT_ROWS, :] = ones


def _proj(x, w_att, w_t, tm):
    b, t, _ = x.shape
    n = b * t
    tiles = t // tm
    row = lambda w: pl.BlockSpec((tm, w), lambda i: (i, 0))
    col = lambda r: pl.BlockSpec((None, r, tm), lambda i: (i // tiles, 0, i % tiles))
    outs = [(QBD_W, BF16), (IQ_W, BF16), (KV_W, BF16), (IDX_DIM, BF16)]
    return pl.pallas_call(
        _proj_kernel,
        grid=(n // tm,),
        in_specs=[row(D_MODEL), _const_spec((PROJ_W, D_MODEL)), _const_spec((PROJ_T, D_MODEL))],
        out_specs=[row(w) for w, _ in outs]
        + [pl.BlockSpec((tm // TQ, N_KV_HEADS * VT_ROWS, TQ), lambda i: (i, 0, 0)), col(KV_W), col(KV_W), col(IDX_DIM),
           pl.BlockSpec((SUBLANES, tm), lambda i: (0, i))],
        out_shape=[jax.ShapeDtypeStruct((n, w), dt) for w, dt in outs]
        + [jax.ShapeDtypeStruct((n // TQ, N_KV_HEADS * VT_ROWS, TQ), BF16), jax.ShapeDtypeStruct((b, KV_W, t), F32),
           jax.ShapeDtypeStruct((b, KV_W, t), F32), jax.ShapeDtypeStruct((b, IDX_DIM, t), F32),
           jax.ShapeDtypeStruct((SUBLANES, n), F32)],
        compiler_params=_params(("parallel",)),
        name="proj",
    )(x.reshape(n, D_MODEL), w_att, w_t)


def _chunk(ref, c, ch):
    return ref[pl.ds(pl.multiple_of(c * ch, ch), ch), :]


def _fold_rows(x, op):
    parts = [x[r:r + SUBLANES, :] for r in range(0, x.shape[0], SUBLANES)]
    accs = parts[:4]
    for i, part in enumerate(parts[4:]):
        accs[i % 4] = op(accs[i % 4], part)
    while len(accs) > 1:
        accs = [op(accs[a], accs[a + 1]) for a in range(0, len(accs) - 1, 2)] + ([accs[-1]] if len(accs) % 2 else [])
    return accs[0]


def _reduce_keys(s_ref, nk, ch, init, f, op, red):
    w = s_ref.shape[1]

    def body(c, acc):
        return op(acc, _fold_rows(f(_chunk(s_ref, c, ch), c), op))

    acc = lax.fori_loop(0, nk, body, jnp.full((SUBLANES, w), init, F32))
    return red(acc, axis=0, keepdims=True)


def _score_stats_init(w):
    return (jnp.full((SUBLANES, w), NEG_INF, F32), jnp.full((SUBLANES, w), POS_INF, F32),
            jnp.zeros((SUBLANES, w), F32))


def _score_stats_update(stats, x):
    mx, mn, cnt = stats
    live = x > NEG_INF
    return (jnp.maximum(mx, _fold_rows(x, jnp.maximum)),
            jnp.minimum(mn, _fold_rows(jnp.where(live, x, POS_INF), jnp.minimum)),
            cnt + _fold_rows(jnp.where(live, 1.0, 0.0), jnp.add))


def _select_bias(s_ref, bias_ref, nk, ch, topk, stats):
    w = s_ref.shape[1]
    kf = float(topk)
    count = lambda pred: _reduce_keys(s_ref, nk, ch, 0.0, lambda x, c: jnp.where(pred(x, c), 1.0, 0.0),
                                      jnp.add, jnp.sum)
    min_above = lambda t: _reduce_keys(s_ref, nk, ch, POS_INF, lambda x, c: jnp.where(x > t, x, POS_INF),
                                       jnp.minimum, jnp.min)

    neg = jnp.full((1, w), NEG_INF, F32)

    hi = jnp.max(stats[0], axis=0, keepdims=True)
    lo_fin = jnp.min(stats[1], axis=0, keepdims=True)
    n_adm = jnp.sum(stats[2], axis=0, keepdims=True)

    def bisect(_, st):
        lo, lo_fin, hi, n_lo = st
        mid = 0.5 * lo_fin + 0.5 * hi
        c = count(lambda x, _: x > mid)
        ok = c >= kf
        return jnp.where(ok, mid, lo), jnp.where(ok, mid, lo_fin), jnp.where(ok, hi, mid), jnp.where(ok, c, n_lo)

    lo, _, _, n_lo = lax.fori_loop(0, N_BISECT, bisect, (neg, lo_fin, hi, n_adm))

    def peel(st):
        lo, thr, n_gt, done = st
        v = min_above(lo)
        c = count(lambda x, _: x > v)
        found = c < kf
        newly = jnp.logical_and(done < 0.5, found)
        return (jnp.where(jnp.logical_or(found, done > 0.5), lo, v), jnp.where(newly, v, thr),
                jnp.where(newly, c, n_gt), jnp.where(found, 1.0, done))

    _, thr, n_gt, _ = lax.while_loop(lambda st: jnp.min(st[3]) < 0.5, peel,
                                     (lo, lo, n_lo, jnp.where(n_lo <= kf, 1.0, 0.0)))

    need = jnp.where(thr == neg, 0.0, kf - n_gt)
    any_copies = jnp.max(need) > 0.0

    @pl.when(any_copies)
    def _():
        tri = (lax.broadcasted_iota(jnp.int32, (ch, ch), 1)
               <= lax.broadcasted_iota(jnp.int32, (ch, ch), 0)).astype(BF16)

        def write(c, seen):
            x = _chunk(s_ref, c, ch)
            eq = x == thr
            rank = seen + _dot(tri, jnp.where(eq, 1.0, 0.0).astype(BF16))
            sel = jnp.logical_or(x > thr, jnp.logical_and(eq, rank <= need))
            bias_ref[pl.ds(pl.multiple_of(c * ch, ch), ch), :] = jnp.where(sel, 0.0, NEG_INF)
            return rank[ch - 1:ch, :]

        lax.fori_loop(0, nk, write, jnp.zeros((1, w), F32))

    @pl.when(jnp.logical_not(any_copies))
    def _():
        def write(c, carry):
            sel = _chunk(s_ref, c, ch) > thr
            bias_ref[pl.ds(pl.multiple_of(c * ch, ch), ch), :] = jnp.where(sel, 0.0, NEG_INF)
            return carry

        lax.fori_loop(0, nk, write, 0)


def _prompt_attn_kernel(qbd_ref, qi_ref, wit_ref, kib_ref, kb_ref, vt_ref, o_ref, s_ref, bias_ref, m_ref, acc_ref,
                        lg_ref, *, topk):
    i = pl.program_id(1)
    nk = i + 1
    qi = qi_ref[...]
    wit = wit_ref[...]
    q_pos = lax.broadcasted_iota(jnp.int32, (TQ, TQ), 1) + i * TQ
    k_off = lax.broadcasted_iota(jnp.int32, (TQ, TQ), 0)

    def score(c, stats):
        kc = _chunk(kib_ref, c, TQ)
        s = jnp.zeros((TQ, TQ), F32)
        for h in range(N_IDX_HEADS):
            z = _dot_nt(kc, qi[:, h * IDX_DIM:(h + 1) * IDX_DIM])
            s = s + jnp.maximum(z, 0.0) * wit[h:h + 1, :]
        s = jnp.where(k_off + c * TQ <= q_pos, s, NEG_INF)
        s_ref[pl.ds(pl.multiple_of(c * TQ, TQ), TQ), :] = s
        return _score_stats_update(stats, s)

    stats = lax.fori_loop(0, nk, score, _score_stats_init(TQ))
    _select_bias(s_ref, bias_ref, nk, TQ, topk, stats)

    m_ref[...] = jnp.full(m_ref.shape, NEG_INF, F32)
    acc_ref[...] = jnp.zeros(acc_ref.shape, F32)

    def attend(c, carry):
        rows = pl.ds(pl.multiple_of(c * TQ, TQ), TQ)
        mx = []
        for h in range(N_HEADS):
            lg = _dot_nt(kb_ref[rows, :], qbd_ref[:, h * KV_W:(h + 1) * KV_W]) + bias_ref[rows, :]
            lg_ref[h] = lg
            mx.append(_fold_rows(lg, jnp.maximum))
        m_all = m_ref[...]
        m_rows = []
        for h in range(N_HEADS):
            n = h // GROUP
            out = slice(h * VT_ROWS, (h + 1) * VT_ROWS)
            m_old = m_all[h:h + 1, :]
            m_new = jnp.maximum(m_old, jnp.max(mx[h], axis=0, keepdims=True))
            m_rows.append(m_new)
            m_use = jnp.where(m_new == NEG_INF, 0.0, m_new)
            p = jnp.exp2(lg_ref[h] - m_use).astype(BF16)
            vt = vt_ref[c, pl.ds(n * VT_ROWS, VT_ROWS), :]
            acc_ref[out, :] = jnp.exp2(m_old - m_use) * acc_ref[out, :] + _dot(vt, p)
        m_ref[...] = jnp.concatenate(m_rows, axis=0)
        return carry

    lax.fori_loop(0, nk, attend, 0)
    outs = [acc_ref[h * VT_ROWS:h * VT_ROWS + HEAD_DIM, :] / acc_ref[h * VT_ROWS + HEAD_DIM:h * VT_ROWS + HEAD_DIM + 1, :]
            for h in range(N_HEADS)]
    o_ref[...] = jnp.concatenate(outs, axis=0).T.astype(o_ref.dtype)


def _prompt_attention(qbd, qi, wit, kib, kb, vt3, topk):
    b, t, _ = qbd.shape
    nblk = t // TQ
    qblk = lambda w: pl.BlockSpec((None, TQ, w), lambda bi, i: (bi, i, 0))
    full = lambda w: pl.BlockSpec((None, t, w), lambda bi, i: (bi, 0, 0))
    return pl.pallas_call(
        functools.partial(_prompt_attn_kernel, topk=topk),
        grid=(b, nblk),
        in_specs=[qblk(QBD_W), qblk(IQ_W), pl.BlockSpec((SUBLANES, TQ), lambda bi, i: (0, bi * nblk + i)),
                  full(IDX_DIM), full(KV_W), pl.BlockSpec((nblk, N_KV_HEADS * VT_ROWS, TQ), lambda bi, i: (bi, 0, 0))],
        out_specs=qblk(Q_W),
        out_shape=jax.ShapeDtypeStruct((b, t, Q_W), BF16),
        scratch_shapes=[pltpu.VMEM((t, TQ), F32), pltpu.VMEM((t, TQ), F32), pltpu.VMEM((N_HEADS, TQ), F32),
                        pltpu.VMEM((N_HEADS * VT_ROWS, TQ), F32), pltpu.VMEM((N_HEADS, TQ, TQ), F32)],
        compiler_params=_params(("parallel", "arbitrary")),
        name="prompt_attn",
    )(qbd, qi, wit, kib, kb, vt3)


def _group_spec(group, *block):
    nd = len(block)
    return pl.BlockSpec((group,) + block, lambda bi, pt: (bi,) + (0,) * nd)


def _new_key_block(q, k_new):
    qf = q.astype(F32)
    kf = k_new.astype(BF16).astype(F32)
    lane = lax.broadcasted_iota(jnp.int32, (q.shape[0], PAGE_SIZE), 1)
    blk = jnp.zeros((q.shape[0], PAGE_SIZE), F32)
    for j in range(k_new.shape[0]):
        blk = jnp.where(lane == j, jnp.sum(qf * kf[j:j + 1, :], axis=-1, keepdims=True), blk)
    return blk


def _sample_score_kernel(pt_ref, qi_ref, w_ref, new_ref, kidx_hbm, o_ref, buf, sem, *, n_pages):
    group = qi_ref.shape[0]
    slot = _prefetch_pages(*_page_fetcher(pt_ref, (kidx_hbm,), (buf,), sem, group * n_pages))
    t = qi_ref.shape[1] // N_IDX_HEADS
    for g in range(group):
        qi = qi_ref[g]
        w = w_ref[g]
        blocks = [_dot(qi, buf[slot, g * n_pages + j].astype(BF16)) for j in range(n_pages)]
        blocks.append(_new_key_block(qi, new_ref[g]))
        for j, s in enumerate(blocks):
            s = jnp.maximum(s, 0.0) * w
            o_ref[g * t:(g + 1) * t, j * PAGE_SIZE:(j + 1) * PAGE_SIZE] = jnp.sum(
                s.reshape(t, N_IDX_HEADS, PAGE_SIZE), axis=1)


def _sample_scores(page_table_flat, qi32, w32, kidx_t, ki_new_t, n_pages):
    nb, rows, _ = qi32.shape
    t = rows // N_IDX_HEADS
    lk = (n_pages + 1) * PAGE_SIZE
    g = SCORE_GROUP
    return pl.pallas_call(
        functools.partial(_sample_score_kernel, n_pages=n_pages),
        grid_spec=pltpu.PrefetchScalarGridSpec(
            num_scalar_prefetch=1,
            grid=(nb // g,),
            in_specs=[_group_spec(g, rows, IDX_DIM), _group_spec(g, rows, 1), _group_spec(g, t, IDX_DIM),
                      pl.BlockSpec(memory_space=pl.ANY)],
            out_specs=pl.BlockSpec((g * t, lk), lambda bi, pt: (bi, 0)),
            scratch_shapes=[pltpu.VMEM((2, g * n_pages, IDX_DIM, PAGE_SIZE), F32), pltpu.SemaphoreType.DMA((1, 2))],
        ),
        out_shape=jax.ShapeDtypeStruct((nb * t, lk), F32),
        compiler_params=_params(("arbitrary",)),
        name="sample_scores",
    )(page_table_flat, qi32, w32, ki_new_t, kidx_t)


def _sample_select_kernel(s_ref, bias_ref, sm_ref, bt_ref, *, past, t, topk):
    lk = s_ref.shape[1]
    nk = lk // PAGE_SIZE
    qcol = lax.broadcasted_iota(jnp.int32, (PAGE_SIZE, TQ_S), 1)
    qpos = past + (qcol & (t - 1))
    krow = lax.broadcasted_iota(jnp.int32, (PAGE_SIZE, TQ_S), 0)
    stats = _score_stats_init(TQ_S)
    for c in range(nk):
        cols = slice(c * PAGE_SIZE, (c + 1) * PAGE_SIZE)
        s = jnp.where(krow + c * PAGE_SIZE <= qpos, s_ref[:, cols].T, NEG_INF)
        sm_ref[cols, :] = s
        stats = _score_stats_update(stats, s)
    _select_bias(sm_ref, bt_ref, nk, PAGE_SIZE, topk, stats)
    for c in range(nk):
        cols = slice(c * PAGE_SIZE, (c + 1) * PAGE_SIZE)
        bias_ref[:, cols] = bt_ref[cols, :].T


def _sample_select(scores, past, t, topk):
    n, lk = scores.shape
    assert t & (t - 1) == 0 and TQ_S % t == 0, "token index is taken from the low bits of the query index"
    blk = pl.BlockSpec((TQ_S, lk), lambda i: (i, 0))
    return pl.pallas_call(
        functools.partial(_sample_select_kernel, past=past, t=t, topk=topk),
        grid=(n // TQ_S,),
        in_specs=[blk],
        out_specs=blk,
        out_shape=jax.ShapeDtypeStruct((n, lk), F32),
        scratch_shapes=[pltpu.VMEM((lk, TQ_S), F32), pltpu.VMEM((lk, TQ_S), F32)],
        compiler_params=_params(("parallel",)),
        name="sample_select",
    )(scores)


def _page_fetcher(pt_ref, srcs, bufs, sem, n_copy):
    def copy(a, slot, i, page):
        return pltpu.make_async_copy(srcs[a].at[page], bufs[a].at[slot, i], sem.at[a, slot])

    def start(step, slot):
        def body(i, carry):
            page = pt_ref[step * n_copy + i]
            for a in range(len(srcs)):
                copy(a, slot, i, page).start()
            return carry
        lax.fori_loop(0, n_copy, body, 0)

    def wait(slot):
        def body(i, carry):
            for a in range(len(srcs)):
                copy(a, slot, i, 0).wait()
            return carry
        lax.fori_loop(0, n_copy, body, 0)

    return start, wait


def _prefetch_pages(start, wait):
    s = pl.program_id(0)
    slot = s % 2

    @pl.when(s == 0)
    def _():
        start(0, 0)

    @pl.when(s + 1 < pl.num_programs(0))
    def _():
        start(s + 1, 1 - slot)

    wait(slot)
    return slot


def _sample_attn_kernel(pt_ref, q_ref, bias_ref, k_new, v_new, k_hbm, v_hbm, o_ref, kbuf, vbuf, sem, *, n_pages):
    group = q_ref.shape[0]
    slot = _prefetch_pages(*_page_fetcher(pt_ref, (k_hbm, v_hbm), (kbuf, vbuf), sem, group * n_pages))
    kt = lambda page: page.reshape(KV_W, PAGE_SIZE).astype(BF16)
    t = k_new.shape[1]
    for g in range(group):
        qb = q_ref[g]
        bias = jnp.concatenate([bias_ref[g * t:(g + 1) * t, :]] * (qb.shape[0] // t), axis=0)
        blocks = [_dot(qb, kt(kbuf[slot, g * n_pages + j])) for j in range(n_pages)]
        blocks.append(_new_key_block(qb, k_new[g]))
        lg = jnp.concatenate(blocks, axis=-1) + bias
        m = jnp.max(lg, axis=-1, keepdims=True)
        p = jnp.exp2(lg - m)
        l = jnp.sum(p, axis=-1, keepdims=True)
        pb = p.astype(BF16)
        acc = jnp.zeros(o_ref.shape[1:], F32)
        for j in range(n_pages):
            acc = acc + _dot_nt(pb[:, j * PAGE_SIZE:(j + 1) * PAGE_SIZE], kt(vbuf[slot, g * n_pages + j]))
        p_new = pb[:, n_pages * PAGE_SIZE:].astype(F32)
        v_rows = v_new[g].astype(BF16).astype(F32)
        for j in range(v_rows.shape[0]):
            acc = acc + p_new[:, j:j + 1] * v_rows[j:j + 1, :]
        o_ref[g] = acc / l


def _sample_attend(page_table_flat, q_bd, bias, k_t, v_t, k_new_t, v_new_t, n_pages):
    nb, rows, _ = q_bd.shape
    lk = bias.shape[1]
    t = bias.shape[0] // nb
    g = ATTN_GROUP
    page = (N_KV_HEADS, HEAD_DIM, PAGE_SIZE)
    hbm = pl.BlockSpec(memory_space=pl.ANY)
    page_buf = pltpu.VMEM((2, g * n_pages) + page, F32)
    return pl.pallas_call(
        functools.partial(_sample_attn_kernel, n_pages=n_pages),
        grid_spec=pltpu.PrefetchScalarGridSpec(
            num_scalar_prefetch=1,
            grid=(nb // g,),
            in_specs=[_group_spec(g, rows, KV_W), pl.BlockSpec((g * t, lk), lambda bi, pt: (bi, 0)),
                      _group_spec(g, t, KV_W), _group_spec(g, t, KV_W), hbm, hbm],
            out_specs=_group_spec(g, rows, KV_W),
            scratch_shapes=[page_buf, page_buf, pltpu.SemaphoreType.DMA((2, 2))],
        ),
        out_shape=jax.ShapeDtypeStruct((nb, rows, KV_W), F32),
        compiler_params=_params(("arbitrary",)),
        name="sample_attn",
    )(page_table_flat, q_bd, bias, k_new_t, v_new_t, k_t, v_t)


def _glu(xb, w_glu_ref):
    gi = _dot_nt(xb, w_glu_ref[...])
    return gi[:, :C_CONV] * _sigmoid(gi[:, C_CONV:])


def _gates(xb, attn_b, w_g_ref, w_ao_ref, cols=slice(0, D_MODEL)):
    gate_cols = slice(D_MODEL + cols.start, D_MODEL + cols.stop)
    a_term = _sigmoid(_dot_nt(xb, w_g_ref[cols, :])) * _dot(attn_b, w_ao_ref[:, cols])
    return a_term, _sigmoid(_dot_nt(xb, w_g_ref[gate_cols, :]))


def _mix_tail(x, c, a_term, c_gate, lncg_ref, lncb_ref, w_co_ref, w_out_ref, ln1g_ref, ln1b_ref):
    cn = _layer_norm(c, lncg_ref[...], lncb_ref[...])
    c_branch = _dot((cn * _sigmoid(cn)).astype(BF16), w_co_ref[...])
    merged = a_term + c_gate * c_branch
    h = ALPHA * x + _dot(merged.astype(BF16), w_out_ref[...])
    return _layer_norm(h, ln1g_ref[...], ln1b_ref[...])


def _mix_prompt_kernel(x_ref, attn_ref, w_glu_ref, w_g_ref, w_ao_ref, w_dw_ref, b_dw_ref, lncg_ref, lncb_ref,
                       w_co_ref, w_out_ref, ln1g_ref, ln1b_ref, x1_ref, tail_ref, xp_ref):
    tm = x_ref.shape[0]

    @pl.when(pl.program_id(1) == 0)
    def _():
        xp_ref[0:CONV_HALO, :] = jnp.zeros((CONV_HALO, C_CONV), F32)

    x = x_ref[...]
    xb = x.astype(BF16)
    glu = _glu(xb, w_glu_ref)
    xp_ref[CONV_HALO:CONV_HALO + tm, :] = glu
    tail_ref[...] = glu[tm - CONV_HALO:, :]
    attn_b = attn_ref[...]

    first = CONV_HALO - (CONV_WIDTH - 1)
    rb = 128
    n_blk = C_CONV // LANES
    cols, gate_parts = [], []
    for c0 in range(0, C_CONV, LANES):
        lanes = slice(c0, c0 + LANES)
        q = c0 // LANES
        gate_parts.append(_gates(xb, attn_b, w_g_ref, w_ao_ref, slice(q * D_MODEL // n_blk, (q + 1) * D_MODEL // n_blk)))
        blocks = []
        for r0 in range(0, tm, rb):
            y = jnp.broadcast_to(b_dw_ref[:, lanes], (rb, LANES))
            for b in range(SUBLANES):
                rows = rb + (SUBLANES if b else 0)
                part = None
                for j in range(CONV_WIDTH):
                    if (first + j) % SUBLANES == b:
                        a0 = first + j - b + r0
                        term = w_dw_ref[j:j + 1, lanes] * xp_ref[a0:a0 + rows, lanes]
                        part = term if part is None else part + term
                y = y + part[b:b + rb, :]
            blocks.append(y)
        cols.append(jnp.concatenate(blocks, axis=0))
    c = jnp.concatenate(cols, axis=-1)
    a_term = jnp.concatenate([p[0] for p in gate_parts], axis=-1)
    c_gate = jnp.concatenate([p[1] for p in gate_parts], axis=-1)
    xp_ref[0:CONV_HALO, :] = xp_ref[tm:tm + CONV_HALO, :]

    x1_ref[...] = _mix_tail(x, c, a_term, c_gate, lncg_ref, lncb_ref, w_co_ref, w_out_ref, ln1g_ref, ln1b_ref)


def _mix_sample_kernel(x_ref, attn_ref, hist_ref, w_glu_ref, w_g_ref, w_ao_ref, w_dw_ref, b_dw_ref, lncg_ref,
                       lncb_ref, w_co_ref, w_out_ref, ln1g_ref, ln1b_ref, x1_ref, glu_ref):
    nb = hist_ref.shape[1]
    t = x_ref.shape[0] // nb
    n_hist = CONV_WIDTH - 1
    x = x_ref[...]
    xb = x.astype(BF16)
    glu = _glu(xb, w_glu_ref)
    glu_ref[...] = glu

    def slab(m):
        return hist_ref[m] if m < n_hist else glu[(m - n_hist) * nb:(m - n_hist + 1) * nb, :]

    outs = []
    for ti in range(t):
        acc = jnp.broadcast_to(b_dw_ref[...], (nb, C_CONV))
        for j in range(CONV_WIDTH):
            acc = acc + w_dw_ref[j:j + 1, :] * slab(ti + j)
        outs.append(acc)
    c = jnp.concatenate(outs, axis=0)
    a_term, c_gate = _gates(xb, attn_ref[...], w_g_ref, w_ao_ref)
    x1_ref[...] = _mix_tail(x, c, a_term, c_gate, lncg_ref, lncb_ref, w_co_ref, w_out_ref, ln1g_ref, ln1b_ref)


def _mix_weight_specs():
    return [
        _const_spec((2 * C_CONV, D_MODEL)), _const_spec((2 * D_MODEL, D_MODEL)), _const_spec((Q_W, D_MODEL)),
        _const_spec((CONV_WIDTH, C_CONV)), _const_spec((1, C_CONV)), _const_spec((1, C_CONV)),
        _const_spec((1, C_CONV)), _const_spec((C_CONV, D_MODEL)), _const_spec((D_MODEL, D_MODEL)),
        _const_spec((1, D_MODEL)), _const_spec((1, D_MODEL)),
    ]


def _mix_prompt(x, attn, mix_w, tm):
    b, t, _ = x.shape
    rows = lambda w: pl.BlockSpec((None, tm, w), lambda bi, i: (bi, i, 0))
    return pl.pallas_call(
        _mix_prompt_kernel,
        grid=(b, t // tm),
        in_specs=[rows(D_MODEL), rows(Q_W)] + _mix_weight_specs(),
        out_specs=[rows(D_MODEL), pl.BlockSpec((None, CONV_HALO, C_CONV), lambda bi, i: (bi, 0, 0))],
        out_shape=[jax.ShapeDtypeStruct((b, t, D_MODEL), F32), jax.ShapeDtypeStruct((b, CONV_HALO, C_CONV), F32)],
        scratch_shapes=[pltpu.VMEM((CONV_HALO + tm, C_CONV), F32)],
        compiler_params=_params(("parallel", "arbitrary")),
        name="mix_prompt",
    )(x, attn, *mix_w)


def _mix_sample(x_tm, attn_tm, hist_tm, mix_w):
    n = x_tm.shape[0]
    return pl.pallas_call(
        _mix_sample_kernel,
        grid=(1,),
        in_specs=[_const_spec((n, D_MODEL)), _const_spec((n, Q_W)), _const_spec(hist_tm.shape)] + _mix_weight_specs(),
        out_specs=[_whole_spec((n, D_MODEL)), _whole_spec((n, C_CONV))],
        out_shape=[jax.ShapeDtypeStruct((n, D_MODEL), F32), jax.ShapeDtypeStruct((n, C_CONV), F32)],
        compiler_params=_params(("arbitrary",)),
        name="mix_sample",
    )(x_tm, attn_tm, hist_tm, *mix_w)


def _ffn_tail(x1, uc, gate, w_down_ref, ln2g_ref, ln2b_ref):
    f = _dot((jax.nn.gelu(uc) * gate).astype(BF16), w_down_ref[...])
    return _layer_norm(ALPHA * x1 + f, ln2g_ref[...], ln2b_ref[...])


def _ffn_prompt_kernel(x1_ref, w_up_ref, w_gate_ref, w_dw_ref, b_dw_ref, w_down_ref, ln2g_ref, ln2b_ref,
                       y_ref, tail_ref, up_ref):
    tm = x1_ref.shape[0]

    @pl.when(pl.program_id(1) == 0)
    def _():
        up_ref[0:FFN_HALO, :] = jnp.zeros((FFN_HALO, D_FF), F32)

    x1 = x1_ref[...]
    x1b = x1.astype(BF16)
    u = _dot(x1b, w_up_ref[...])
    gate = _dot(x1b, w_gate_ref[...])
    up_ref[FFN_HALO:FFN_HALO + tm, :] = u
    tail_ref[...] = u[tm - FFN_HALO:, :]
    first = FFN_HALO - (FFN_CONV_WIDTH - 1)
    uc = b_dw_ref[...]
    for j in range(FFN_CONV_WIDTH):
        uc = uc + w_dw_ref[j:j + 1, :] * up_ref[first + j:first + j + tm, :]
    up_ref[0:FFN_HALO, :] = up_ref[tm:tm + FFN_HALO, :]
    y_ref[...] = _ffn_tail(x1, uc, gate, w_down_ref, ln2g_ref, ln2b_ref)


def _ffn_sample_kernel(x1_ref, hist_ref, w_up_ref, w_gate_ref, w_dw_ref, b_dw_ref, w_down_ref, ln2g_ref, ln2b_ref,
                       y_ref, u_ref):
    nb = hist_ref.shape[1]
    t = x1_ref.shape[0] // nb
    n_hist = FFN_CONV_WIDTH - 1
    x1 = x1_ref[...]
    x1b = x1.astype(BF16)
    u = _dot(x1b, w_up_ref[...])
    gate = _dot(x1b, w_gate_ref[...])
    u_ref[...] = u

    def slab(m):
        return hist_ref[m] if m < n_hist else u[(m - n_hist) * nb:(m - n_hist + 1) * nb, :]

    outs = []
    for ti in range(t):
        acc = jnp.broadcast_to(b_dw_ref[...], (nb, D_FF))
        for j in range(FFN_CONV_WIDTH):
            acc = acc + w_dw_ref[j:j + 1, :] * slab(ti + j)
        outs.append(acc)
    uc = jnp.concatenate(outs, axis=0)
    y_ref[...] = _ffn_tail(x1, uc, gate, w_down_ref, ln2g_ref, ln2b_ref)


def _ffn_weight_specs():
    return [
        _const_spec((D_MODEL, D_FF)), _const_spec((D_MODEL, D_FF)), _const_spec((FFN_CONV_WIDTH, D_FF)),
        _const_spec((1, D_FF)), _const_spec((D_FF, D_MODEL)), _const_spec((1, D_MODEL)), _const_spec((1, D_MODEL)),
    ]


def _ffn_prompt(x1, ffn_w, tm):
    b, t, _ = x1.shape
    rows = pl.BlockSpec((None, tm, D_MODEL), lambda bi, i: (bi, i, 0))
    return pl.pallas_call(
        _ffn_prompt_kernel,
        grid=(b, t // tm),
        in_specs=[rows] + _ffn_weight_specs(),
        out_specs=[rows, pl.BlockSpec((None, FFN_HALO, D_FF), lambda bi, i: (bi, 0, 0))],
        out_shape=[jax.ShapeDtypeStruct((b, t, D_MODEL), F32), jax.ShapeDtypeStruct((b, FFN_HALO, D_FF), F32)],
        scratch_shapes=[pltpu.VMEM((FFN_HALO + tm, D_FF), F32)],
        compiler_params=_params(("parallel", "arbitrary")),
        name="ffn_prompt",
    )(x1, *ffn_w)


def _ffn_sample(x1_tm, hist_tm, ffn_w):
    n = x1_tm.shape[0]
    return pl.pallas_call(
        _ffn_sample_kernel,
        grid=(1,),
        in_specs=[_const_spec((n, D_MODEL)), _const_spec(hist_tm.shape)] + _ffn_weight_specs(),
        out_specs=[_whole_spec((n, D_MODEL)), _whole_spec((n, D_FF))],
        out_shape=[jax.ShapeDtypeStruct((n, D_MODEL), F32), jax.ShapeDtypeStruct((n, D_FF), F32)],
        compiler_params=_params(("arbitrary",)),
        name="ffn_sample",
    )(x1_tm, hist_tm, *ffn_w)


def _token_major(a):
    b, t, w = a.shape
    return jnp.transpose(a, (1, 0, 2)).reshape(t * b, w)


def _batch_major(a, b):
    tb, w = a.shape
    return jnp.transpose(a.reshape(tb // b, b, w), (1, 0, 2))


def _stage_weights(w_in):
    wt = w_in.T
    o = 0
    w_q = wt[o:o + Q_W]; o += Q_W
    w_k = wt[o:o + KV_W]; o += KV_W
    w_v = wt[o:o + KV_W]; o += KV_W
    w_qi = wt[o:o + IQ_W]; o += IQ_W
    w_ki = wt[o:o + IDX_DIM]; o += IDX_DIM
    w_wi = wt[o:o + N_IDX_HEADS]; o += N_IDX_HEADS
    w_glu = wt[o:o + 2 * C_CONV]; o += 2 * C_CONV
    w_g = wt[o:o + 2 * D_MODEL]
    head = lambda h: w_q[h * HEAD_DIM:(h + 1) * HEAD_DIM]
    q_pairs = [w for i in range(GROUP) for w in (head(i), head(GROUP + i))]
    zero = jnp.zeros((LANES - IDX_DIM, D_MODEL), w_in.dtype)
    w_att = jnp.concatenate(q_pairs + [w_k, w_qi, w_ki, zero], axis=0).astype(BF16)
    w_t = jnp.concatenate([w_k, w_v, w_ki, w_wi], axis=0).astype(BF16)
    return w_att, w_t, w_glu.astype(BF16), w_g.astype(BF16)


def kernel(x_prompt, x_sample, cache_k, cache_v, cache_kidx, state_conv, state_ffn, page_table, w_in, w_attn_o, w_conv_dw, b_conv_dw, ln_conv_g, ln_conv_b, w_conv_o, w_out, ln1_g, ln1_b, w_ffn_up, w_ffn_gate, w_ffn_dw, b_ffn_dw, w_ffn_down, ln2_g, ln2_b):
    bp, tp, _ = x_prompt.shape
    bs, ts, _ = x_sample.shape
    n_pages = page_table.shape[1]
    past = n_pages * PAGE_SIZE
    assert N_KV_HEADS == 2 and N_IDX_HEADS == SUBLANES

    w_att, w_t, w_glu, w_g = _stage_weights(w_in)
    row2 = lambda a: a.reshape(1, -1)
    mix_w = (w_glu, w_g, w_attn_o.astype(BF16), w_conv_dw, row2(b_conv_dw), row2(ln_conv_g), row2(ln_conv_b),
             w_conv_o.astype(BF16), w_out.astype(BF16), row2(ln1_g), row2(ln1_b))
    ffn_w = (w_ffn_up.astype(BF16), w_ffn_gate.astype(BF16), w_ffn_dw, row2(b_ffn_dw), w_ffn_down.astype(BF16),
             row2(ln2_g), row2(ln2_b))

    qbd, qi, kb, kib, vt3, k_t, v_t, ki_t, wit = _proj(x_prompt, w_att, w_t, 1024)
    seq = lambda a: a.reshape(bp, tp, a.shape[-1])
    attn_p = _prompt_attention(seq(qbd), seq(qi), wit, seq(kib), seq(kb), vt3, min(TOPK_MAX, tp // 4))
    x1_p, conv_tail = _mix_prompt(x_prompt, attn_p, mix_w, 512)
    y_p, ffn_tail = _ffn_prompt(x1_p, ffn_w, 512)
    heads_last = lambda a_t: jnp.transpose(a_t.reshape(bp, N_KV_HEADS, HEAD_DIM, tp), (0, 3, 1, 2))
    k_p = heads_last(k_t)
    v_p = heads_last(v_t)
    ki_p = jnp.transpose(ki_t, (0, 2, 1))
    conv_p = conv_tail[:, CONV_HALO - (CONV_WIDTH - 1):, :]
    ffn_p = ffn_tail[:, FFN_HALO - (FFN_CONV_WIDTH - 1):, :]

    n_s = bs * ts
    qbd_s, qis, _, _, _, ks_t, vs_t, kis_t, wit_s = _proj(x_sample.reshape(1, n_s, D_MODEL), w_att, w_t, n_s)
    ks_t, vs_t, kis_t = ks_t[0], vs_t[0], kis_t[0]
    pt_flat = page_table.reshape(-1).astype(jnp.int32)
    new_rows = lambda a_t: a_t.T.reshape(bs, ts, -1)
    qi32 = qis.reshape(bs, ts * N_IDX_HEADS, IDX_DIM)
    w32 = wit_s.T.reshape(bs, ts * N_IDX_HEADS, 1)
    scores = _sample_scores(pt_flat, qi32, w32, jnp.transpose(cache_kidx, (0, 2, 1)), new_rows(kis_t), n_pages)
    bias = _sample_select(scores, past, ts, min(TOPK_MAX, (past + ts) // 4))
    q_bd = jnp.transpose(qbd_s.reshape(bs, ts, N_HEADS, KV_W), (0, 2, 1, 3)).reshape(bs, N_HEADS * ts, KV_W)
    as_pages = lambda a: jnp.transpose(a, (0, 2, 3, 1))
    o_bd = _sample_attend(pt_flat, q_bd, bias, as_pages(cache_k), as_pages(cache_v),
                          new_rows(ks_t), new_rows(vs_t), n_pages)
    o6 = o_bd.reshape(bs, N_KV_HEADS, GROUP, ts, N_KV_HEADS, HEAD_DIM)
    o_sel = jnp.stack([o6[:, n, :, :, n, :] for n in range(N_KV_HEADS)], axis=1)
    attn_s = jnp.transpose(o_sel, (0, 3, 1, 2, 4)).reshape(bs, ts, Q_W)

    x1_s, glu_s = _mix_sample(_token_major(x_sample), _token_major(attn_s).astype(BF16),
                              jnp.transpose(state_conv, (1, 0, 2)), mix_w)
    y_s, u_s = _ffn_sample(x1_s, jnp.transpose(state_ffn, (1, 0, 2)), ffn_w)
    y_s = _batch_major(y_s, bs)
    conv_s = jnp.concatenate([state_conv, _batch_major(glu_s, bs)], axis=1)[:, -(CONV_WIDTH - 1):, :]
    ffn_s = jnp.concatenate([state_ffn, _batch_major(u_s, bs)], axis=1)[:, -(FFN_CONV_WIDTH - 1):, :]
    k_s = new_rows(ks_t).reshape(bs, ts, N_KV_HEADS, HEAD_DIM)
    v_s = new_rows(vs_t).reshape(bs, ts, N_KV_HEADS, HEAD_DIM)
    ki_s = new_rows(kis_t)

    return (y_p, y_s, k_p, v_p, ki_p, conv_p, ffn_p, k_s, v_s, ki_s, conv_s, ffn_s)
```

```python
import functools

import jax
import jax.numpy as jnp
from jax import lax
from jax.experimental import pallas as pl
from jax.experimental.pallas import tpu as pltpu

D_MODEL = 1024
N_HEADS = 8
HEAD_DIM = 64
N_KV_HEADS = 2
N_IDX_HEADS = 8
IDX_DIM = 64
TOPK_MAX = 256
C_CONV = D_MODEL // 2
CONV_WIDTH = 31
D_FF = 2816
FFN_CONV_WIDTH = 3
LN_EPS = 1e-5
DEPTH = 1
ALPHA = (2.0 * DEPTH) ** 0.25
PAGE_SIZE = 128

Q_W = N_HEADS * HEAD_DIM
KV_W = N_KV_HEADS * HEAD_DIM
IQ_W = N_IDX_HEADS * IDX_DIM
GROUP = N_HEADS // N_KV_HEADS

LANES = 128
SUBLANES = 8
QBD_W = N_HEADS * KV_W
PROJ_W = Q_W + KV_W + IQ_W + LANES
PROJ_T = 2 * KV_W + IDX_DIM + N_IDX_HEADS
TQ = 256
TQ_S = 256
SCORE_GROUP = 8
ATTN_GROUP = 4
CONV_HALO = 32
FFN_HALO = 8
N_BISECT = 19
VT_ROWS = HEAD_DIM + 16
LOG2E = 1.4426950408889634
VMEM_LIMIT = 56 * 1024 * 1024

F32 = jnp.float32
BF16 = jnp.bfloat16
NEG_INF = float("-inf")
POS_INF = float("inf")


def _dot(a, b):
    return jnp.dot(a, b, preferred_element_type=F32)


def _dot_nt(a, b):
    return lax.dot_general(a, b, (((1,), (1,)), ((), ())), preferred_element_type=F32)


def _sigmoid(x):
    return 0.5 * jnp.tanh(0.5 * x) + 0.5


def _layer_norm(x, g, b):
    mu = jnp.mean(x, axis=-1, keepdims=True)
    xc = x - mu
    var = jnp.mean(xc * xc, axis=-1, keepdims=True)
    return xc * lax.rsqrt(var + LN_EPS) * g + b


def _params(sem):
    return pltpu.CompilerParams(dimension_semantics=sem, vmem_limit_bytes=VMEM_LIMIT)


def _whole_spec(shape):
    nd = len(shape)
    return pl.BlockSpec(shape, lambda *_: (0,) * nd)


def _const_spec(shape):
    nd = len(shape)
    return pl.BlockSpec(shape, lambda *_: (0,) * nd, pipeline_mode=pl.Buffered(1))


def _proj_kernel(x_ref, w_ref, wt_ref, qbd_ref, qi_ref, kb_ref, kib_ref, vt_ref, kt_ref, vtf_ref, kit_ref, wit_ref):
    xb = x_ref[...].astype(BF16)
    o = 0
    qp = _dot_nt(xb, w_ref[o:o + Q_W, :]); o += Q_W
    k = _dot_nt(xb, w_ref[o:o + KV_W, :]); o += KV_W
    qi = _dot_nt(xb, w_ref[o:o + IQ_W, :]); o += IQ_W
    ki = _dot_nt(xb, w_ref[o:o + LANES, :])[:, :IDX_DIM]
    qp = (qp * (HEAD_DIM ** -0.5 * LOG2E)).astype(BF16)
    low = lax.broadcasted_iota(jnp.int32, (qp.shape[0], KV_W), 1) < HEAD_DIM
    for h in range(N_HEADS):
        pair = qp[:, (h % GROUP) * KV_W:(h % GROUP + 1) * KV_W]
        qbd_ref[:, h * KV_W:(h + 1) * KV_W] = jnp.where(low if h < GROUP else jnp.logical_not(low), pair, 0.0)
    qi_ref[...] = qi.astype(BF16)
    kb_ref[...] = k.astype(BF16)
    kib_ref[...] = ki.astype(BF16)
    t = _dot_nt(wt_ref[...], xb)
    kt_ref[...] = t[:KV_W, :]
    vtf_ref[...] = t[KV_W:2 * KV_W, :]
    kit_ref[...] = t[2 * KV_W:2 * KV_W + IDX_DIM, :]
    wit_ref[...] = t[2 * KV_W + IDX_DIM:, :] * ((IDX_DIM ** -0.5) * (N_IDX_HEADS ** -0.5))
    vt = t[KV_W:2 * KV_W, :].astype(BF16)
    ones = jnp.ones((VT_ROWS - HEAD_DIM, TQ), BF16)
    for j in range(vt_ref.shape[0]):
        for n in range(N_KV_HEADS):
            vt_ref[j, n * VT_ROWS:n * VT_ROWS + HEAD_DIM, :] = vt[n * HEAD_DIM:(n + 1) * HEAD_DIM, j * TQ:(j + 1) * TQ]
            vt_ref[j, n * VT_ROWS + HEAD_DIM:(n + 1) * VT_ROWS, :] = ones


def _proj(x, w_att, w_t, tm):
    b, t, _ = x.shape
    n = b * t
    tiles = t // tm
    row = lambda w: pl.BlockSpec((tm, w), lambda i: (i, 0))
    col = lambda r: pl.BlockSpec((None, r, tm), lambda i: (i // tiles, 0, i % tiles))
    outs = [(QBD_W, BF16), (IQ_W, BF16), (KV_W, BF16), (IDX_DIM, BF16)]
    return pl.pallas_call(
        _proj_kernel,
        grid=(n // tm,),
        in_specs=[row(D_MODEL), _const_spec((PROJ_W, D_MODEL)), _const_spec((PROJ_T, D_MODEL))],
        out_specs=[row(w) for w, _ in outs]
        + [pl.BlockSpec((tm // TQ, N_KV_HEADS * VT_ROWS, TQ), lambda i: (i, 0, 0)), col(KV_W), col(KV_W), col(IDX_DIM),
           pl.BlockSpec((SUBLANES, tm), lambda i: (0, i))],
        out_shape=[jax.ShapeDtypeStruct((n, w), dt) for w, dt in outs]
        + [jax.ShapeDtypeStruct((n // TQ, N_KV_HEADS * VT_ROWS, TQ), BF16), jax.ShapeDtypeStruct((b, KV_W, t), F32),
           jax.ShapeDtypeStruct((b, KV_W, t), F32), jax.ShapeDtypeStruct((b, IDX_DIM, t), F32),
           jax.ShapeDtypeStruct((SUBLANES, n), F32)],
        compiler_params=_params(("parallel",)),
        name="proj",
    )(x.reshape(n, D_MODEL), w_att, w_t)


def _chunk(ref, c, ch):
    return ref[pl.ds(pl.multiple_of(c * ch, ch), ch), :]


def _fold_rows(x, op):
    parts = [x[r:r + SUBLANES, :] for r in range(0, x.shape[0], SUBLANES)]
    accs = parts[:4]
    for i, part in enumerate(parts[4:]):
        accs[i % 4] = op(accs[i % 4], part)
    while len(accs) > 1:
        accs = [op(accs[a], accs[a + 1]) for a in range(0, len(accs) - 1, 2)] + ([accs[-1]] if len(accs) % 2 else [])
    return accs[0]


def _reduce_keys(s_ref, nk, ch, init, f, op, red):
    w = s_ref.shape[1]

    def body(c, acc):
        return op(acc, _fold_rows(f(_chunk(s_ref, c, ch), c), op))

    acc = lax.fori_loop(0, nk, body, jnp.full((SUBLANES, w), init, F32))
    return red(acc, axis=0, keepdims=True)


def _score_stats_init(w):
    return (jnp.full((SUBLANES, w), NEG_INF, F32), jnp.full((SUBLANES, w), POS_INF, F32),
            jnp.zeros((SUBLANES, w), F32))


def _score_stats_update(stats, x):
    mx, mn, cnt = stats
    live = x > NEG_INF
    return (jnp.maximum(mx, _fold_rows(x, jnp.maximum)),
            jnp.minimum(mn, _fold_rows(jnp.where(live, x, POS_INF), jnp.minimum)),
            cnt + _fold_rows(jnp.where(live, 1.0, 0.0), jnp.add))


def _select_bias(s_ref, bias_ref, nk, ch, topk, stats):
    w = s_ref.shape[1]
    kf = float(topk)
    count = lambda pred: _reduce_keys(s_ref, nk, ch, 0.0, lambda x, c: jnp.where(pred(x, c), 1.0, 0.0),
                                      jnp.add, jnp.sum)
    min_above = lambda t: _reduce_keys(s_ref, nk, ch, POS_INF, lambda x, c: jnp.where(x > t, x, POS_INF),
                                       jnp.minimum, jnp.min)

    neg = jnp.full((1, w), NEG_INF, F32)

    hi = jnp.max(stats[0], axis=0, keepdims=True)
    lo_fin = jnp.min(stats[1], axis=0, keepdims=True)
    n_adm = jnp.sum(stats[2], axis=0, keepdims=True)

    def bisect(_, st):
        lo, lo_fin, hi, n_lo = st
        mid = 0.5 * lo_fin + 0.5 * hi
        c = count(lambda x, _: x > mid)
        ok = c >= kf
        return jnp.where(ok, mid, lo), jnp.where(ok, mid, lo_fin), jnp.where(ok, hi, mid), jnp.where(ok, c, n_lo)

    lo, _, _, n_lo = lax.fori_loop(0, N_BISECT, bisect, (neg, lo_fin, hi, n_adm))

    def peel(st):
        lo, thr, n_gt, done = st
        v = min_above(lo)
        c = count(lambda x, _: x > v)
        found = c < kf
        newly = jnp.logical_and(done < 0.5, found)
        return (jnp.where(jnp.logical_or(found, done > 0.5), lo, v), jnp.where(newly, v, thr),
                jnp.where(newly, c, n_gt), jnp.where(found, 1.0, done))

    _, thr, n_gt, _ = lax.while_loop(lambda st: jnp.min(st[3]) < 0.5, peel,
                                     (lo, lo, n_lo, jnp.where(n_lo <= kf, 1.0, 0.0)))

    need = jnp.where(thr == neg, 0.0, kf - n_gt)
    any_copies = jnp.max(need) > 0.0

    @pl.when(any_copies)
    def _():
        tri = (lax.broadcasted_iota(jnp.int32, (ch, ch), 1)
               <= lax.broadcasted_iota(jnp.int32, (ch, ch), 0)).astype(BF16)

        def write(c, seen):
            x = _chunk(s_ref, c, ch)
            eq = x == thr
            rank = seen + _dot(tri, jnp.where(eq, 1.0, 0.0).astype(BF16))
            sel = jnp.logical_or(x > thr, jnp.logical_and(eq, rank <= need))
            bias_ref[pl.ds(pl.multiple_of(c * ch, ch), ch), :] = jnp.where(sel, 0.0, NEG_INF)
            return rank[ch - 1:ch, :]

        lax.fori_loop(0, nk, write, jnp.zeros((1, w), F32))

    @pl.when(jnp.logical_not(any_copies))
    def _():
        def write(c, carry):
            sel = _chunk(s_ref, c, ch) > thr
            bias_ref[pl.ds(pl.multiple_of(c * ch, ch), ch), :] = jnp.where(sel, 0.0, NEG_INF)
            return carry

        lax.fori_loop(0, nk, write, 0)


def _prompt_attn_kernel(qbd_ref, qi_ref, wit_ref, kib_ref, kb_ref, vt_ref, o_ref, s_ref, bias_ref, acc_ref, lg_ref,
                        lg2_ref, *, topk):
    i = pl.program_id(1)
    nk = i + 1
    qi = qi_ref[...]
    wit = wit_ref[...]
    q_pos = lax.broadcasted_iota(jnp.int32, (TQ, TQ), 1) + i * TQ
    k_off = lax.broadcasted_iota(jnp.int32, (TQ, TQ), 0)

    def score(c, stats):
        kc = _chunk(kib_ref, c, TQ)
        s = jnp.zeros((TQ, TQ), F32)
        for h in range(N_IDX_HEADS):
            z = _dot_nt(kc, qi[:, h * IDX_DIM:(h + 1) * IDX_DIM])
            s = s + jnp.maximum(z, 0.0) * wit[h:h + 1, :]
        s = jnp.where(k_off + c * TQ <= q_pos, s, NEG_INF)
        s_ref[pl.ds(pl.multiple_of(c * TQ, TQ), TQ), :] = s
        return _score_stats_update(stats, s)

    stats = lax.fori_loop(0, nk, score, _score_stats_init(TQ))
    _select_bias(s_ref, bias_ref, nk, TQ, topk, stats)

    acc_ref[...] = jnp.zeros(acc_ref.shape, F32)

    bufs = (lg_ref, lg2_ref)

    def logits(c, h, buf):
        rows = pl.ds(pl.multiple_of(c * TQ, TQ), TQ)
        lg = _dot_nt(kb_ref[rows, :], qbd_ref[:, h * KV_W:(h + 1) * KV_W]) + bias_ref[rows, :]
        buf[h] = lg
        return jnp.max(_fold_rows(lg, jnp.maximum), axis=0, keepdims=True)

    def weigh(c, h, buf, m_use, alpha):
        out = slice(h * VT_ROWS, (h + 1) * VT_ROWS)
        p = jnp.exp2(buf[h] - m_use[h:h + 1, :]).astype(BF16)
        vt = vt_ref[c, pl.ds((h // GROUP) * VT_ROWS, VT_ROWS), :]
        acc_ref[out, :] = alpha[h:h + 1, :] * acc_ref[out, :] + _dot(vt, p)

    def advance(m_old, chunk_max):
        m_new = jnp.maximum(m_old, jnp.concatenate(chunk_max, axis=0))
        m_use = jnp.where(m_new == NEG_INF, 0.0, m_new)
        return m_new, m_use, jnp.exp2(m_old - m_use)

    def step(c, parity, state):
        m, m_use, alpha = state
        chunk_max = []
        for h in range(N_HEADS):
            chunk_max.append(logits(c, h, bufs[parity]))
            weigh(c - 1, h, bufs[1 - parity], m_use, alpha)
        return advance(m, chunk_max)

    def finish(c, parity, state):
        for h in range(N_HEADS):
            weigh(c, h, bufs[parity], state[1], state[2])

    state = advance(jnp.full((N_HEADS, TQ), NEG_INF, F32), [logits(0, h, bufs[0]) for h in range(N_HEADS)])
    n_pairs = (nk - 1) // 2
    state = lax.fori_loop(0, n_pairs, lambda p, st: step(2 * p + 2, 0, step(2 * p + 1, 1, st)), state)
    odd_tail = (nk - 1) % 2 == 1

    @pl.when(odd_tail)
    def _():
        finish(nk - 1, 1, step(nk - 1, 1, state))

    @pl.when(jnp.logical_not(odd_tail))
    def _():
        finish(nk - 1, 0, state)
    outs = [acc_ref[h * VT_ROWS:h * VT_ROWS + HEAD_DIM, :] / acc_ref[h * VT_ROWS + HEAD_DIM:h * VT_ROWS + HEAD_DIM + 1, :]
            for h in range(N_HEADS)]
    o_ref[...] = jnp.concatenate(outs, axis=0).T.astype(o_ref.dtype)


def _prompt_attention(qbd, qi, wit, kib, kb, vt3, topk):
    b, t, _ = qbd.shape
    nblk = t // TQ
    qblk = lambda w: pl.BlockSpec((None, TQ, w), lambda bi, i: (bi, i, 0))
    full = lambda w: pl.BlockSpec((None, t, w), lambda bi, i: (bi, 0, 0))
    return pl.pallas_call(
        functools.partial(_prompt_attn_kernel, topk=topk),
        grid=(b, nblk),
        in_specs=[qblk(QBD_W), qblk(IQ_W), pl.BlockSpec((SUBLANES, TQ), lambda bi, i: (0, bi * nblk + i)),
                  full(IDX_DIM), full(KV_W), pl.BlockSpec((nblk, N_KV_HEADS * VT_ROWS, TQ), lambda bi, i: (bi, 0, 0))],
        out_specs=qblk(Q_W),
        out_shape=jax.ShapeDtypeStruct((b, t, Q_W), BF16),
        scratch_shapes=[pltpu.VMEM((t, TQ), F32), pltpu.VMEM((t, TQ), F32), pltpu.VMEM((N_HEADS * VT_ROWS, TQ), F32),
                        pltpu.VMEM((N_HEADS, TQ, TQ), F32), pltpu.VMEM((N_HEADS, TQ, TQ), F32)],
        compiler_params=_params(("parallel", "arbitrary")),
        name="prompt_attn",
    )(qbd, qi, wit, kib, kb, vt3)


def _group_spec(group, *block):
    nd = len(block)
    return pl.BlockSpec((group,) + block, lambda bi, pt: (bi,) + (0,) * nd)


def _new_key_block(q, k_new):
    qf = q.astype(F32)
    kf = k_new.astype(BF16).astype(F32)
    lane = lax.broadcasted_iota(jnp.int32, (q.shape[0], PAGE_SIZE), 1)
    blk = jnp.zeros((q.shape[0], PAGE_SIZE), F32)
    for j in range(k_new.shape[0]):
        blk = jnp.where(lane == j, jnp.sum(qf * kf[j:j + 1, :], axis=-1, keepdims=True), blk)
    return blk


def _sample_score_kernel(pt_ref, qi_ref, w_ref, new_ref, kidx_hbm, o_ref, buf, sem, *, n_pages):
    group = qi_ref.shape[0]
    slot = _prefetch_pages(*_page_fetcher(pt_ref, (kidx_hbm,), (buf,), sem, group * n_pages))
    t = qi_ref.shape[1] // N_IDX_HEADS
    for g in range(group):
        qi = qi_ref[g]
        w = w_ref[g]
        blocks = [_dot(qi, buf[slot, g * n_pages + j].astype(BF16)) for j in range(n_pages)]
        blocks.append(_new_key_block(qi, new_ref[g]))
        for j, s in enumerate(blocks):
            s = jnp.maximum(s, 0.0) * w
            o_ref[g * t:(g + 1) * t, j * PAGE_SIZE:(j + 1) * PAGE_SIZE] = jnp.sum(
                s.reshape(t, N_IDX_HEADS, PAGE_SIZE), axis=1)


def _sample_scores(page_table_flat, qi32, w32, kidx_t, ki_new_t, n_pages):
    nb, rows, _ = qi32.shape
    t = rows // N_IDX_HEADS
    lk = (n_pages + 1) * PAGE_SIZE
    g = SCORE_GROUP
    return pl.pallas_call(
        functools.partial(_sample_score_kernel, n_pages=n_pages),
        grid_spec=pltpu.PrefetchScalarGridSpec(
            num_scalar_prefetch=1,
            grid=(nb // g,),
            in_specs=[_group_spec(g, rows, IDX_DIM), _group_spec(g, rows, 1), _group_spec(g, t, IDX_DIM),
                      pl.BlockSpec(memory_space=pl.ANY)],
            out_specs=pl.BlockSpec((g * t, lk), lambda bi, pt: (bi, 0)),
            scratch_shapes=[pltpu.VMEM((2, g * n_pages, IDX_DIM, PAGE_SIZE), F32), pltpu.SemaphoreType.DMA((1, 2))],
        ),
        out_shape=jax.ShapeDtypeStruct((nb * t, lk), F32),
        compiler_params=_params(("arbitrary",)),
        name="sample_scores",
    )(page_table_flat, qi32, w32, ki_new_t, kidx_t)


def _sample_select_kernel(s_ref, bias_ref, sm_ref, bt_ref, *, past, t, topk):
    lk = s_ref.shape[1]
    nk = lk // PAGE_SIZE
    qcol = lax.broadcasted_iota(jnp.int32, (PAGE_SIZE, TQ_S), 1)
    qpos = past + (qcol & (t - 1))
    krow = lax.broadcasted_iota(jnp.int32, (PAGE_SIZE, TQ_S), 0)
    stats = _score_stats_init(TQ_S)
    for c in range(nk):
        cols = slice(c * PAGE_SIZE, (c + 1) * PAGE_SIZE)
        s = jnp.where(krow + c * PAGE_SIZE <= qpos, s_ref[:, cols].T, NEG_INF)
        sm_ref[cols, :] = s
        stats = _score_stats_update(stats, s)
    _select_bias(sm_ref, bt_ref, nk, PAGE_SIZE, topk, stats)
    for c in range(nk):
        cols = slice(c * PAGE_SIZE, (c + 1) * PAGE_SIZE)
        bias_ref[:, cols] = bt_ref[cols, :].T


def _sample_select(scores, past, t, topk):
    n, lk = scores.shape
    assert t & (t - 1) == 0 and TQ_S % t == 0, "token index is taken from the low bits of the query index"
    blk = pl.BlockSpec((TQ_S, lk), lambda i: (i, 0))
    return pl.pallas_call(
        functools.partial(_sample_select_kernel, past=past, t=t, topk=topk),
        grid=(n // TQ_S,),
        in_specs=[blk],
        out_specs=blk,
        out_shape=jax.ShapeDtypeStruct((n, lk), F32),
        scratch_shapes=[pltpu.VMEM((lk, TQ_S), F32), pltpu.VMEM((lk, TQ_S), F32)],
        compiler_params=_params(("parallel",)),
        name="sample_select",
    )(scores)


def _page_fetcher(pt_ref, srcs, bufs, sem, n_copy):
    def copy(a, slot, i, page):
        return pltpu.make_async_copy(srcs[a].at[page], bufs[a].at[slot, i], sem.at[a, slot])

    def start(step, slot):
        def body(i, carry):
            page = pt_ref[step * n_copy + i]
            for a in range(len(srcs)):
                copy(a, slot, i, page).start()
            return carry
        lax.fori_loop(0, n_copy, body, 0)

    def wait(slot):
        def body(i, carry):
            for a in range(len(srcs)):
                copy(a, slot, i, 0).wait()
            return carry
        lax.fori_loop(0, n_copy, body, 0)

    return start, wait


def _prefetch_pages(start, wait):
    s = pl.program_id(0)
    slot = s % 2

    @pl.when(s == 0)
    def _():
        start(0, 0)

    @pl.when(s + 1 < pl.num_programs(0))
    def _():
        start(s + 1, 1 - slot)

    wait(slot)
    return slot


def _sample_attn_kernel(pt_ref, q_ref, bias_ref, k_new, v_new, k_hbm, v_hbm, o_ref, kbuf, vbuf, sem, *, n_pages):
    group = q_ref.shape[0]
    slot = _prefetch_pages(*_page_fetcher(pt_ref, (k_hbm, v_hbm), (kbuf, vbuf), sem, group * n_pages))
    kt = lambda page: page.reshape(KV_W, PAGE_SIZE).astype(BF16)
    t = k_new.shape[1]
    for g in range(group):
        qb = q_ref[g]
        bias = jnp.concatenate([bias_ref[g * t:(g + 1) * t, :]] * (qb.shape[0] // t), axis=0)
        blocks = [_dot(qb, kt(kbuf[slot, g * n_pages + j])) for j in range(n_pages)]
        blocks.append(_new_key_block(qb, k_new[g]))
        lg = jnp.concatenate(blocks, axis=-1) + bias
        m = jnp.max(lg, axis=-1, keepdims=True)
        p = jnp.exp2(lg - m)
        l = jnp.sum(p, axis=-1, keepdims=True)
        pb = p.astype(BF16)
        acc = jnp.zeros(o_ref.shape[1:], F32)
        for j in range(n_pages):
            acc = acc + _dot_nt(pb[:, j * PAGE_SIZE:(j + 1) * PAGE_SIZE], kt(vbuf[slot, g * n_pages + j]))
        p_new = pb[:, n_pages * PAGE_SIZE:].astype(F32)
        v_rows = v_new[g].astype(BF16).astype(F32)
        for j in range(v_rows.shape[0]):
            acc = acc + p_new[:, j:j + 1] * v_rows[j:j + 1, :]
        o_ref[g] = acc / l


def _sample_attend(page_table_flat, q_bd, bias, k_t, v_t, k_new_t, v_new_t, n_pages):
    nb, rows, _ = q_bd.shape
    lk = bias.shape[1]
    t = bias.shape[0] // nb
    g = ATTN_GROUP
    page = (N_KV_HEADS, HEAD_DIM, PAGE_SIZE)
    hbm = pl.BlockSpec(memory_space=pl.ANY)
    page_buf = pltpu.VMEM((2, g * n_pages) + page, F32)
    return pl.pallas_call(
        functools.partial(_sample_attn_kernel, n_pages=n_pages),
        grid_spec=pltpu.PrefetchScalarGridSpec(
            num_scalar_prefetch=1,
            grid=(nb // g,),
            in_specs=[_group_spec(g, rows, KV_W), pl.BlockSpec((g * t, lk), lambda bi, pt: (bi, 0)),
                      _group_spec(g, t, KV_W), _group_spec(g, t, KV_W), hbm, hbm],
            out_specs=_group_spec(g, rows, KV_W),
            scratch_shapes=[page_buf, page_buf, pltpu.SemaphoreType.DMA((2, 2))],
        ),
        out_shape=jax.ShapeDtypeStruct((nb, rows, KV_W), F32),
        compiler_params=_params(("arbitrary",)),
        name="sample_attn",
    )(page_table_flat, q_bd, bias, k_new_t, v_new_t, k_t, v_t)


def _glu(xb, w_glu_ref):
    gi = _dot_nt(xb, w_glu_ref[...])
    return gi[:, :C_CONV] * _sigmoid(gi[:, C_CONV:])


def _gates(xb, attn_b, w_g_ref, w_ao_ref, cols=slice(0, D_MODEL)):
    gate_cols = slice(D_MODEL + cols.start, D_MODEL + cols.stop)
    a_term = _sigmoid(_dot_nt(xb, w_g_ref[cols, :])) * _dot(attn_b, w_ao_ref[:, cols])
    return a_term, _sigmoid(_dot_nt(xb, w_g_ref[gate_cols, :]))


def _mix_tail(x, c, a_term, c_gate, lncg_ref, lncb_ref, w_co_ref, w_out_ref, ln1g_ref, ln1b_ref):
    cn = _layer_norm(c, lncg_ref[...], lncb_ref[...])
    c_branch = _dot((cn * _sigmoid(cn)).astype(BF16), w_co_ref[...])
    merged = a_term + c_gate * c_branch
    h = ALPHA * x + _dot(merged.astype(BF16), w_out_ref[...])
    return _layer_norm(h, ln1g_ref[...], ln1b_ref[...])


def _mix_prompt_kernel(x_ref, attn_ref, w_glu_ref, w_g_ref, w_ao_ref, w_dw_ref, b_dw_ref, lncg_ref, lncb_ref,
                       w_co_ref, w_out_ref, ln1g_ref, ln1b_ref, x1_ref, tail_ref, xp_ref):
    tm = x_ref.shape[0]

    @pl.when(pl.program_id(1) == 0)
    def _():
        xp_ref[0:CONV_HALO, :] = jnp.zeros((CONV_HALO, C_CONV), F32)

    x = x_ref[...]
    xb = x.astype(BF16)
    glu = _glu(xb, w_glu_ref)
    xp_ref[CONV_HALO:CONV_HALO + tm, :] = glu
    tail_ref[...] = glu[tm - CONV_HALO:, :]
    attn_b = attn_ref[...]

    first = CONV_HALO - (CONV_WIDTH - 1)
    rb = 128
    n_blk = C_CONV // LANES
    cols, gate_parts = [], []
    for c0 in range(0, C_CONV, LANES):
        lanes = slice(c0, c0 + LANES)
        q = c0 // LANES
        gate_parts.append(_gates(xb, attn_b, w_g_ref, w_ao_ref, slice(q * D_MODEL // n_blk, (q + 1) * D_MODEL // n_blk)))
        blocks = []
        for r0 in range(0, tm, rb):
            y = jnp.broadcast_to(b_dw_ref[:, lanes], (rb, LANES))
            for b in range(SUBLANES):
                rows = rb + (SUBLANES if b else 0)
                part = None
                for j in range(CONV_WIDTH):
                    if (first + j) % SUBLANES == b:
                        a0 = first + j - b + r0
                        term = w_dw_ref[j:j + 1, lanes] * xp_ref[a0:a0 + rows, lanes]
                        part = term if part is None else part + term
                y = y + part[b:b + rb, :]
            blocks.append(y)
        cols.append(jnp.concatenate(blocks, axis=0))
    c = jnp.concatenate(cols, axis=-1)
    a_term = jnp.concatenate([p[0] for p in gate_parts], axis=-1)
    c_gate = jnp.concatenate([p[1] for p in gate_parts], axis=-1)
    xp_ref[0:CONV_HALO, :] = xp_ref[tm:tm + CONV_HALO, :]

    x1_ref[...] = _mix_tail(x, c, a_term, c_gate, lncg_ref, lncb_ref, w_co_ref, w_out_ref, ln1g_ref, ln1b_ref)


def _mix_sample_kernel(x_ref, attn_ref, hist_ref, w_glu_ref, w_g_ref, w_ao_ref, w_dw_ref, b_dw_ref, lncg_ref,
                       lncb_ref, w_co_ref, w_out_ref, ln1g_ref, ln1b_ref, x1_ref, glu_ref):
    nb = hist_ref.shape[1]
    t = x_ref.shape[0] // nb
    n_hist = CONV_WIDTH - 1
    x = x_ref[...]
    xb = x.astype(BF16)
    glu = _glu(xb, w_glu_ref)
    glu_ref[...] = glu

    def slab(m):
        return hist_ref[m] if m < n_hist else glu[(m - n_hist) * nb:(m - n_hist + 1) * nb, :]

    outs = []
    for ti in range(t):
        acc = jnp.broadcast_to(b_dw_ref[...], (nb, C_CONV))
        for j in range(CONV_WIDTH):
            acc = acc + w_dw_ref[j:j + 1, :] * slab(ti + j)
        outs.append(acc)
    c = jnp.concatenate(outs, axis=0)
    a_term, c_gate = _gates(xb, attn_ref[...], w_g_ref, w_ao_ref)
    x1_ref[...] = _mix_tail(x, c, a_term, c_gate, lncg_ref, lncb_ref, w_co_ref, w_out_ref, ln1g_ref, ln1b_ref)


def _mix_weight_specs():
    return [
        _const_spec((2 * C_CONV, D_MODEL)), _const_spec((2 * D_MODEL, D_MODEL)), _const_spec((Q_W, D_MODEL)),
        _const_spec((CONV_WIDTH, C_CONV)), _const_spec((1, C_CONV)), _const_spec((1, C_CONV)),
        _const_spec((1, C_CONV)), _const_spec((C_CONV, D_MODEL)), _const_spec((D_MODEL, D_MODEL)),
        _const_spec((1, D_MODEL)), _const_spec((1, D_MODEL)),
    ]


def _mix_prompt(x, attn, mix_w, tm):
    b, t, _ = x.shape
    rows = lambda w: pl.BlockSpec((None, tm, w), lambda bi, i: (bi, i, 0))
    return pl.pallas_call(
        _mix_prompt_kernel,
        grid=(b, t // tm),
        in_specs=[rows(D_MODEL), rows(Q_W)] + _mix_weight_specs(),
        out_specs=[rows(D_MODEL), pl.BlockSpec((None, CONV_HALO, C_CONV), lambda bi, i: (bi, 0, 0))],
        out_shape=[jax.ShapeDtypeStruct((b, t, D_MODEL), F32), jax.ShapeDtypeStruct((b, CONV_HALO, C_CONV), F32)],
        scratch_shapes=[pltpu.VMEM((CONV_HALO + tm, C_CONV), F32)],
        compiler_params=_params(("parallel", "arbitrary")),
        name="mix_prompt",
    )(x, attn, *mix_w)


def _mix_sample(x_tm, attn_tm, hist_tm, mix_w):
    n = x_tm.shape[0]
    return pl.pallas_call(
        _mix_sample_kernel,
        grid=(1,),
        in_specs=[_const_spec((n, D_MODEL)), _const_spec((n, Q_W)), _const_spec(hist_tm.shape)] + _mix_weight_specs(),
        out_specs=[_whole_spec((n, D_MODEL)), _whole_spec((n, C_CONV))],
        out_shape=[jax.ShapeDtypeStruct((n, D_MODEL), F32), jax.ShapeDtypeStruct((n, C_CONV), F32)],
        compiler_params=_params(("arbitrary",)),
        name="mix_sample",
    )(x_tm, attn_tm, hist_tm, *mix_w)


def _ffn_tail(x1, uc, gate, w_down_ref, ln2g_ref, ln2b_ref):
    f = _dot((jax.nn.gelu(uc) * gate).astype(BF16), w_down_ref[...])
    return _layer_norm(ALPHA * x1 + f, ln2g_ref[...], ln2b_ref[...])


def _ffn_prompt_kernel(x1_ref, w_up_ref, w_gate_ref, w_dw_ref, b_dw_ref, w_down_ref, ln2g_ref, ln2b_ref,
                       y_ref, tail_ref, up_ref):
    tm = x1_ref.shape[0]

    @pl.when(pl.program_id(1) == 0)
    def _():
        up_ref[0:FFN_HALO, :] = jnp.zeros((FFN_HALO, D_FF), F32)

    x1 = x1_ref[...]
    x1b = x1.astype(BF16)
    u = _dot(x1b, w_up_ref[...])
    gate = _dot(x1b, w_gate_ref[...])
    up_ref[FFN_HALO:FFN_HALO + tm, :] = u
    tail_ref[...] = u[tm - FFN_HALO:, :]
    first = FFN_HALO - (FFN_CONV_WIDTH - 1)
    uc = b_dw_ref[...]
    for j in range(FFN_CONV_WIDTH):
        uc = uc + w_dw_ref[j:j + 1, :] * up_ref[first + j:first + j + tm, :]
    up_ref[0:FFN_HALO, :] = up_ref[tm:tm + FFN_HALO, :]
    y_ref[...] = _ffn_tail(x1, uc, gate, w_down_ref, ln2g_ref, ln2b_ref)


def _ffn_sample_kernel(x1_ref, hist_ref, w_up_ref, w_gate_ref, w_dw_ref, b_dw_ref, w_down_ref, ln2g_ref, ln2b_ref,
                       y_ref, u_ref):
    nb = hist_ref.shape[1]
    t = x1_ref.shape[0] // nb
    n_hist = FFN_CONV_WIDTH - 1
    x1 = x1_ref[...]
    x1b = x1.astype(BF16)
    u = _dot(x1b, w_up_ref[...])
    gate = _dot(x1b, w_gate_ref[...])
    u_ref[...] = u

    def slab(m):
        return hist_ref[m] if m < n_hist else u[(m - n_hist) * nb:(m - n_hist + 1) * nb, :]

    outs = []
    for ti in range(t):
        acc = jnp.broadcast_to(b_dw_ref[...], (nb, D_FF))
        for j in range(FFN_CONV_WIDTH):
            acc = acc + w_dw_ref[j:j + 1, :] * slab(ti + j)
        outs.append(acc)
    uc = jnp.concatenate(outs, axis=0)
    y_ref[...] = _ffn_tail(x1, uc, gate, w_down_ref, ln2g_ref, ln2b_ref)


def _ffn_weight_specs():
    return [
        _const_spec((D_MODEL, D_FF)), _const_spec((D_MODEL, D_FF)), _const_spec((FFN_CONV_WIDTH, D_FF)),
        _const_spec((1, D_FF)), _const_spec((D_FF, D_MODEL)), _const_spec((1, D_MODEL)), _const_spec((1, D_MODEL)),
    ]


def _ffn_prompt(x1, ffn_w, tm):
    b, t, _ = x1.shape
    rows = pl.BlockSpec((None, tm, D_MODEL), lambda bi, i: (bi, i, 0))
    return pl.pallas_call(
        _ffn_prompt_kernel,
        grid=(b, t // tm),
        in_specs=[rows] + _ffn_weight_specs(),
        out_specs=[rows, pl.BlockSpec((None, FFN_HALO, D_FF), lambda bi, i: (bi, 0, 0))],
        out_shape=[jax.ShapeDtypeStruct((b, t, D_MODEL), F32), jax.ShapeDtypeStruct((b, FFN_HALO, D_FF), F32)],
        scratch_shapes=[pltpu.VMEM((FFN_HALO + tm, D_FF), F32)],
        compiler_params=_params(("parallel", "arbitrary")),
        name="ffn_prompt",
    )(x1, *ffn_w)


def _ffn_sample(x1_tm, hist_tm, ffn_w):
    n = x1_tm.shape[0]
    return pl.pallas_call(
        _ffn_sample_kernel,
        grid=(1,),
        in_specs=[_const_spec((n, D_MODEL)), _const_spec(hist_tm.shape)] + _ffn_weight_specs(),
        out_specs=[_whole_spec((n, D_MODEL)), _whole_spec((n, D_FF))],
        out_shape=[jax.ShapeDtypeStruct((n, D_MODEL), F32), jax.ShapeDtypeStruct((n, D_FF), F32)],
        compiler_params=_params(("arbitrary",)),
        name="ffn_sample",
    )(x1_tm, hist_tm, *ffn_w)


def _token_major(a):
    b, t, w = a.shape
    return jnp.transpose(a, (1, 0, 2)).reshape(t * b, w)


def _batch_major(a, b):
    tb, w = a.shape
    return jnp.transpose(a.reshape(tb // b, b, w), (1, 0, 2))


def _stage_weights(w_in):
    wt = w_in.T
    o = 0
    w_q = wt[o:o + Q_W]; o += Q_W
    w_k = wt[o:o + KV_W]; o += KV_W
    w_v = wt[o:o + KV_W]; o += KV_W
    w_qi = wt[o:o + IQ_W]; o += IQ_W
    w_ki = wt[o:o + IDX_DIM]; o += IDX_DIM
    w_wi = wt[o:o + N_IDX_HEADS]; o += N_IDX_HEADS
    w_glu = wt[o:o + 2 * C_CONV]; o += 2 * C_CONV
    w_g = wt[o:o + 2 * D_MODEL]
    head = lambda h: w_q[h * HEAD_DIM:(h + 1) * HEAD_DIM]
    q_pairs = [w for i in range(GROUP) for w in (head(i), head(GROUP + i))]
    zero = jnp.zeros((LANES - IDX_DIM, D_MODEL), w_in.dtype)
    w_att = jnp.concatenate(q_pairs + [w_k, w_qi, w_ki, zero], axis=0).astype(BF16)
    w_t = jnp.concatenate([w_k, w_v, w_ki, w_wi], axis=0).astype(BF16)
    return w_att, w_t, w_glu.astype(BF16), w_g.astype(BF16)


def kernel(x_prompt, x_sample, cache_k, cache_v, cache_kidx, state_conv, state_ffn, page_table, w_in, w_attn_o, w_conv_dw, b_conv_dw, ln_conv_g, ln_conv_b, w_conv_o, w_out, ln1_g, ln1_b, w_ffn_up, w_ffn_gate, w_ffn_dw, b_ffn_dw, w_ffn_down, ln2_g, ln2_b):
    bp, tp, _ = x_prompt.shape
    bs, ts, _ = x_sample.shape
    n_pages = page_table.shape[1]
    past = n_pages * PAGE_SIZE
    assert N_KV_HEADS == 2 and N_IDX_HEADS == SUBLANES

    w_att, w_t, w_glu, w_g = _stage_weights(w_in)
    row2 = lambda a: a.reshape(1, -1)
    mix_w = (w_glu, w_g, w_attn_o.astype(BF16), w_conv_dw, row2(b_conv_dw), row2(ln_conv_g), row2(ln_conv_b),
             w_conv_o.astype(BF16), w_out.astype(BF16), row2(ln1_g), row2(ln1_b))
    ffn_w = (w_ffn_up.astype(BF16), w_ffn_gate.astype(BF16), w_ffn_dw, row2(b_ffn_dw), w_ffn_down.astype(BF16),
             row2(ln2_g), row2(ln2_b))

    qbd, qi, kb, kib, vt3, k_t, v_t, ki_t, wit = _proj(x_prompt, w_att, w_t, 1024)
    seq = lambda a: a.reshape(bp, tp, a.shape[-1])
    attn_p = _prompt_attention(seq(qbd), seq(qi), wit, seq(kib), seq(kb), vt3, min(TOPK_MAX, tp // 4))
    x1_p, conv_tail = _mix_prompt(x_prompt, attn_p, mix_w, 512)
    y_p, ffn_tail = _ffn_prompt(x1_p, ffn_w, 512)
    heads_last = lambda a_t: jnp.transpose(a_t.reshape(bp, N_KV_HEADS, HEAD_DIM, tp), (0, 3, 1, 2))
    k_p = heads_last(k_t)
    v_p = heads_last(v_t)
    ki_p = jnp.transpose(ki_t, (0, 2, 1))
    conv_p = conv_tail[:, CONV_HALO - (CONV_WIDTH - 1):, :]
    ffn_p = ffn_tail[:, FFN_HALO - (FFN_CONV_WIDTH - 1):, :]

    n_s = bs * ts
    qbd_s, qis, _, _, _, ks_t, vs_t, kis_t, wit_s = _proj(x_sample.reshape(1, n_s, D_MODEL), w_att, w_t, n_s)
    ks_t, vs_t, kis_t = ks_t[0], vs_t[0], kis_t[0]
    pt_flat = page_table.reshape(-1).astype(jnp.int32)
    new_rows = lambda a_t: a_t.T.reshape(bs, ts, -1)
    qi32 = qis.reshape(bs, ts * N_IDX_HEADS, IDX_DIM)
    w32 = wit_s.T.reshape(bs, ts * N_IDX_HEADS, 1)
    scores = _sample_scores(pt_flat, qi32, w32, jnp.transpose(cache_kidx, (0, 2, 1)), new_rows(kis_t), n_pages)
    bias = _sample_select(scores, past, ts, min(TOPK_MAX, (past + ts) // 4))
    q_bd = jnp.transpose(qbd_s.reshape(bs, ts, N_HEADS, KV_W), (0, 2, 1, 3)).reshape(bs, N_HEADS * ts, KV_W)
    as_pages = lambda a: jnp.transpose(a, (0, 2, 3, 1))
    o_bd = _sample_attend(pt_flat, q_bd, bias, as_pages(cache_k), as_pages(cache_v),
                          new_rows(ks_t), new_rows(vs_t), n_pages)
    o6 = o_bd.reshape(bs, N_KV_HEADS, GROUP, ts, N_KV_HEADS, HEAD_DIM)
    o_sel = jnp.stack([o6[:, n, :, :, n, :] for n in range(N_KV_HEADS)], axis=1)
    attn_s = jnp.transpose(o_sel, (0, 3, 1, 2, 4)).reshape(bs, ts, Q_W)

    x1_s, glu_s = _mix_sample(_token_major(x_sample), _token_major(attn_s).astype(BF16),
                              jnp.transpose(state_conv, (1, 0, 2)), mix_w)
    y_s, u_s = _ffn_sample(x1_s, jnp.transpose(state_ffn, (1, 0, 2)), ffn_w)
    y_s = _batch_major(y_s, bs)
    conv_s = jnp.concatenate([state_conv, _batch_major(glu_s, bs)], axis=1)[:, -(CONV_WIDTH - 1):, :]
    ffn_s = jnp.concatenate([state_ffn, _batch_major(u_s, bs)], axis=1)[:, -(FFN_CONV_WIDTH - 1):, :]
    k_s = new_rows(ks_t).reshape(bs, ts, N_KV_HEADS, HEAD_DIM)
    v_s = new_rows(vs_t).reshape(bs, ts, N_KV_HEADS, HEAD_DIM)
    ki_s = new_rows(kis_t)

    return (y_p, y_s, k_p, v_p, ki_p, conv_p, ffn_p, k_s, v_s, ki_s, conv_s, ffn_s)
```

```python
import functools

import jax
import jax.numpy as jnp
from jax import lax
from jax.experimental import pallas as pl
from jax.experimental.pallas import tpu as pltpu

D_MODEL = 1024
N_HEADS = 8
HEAD_DIM = 64
N_KV_HEADS = 2
N_IDX_HEADS = 8
IDX_DIM = 64
TOPK_MAX = 256
C_CONV = D_MODEL // 2
CONV_WIDTH = 31
D_FF = 2816
FFN_CONV_WIDTH = 3
LN_EPS = 1e-5
DEPTH = 1
ALPHA = (2.0 * DEPTH) ** 0.25
PAGE_SIZE = 128

Q_W = N_HEADS * HEAD_DIM
KV_W = N_KV_HEADS * HEAD_DIM
IQ_W = N_IDX_HEADS * IDX_DIM
GROUP = N_HEADS // N_KV_HEADS

LANES = 128
SUBLANES = 8
QBD_W = N_HEADS * KV_W
PROJ_W = Q_W + KV_W + IQ_W + LANES
PROJ_T = 2 * KV_W + IDX_DIM + N_IDX_HEADS
TQ = 256
TQ_S = 256
SCORE_GROUP = 8
ATTN_GROUP = 4
CONV_HALO = 32
FFN_HALO = 8
N_BISECT = 19
VT_ROWS = HEAD_DIM + 16
LOG2E = 1.4426950408889634
VMEM_LIMIT = 56 * 1024 * 1024

F32 = jnp.float32
BF16 = jnp.bfloat16
NEG_INF = float("-inf")
POS_INF = float("inf")


def _dot(a, b):
    return jnp.dot(a, b, preferred_element_type=F32)


def _dot_nt(a, b):
    return lax.dot_general(a, b, (((1,), (1,)), ((), ())), preferred_element_type=F32)


def _sigmoid(x):
    return 0.5 * jnp.tanh(0.5 * x) + 0.5


def _layer_norm(x, g, b):
    mu = jnp.mean(x, axis=-1, keepdims=True)
    xc = x - mu
    var = jnp.mean(xc * xc, axis=-1, keepdims=True)
    return xc * lax.rsqrt(var + LN_EPS) * g + b


def _params(sem):
    return pltpu.CompilerParams(dimension_semantics=sem, vmem_limit_bytes=VMEM_LIMIT)


def _whole_spec(shape):
    nd = len(shape)
    return pl.BlockSpec(shape, lambda *_: (0,) * nd)


def _const_spec(shape):
    nd = len(shape)
    return pl.BlockSpec(shape, lambda *_: (0,) * nd, pipeline_mode=pl.Buffered(1))


def _proj_kernel(x_ref, w_ref, wt_ref, qbd_ref, qi_ref, kb_ref, kib_ref, vt_ref, kt_ref, vtf_ref, kit_ref, wit_ref):
    xb = x_ref[...].astype(BF16)
    o = 0
    qp = _dot_nt(xb, w_ref[o:o + Q_W, :]); o += Q_W
    k = _dot_nt(xb, w_ref[o:o + KV_W, :]); o += KV_W
    qi = _dot_nt(xb, w_ref[o:o + IQ_W, :]); o += IQ_W
    ki = _dot_nt(xb, w_ref[o:o + LANES, :])[:, :IDX_DIM]
    qp = (qp * (HEAD_DIM ** -0.5 * LOG2E)).astype(BF16)
    low = lax.broadcasted_iota(jnp.int32, (qp.shape[0], KV_W), 1) < HEAD_DIM
    for h in range(N_HEADS):
        pair = qp[:, (h % GROUP) * KV_W:(h % GROUP + 1) * KV_W]
        qbd_ref[:, h * KV_W:(h + 1) * KV_W] = jnp.where(low if h < GROUP else jnp.logical_not(low), pair, 0.0)
    qi_ref[...] = qi.astype(BF16)
    kb_ref[...] = k.astype(BF16)
    kib_ref[...] = ki.astype(BF16)
    t = _dot_nt(wt_ref[...], xb)
    kt_ref[...] = t[:KV_W, :]
    vtf_ref[...] = t[KV_W:2 * KV_W, :]
    kit_ref[...] = t[2 * KV_W:2 * KV_W + IDX_DIM, :]
    wit_ref[...] = t[2 * KV_W + IDX_DIM:, :] * ((IDX_DIM ** -0.5) * (N_IDX_HEADS ** -0.5))
    vt = t[KV_W:2 * KV_W, :].astype(BF16)
    ones = jnp.ones((VT_ROWS - HEAD_DIM, TQ), BF16)
    for j in range(vt_ref.shape[0]):
        for n in range(N_KV_HEADS):
            vt_ref[j, n * VT_ROWS:n * VT_ROWS + HEAD_DIM, :] = vt[n * HEAD_DIM:(n + 1) * HEAD_DIM, j * TQ:(j + 1) * TQ]
            vt_ref[j, n * VT_ROWS + HEAD_DIM:(n + 1) * VT_ROWS, :] = ones


def _proj(x, w_att, w_t, tm):
    b, t, _ = x.shape
    n = b * t
    tiles = t // tm
    row = lambda w: pl.BlockSpec((tm, w), lambda i: (i, 0))
    col = lambda r: pl.BlockSpec((None, r, tm), lambda i: (i // tiles, 0, i % tiles))
    outs = [(QBD_W, BF16), (IQ_W, BF16), (KV_W, BF16), (IDX_DIM, BF16)]
    return pl.pallas_call(
        _proj_kernel,
        grid=(n // tm,),
        in_specs=[row(D_MODEL), _const_spec((PROJ_W, D_MODEL)), _const_spec((PROJ_T, D_MODEL))],
        out_specs=[row(w) for w, _ in outs]
        + [pl.BlockSpec((tm // TQ, N_KV_HEADS * VT_ROWS, TQ), lambda i: (i, 0, 0)), col(KV_W), col(KV_W), col(IDX_DIM),
           pl.BlockSpec((SUBLANES, tm), lambda i: (0, i))],
        out_shape=[jax.ShapeDtypeStruct((n, w), dt) for w, dt in outs]
        + [jax.ShapeDtypeStruct((n // TQ, N_KV_HEADS * VT_ROWS, TQ), BF16), jax.ShapeDtypeStruct((b, KV_W, t), F32),
           jax.ShapeDtypeStruct((b, KV_W, t), F32), jax.ShapeDtypeStruct((b, IDX_DIM, t), F32),
           jax.ShapeDtypeStruct((SUBLANES, n), F32)],
        compiler_params=_params(("parallel",)),
        name="proj",
    )(x.reshape(n, D_MODEL), w_att, w_t)


def _chunk(ref, c, ch):
    return ref[pl.ds(pl.multiple_of(c * ch, ch), ch), :]


def _fold_rows(x, op):
    parts = [x[r:r + SUBLANES, :] for r in range(0, x.shape[0], SUBLANES)]
    accs = parts[:4]
    for i, part in enumerate(parts[4:]):
        accs[i % 4] = op(accs[i % 4], part)
    while len(accs) > 1:
        accs = [op(accs[a], accs[a + 1]) for a in range(0, len(accs) - 1, 2)] + ([accs[-1]] if len(accs) % 2 else [])
    return accs[0]


def _reduce_keys(s_ref, nk, ch, init, f, op, red):
    w = s_ref.shape[1]

    def body(c, acc):
        return op(acc, _fold_rows(f(_chunk(s_ref, c, ch), c), op))

    acc = lax.fori_loop(0, nk, body, jnp.full((SUBLANES, w), init, F32))
    return red(acc, axis=0, keepdims=True)


def _score_stats_init(w):
    return (jnp.full((SUBLANES, w), NEG_INF, F32), jnp.full((SUBLANES, w), POS_INF, F32),
            jnp.zeros((SUBLANES, w), F32))


def _score_stats_update(stats, x):
    mx, mn, cnt = stats
    live = x > NEG_INF
    return (jnp.maximum(mx, _fold_rows(x, jnp.maximum)),
            jnp.minimum(mn, _fold_rows(jnp.where(live, x, POS_INF), jnp.minimum)),
            cnt + _fold_rows(jnp.where(live, 1.0, 0.0), jnp.add))


def _select_bias(s_ref, bias_ref, nk, ch, topk, stats):
    w = s_ref.shape[1]
    kf = float(topk)
    count = lambda pred: _reduce_keys(s_ref, nk, ch, 0.0, lambda x, c: jnp.where(pred(x, c), 1.0, 0.0),
                                      jnp.add, jnp.sum)
    min_above = lambda t: _reduce_keys(s_ref, nk, ch, POS_INF, lambda x, c: jnp.where(x > t, x, POS_INF),
                                       jnp.minimum, jnp.min)

    neg = jnp.full((1, w), NEG_INF, F32)

    hi = jnp.max(stats[0], axis=0, keepdims=True)
    lo_fin = jnp.min(stats[1], axis=0, keepdims=True)
    n_adm = jnp.sum(stats[2], axis=0, keepdims=True)

    def bisect(_, st):
        lo, lo_fin, hi, n_lo = st
        mid = 0.5 * lo_fin + 0.5 * hi
        c = count(lambda x, _: x > mid)
        ok = c >= kf
        return jnp.where(ok, mid, lo), jnp.where(ok, mid, lo_fin), jnp.where(ok, hi, mid), jnp.where(ok, c, n_lo)

    lo, _, _, n_lo = lax.fori_loop(0, N_BISECT, bisect, (neg, lo_fin, hi, n_adm))

    def peel(st):
        lo, thr, n_gt, done = st
        v = min_above(lo)
        c = count(lambda x, _: x > v)
        found = c < kf
        newly = jnp.logical_and(done < 0.5, found)
        return (jnp.where(jnp.logical_or(found, done > 0.5), lo, v), jnp.where(newly, v, thr),
                jnp.where(newly, c, n_gt), jnp.where(found, 1.0, done))

    _, thr, n_gt, _ = lax.while_loop(lambda st: jnp.min(st[3]) < 0.5, peel,
                                     (lo, lo, n_lo, jnp.where(n_lo <= kf, 1.0, 0.0)))

    need = jnp.where(thr == neg, 0.0, kf - n_gt)
    any_copies = jnp.max(need) > 0.0

    @pl.when(any_copies)
    def _():
        tri = (lax.broadcasted_iota(jnp.int32, (ch, ch), 1)
               <= lax.broadcasted_iota(jnp.int32, (ch, ch), 0)).astype(BF16)

        def write(c, seen):
            x = _chunk(s_ref, c, ch)
            eq = x == thr
            rank = seen + _dot(tri, jnp.where(eq, 1.0, 0.0).astype(BF16))
            sel = jnp.logical_or(x > thr, jnp.logical_and(eq, rank <= need))
            bias_ref[pl.ds(pl.multiple_of(c * ch, ch), ch), :] = jnp.where(sel, 0.0, NEG_INF)
            return rank[ch - 1:ch, :]

        lax.fori_loop(0, nk, write, jnp.zeros((1, w), F32))

    @pl.when(jnp.logical_not(any_copies))
    def _():
        def write(c, carry):
            sel = _chunk(s_ref, c, ch) > thr
            bias_ref[pl.ds(pl.multiple_of(c * ch, ch), ch), :] = jnp.where(sel, 0.0, NEG_INF)
            return carry

        lax.fori_loop(0, nk, write, 0)


def _prompt_attn_kernel(qbd_ref, qi_ref, wit_ref, kib_ref, kb_ref, vt_ref, o_ref, s_ref, bias_ref, acc_ref, lg_ref,
                        lg2_ref, *, topk):
    i = pl.program_id(1)
    nk = i + 1
    qi = qi_ref[...]
    wit = wit_ref[...]
    q_pos = lax.broadcasted_iota(jnp.int32, (TQ, TQ), 1) + i * TQ
    k_off = lax.broadcasted_iota(jnp.int32, (TQ, TQ), 0)

    bufs = (lg_ref, lg2_ref)
    n_pairs = (nk - 1) // 2
    odd_tail = (nk - 1) % 2 == 1

    def products(kc, h, buf):
        buf[h] = _dot_nt(kc, qi[:, h * IDX_DIM:(h + 1) * IDX_DIM])

    def accumulate(h, buf, s):
        return s + jnp.maximum(buf[h], 0.0) * wit[h:h + 1, :]

    def emit(c, s, stats):
        s = jnp.where(k_off + c * TQ <= q_pos, s, NEG_INF)
        s_ref[pl.ds(pl.multiple_of(c * TQ, TQ), TQ), :] = s
        return _score_stats_update(stats, s)

    def score_step(c, parity, stats):
        kc = _chunk(kib_ref, c, TQ)
        s = jnp.zeros((TQ, TQ), F32)
        for h in range(N_IDX_HEADS):
            products(kc, h, bufs[parity])
            s = accumulate(h, bufs[1 - parity], s)
        return emit(c - 1, s, stats)

    def score_last(c, parity, stats):
        s = jnp.zeros((TQ, TQ), F32)
        for h in range(N_IDX_HEADS):
            s = accumulate(h, bufs[parity], s)
        return emit(c, s, stats)

    kc0 = _chunk(kib_ref, 0, TQ)
    for h in range(N_IDX_HEADS):
        products(kc0, h, bufs[0])
    stats = lax.fori_loop(0, n_pairs, lambda p, st: score_step(2 * p + 2, 0, score_step(2 * p + 1, 1, st)),
                          _score_stats_init(TQ))
    stats = lax.cond(odd_tail, lambda: score_last(nk - 1, 1, score_step(nk - 1, 1, stats)),
                     lambda: score_last(nk - 1, 0, stats))
    _select_bias(s_ref, bias_ref, nk, TQ, topk, stats)

    acc_ref[...] = jnp.zeros(acc_ref.shape, F32)

    def logits(c, h, buf):
        rows = pl.ds(pl.multiple_of(c * TQ, TQ), TQ)
        lg = _dot_nt(kb_ref[rows, :], qbd_ref[:, h * KV_W:(h + 1) * KV_W]) + bias_ref[rows, :]
        buf[h] = lg
        return jnp.max(_fold_rows(lg, jnp.maximum), axis=0, keepdims=True)

    def weigh(c, h, buf, m_use, alpha):
        out = slice(h * VT_ROWS, (h + 1) * VT_ROWS)
        p = jnp.exp2(buf[h] - m_use[h:h + 1, :]).astype(BF16)
        vt = vt_ref[c, pl.ds((h // GROUP) * VT_ROWS, VT_ROWS), :]
        acc_ref[out, :] = alpha[h:h + 1, :] * acc_ref[out, :] + _dot(vt, p)

    def advance(m_old, chunk_max):
        m_new = jnp.maximum(m_old, jnp.concatenate(chunk_max, axis=0))
        m_use = jnp.where(m_new == NEG_INF, 0.0, m_new)
        return m_new, m_use, jnp.exp2(m_old - m_use)

    def step(c, parity, state):
        m, m_use, alpha = state
        chunk_max = []
        for h in range(N_HEADS):
            chunk_max.append(logits(c, h, bufs[parity]))
            weigh(c - 1, h, bufs[1 - parity], m_use, alpha)
        return advance(m, chunk_max)

    def finish(c, parity, state):
        for h in range(N_HEADS):
            weigh(c, h, bufs[parity], state[1], state[2])

    state = advance(jnp.full((N_HEADS, TQ), NEG_INF, F32), [logits(0, h, bufs[0]) for h in range(N_HEADS)])
    state = lax.fori_loop(0, n_pairs, lambda p, st: step(2 * p + 2, 0, step(2 * p + 1, 1, st)), state)

    @pl.when(odd_tail)
    def _():
        finish(nk - 1, 1, step(nk - 1, 1, state))

    @pl.when(jnp.logical_not(odd_tail))
    def _():
        finish(nk - 1, 0, state)
    outs = [acc_ref[h * VT_ROWS:h * VT_ROWS + HEAD_DIM, :] / acc_ref[h * VT_ROWS + HEAD_DIM:h * VT_ROWS + HEAD_DIM + 1, :]
            for h in range(N_HEADS)]
    o_ref[...] = jnp.concatenate(outs, axis=0).T.astype(o_ref.dtype)


def _prompt_attention(qbd, qi, wit, kib, kb, vt3, topk):
    b, t, _ = qbd.shape
    nblk = t // TQ
    qblk = lambda w: pl.BlockSpec((None, TQ, w), lambda bi, i: (bi, i, 0))
    full = lambda w: pl.BlockSpec((None, t, w), lambda bi, i: (bi, 0, 0))
    return pl.pallas_call(
        functools.partial(_prompt_attn_kernel, topk=topk),
        grid=(b, nblk),
        in_specs=[qblk(QBD_W), qblk(IQ_W), pl.BlockSpec((SUBLANES, TQ), lambda bi, i: (0, bi * nblk + i)),
                  full(IDX_DIM), full(KV_W), pl.BlockSpec((nblk, N_KV_HEADS * VT_ROWS, TQ), lambda bi, i: (bi, 0, 0))],
        out_specs=qblk(Q_W),
        out_shape=jax.ShapeDtypeStruct((b, t, Q_W), BF16),
        scratch_shapes=[pltpu.VMEM((t, TQ), F32), pltpu.VMEM((t, TQ), F32), pltpu.VMEM((N_HEADS * VT_ROWS, TQ), F32),
                        pltpu.VMEM((N_HEADS, TQ, TQ), F32), pltpu.VMEM((N_HEADS, TQ, TQ), F32)],
        compiler_params=_params(("parallel", "arbitrary")),
        name="prompt_attn",
    )(qbd, qi, wit, kib, kb, vt3)


def _group_spec(group, *block):
    nd = len(block)
    return pl.BlockSpec((group,) + block, lambda bi, pt: (bi,) + (0,) * nd)


def _new_key_block(q, k_new):
    qf = q.astype(F32)
    kf = k_new.astype(BF16).astype(F32)
    lane = lax.broadcasted_iota(jnp.int32, (q.shape[0], PAGE_SIZE), 1)
    blk = jnp.zeros((q.shape[0], PAGE_SIZE), F32)
    for j in range(k_new.shape[0]):
        blk = jnp.where(lane == j, jnp.sum(qf * kf[j:j + 1, :], axis=-1, keepdims=True), blk)
    return blk


def _sample_score_kernel(pt_ref, qi_ref, w_ref, new_ref, kidx_hbm, o_ref, buf, sem, *, n_pages):
    group = qi_ref.shape[0]
    slot = _prefetch_pages(*_page_fetcher(pt_ref, (kidx_hbm,), (buf,), sem, group * n_pages))
    t = qi_ref.shape[1] // N_IDX_HEADS
    for g in range(group):
        qi = qi_ref[g]
        w = w_ref[g]
        blocks = [_dot(qi, buf[slot, g * n_pages + j].astype(BF16)) for j in range(n_pages)]
        blocks.append(_new_key_block(qi, new_ref[g]))
        for j, s in enumerate(blocks):
            s = jnp.maximum(s, 0.0) * w
            o_ref[g * t:(g + 1) * t, j * PAGE_SIZE:(j + 1) * PAGE_SIZE] = jnp.sum(
                s.reshape(t, N_IDX_HEADS, PAGE_SIZE), axis=1)


def _sample_scores(page_table_flat, qi32, w32, kidx_t, ki_new_t, n_pages):
    nb, rows, _ = qi32.shape
    t = rows // N_IDX_HEADS
    lk = (n_pages + 1) * PAGE_SIZE
    g = SCORE_GROUP
    return pl.pallas_call(
        functools.partial(_sample_score_kernel, n_pages=n_pages),
        grid_spec=pltpu.PrefetchScalarGridSpec(
            num_scalar_prefetch=1,
            grid=(nb // g,),
            in_specs=[_group_spec(g, rows, IDX_DIM), _group_spec(g, rows, 1), _group_spec(g, t, IDX_DIM),
                      pl.BlockSpec(memory_space=pl.ANY)],
            out_specs=pl.BlockSpec((g * t, lk), lambda bi, pt: (bi, 0)),
            scratch_shapes=[pltpu.VMEM((2, g * n_pages, IDX_DIM, PAGE_SIZE), F32), pltpu.SemaphoreType.DMA((1, 2))],
        ),
        out_shape=jax.ShapeDtypeStruct((nb * t, lk), F32),
        compiler_params=_params(("arbitrary",)),
        name="sample_scores",
    )(page_table_flat, qi32, w32, ki_new_t, kidx_t)


def _sample_select_kernel(s_ref, bias_ref, sm_ref, bt_ref, *, past, t, topk):
    lk = s_ref.shape[1]
    nk = lk // PAGE_SIZE
    qcol = lax.broadcasted_iota(jnp.int32, (PAGE_SIZE, TQ_S), 1)
    qpos = past + (qcol & (t - 1))
    krow = lax.broadcasted_iota(jnp.int32, (PAGE_SIZE, TQ_S), 0)
    stats = _score_stats_init(TQ_S)
    for c in range(nk):
        cols = slice(c * PAGE_SIZE, (c + 1) * PAGE_SIZE)
        s = jnp.where(krow + c * PAGE_SIZE <= qpos, s_ref[:, cols].T, NEG_INF)
        sm_ref[cols, :] = s
        stats = _score_stats_update(stats, s)
    _select_bias(sm_ref, bt_ref, nk, PAGE_SIZE, topk, stats)
    for c in range(nk):
        cols = slice(c * PAGE_SIZE, (c + 1) * PAGE_SIZE)
        bias_ref[:, cols] = bt_ref[cols, :].T


def _sample_select(scores, past, t, topk):
    n, lk = scores.shape
    assert t & (t - 1) == 0 and TQ_S % t == 0, "token index is taken from the low bits of the query index"
    blk = pl.BlockSpec((TQ_S, lk), lambda i: (i, 0))
    return pl.pallas_call(
        functools.partial(_sample_select_kernel, past=past, t=t, topk=topk),
        grid=(n // TQ_S,),
        in_specs=[blk],
        out_specs=blk,
        out_shape=jax.ShapeDtypeStruct((n, lk), F32),
        scratch_shapes=[pltpu.VMEM((lk, TQ_S), F32), pltpu.VMEM((lk, TQ_S), F32)],
        compiler_params=_params(("parallel",)),
        name="sample_select",
    )(scores)


def _page_fetcher(pt_ref, srcs, bufs, sem, n_copy):
    def copy(a, slot, i, page):
        return pltpu.make_async_copy(srcs[a].at[page], bufs[a].at[slot, i], sem.at[a, slot])

    def start(step, slot):
        def body(i, carry):
            page = pt_ref[step * n_copy + i]
            for a in range(len(srcs)):
                copy(a, slot, i, page).start()
            return carry
        lax.fori_loop(0, n_copy, body, 0)

    def wait(slot):
        def body(i, carry):
            for a in range(len(srcs)):
                copy(a, slot, i, 0).wait()
            return carry
        lax.fori_loop(0, n_copy, body, 0)

    return start, wait


def _prefetch_pages(start, wait):
    s = pl.program_id(0)
    slot = s % 2

    @pl.when(s == 0)
    def _():
        start(0, 0)

    @pl.when(s + 1 < pl.num_programs(0))
    def _():
        start(s + 1, 1 - slot)

    wait(slot)
    return slot


def _sample_attn_kernel(pt_ref, q_ref, bias_ref, k_new, v_new, k_hbm, v_hbm, o_ref, kbuf, vbuf, sem, *, n_pages):
    group = q_ref.shape[0]
    slot = _prefetch_pages(*_page_fetcher(pt_ref, (k_hbm, v_hbm), (kbuf, vbuf), sem, group * n_pages))
    kt = lambda page: page.reshape(KV_W, PAGE_SIZE).astype(BF16)
    t = k_new.shape[1]
    for g in range(group):
        qb = q_ref[g]
        bias = jnp.concatenate([bias_ref[g * t:(g + 1) * t, :]] * (qb.shape[0] // t), axis=0)
        blocks = [_dot(qb, kt(kbuf[slot, g * n_pages + j])) for j in range(n_pages)]
        blocks.append(_new_key_block(qb, k_new[g]))
        lg = jnp.concatenate(blocks, axis=-1) + bias
        m = jnp.max(lg, axis=-1, keepdims=True)
        p = jnp.exp2(lg - m)
        l = jnp.sum(p, axis=-1, keepdims=True)
        pb = p.astype(BF16)
        acc = jnp.zeros(o_ref.shape[1:], F32)
        for j in range(n_pages):
            acc = acc + _dot_nt(pb[:, j * PAGE_SIZE:(j + 1) * PAGE_SIZE], kt(vbuf[slot, g * n_pages + j]))
        p_new = pb[:, n_pages * PAGE_SIZE:].astype(F32)
        v_rows = v_new[g].astype(BF16).astype(F32)
        for j in range(v_rows.shape[0]):
            acc = acc + p_new[:, j:j + 1] * v_rows[j:j + 1, :]
        o_ref[g] = acc / l


def _sample_attend(page_table_flat, q_bd, bias, k_t, v_t, k_new_t, v_new_t, n_pages):
    nb, rows, _ = q_bd.shape
    lk = bias.shape[1]
    t = bias.shape[0] // nb
    g = ATTN_GROUP
    page = (N_KV_HEADS, HEAD_DIM, PAGE_SIZE)
    hbm = pl.BlockSpec(memory_space=pl.ANY)
    page_buf = pltpu.VMEM((2, g * n_pages) + page, F32)
    return pl.pallas_call(
        functools.partial(_sample_attn_kernel, n_pages=n_pages),
        grid_spec=pltpu.PrefetchScalarGridSpec(
            num_scalar_prefetch=1,
            grid=(nb // g,),
            in_specs=[_group_spec(g, rows, KV_W), pl.BlockSpec((g * t, lk), lambda bi, pt: (bi, 0)),
                      _group_spec(g, t, KV_W), _group_spec(g, t, KV_W), hbm, hbm],
            out_specs=_group_spec(g, rows, KV_W),
            scratch_shapes=[page_buf, page_buf, pltpu.SemaphoreType.DMA((2, 2))],
        ),
        out_shape=jax.ShapeDtypeStruct((nb, rows, KV_W), F32),
        compiler_params=_params(("arbitrary",)),
        name="sample_attn",
    )(page_table_flat, q_bd, bias, k_new_t, v_new_t, k_t, v_t)


def _glu(xb, w_glu_ref):
    gi = _dot_nt(xb, w_glu_ref[...])
    return gi[:, :C_CONV] * _sigmoid(gi[:, C_CONV:])


def _gates(xb, attn_b, w_g_ref, w_ao_ref, cols=slice(0, D_MODEL)):
    gate_cols = slice(D_MODEL + cols.start, D_MODEL + cols.stop)
    a_term = _sigmoid(_dot_nt(xb, w_g_ref[cols, :])) * _dot(attn_b, w_ao_ref[:, cols])
    return a_term, _sigmoid(_dot_nt(xb, w_g_ref[gate_cols, :]))


def _mix_tail(x, c, a_term, c_gate, lncg_ref, lncb_ref, w_co_ref, w_out_ref, ln1g_ref, ln1b_ref):
    cn = _layer_norm(c, lncg_ref[...], lncb_ref[...])
    c_branch = _dot((cn * _sigmoid(cn)).astype(BF16), w_co_ref[...])
    merged = a_term + c_gate * c_branch
    h = ALPHA * x + _dot(merged.astype(BF16), w_out_ref[...])
    return _layer_norm(h, ln1g_ref[...], ln1b_ref[...])


def _mix_prompt_kernel(x_ref, attn_ref, w_glu_ref, w_g_ref, w_ao_ref, w_dw_ref, b_dw_ref, lncg_ref, lncb_ref,
                       w_co_ref, w_out_ref, ln1g_ref, ln1b_ref, x1_ref, tail_ref, xp_ref):
    tm = x_ref.shape[0]

    @pl.when(pl.program_id(1) == 0)
    def _():
        xp_ref[0:CONV_HALO, :] = jnp.zeros((CONV_HALO, C_CONV), F32)

    x = x_ref[...]
    xb = x.astype(BF16)
    glu = _glu(xb, w_glu_ref)
    xp_ref[CONV_HALO:CONV_HALO + tm, :] = glu
    tail_ref[...] = glu[tm - CONV_HALO:, :]
    attn_b = attn_ref[...]

    first = CONV_HALO - (CONV_WIDTH - 1)
    rb = 128
    n_blk = C_CONV // LANES
    cols, gate_parts = [], []
    for c0 in range(0, C_CONV, LANES):
        lanes = slice(c0, c0 + LANES)
        q = c0 // LANES
        gate_parts.append(_gates(xb, attn_b, w_g_ref, w_ao_ref, slice(q * D_MODEL // n_blk, (q + 1) * D_MODEL // n_blk)))
        blocks = []
        for r0 in range(0, tm, rb):
            y = jnp.broadcast_to(b_dw_ref[:, lanes], (rb, LANES))
            for b in range(SUBLANES):
                rows = rb + (SUBLANES if b else 0)
                part = None
                for j in range(CONV_WIDTH):
                    if (first + j) % SUBLANES == b:
                        a0 = first + j - b + r0
                        term = w_dw_ref[j:j + 1, lanes] * xp_ref[a0:a0 + rows, lanes]
                        part = term if part is None else part + term
                y = y + part[b:b + rb, :]
            blocks.append(y)
        cols.append(jnp.concatenate(blocks, axis=0))
    c = jnp.concatenate(cols, axis=-1)
    a_term = jnp.concatenate([p[0] for p in gate_parts], axis=-1)
    c_gate = jnp.concatenate([p[1] for p in gate_parts], axis=-1)
    xp_ref[0:CONV_HALO, :] = xp_ref[tm:tm + CONV_HALO, :]

    x1_ref[...] = _mix_tail(x, c, a_term, c_gate, lncg_ref, lncb_ref, w_co_ref, w_out_ref, ln1g_ref, ln1b_ref)


def _mix_sample_kernel(x_ref, attn_ref, hist_ref, w_glu_ref, w_g_ref, w_ao_ref, w_dw_ref, b_dw_ref, lncg_ref,
                       lncb_ref, w_co_ref, w_out_ref, ln1g_ref, ln1b_ref, x1_ref, glu_ref):
    nb = hist_ref.shape[1]
    t = x_ref.shape[0] // nb
    n_hist = CONV_WIDTH - 1
    x = x_ref[...]
    xb = x.astype(BF16)
    glu = _glu(xb, w_glu_ref)
    glu_ref[...] = glu

    def slab(m):
        return hist_ref[m] if m < n_hist else glu[(m - n_hist) * nb:(m - n_hist + 1) * nb, :]

    outs = []
    for ti in range(t):
        acc = jnp.broadcast_to(b_dw_ref[...], (nb, C_CONV))
        for j in range(CONV_WIDTH):
            acc = acc + w_dw_ref[j:j + 1, :] * slab(ti + j)
        outs.append(acc)
    c = jnp.concatenate(outs, axis=0)
    a_term, c_gate = _gates(xb, attn_ref[...], w_g_ref, w_ao_ref)
    x1_ref[...] = _mix_tail(x, c, a_term, c_gate, lncg_ref, lncb_ref, w_co_ref, w_out_ref, ln1g_ref, ln1b_ref)


def _mix_weight_specs():
    return [
        _const_spec((2 * C_CONV, D_MODEL)), _const_spec((2 * D_MODEL, D_MODEL)), _const_spec((Q_W, D_MODEL)),
        _const_spec((CONV_WIDTH, C_CONV)), _const_spec((1, C_CONV)), _const_spec((1, C_CONV)),
        _const_spec((1, C_CONV)), _const_spec((C_CONV, D_MODEL)), _const_spec((D_MODEL, D_MODEL)),
        _const_spec((1, D_MODEL)), _const_spec((1, D_MODEL)),
    ]


def _mix_prompt(x, attn, mix_w, tm):
    b, t, _ = x.shape
    rows = lambda w: pl.BlockSpec((None, tm, w), lambda bi, i: (bi, i, 0))
    return pl.pallas_call(
        _mix_prompt_kernel,
        grid=(b, t // tm),
        in_specs=[rows(D_MODEL), rows(Q_W)] + _mix_weight_specs(),
        out_specs=[rows(D_MODEL), pl.BlockSpec((None, CONV_HALO, C_CONV), lambda bi, i: (bi, 0, 0))],
        out_shape=[jax.ShapeDtypeStruct((b, t, D_MODEL), F32), jax.ShapeDtypeStruct((b, CONV_HALO, C_CONV), F32)],
        scratch_shapes=[pltpu.VMEM((CONV_HALO + tm, C_CONV), F32)],
        compiler_params=_params(("parallel", "arbitrary")),
        name="mix_prompt",
    )(x, attn, *mix_w)


def _mix_sample(x_tm, attn_tm, hist_tm, mix_w):
    n = x_tm.shape[0]
    return pl.pallas_call(
        _mix_sample_kernel,
        grid=(1,),
        in_specs=[_const_spec((n, D_MODEL)), _const_spec((n, Q_W)), _const_spec(hist_tm.shape)] + _mix_weight_specs(),
        out_specs=[_whole_spec((n, D_MODEL)), _whole_spec((n, C_CONV))],
        out_shape=[jax.ShapeDtypeStruct((n, D_MODEL), F32), jax.ShapeDtypeStruct((n, C_CONV), F32)],
        compiler_params=_params(("arbitrary",)),
        name="mix_sample",
    )(x_tm, attn_tm, hist_tm, *mix_w)


def _ffn_tail(x1, uc, gate, w_down_ref, ln2g_ref, ln2b_ref):
    f = _dot((jax.nn.gelu(uc) * gate).astype(BF16), w_down_ref[...])
    return _layer_norm(ALPHA * x1 + f, ln2g_ref[...], ln2b_ref[...])


def _ffn_prompt_kernel(x1_ref, w_up_ref, w_gate_ref, w_dw_ref, b_dw_ref, w_down_ref, ln2g_ref, ln2b_ref,
                       y_ref, tail_ref, up_ref):
    tm = x1_ref.shape[0]

    @pl.when(pl.program_id(1) == 0)
    def _():
        up_ref[0:FFN_HALO, :] = jnp.zeros((FFN_HALO, D_FF), F32)

    x1 = x1_ref[...]
    x1b = x1.astype(BF16)
    u = _dot(x1b, w_up_ref[...])
    gate = _dot(x1b, w_gate_ref[...])
    up_ref[FFN_HALO:FFN_HALO + tm, :] = u
    tail_ref[...] = u[tm - FFN_HALO:, :]
    first = FFN_HALO - (FFN_CONV_WIDTH - 1)
    uc = b_dw_ref[...]
    for j in range(FFN_CONV_WIDTH):
        uc = uc + w_dw_ref[j:j + 1, :] * up_ref[first + j:first + j + tm, :]
    up_ref[0:FFN_HALO, :] = up_ref[tm:tm + FFN_HALO, :]
    y_ref[...] = _ffn_tail(x1, uc, gate, w_down_ref, ln2g_ref, ln2b_ref)


def _ffn_sample_kernel(x1_ref, hist_ref, w_up_ref, w_gate_ref, w_dw_ref, b_dw_ref, w_down_ref, ln2g_ref, ln2b_ref,
                       y_ref, u_ref):
    nb = hist_ref.shape[1]
    t = x1_ref.shape[0] // nb
    n_hist = FFN_CONV_WIDTH - 1
    x1 = x1_ref[...]
    x1b = x1.astype(BF16)
    u = _dot(x1b, w_up_ref[...])
    gate = _dot(x1b, w_gate_ref[...])
    u_ref[...] = u

    def slab(m):
        return hist_ref[m] if m < n_hist else u[(m - n_hist) * nb:(m - n_hist + 1) * nb, :]

    outs = []
    for ti in range(t):
        acc = jnp.broadcast_to(b_dw_ref[...], (nb, D_FF))
        for j in range(FFN_CONV_WIDTH):
            acc = acc + w_dw_ref[j:j + 1, :] * slab(ti + j)
        outs.append(acc)
    uc = jnp.concatenate(outs, axis=0)
    y_ref[...] = _ffn_tail(x1, uc, gate, w_down_ref, ln2g_ref, ln2b_ref)


def _ffn_weight_specs():
    return [
        _const_spec((D_MODEL, D_FF)), _const_spec((D_MODEL, D_FF)), _const_spec((FFN_CONV_WIDTH, D_FF)),
        _const_spec((1, D_FF)), _const_spec((D_FF, D_MODEL)), _const_spec((1, D_MODEL)), _const_spec((1, D_MODEL)),
    ]


def _ffn_prompt(x1, ffn_w, tm):
    b, t, _ = x1.shape
    rows = pl.BlockSpec((None, tm, D_MODEL), lambda bi, i: (bi, i, 0))
    return pl.pallas_call(
        _ffn_prompt_kernel,
        grid=(b, t // tm),
        in_specs=[rows] + _ffn_weight_specs(),
        out_specs=[rows, pl.BlockSpec((None, FFN_HALO, D_FF), lambda bi, i: (bi, 0, 0))],
        out_shape=[jax.ShapeDtypeStruct((b, t, D_MODEL), F32), jax.ShapeDtypeStruct((b, FFN_HALO, D_FF), F32)],
        scratch_shapes=[pltpu.VMEM((FFN_HALO + tm, D_FF), F32)],
        compiler_params=_params(("parallel", "arbitrary")),
        name="ffn_prompt",
    )(x1, *ffn_w)


def _ffn_sample(x1_tm, hist_tm, ffn_w):
    n = x1_tm.shape[0]
    return pl.pallas_call(
        _ffn_sample_kernel,
        grid=(1,),
        in_specs=[_const_spec((n, D_MODEL)), _const_spec(hist_tm.shape)] + _ffn_weight_specs(),
        out_specs=[_whole_spec((n, D_MODEL)), _whole_spec((n, D_FF))],
        out_shape=[jax.ShapeDtypeStruct((n, D_MODEL), F32), jax.ShapeDtypeStruct((n, D_FF), F32)],
        compiler_params=_params(("arbitrary",)),
        name="ffn_sample",
    )(x1_tm, hist_tm, *ffn_w)


def _token_major(a):
    b, t, w = a.shape
    return jnp.transpose(a, (1, 0, 2)).reshape(t * b, w)


def _batch_major(a, b):
    tb, w = a.shape
    return jnp.transpose(a.reshape(tb // b, b, w), (1, 0, 2))


def _stage_weights(w_in):
    wt = w_in.T
    o = 0
    w_q = wt[o:o + Q_W]; o += Q_W
    w_k = wt[o:o + KV_W]; o += KV_W
    w_v = wt[o:o + KV_W]; o += KV_W
    w_qi = wt[o:o + IQ_W]; o += IQ_W
    w_ki = wt[o:o + IDX_DIM]; o += IDX_DIM
    w_wi = wt[o:o + N_IDX_HEADS]; o += N_IDX_HEADS
    w_glu = wt[o:o + 2 * C_CONV]; o += 2 * C_CONV
    w_g = wt[o:o + 2 * D_MODEL]
    head = lambda h: w_q[h * HEAD_DIM:(h + 1) * HEAD_DIM]
    q_pairs = [w for i in range(GROUP) for w in (head(i), head(GROUP + i))]
    zero = jnp.zeros((LANES - IDX_DIM, D_MODEL), w_in.dtype)
    w_att = jnp.concatenate(q_pairs + [w_k, w_qi, w_ki, zero], axis=0).astype(BF16)
    w_t = jnp.concatenate([w_k, w_v, w_ki, w_wi], axis=0).astype(BF16)
    return w_att, w_t, w_glu.astype(BF16), w_g.astype(BF16)


def kernel(x_prompt, x_sample, cache_k, cache_v, cache_kidx, state_conv, state_ffn, page_table, w_in, w_attn_o, w_conv_dw, b_conv_dw, ln_conv_g, ln_conv_b, w_conv_o, w_out, ln1_g, ln1_b, w_ffn_up, w_ffn_gate, w_ffn_dw, b_ffn_dw, w_ffn_down, ln2_g, ln2_b):
    bp, tp, _ = x_prompt.shape
    bs, ts, _ = x_sample.shape
    n_pages = page_table.shape[1]
    past = n_pages * PAGE_SIZE
    assert N_KV_HEADS == 2 and N_IDX_HEADS == SUBLANES

    w_att, w_t, w_glu, w_g = _stage_weights(w_in)
    row2 = lambda a: a.reshape(1, -1)
    mix_w = (w_glu, w_g, w_attn_o.astype(BF16), w_conv_dw, row2(b_conv_dw), row2(ln_conv_g), row2(ln_conv_b),
             w_conv_o.astype(BF16), w_out.astype(BF16), row2(ln1_g), row2(ln1_b))
    ffn_w = (w_ffn_up.astype(BF16), w_ffn_gate.astype(BF16), w_ffn_dw, row2(b_ffn_dw), w_ffn_down.astype(BF16),
             row2(ln2_g), row2(ln2_b))

    qbd, qi, kb, kib, vt3, k_t, v_t, ki_t, wit = _proj(x_prompt, w_att, w_t, 1024)
    seq = lambda a: a.reshape(bp, tp, a.shape[-1])
    attn_p = _prompt_attention(seq(qbd), seq(qi), wit, seq(kib), seq(kb), vt3, min(TOPK_MAX, tp // 4))
    x1_p, conv_tail = _mix_prompt(x_prompt, attn_p, mix_w, 512)
    y_p, ffn_tail = _ffn_prompt(x1_p, ffn_w, 512)
    heads_last = lambda a_t: jnp.transpose(a_t.reshape(bp, N_KV_HEADS, HEAD_DIM, tp), (0, 3, 1, 2))
    k_p = heads_last(k_t)
    v_p = heads_last(v_t)
    ki_p = jnp.transpose(ki_t, (0, 2, 1))
    conv_p = conv_tail[:, CONV_HALO - (CONV_WIDTH - 1):, :]
    ffn_p = ffn_tail[:, FFN_HALO - (FFN_CONV_WIDTH - 1):, :]

    n_s = bs * ts
    qbd_s, qis, _, _, _, ks_t, vs_t, kis_t, wit_s = _proj(x_sample.reshape(1, n_s, D_MODEL), w_att, w_t, n_s)
    ks_t, vs_t, kis_t = ks_t[0], vs_t[0], kis_t[0]
    pt_flat = page_table.reshape(-1).astype(jnp.int32)
    new_rows = lambda a_t: a_t.T.reshape(bs, ts, -1)
    qi32 = qis.reshape(bs, ts * N_IDX_HEADS, IDX_DIM)
    w32 = wit_s.T.reshape(bs, ts * N_IDX_HEADS, 1)
    scores = _sample_scores(pt_flat, qi32, w32, jnp.transpose(cache_kidx, (0, 2, 1)), new_rows(kis_t), n_pages)
    bias = _sample_select(scores, past, ts, min(TOPK_MAX, (past + ts) // 4))
    q_bd = jnp.transpose(qbd_s.reshape(bs, ts, N_HEADS, KV_W), (0, 2, 1, 3)).reshape(bs, N_HEADS * ts, KV_W)
    as_pages = lambda a: jnp.transpose(a, (0, 2, 3, 1))
    o_bd = _sample_attend(pt_flat, q_bd, bias, as_pages(cache_k), as_pages(cache_v),
                          new_rows(ks_t), new_rows(vs_t), n_pages)
    o6 = o_bd.reshape(bs, N_KV_HEADS, GROUP, ts, N_KV_HEADS, HEAD_DIM)
    o_sel = jnp.stack([o6[:, n, :, :, n, :] for n in range(N_KV_HEADS)], axis=1)
    attn_s = jnp.transpose(o_sel, (0, 3, 1, 2, 4)).reshape(bs, ts, Q_W)

    x1_s, glu_s = _mix_sample(_token_major(x_sample), _token_major(attn_s).astype(BF16),
                              jnp.transpose(state_conv, (1, 0, 2)), mix_w)
    y_s, u_s = _ffn_sample(x1_s, jnp.transpose(state_ffn, (1, 0, 2)), ffn_w)
    y_s = _batch_major(y_s, bs)
    conv_s = jnp.concatenate([state_conv, _batch_major(glu_s, bs)], axis=1)[:, -(CONV_WIDTH - 1):, :]
    ffn_s = jnp.concatenate([state_ffn, _batch_major(u_s, bs)], axis=1)[:, -(FFN_CONV_WIDTH - 1):, :]
    k_s = new_rows(ks_t).reshape(bs, ts, N_KV_HEADS, HEAD_DIM)
    v_s = new_rows(vs_t).reshape(bs, ts, N_KV_HEADS, HEAD_DIM)
    ki_s = new_rows(kis_t)

    return (y_p, y_s, k_p, v_p, ki_p, conv_p, ffn_p, k_s, v_s, ki_s, conv_s, ffn_s)
```

```python
import functools

import jax
import jax.numpy as jnp
from jax import lax
from jax.experimental import pallas as pl
from jax.experimental.pallas import tpu as pltpu

D_MODEL = 1024
N_HEADS = 8
HEAD_DIM = 64
N_KV_HEADS = 2
N_IDX_HEADS = 8
IDX_DIM = 64
TOPK_MAX = 256
C_CONV = D_MODEL // 2
CONV_WIDTH = 31
D_FF = 2816
FFN_CONV_WIDTH = 3
LN_EPS = 1e-5
DEPTH = 1
ALPHA = (2.0 * DEPTH) ** 0.25
PAGE_SIZE = 128

Q_W = N_HEADS * HEAD_DIM
KV_W = N_KV_HEADS * HEAD_DIM
IQ_W = N_IDX_HEADS * IDX_DIM
GROUP = N_HEADS // N_KV_HEADS

LANES = 128
SUBLANES = 8
QBD_W = N_HEADS * KV_W
PROJ_W = Q_W + KV_W + IQ_W + LANES
PROJ_T = 2 * KV_W + IDX_DIM + N_IDX_HEADS
TQ = 256
TQ_S = 256
SCORE_GROUP = 8
ATTN_GROUP = 4
CONV_HALO = 32
FFN_HALO = 8
N_BISECT = 19
VT_ROWS = HEAD_DIM + 16
LOG2E = 1.4426950408889634
VMEM_LIMIT = 56 * 1024 * 1024

F32 = jnp.float32
BF16 = jnp.bfloat16
NEG_INF = float("-inf")
POS_INF = float("inf")


def _dot(a, b):
    return jnp.dot(a, b, preferred_element_type=F32)


def _dot_nt(a, b):
    return lax.dot_general(a, b, (((1,), (1,)), ((), ())), preferred_element_type=F32)


def _sigmoid(x):
    return 0.5 * jnp.tanh(0.5 * x) + 0.5


def _layer_norm(x, g, b):
    mu = jnp.mean(x, axis=-1, keepdims=True)
    xc = x - mu
    var = jnp.mean(xc * xc, axis=-1, keepdims=True)
    return xc * lax.rsqrt(var + LN_EPS) * g + b


def _params(sem):
    return pltpu.CompilerParams(dimension_semantics=sem, vmem_limit_bytes=VMEM_LIMIT)


def _whole_spec(shape):
    nd = len(shape)
    return pl.BlockSpec(shape, lambda *_: (0,) * nd)


def _const_spec(shape):
    nd = len(shape)
    return pl.BlockSpec(shape, lambda *_: (0,) * nd, pipeline_mode=pl.Buffered(1))


def _proj_kernel(x_ref, w_ref, wt_ref, qbd_ref, qi_ref, kb_ref, kib_ref, vt_ref, kt_ref, vtf_ref, kit_ref, wit_ref):
    xb = x_ref[...].astype(BF16)
    o = 0
    qp = _dot_nt(xb, w_ref[o:o + Q_W, :]); o += Q_W
    k = _dot_nt(xb, w_ref[o:o + KV_W, :]); o += KV_W
    qi = _dot_nt(xb, w_ref[o:o + IQ_W, :]); o += IQ_W
    ki = _dot_nt(xb, w_ref[o:o + LANES, :])[:, :IDX_DIM]
    qp = (qp * (HEAD_DIM ** -0.5 * LOG2E)).astype(BF16)
    low = lax.broadcasted_iota(jnp.int32, (qp.shape[0], KV_W), 1) < HEAD_DIM
    for h in range(N_HEADS):
        pair = qp[:, (h % GROUP) * KV_W:(h % GROUP + 1) * KV_W]
        qbd_ref[:, h * KV_W:(h + 1) * KV_W] = jnp.where(low if h < GROUP else jnp.logical_not(low), pair, 0.0)
    qi_ref[...] = qi.astype(BF16)
    kb_ref[...] = k.astype(BF16)
    kib_ref[...] = ki.astype(BF16)
    t = _dot_nt(wt_ref[...], xb)
    kt_ref[...] = t[:KV_W, :]
    vtf_ref[...] = t[KV_W:2 * KV_W, :]
    kit_ref[...] = t[2 * KV_W:2 * KV_W + IDX_DIM, :]
    wit_ref[...] = t[2 * KV_W + IDX_DIM:, :] * ((IDX_DIM ** -0.5) * (N_IDX_HEADS ** -0.5))
    vt = t[KV_W:2 * KV_W, :].astype(BF16)
    ones = jnp.ones((VT_ROWS - HEAD_DIM, TQ), BF16)
    for j in range(vt_ref.shape[0]):
        for n in range(N_KV_HEADS):
            vt_ref[j, n * VT_ROWS:n * VT_ROWS + HEAD_DIM, :] = vt[n * HEAD_DIM:(n + 1) * HEAD_DIM, j * TQ:(j + 1) * TQ]
            vt_ref[j, n * VT_ROWS + HEAD_DIM:(n + 1) * VT_ROWS, :] = ones


def _proj(x, w_att, w_t, tm):
    b, t, _ = x.shape
    n = b * t
    tiles = t // tm
    row = lambda w: pl.BlockSpec((tm, w), lambda i: (i, 0))
    col = lambda r: pl.BlockSpec((None, r, tm), lambda i: (i // tiles, 0, i % tiles))
    outs = [(QBD_W, BF16), (IQ_W, BF16), (KV_W, BF16), (IDX_DIM, BF16)]
    return pl.pallas_call(
        _proj_kernel,
        grid=(n // tm,),
        in_specs=[row(D_MODEL), _const_spec((PROJ_W, D_MODEL)), _const_spec((PROJ_T, D_MODEL))],
        out_specs=[row(w) for w, _ in outs]
        + [pl.BlockSpec((tm // TQ, N_KV_HEADS * VT_ROWS, TQ), lambda i: (i, 0, 0)), col(KV_W), col(KV_W), col(IDX_DIM),
           pl.BlockSpec((SUBLANES, tm), lambda i: (0, i))],
        out_shape=[jax.ShapeDtypeStruct((n, w), dt) for w, dt in outs]
        + [jax.ShapeDtypeStruct((n // TQ, N_KV_HEADS * VT_ROWS, TQ), BF16), jax.ShapeDtypeStruct((b, KV_W, t), F32),
           jax.ShapeDtypeStruct((b, KV_W, t), F32), jax.ShapeDtypeStruct((b, IDX_DIM, t), F32),
           jax.ShapeDtypeStruct((SUBLANES, n), F32)],
        compiler_params=_params(("parallel",)),
        name="proj",
    )(x.reshape(n, D_MODEL), w_att, w_t)


def _chunk(ref, c, ch):
    return ref[pl.ds(pl.multiple_of(c * ch, ch), ch), :]


def _fold_rows(x, op):
    parts = [x[r:r + SUBLANES, :] for r in range(0, x.shape[0], SUBLANES)]
    accs = parts[:4]
    for i, part in enumerate(parts[4:]):
        accs[i % 4] = op(accs[i % 4], part)
    while len(accs) > 1:
        accs = [op(accs[a], accs[a + 1]) for a in range(0, len(accs) - 1, 2)] + ([accs[-1]] if len(accs) % 2 else [])
    return accs[0]


def _reduce_keys(s_ref, nk, ch, init, f, op, red):
    w = s_ref.shape[1]

    def body(c, acc):
        return op(acc, _fold_rows(f(_chunk(s_ref, c, ch), c), op))

    acc = lax.fori_loop(0, nk, body, jnp.full((SUBLANES, w), init, F32))
    return red(acc, axis=0, keepdims=True)


def _score_stats_init(w):
    return (jnp.full((SUBLANES, w), NEG_INF, F32), jnp.full((SUBLANES, w), POS_INF, F32),
            jnp.zeros((SUBLANES, w), F32))


def _score_stats_update(stats, x):
    mx, mn, cnt = stats
    live = x > NEG_INF
    return (jnp.maximum(mx, _fold_rows(x, jnp.maximum)),
            jnp.minimum(mn, _fold_rows(jnp.where(live, x, POS_INF), jnp.minimum)),
            cnt + _fold_rows(jnp.where(live, 1.0, 0.0), jnp.add))


def _select_bias(s_ref, bias_ref, nk, ch, topk, stats):
    w = s_ref.shape[1]
    kf = float(topk)
    def count(pred):
        def body(c, accs):
            hit = pred(_chunk(s_ref, c, ch), c)
            accs = list(accs)
            for i, r in enumerate(range(0, ch, SUBLANES)):
                k = i % len(accs)
                accs[k] = jnp.where(hit[r:r + SUBLANES, :], accs[k] + 1.0, accs[k])
            return tuple(accs)

        accs = lax.fori_loop(0, nk, body, tuple(jnp.zeros((SUBLANES, w), F32) for _ in range(4)))
        return jnp.sum((accs[0] + accs[1]) + (accs[2] + accs[3]), axis=0, keepdims=True)
    min_above = lambda t: _reduce_keys(s_ref, nk, ch, POS_INF, lambda x, c: jnp.where(x > t, x, POS_INF),
                                       jnp.minimum, jnp.min)

    neg = jnp.full((1, w), NEG_INF, F32)

    hi = jnp.max(stats[0], axis=0, keepdims=True)
    lo_fin = jnp.min(stats[1], axis=0, keepdims=True)
    n_adm = jnp.sum(stats[2], axis=0, keepdims=True)

    def bisect(_, st):
        lo, lo_fin, hi, n_lo = st
        mid = 0.5 * lo_fin + 0.5 * hi
        c = count(lambda x, _: x > mid)
        ok = c >= kf
        return jnp.where(ok, mid, lo), jnp.where(ok, mid, lo_fin), jnp.where(ok, hi, mid), jnp.where(ok, c, n_lo)

    lo, _, _, n_lo = lax.fori_loop(0, N_BISECT, bisect, (neg, lo_fin, hi, n_adm))

    def peel(st):
        lo, thr, n_gt, done = st
        v = min_above(lo)
        c = count(lambda x, _: x > v)
        found = c < kf
        newly = jnp.logical_and(done < 0.5, found)
        return (jnp.where(jnp.logical_or(found, done > 0.5), lo, v), jnp.where(newly, v, thr),
                jnp.where(newly, c, n_gt), jnp.where(found, 1.0, done))

    _, thr, n_gt, _ = lax.while_loop(lambda st: jnp.min(st[3]) < 0.5, peel,
                                     (lo, lo, n_lo, jnp.where(n_lo <= kf, 1.0, 0.0)))

    need = jnp.where(thr == neg, 0.0, kf - n_gt)
    any_copies = jnp.max(need) > 0.0

    @pl.when(any_copies)
    def _():
        tri = (lax.broadcasted_iota(jnp.int32, (ch, ch), 1)
               <= lax.broadcasted_iota(jnp.int32, (ch, ch), 0)).astype(BF16)

        def write(c, seen):
            x = _chunk(s_ref, c, ch)
            eq = x == thr
            rank = seen + _dot(tri, jnp.where(eq, 1.0, 0.0).astype(BF16))
            sel = jnp.logical_or(x > thr, jnp.logical_and(eq, rank <= need))
            bias_ref[pl.ds(pl.multiple_of(c * ch, ch), ch), :] = jnp.where(sel, 0.0, NEG_INF)
            return rank[ch - 1:ch, :]

        lax.fori_loop(0, nk, write, jnp.zeros((1, w), F32))

    @pl.when(jnp.logical_not(any_copies))
    def _():
        def write(c, carry):
            sel = _chunk(s_ref, c, ch) > thr
            bias_ref[pl.ds(pl.multiple_of(c * ch, ch), ch), :] = jnp.where(sel, 0.0, NEG_INF)
            return carry

        lax.fori_loop(0, nk, write, 0)


def _prompt_attn_kernel(qbd_ref, qi_ref, wit_ref, kib_ref, kb_ref, vt_ref, o_ref, s_ref, bias_ref, acc_ref, lg_ref,
                        lg2_ref, *, topk):
    i = pl.program_id(1)
    nk = i + 1
    qi = qi_ref[...]
    wit = wit_ref[...]
    q_pos = lax.broadcasted_iota(jnp.int32, (TQ, TQ), 1) + i * TQ
    k_off = lax.broadcasted_iota(jnp.int32, (TQ, TQ), 0)

    bufs = (lg_ref, lg2_ref)
    n_pairs = (nk - 1) // 2
    odd_tail = (nk - 1) % 2 == 1

    def products(kc, h, buf):
        buf[h] = _dot_nt(kc, qi[:, h * IDX_DIM:(h + 1) * IDX_DIM])

    def accumulate(h, buf, s):
        return s + jnp.maximum(buf[h], 0.0) * wit[h:h + 1, :]

    def emit(c, s, stats):
        s = jnp.where(k_off + c * TQ <= q_pos, s, NEG_INF)
        s_ref[pl.ds(pl.multiple_of(c * TQ, TQ), TQ), :] = s
        return _score_stats_update(stats, s)

    def score_step(c, parity, stats):
        kc = _chunk(kib_ref, c, TQ)
        s = jnp.zeros((TQ, TQ), F32)
        for h in range(N_IDX_HEADS):
            products(kc, h, bufs[parity])
            s = accumulate(h, bufs[1 - parity], s)
        return emit(c - 1, s, stats)

    def score_last(c, parity, stats):
        s = jnp.zeros((TQ, TQ), F32)
        for h in range(N_IDX_HEADS):
            s = accumulate(h, bufs[parity], s)
        return emit(c, s, stats)

    kc0 = _chunk(kib_ref, 0, TQ)
    for h in range(N_IDX_HEADS):
        products(kc0, h, bufs[0])
    stats = lax.fori_loop(0, n_pairs, lambda p, st: score_step(2 * p + 2, 0, score_step(2 * p + 1, 1, st)),
                          _score_stats_init(TQ))
    stats = lax.cond(odd_tail, lambda: score_last(nk - 1, 1, score_step(nk - 1, 1, stats)),
                     lambda: score_last(nk - 1, 0, stats))
    _select_bias(s_ref, bias_ref, nk, TQ, topk, stats)

    acc_ref[...] = jnp.zeros(acc_ref.shape, F32)

    def logits(c, h, buf):
        rows = pl.ds(pl.multiple_of(c * TQ, TQ), TQ)
        lg = _dot_nt(kb_ref[rows, :], qbd_ref[:, h * KV_W:(h + 1) * KV_W]) + bias_ref[rows, :]
        buf[h] = lg
        return jnp.max(_fold_rows(lg, jnp.maximum), axis=0, keepdims=True)

    def weigh(c, h, buf, m_use, alpha):
        out = slice(h * VT_ROWS, (h + 1) * VT_ROWS)
        p = jnp.exp2(buf[h] - m_use[h:h + 1, :]).astype(BF16)
        vt = vt_ref[c, pl.ds((h // GROUP) * VT_ROWS, VT_ROWS), :]
        acc_ref[out, :] = alpha[h:h + 1, :] * acc_ref[out, :] + _dot(vt, p)

    def advance(m_old, chunk_max):
        m_new = jnp.maximum(m_old, jnp.concatenate(chunk_max, axis=0))
        m_use = jnp.where(m_new == NEG_INF, 0.0, m_new)
        return m_new, m_use, jnp.exp2(m_old - m_use)

    def step(c, parity, state):
        m, m_use, alpha = state
        chunk_max = []
        for h in range(N_HEADS):
            chunk_max.append(logits(c, h, bufs[parity]))
            weigh(c - 1, h, bufs[1 - parity], m_use, alpha)
        return advance(m, chunk_max)

    def finish(c, parity, state):
        for h in range(N_HEADS):
            weigh(c, h, bufs[parity], state[1], state[2])

    state = advance(jnp.full((N_HEADS, TQ), NEG_INF, F32), [logits(0, h, bufs[0]) for h in range(N_HEADS)])
    state = lax.fori_loop(0, n_pairs, lambda p, st: step(2 * p + 2, 0, step(2 * p + 1, 1, st)), state)

    @pl.when(odd_tail)
    def _():
        finish(nk - 1, 1, step(nk - 1, 1, state))

    @pl.when(jnp.logical_not(odd_tail))
    def _():
        finish(nk - 1, 0, state)
    outs = [acc_ref[h * VT_ROWS:h * VT_ROWS + HEAD_DIM, :] / acc_ref[h * VT_ROWS + HEAD_DIM:h * VT_ROWS + HEAD_DIM + 1, :]
            for h in range(N_HEADS)]
    o_ref[...] = jnp.concatenate(outs, axis=0).T.astype(o_ref.dtype)


def _prompt_attention(qbd, qi, wit, kib, kb, vt3, topk):
    b, t, _ = qbd.shape
    nblk = t // TQ
    qblk = lambda w: pl.BlockSpec((None, TQ, w), lambda bi, i: (bi, i, 0))
    full = lambda w: pl.BlockSpec((None, t, w), lambda bi, i: (bi, 0, 0))
    return pl.pallas_call(
        functools.partial(_prompt_attn_kernel, topk=topk),
        grid=(b, nblk),
        in_specs=[qblk(QBD_W), qblk(IQ_W), pl.BlockSpec((SUBLANES, TQ), lambda bi, i: (0, bi * nblk + i)),
                  full(IDX_DIM), full(KV_W), pl.BlockSpec((nblk, N_KV_HEADS * VT_ROWS, TQ), lambda bi, i: (bi, 0, 0))],
        out_specs=qblk(Q_W),
        out_shape=jax.ShapeDtypeStruct((b, t, Q_W), BF16),
        scratch_shapes=[pltpu.VMEM((t, TQ), F32), pltpu.VMEM((t, TQ), F32), pltpu.VMEM((N_HEADS * VT_ROWS, TQ), F32),
                        pltpu.VMEM((N_HEADS, TQ, TQ), F32), pltpu.VMEM((N_HEADS, TQ, TQ), F32)],
        compiler_params=_params(("parallel", "arbitrary")),
        name="prompt_attn",
    )(qbd, qi, wit, kib, kb, vt3)


def _group_spec(group, *block):
    nd = len(block)
    return pl.BlockSpec((group,) + block, lambda bi, pt: (bi,) + (0,) * nd)


def _new_key_block(q, k_new):
    qf = q.astype(F32)
    kf = k_new.astype(BF16).astype(F32)
    lane = lax.broadcasted_iota(jnp.int32, (q.shape[0], PAGE_SIZE), 1)
    blk = jnp.zeros((q.shape[0], PAGE_SIZE), F32)
    for j in range(k_new.shape[0]):
        blk = jnp.where(lane == j, jnp.sum(qf * kf[j:j + 1, :], axis=-1, keepdims=True), blk)
    return blk


def _sample_score_kernel(pt_ref, qi_ref, w_ref, new_ref, kidx_hbm, o_ref, buf, sem, *, n_pages):
    group = qi_ref.shape[0]
    slot = _prefetch_pages(*_page_fetcher(pt_ref, (kidx_hbm,), (buf,), sem, group * n_pages))
    t = qi_ref.shape[1] // N_IDX_HEADS
    for g in range(group):
        qi = qi_ref[g]
        w = w_ref[g]
        blocks = [_dot(qi, buf[slot, g * n_pages + j].astype(BF16)) for j in range(n_pages)]
        blocks.append(_new_key_block(qi, new_ref[g]))
        for j, s in enumerate(blocks):
            s = jnp.maximum(s, 0.0) * w
            o_ref[g * t:(g + 1) * t, j * PAGE_SIZE:(j + 1) * PAGE_SIZE] = jnp.sum(
                s.reshape(t, N_IDX_HEADS, PAGE_SIZE), axis=1)


def _sample_scores(page_table_flat, qi32, w32, kidx_t, ki_new_t, n_pages):
    nb, rows, _ = qi32.shape
    t = rows // N_IDX_HEADS
    lk = (n_pages + 1) * PAGE_SIZE
    g = SCORE_GROUP
    return pl.pallas_call(
        functools.partial(_sample_score_kernel, n_pages=n_pages),
        grid_spec=pltpu.PrefetchScalarGridSpec(
            num_scalar_prefetch=1,
            grid=(nb // g,),
            in_specs=[_group_spec(g, rows, IDX_DIM), _group_spec(g, rows, 1), _group_spec(g, t, IDX_DIM),
                      pl.BlockSpec(memory_space=pl.ANY)],
            out_specs=pl.BlockSpec((g * t, lk), lambda bi, pt: (bi, 0)),
            scratch_shapes=[pltpu.VMEM((2, g * n_pages, IDX_DIM, PAGE_SIZE), F32), pltpu.SemaphoreType.DMA((1, 2))],
        ),
        out_shape=jax.ShapeDtypeStruct((nb * t, lk), F32),
        compiler_params=_params(("arbitrary",)),
        name="sample_scores",
    )(page_table_flat, qi32, w32, ki_new_t, kidx_t)


def _sample_select_kernel(s_ref, bias_ref, sm_ref, bt_ref, *, past, t, topk):
    lk = s_ref.shape[1]
    nk = lk // PAGE_SIZE
    qcol = lax.broadcasted_iota(jnp.int32, (PAGE_SIZE, TQ_S), 1)
    qpos = past + (qcol & (t - 1))
    krow = lax.broadcasted_iota(jnp.int32, (PAGE_SIZE, TQ_S), 0)
    stats = _score_stats_init(TQ_S)
    for c in range(nk):
        cols = slice(c * PAGE_SIZE, (c + 1) * PAGE_SIZE)
        s = jnp.where(krow + c * PAGE_SIZE <= qpos, s_ref[:, cols].T, NEG_INF)
        sm_ref[cols, :] = s
        stats = _score_stats_update(stats, s)
    _select_bias(sm_ref, bt_ref, nk, PAGE_SIZE, topk, stats)
    for c in range(nk):
        cols = slice(c * PAGE_SIZE, (c + 1) * PAGE_SIZE)
        bias_ref[:, cols] = bt_ref[cols, :].T


def _sample_select(scores, past, t, topk):
    n, lk = scores.shape
    assert t & (t - 1) == 0 and TQ_S % t == 0, "token index is taken from the low bits of the query index"
    blk = pl.BlockSpec((TQ_S, lk), lambda i: (i, 0))
    return pl.pallas_call(
        functools.partial(_sample_select_kernel, past=past, t=t, topk=topk),
        grid=(n // TQ_S,),
        in_specs=[blk],
        out_specs=blk,
        out_shape=jax.ShapeDtypeStruct((n, lk), F32),
        scratch_shapes=[pltpu.VMEM((lk, TQ_S), F32), pltpu.VMEM((lk, TQ_S), F32)],
        compiler_params=_params(("parallel",)),
        name="sample_select",
    )(scores)


def _page_fetcher(pt_ref, srcs, bufs, sem, n_copy):
    def copy(a, slot, i, page):
        return pltpu.make_async_copy(srcs[a].at[page], bufs[a].at[slot, i], sem.at[a, slot])

    def start(step, slot):
        def body(i, carry):
            page = pt_ref[step * n_copy + i]
            for a in range(len(srcs)):
                copy(a, slot, i, page).start()
            return carry
        lax.fori_loop(0, n_copy, body, 0)

    def wait(slot):
        def body(i, carry):
            for a in range(len(srcs)):
                copy(a, slot, i, 0).wait()
            return carry
        lax.fori_loop(0, n_copy, body, 0)

    return start, wait


def _prefetch_pages(start, wait):
    s = pl.program_id(0)
    slot = s % 2

    @pl.when(s == 0)
    def _():
        start(0, 0)

    @pl.when(s + 1 < pl.num_programs(0))
    def _():
        start(s + 1, 1 - slot)

    wait(slot)
    return slot


def _sample_attn_kernel(pt_ref, q_ref, bias_ref, k_new, v_new, k_hbm, v_hbm, o_ref, kbuf, vbuf, sem, *, n_pages):
    group = q_ref.shape[0]
    slot = _prefetch_pages(*_page_fetcher(pt_ref, (k_hbm, v_hbm), (kbuf, vbuf), sem, group * n_pages))
    kt = lambda page: page.reshape(KV_W, PAGE_SIZE).astype(BF16)
    t = k_new.shape[1]
    for g in range(group):
        qb = q_ref[g]
        bias = jnp.concatenate([bias_ref[g * t:(g + 1) * t, :]] * (qb.shape[0] // t), axis=0)
        blocks = [_dot(qb, kt(kbuf[slot, g * n_pages + j])) for j in range(n_pages)]
        blocks.append(_new_key_block(qb, k_new[g]))
        lg = jnp.concatenate(blocks, axis=-1) + bias
        m = jnp.max(lg, axis=-1, keepdims=True)
        p = jnp.exp2(lg - m)
        l = jnp.sum(p, axis=-1, keepdims=True)
        pb = p.astype(BF16)
        acc = jnp.zeros(o_ref.shape[1:], F32)
        for j in range(n_pages):
            acc = acc + _dot_nt(pb[:, j * PAGE_SIZE:(j + 1) * PAGE_SIZE], kt(vbuf[slot, g * n_pages + j]))
        p_new = pb[:, n_pages * PAGE_SIZE:].astype(F32)
        v_rows = v_new[g].astype(BF16).astype(F32)
        for j in range(v_rows.shape[0]):
            acc = acc + p_new[:, j:j + 1] * v_rows[j:j + 1, :]
        o_ref[g] = acc / l


def _sample_attend(page_table_flat, q_bd, bias, k_t, v_t, k_new_t, v_new_t, n_pages):
    nb, rows, _ = q_bd.shape
    lk = bias.shape[1]
    t = bias.shape[0] // nb
    g = ATTN_GROUP
    page = (N_KV_HEADS, HEAD_DIM, PAGE_SIZE)
    hbm = pl.BlockSpec(memory_space=pl.ANY)
    page_buf = pltpu.VMEM((2, g * n_pages) + page, F32)
    return pl.pallas_call(
        functools.partial(_sample_attn_kernel, n_pages=n_pages),
        grid_spec=pltpu.PrefetchScalarGridSpec(
            num_scalar_prefetch=1,
            grid=(nb // g,),
            in_specs=[_group_spec(g, rows, KV_W), pl.BlockSpec((g * t, lk), lambda bi, pt: (bi, 0)),
                      _group_spec(g, t, KV_W), _group_spec(g, t, KV_W), hbm, hbm],
            out_specs=_group_spec(g, rows, KV_W),
            scratch_shapes=[page_buf, page_buf, pltpu.SemaphoreType.DMA((2, 2))],
        ),
        out_shape=jax.ShapeDtypeStruct((nb, rows, KV_W), F32),
        compiler_params=_params(("arbitrary",)),
        name="sample_attn",
    )(page_table_flat, q_bd, bias, k_new_t, v_new_t, k_t, v_t)


def _glu(xb, w_glu_ref):
    gi = _dot_nt(xb, w_glu_ref[...])
    return gi[:, :C_CONV] * _sigmoid(gi[:, C_CONV:])


def _gates(xb, attn_b, w_g_ref, w_ao_ref, cols=slice(0, D_MODEL)):
    gate_cols = slice(D_MODEL + cols.start, D_MODEL + cols.stop)
    a_term = _sigmoid(_dot_nt(xb, w_g_ref[cols, :])) * _dot(attn_b, w_ao_ref[:, cols])
    return a_term, _sigmoid(_dot_nt(xb, w_g_ref[gate_cols, :]))


def _mix_tail(x, c, a_term, c_gate, lncg_ref, lncb_ref, w_co_ref, w_out_ref, ln1g_ref, ln1b_ref):
    cn = _layer_norm(c, lncg_ref[...], lncb_ref[...])
    c_branch = _dot((cn * _sigmoid(cn)).astype(BF16), w_co_ref[...])
    merged = a_term + c_gate * c_branch
    h = ALPHA * x + _dot(merged.astype(BF16), w_out_ref[...])
    return _layer_norm(h, ln1g_ref[...], ln1b_ref[...])


def _mix_prompt_kernel(x_ref, attn_ref, w_glu_ref, w_g_ref, w_ao_ref, w_dw_ref, b_dw_ref, lncg_ref, lncb_ref,
                       w_co_ref, w_out_ref, ln1g_ref, ln1b_ref, x1_ref, tail_ref, xp_ref):
    tm = x_ref.shape[0]

    @pl.when(pl.program_id(1) == 0)
    def _():
        xp_ref[0:CONV_HALO, :] = jnp.zeros((CONV_HALO, C_CONV), F32)

    x = x_ref[...]
    xb = x.astype(BF16)
    glu = _glu(xb, w_glu_ref)
    xp_ref[CONV_HALO:CONV_HALO + tm, :] = glu
    tail_ref[...] = glu[tm - CONV_HALO:, :]
    attn_b = attn_ref[...]

    first = CONV_HALO - (CONV_WIDTH - 1)
    rb = 128
    n_blk = C_CONV // LANES
    cols, gate_parts = [], []
    for c0 in range(0, C_CONV, LANES):
        lanes = slice(c0, c0 + LANES)
        q = c0 // LANES
        gate_parts.append(_gates(xb, attn_b, w_g_ref, w_ao_ref, slice(q * D_MODEL // n_blk, (q + 1) * D_MODEL // n_blk)))
        blocks = []
        for r0 in range(0, tm, rb):
            y = jnp.broadcast_to(b_dw_ref[:, lanes], (rb, LANES))
            for b in range(SUBLANES):
                rows = rb + (SUBLANES if b else 0)
                part = None
                for j in range(CONV_WIDTH):
                    if (first + j) % SUBLANES == b:
                        a0 = first + j - b + r0
                        term = w_dw_ref[j:j + 1, lanes] * xp_ref[a0:a0 + rows, lanes]
                        part = term if part is None else part + term
                y = y + part[b:b + rb, :]
            blocks.append(y)
        cols.append(jnp.concatenate(blocks, axis=0))
    c = jnp.concatenate(cols, axis=-1)
    a_term = jnp.concatenate([p[0] for p in gate_parts], axis=-1)
    c_gate = jnp.concatenate([p[1] for p in gate_parts], axis=-1)
    xp_ref[0:CONV_HALO, :] = xp_ref[tm:tm + CONV_HALO, :]

    x1_ref[...] = _mix_tail(x, c, a_term, c_gate, lncg_ref, lncb_ref, w_co_ref, w_out_ref, ln1g_ref, ln1b_ref)


def _mix_sample_kernel(x_ref, attn_ref, hist_ref, w_glu_ref, w_g_ref, w_ao_ref, w_dw_ref, b_dw_ref, lncg_ref,
                       lncb_ref, w_co_ref, w_out_ref, ln1g_ref, ln1b_ref, x1_ref, glu_ref):
    nb = hist_ref.shape[1]
    t = x_ref.shape[0] // nb
    n_hist = CONV_WIDTH - 1
    x = x_ref[...]
    xb = x.astype(BF16)
    glu = _glu(xb, w_glu_ref)
    glu_ref[...] = glu

    def slab(m):
        return hist_ref[m] if m < n_hist else glu[(m - n_hist) * nb:(m - n_hist + 1) * nb, :]

    outs = []
    for ti in range(t):
        acc = jnp.broadcast_to(b_dw_ref[...], (nb, C_CONV))
        for j in range(CONV_WIDTH):
            acc = acc + w_dw_ref[j:j + 1, :] * slab(ti + j)
        outs.append(acc)
    c = jnp.concatenate(outs, axis=0)
    a_term, c_gate = _gates(xb, attn_ref[...], w_g_ref, w_ao_ref)
    x1_ref[...] = _mix_tail(x, c, a_term, c_gate, lncg_ref, lncb_ref, w_co_ref, w_out_ref, ln1g_ref, ln1b_ref)


def _mix_weight_specs():
    return [
        _const_spec((2 * C_CONV, D_MODEL)), _const_spec((2 * D_MODEL, D_MODEL)), _const_spec((Q_W, D_MODEL)),
        _const_spec((CONV_WIDTH, C_CONV)), _const_spec((1, C_CONV)), _const_spec((1, C_CONV)),
        _const_spec((1, C_CONV)), _const_spec((C_CONV, D_MODEL)), _const_spec((D_MODEL, D_MODEL)),
        _const_spec((1, D_MODEL)), _const_spec((1, D_MODEL)),
    ]


def _mix_prompt(x, attn, mix_w, tm):
    b, t, _ = x.shape
    rows = lambda w: pl.BlockSpec((None, tm, w), lambda bi, i: (bi, i, 0))
    return pl.pallas_call(
        _mix_prompt_kernel,
        grid=(b, t // tm),
        in_specs=[rows(D_MODEL), rows(Q_W)] + _mix_weight_specs(),
        out_specs=[rows(D_MODEL), pl.BlockSpec((None, CONV_HALO, C_CONV), lambda bi, i: (bi, 0, 0))],
        out_shape=[jax.ShapeDtypeStruct((b, t, D_MODEL), F32), jax.ShapeDtypeStruct((b, CONV_HALO, C_CONV), F32)],
        scratch_shapes=[pltpu.VMEM((CONV_HALO + tm, C_CONV), F32)],
        compiler_params=_params(("parallel", "arbitrary")),
        name="mix_prompt",
    )(x, attn, *mix_w)


def _mix_sample(x_tm, attn_tm, hist_tm, mix_w):
    n = x_tm.shape[0]
    return pl.pallas_call(
        _mix_sample_kernel,
        grid=(1,),
        in_specs=[_const_spec((n, D_MODEL)), _const_spec((n, Q_W)), _const_spec(hist_tm.shape)] + _mix_weight_specs(),
        out_specs=[_whole_spec((n, D_MODEL)), _whole_spec((n, C_CONV))],
        out_shape=[jax.ShapeDtypeStruct((n, D_MODEL), F32), jax.ShapeDtypeStruct((n, C_CONV), F32)],
        compiler_params=_params(("arbitrary",)),
        name="mix_sample",
    )(x_tm, attn_tm, hist_tm, *mix_w)


def _ffn_tail(x1, uc, gate, w_down_ref, ln2g_ref, ln2b_ref):
    f = _dot((jax.nn.gelu(uc) * gate).astype(BF16), w_down_ref[...])
    return _layer_norm(ALPHA * x1 + f, ln2g_ref[...], ln2b_ref[...])


def _ffn_prompt_kernel(x1_ref, w_up_ref, w_gate_ref, w_dw_ref, b_dw_ref, w_down_ref, ln2g_ref, ln2b_ref,
                       y_ref, tail_ref, up_ref):
    tm = x1_ref.shape[0]

    @pl.when(pl.program_id(1) == 0)
    def _():
        up_ref[0:FFN_HALO, :] = jnp.zeros((FFN_HALO, D_FF), F32)

    x1 = x1_ref[...]
    x1b = x1.astype(BF16)
    u = _dot(x1b, w_up_ref[...])
    gate = _dot(x1b, w_gate_ref[...])
    up_ref[FFN_HALO:FFN_HALO + tm, :] = u
    tail_ref[...] = u[tm - FFN_HALO:, :]
    first = FFN_HALO - (FFN_CONV_WIDTH - 1)
    uc = b_dw_ref[...]
    for j in range(FFN_CONV_WIDTH):
        uc = uc + w_dw_ref[j:j + 1, :] * up_ref[first + j:first + j + tm, :]
    up_ref[0:FFN_HALO, :] = up_ref[tm:tm + FFN_HALO, :]
    y_ref[...] = _ffn_tail(x1, uc, gate, w_down_ref, ln2g_ref, ln2b_ref)


def _ffn_sample_kernel(x1_ref, hist_ref, w_up_ref, w_gate_ref, w_dw_ref, b_dw_ref, w_down_ref, ln2g_ref, ln2b_ref,
                       y_ref, u_ref):
    nb = hist_ref.shape[1]
    t = x1_ref.shape[0] // nb
    n_hist = FFN_CONV_WIDTH - 1
    x1 = x1_ref[...]
    x1b = x1.astype(BF16)
    u = _dot(x1b, w_up_ref[...])
    gate = _dot(x1b, w_gate_ref[...])
    u_ref[...] = u

    def slab(m):
        return hist_ref[m] if m < n_hist else u[(m - n_hist) * nb:(m - n_hist + 1) * nb, :]

    outs = []
    for ti in range(t):
        acc = jnp.broadcast_to(b_dw_ref[...], (nb, D_FF))
        for j in range(FFN_CONV_WIDTH):
            acc = acc + w_dw_ref[j:j + 1, :] * slab(ti + j)
        outs.append(acc)
    uc = jnp.concatenate(outs, axis=0)
    y_ref[...] = _ffn_tail(x1, uc, gate, w_down_ref, ln2g_ref, ln2b_ref)


def _ffn_weight_specs():
    return [
        _const_spec((D_MODEL, D_FF)), _const_spec((D_MODEL, D_FF)), _const_spec((FFN_CONV_WIDTH, D_FF)),
        _const_spec((1, D_FF)), _const_spec((D_FF, D_MODEL)), _const_spec((1, D_MODEL)), _const_spec((1, D_MODEL)),
    ]


def _ffn_prompt(x1, ffn_w, tm):
    b, t, _ = x1.shape
    rows = pl.BlockSpec((None, tm, D_MODEL), lambda bi, i: (bi, i, 0))
    return pl.pallas_call(
        _ffn_prompt_kernel,
        grid=(b, t // tm),
        in_specs=[rows] + _ffn_weight_specs(),
        out_specs=[rows, pl.BlockSpec((None, FFN_HALO, D_FF), lambda bi, i: (bi, 0, 0))],
        out_shape=[jax.ShapeDtypeStruct((b, t, D_MODEL), F32), jax.ShapeDtypeStruct((b, FFN_HALO, D_FF), F32)],
        scratch_shapes=[pltpu.VMEM((FFN_HALO + tm, D_FF), F32)],
        compiler_params=_params(("parallel", "arbitrary")),
        name="ffn_prompt",
    )(x1, *ffn_w)


def _ffn_sample(x1_tm, hist_tm, ffn_w):
    n = x1_tm.shape[0]
    return pl.pallas_call(
        _ffn_sample_kernel,
        grid=(1,),
        in_specs=[_const_spec((n, D_MODEL)), _const_spec(hist_tm.shape)] + _ffn_weight_specs(),
        out_specs=[_whole_spec((n, D_MODEL)), _whole_spec((n, D_FF))],
        out_shape=[jax.ShapeDtypeStruct((n, D_MODEL), F32), jax.ShapeDtypeStruct((n, D_FF), F32)],
        compiler_params=_params(("arbitrary",)),
        name="ffn_sample",
    )(x1_tm, hist_tm, *ffn_w)


def _token_major(a):
    b, t, w = a.shape
    return jnp.transpose(a, (1, 0, 2)).reshape(t * b, w)


def _batch_major(a, b):
    tb, w = a.shape
    return jnp.transpose(a.reshape(tb // b, b, w), (1, 0, 2))


def _stage_weights(w_in):
    wt = w_in.T
    o = 0
    w_q = wt[o:o + Q_W]; o += Q_W
    w_k = wt[o:o + KV_W]; o += KV_W
    w_v = wt[o:o + KV_W]; o += KV_W
    w_qi = wt[o:o + IQ_W]; o += IQ_W
    w_ki = wt[o:o + IDX_DIM]; o += IDX_DIM
    w_wi = wt[o:o + N_IDX_HEADS]; o += N_IDX_HEADS
    w_glu = wt[o:o + 2 * C_CONV]; o += 2 * C_CONV
    w_g = wt[o:o + 2 * D_MODEL]
    head = lambda h: w_q[h * HEAD_DIM:(h + 1) * HEAD_DIM]
    q_pairs = [w for i in range(GROUP) for w in (head(i), head(GROUP + i))]
    zero = jnp.zeros((LANES - IDX_DIM, D_MODEL), w_in.dtype)
    w_att = jnp.concatenate(q_pairs + [w_k, w_qi, w_ki, zero], axis=0).astype(BF16)
    w_t = jnp.concatenate([w_k, w_v, w_ki, w_wi], axis=0).astype(BF16)
    return w_att, w_t, w_glu.astype(BF16), w_g.astype(BF16)


def kernel(x_prompt, x_sample, cache_k, cache_v, cache_kidx, state_conv, state_ffn, page_table, w_in, w_attn_o, w_conv_dw, b_conv_dw, ln_conv_g, ln_conv_b, w_conv_o, w_out, ln1_g, ln1_b, w_ffn_up, w_ffn_gate, w_ffn_dw, b_ffn_dw, w_ffn_down, ln2_g, ln2_b):
    bp, tp, _ = x_prompt.shape
    bs, ts, _ = x_sample.shape
    n_pages = page_table.shape[1]
    past = n_pages * PAGE_SIZE
    assert N_KV_HEADS == 2 and N_IDX_HEADS == SUBLANES

    w_att, w_t, w_glu, w_g = _stage_weights(w_in)
    row2 = lambda a: a.reshape(1, -1)
    mix_w = (w_glu, w_g, w_attn_o.astype(BF16), w_conv_dw, row2(b_conv_dw), row2(ln_conv_g), row2(ln_conv_b),
             w_conv_o.astype(BF16), w_out.astype(BF16), row2(ln1_g), row2(ln1_b))
    ffn_w = (w_ffn_up.astype(BF16), w_ffn_gate.astype(BF16), w_ffn_dw, row2(b_ffn_dw), w_ffn_down.astype(BF16),
             row2(ln2_g), row2(ln2_b))

    qbd, qi, kb, kib, vt3, k_t, v_t, ki_t, wit = _proj(x_prompt, w_att, w_t, 1024)
    seq = lambda a: a.reshape(bp, tp, a.shape[-1])
    attn_p = _prompt_attention(seq(qbd), seq(qi), wit, seq(kib), seq(kb), vt3, min(TOPK_MAX, tp // 4))
    x1_p, conv_tail = _mix_prompt(x_prompt, attn_p, mix_w, 512)
    y_p, ffn_tail = _ffn_prompt(x1_p, ffn_w, 512)
    heads_last = lambda a_t: jnp.transpose(a_t.reshape(bp, N_KV_HEADS, HEAD_DIM, tp), (0, 3, 1, 2))
    k_p = heads_last(k_t)
    v_p = heads_last(v_t)
    ki_p = jnp.transpose(ki_t, (0, 2, 1))
    conv_p = conv_tail[:, CONV_HALO - (CONV_WIDTH - 1):, :]
    ffn_p = ffn_tail[:, FFN_HALO - (FFN_CONV_WIDTH - 1):, :]

    n_s = bs * ts
    qbd_s, qis, _, _, _, ks_t, vs_t, kis_t, wit_s = _proj(x_sample.reshape(1, n_s, D_MODEL), w_att, w_t, n_s)
    ks_t, vs_t, kis_t = ks_t[0], vs_t[0], kis_t[0]
    pt_flat = page_table.reshape(-1).astype(jnp.int32)
    new_rows = lambda a_t: a_t.T.reshape(bs, ts, -1)
    qi32 = qis.reshape(bs, ts * N_IDX_HEADS, IDX_DIM)
    w32 = wit_s.T.reshape(bs, ts * N_IDX_HEADS, 1)
    scores = _sample_scores(pt_flat, qi32, w32, jnp.transpose(cache_kidx, (0, 2, 1)), new_rows(kis_t), n_pages)
    bias = _sample_select(scores, past, ts, min(TOPK_MAX, (past + ts) // 4))
    q_bd = jnp.transpose(qbd_s.reshape(bs, ts, N_HEADS, KV_W), (0, 2, 1, 3)).reshape(bs, N_HEADS * ts, KV_W)
    as_pages = lambda a: jnp.transpose(a, (0, 2, 3, 1))
    o_bd = _sample_attend(pt_flat, q_bd, bias, as_pages(cache_k), as_pages(cache_v),
                          new_rows(ks_t), new_rows(vs_t), n_pages)
    o6 = o_bd.reshape(bs, N_KV_HEADS, GROUP, ts, N_KV_HEADS, HEAD_DIM)
    o_sel = jnp.stack([o6[:, n, :, :, n, :] for n in range(N_KV_HEADS)], axis=1)
    attn_s = jnp.transpose(o_sel, (0, 3, 1, 2, 4)).reshape(bs, ts, Q_W)

    x1_s, glu_s = _mix_sample(_token_major(x_sample), _token_major(attn_s).astype(BF16),
                              jnp.transpose(state_conv, (1, 0, 2)), mix_w)
    y_s, u_s = _ffn_sample(x1_s, jnp.transpose(state_ffn, (1, 0, 2)), ffn_w)
    y_s = _batch_major(y_s, bs)
    conv_s = jnp.concatenate([state_conv, _batch_major(glu_s, bs)], axis=1)[:, -(CONV_WIDTH - 1):, :]
    ffn_s = jnp.concatenate([state_ffn, _batch_major(u_s, bs)], axis=1)[:, -(FFN_CONV_WIDTH - 1):, :]
    k_s = new_rows(ks_t).reshape(bs, ts, N_KV_HEADS, HEAD_DIM)
    v_s = new_rows(vs_t).reshape(bs, ts, N_KV_HEADS, HEAD_DIM)
    ki_s = new_rows(kis_t)

    return (y_p, y_s, k_p, v_p, ki_p, conv_p, ffn_p, k_s, v_s, ki_s, conv_s, ffn_s)
```

```python
import functools

import jax
import jax.numpy as jnp
from jax import lax
from jax.experimental import pallas as pl
from jax.experimental.pallas import tpu as pltpu

D_MODEL = 1024
N_HEADS = 8
HEAD_DIM = 64
N_KV_HEADS = 2
N_IDX_HEADS = 8
IDX_DIM = 64
TOPK_MAX = 256
C_CONV = D_MODEL // 2
CONV_WIDTH = 31
D_FF = 2816
FFN_CONV_WIDTH = 3
LN_EPS = 1e-5
DEPTH = 1
ALPHA = (2.0 * DEPTH) ** 0.25
PAGE_SIZE = 128

Q_W = N_HEADS * HEAD_DIM
KV_W = N_KV_HEADS * HEAD_DIM
IQ_W = N_IDX_HEADS * IDX_DIM
GROUP = N_HEADS // N_KV_HEADS

LANES = 128
SUBLANES = 8
QBD_W = N_HEADS * KV_W
PROJ_W = Q_W + KV_W + IQ_W + LANES
PROJ_T = 2 * KV_W + IDX_DIM + N_IDX_HEADS
TQ = 256
TQ_S = 256
SCORE_GROUP = 8
ATTN_GROUP = 4
CONV_HALO = 32
FFN_HALO = 8
N_BISECT = 18
VT_ROWS = HEAD_DIM + 16
LOG2E = 1.4426950408889634
VMEM_LIMIT = 56 * 1024 * 1024

F32 = jnp.float32
BF16 = jnp.bfloat16
NEG_INF = float("-inf")
POS_INF = float("inf")


def _dot(a, b):
    return jnp.dot(a, b, preferred_element_type=F32)


def _dot_nt(a, b):
    return lax.dot_general(a, b, (((1,), (1,)), ((), ())), preferred_element_type=F32)


def _sigmoid(x):
    return 0.5 * jnp.tanh(0.5 * x) + 0.5


def _layer_norm(x, g, b):
    mu = jnp.mean(x, axis=-1, keepdims=True)
    xc = x - mu
    var = jnp.mean(xc * xc, axis=-1, keepdims=True)
    return xc * lax.rsqrt(var + LN_EPS) * g + b


def _params(sem):
    return pltpu.CompilerParams(dimension_semantics=sem, vmem_limit_bytes=VMEM_LIMIT)


def _whole_spec(shape):
    nd = len(shape)
    return pl.BlockSpec(shape, lambda *_: (0,) * nd)


def _const_spec(shape):
    nd = len(shape)
    return pl.BlockSpec(shape, lambda *_: (0,) * nd, pipeline_mode=pl.Buffered(1))


def _proj_kernel(x_ref, w_ref, wt_ref, qbd_ref, qi_ref, kb_ref, kib_ref, vt_ref, kt_ref, vtf_ref, kit_ref, wit_ref):
    xb = x_ref[...].astype(BF16)
    o = 0
    qp = _dot_nt(xb, w_ref[o:o + Q_W, :]); o += Q_W
    k = _dot_nt(xb, w_ref[o:o + KV_W, :]); o += KV_W
    qi = _dot_nt(xb, w_ref[o:o + IQ_W, :]); o += IQ_W
    ki = _dot_nt(xb, w_ref[o:o + LANES, :])[:, :IDX_DIM]
    qp = (qp * (HEAD_DIM ** -0.5 * LOG2E)).astype(BF16)
    low = lax.broadcasted_iota(jnp.int32, (qp.shape[0], KV_W), 1) < HEAD_DIM
    for h in range(N_HEADS):
        pair = qp[:, (h % GROUP) * KV_W:(h % GROUP + 1) * KV_W]
        qbd_ref[:, h * KV_W:(h + 1) * KV_W] = jnp.where(low if h < GROUP else jnp.logical_not(low), pair, 0.0)
    qi_ref[...] = qi.astype(BF16)
    kb_ref[...] = k.astype(BF16)
    kib_ref[...] = ki.astype(BF16)
    t = _dot_nt(wt_ref[...], xb)
    kt_ref[...] = t[:KV_W, :]
    vtf_ref[...] = t[KV_W:2 * KV_W, :]
    kit_ref[...] = t[2 * KV_W:2 * KV_W + IDX_DIM, :]
    wit_ref[...] = t[2 * KV_W + IDX_DIM:, :] * ((IDX_DIM ** -0.5) * (N_IDX_HEADS ** -0.5))
    vt = t[KV_W:2 * KV_W, :].astype(BF16)
    ones = jnp.ones((VT_ROWS - HEAD_DIM, TQ), BF16)
    for j in range(vt_ref.shape[0]):
        for n in range(N_KV_HEADS):
            vt_ref[j, n * VT_ROWS:n * VT_ROWS + HEAD_DIM, :] = vt[n * HEAD_DIM:(n + 1) * HEAD_DIM, j * TQ:(j + 1) * TQ]
            vt_ref[j, n * VT_ROWS + HEAD_DIM:(n + 1) * VT_ROWS, :] = ones


def _proj(x, w_att, w_t, tm):
    b, t, _ = x.shape
    n = b * t
    tiles = t // tm
    row = lambda w: pl.BlockSpec((tm, w), lambda i: (i, 0))
    col = lambda r: pl.BlockSpec((None, r, tm), lambda i: (i // tiles, 0, i % tiles))
    outs = [(QBD_W, BF16), (IQ_W, BF16), (KV_W, BF16), (IDX_DIM, BF16)]
    return pl.pallas_call(
        _proj_kernel,
        grid=(n // tm,),
        in_specs=[row(D_MODEL), _const_spec((PROJ_W, D_MODEL)), _const_spec((PROJ_T, D_MODEL))],
        out_specs=[row(w) for w, _ in outs]
        + [pl.BlockSpec((tm // TQ, N_KV_HEADS * VT_ROWS, TQ), lambda i: (i, 0, 0)), col(KV_W), col(KV_W), col(IDX_DIM),
           pl.BlockSpec((SUBLANES, tm), lambda i: (0, i))],
        out_shape=[jax.ShapeDtypeStruct((n, w), dt) for w, dt in outs]
        + [jax.ShapeDtypeStruct((n // TQ, N_KV_HEADS * VT_ROWS, TQ), BF16), jax.ShapeDtypeStruct((b, KV_W, t), F32),
           jax.ShapeDtypeStruct((b, KV_W, t), F32), jax.ShapeDtypeStruct((b, IDX_DIM, t), F32),
           jax.ShapeDtypeStruct((SUBLANES, n), F32)],
        compiler_params=_params(("parallel",)),
        name="proj",
    )(x.reshape(n, D_MODEL), w_att, w_t)


def _chunk(ref, c, ch):
    return ref[pl.ds(pl.multiple_of(c * ch, ch), ch), :]


def _fold_rows(x, op):
    parts = [x[r:r + SUBLANES, :] for r in range(0, x.shape[0], SUBLANES)]
    accs = parts[:4]
    for i, part in enumerate(parts[4:]):
        accs[i % 4] = op(accs[i % 4], part)
    while len(accs) > 1:
        accs = [op(accs[a], accs[a + 1]) for a in range(0, len(accs) - 1, 2)] + ([accs[-1]] if len(accs) % 2 else [])
    return accs[0]


def _reduce_keys(s_ref, nk, ch, init, f, op, red):
    w = s_ref.shape[1]

    def body(c, acc):
        return op(acc, _fold_rows(f(_chunk(s_ref, c, ch), c), op))

    acc = lax.fori_loop(0, nk, body, jnp.full((SUBLANES, w), init, F32))
    return red(acc, axis=0, keepdims=True)


def _score_stats_init(w):
    return (jnp.full((SUBLANES, w), NEG_INF, F32), jnp.full((SUBLANES, w), POS_INF, F32),
            jnp.zeros((SUBLANES, w), F32))


def _score_stats_update(stats, x):
    mx, mn, cnt = stats
    live = x > NEG_INF
    return (jnp.maximum(mx, _fold_rows(x, jnp.maximum)),
            jnp.minimum(mn, _fold_rows(jnp.where(live, x, POS_INF), jnp.minimum)),
            cnt + _fold_rows(jnp.where(live, 1.0, 0.0), jnp.add))


def _select_bias(s_ref, bias_ref, nk, ch, topk, stats):
    w = s_ref.shape[1]
    kf = float(topk)
    def count(pred):
        def body(c, accs):
            hit = pred(_chunk(s_ref, c, ch), c)
            accs = list(accs)
            for i, r in enumerate(range(0, ch, SUBLANES)):
                k = i % len(accs)
                accs[k] = jnp.where(hit[r:r + SUBLANES, :], accs[k] + 1.0, accs[k])
            return tuple(accs)

        accs = lax.fori_loop(0, nk, body, tuple(jnp.zeros((SUBLANES, w), F32) for _ in range(4)))
        return jnp.sum((accs[0] + accs[1]) + (accs[2] + accs[3]), axis=0, keepdims=True)
    min_above = lambda t: _reduce_keys(s_ref, nk, ch, POS_INF, lambda x, c: jnp.where(x > t, x, POS_INF),
                                       jnp.minimum, jnp.min)

    neg = jnp.full((1, w), NEG_INF, F32)

    hi = jnp.max(stats[0], axis=0, keepdims=True)
    lo_fin = jnp.min(stats[1], axis=0, keepdims=True)
    n_adm = jnp.sum(stats[2], axis=0, keepdims=True)

    def bisect(_, st):
        lo, lo_fin, hi, n_lo = st
        mid = 0.5 * lo_fin + 0.5 * hi
        c = count(lambda x, _: x > mid)
        ok = c >= kf
        return jnp.where(ok, mid, lo), jnp.where(ok, mid, lo_fin), jnp.where(ok, hi, mid), jnp.where(ok, c, n_lo)

    lo, _, _, n_lo = lax.fori_loop(0, N_BISECT, bisect, (neg, lo_fin, hi, n_adm))

    def peel(st):
        lo, thr, n_gt, done = st
        v = min_above(lo)
        c = count(lambda x, _: x > v)
        found = c < kf
        newly = jnp.logical_and(done < 0.5, found)
        return (jnp.where(jnp.logical_or(found, done > 0.5), lo, v), jnp.where(newly, v, thr),
                jnp.where(newly, c, n_gt), jnp.where(found, 1.0, done))

    _, thr, n_gt, _ = lax.while_loop(lambda st: jnp.min(st[3]) < 0.5, peel,
                                     (lo, lo, n_lo, jnp.where(n_lo <= kf, 1.0, 0.0)))

    need = jnp.where(thr == neg, 0.0, kf - n_gt)
    any_copies = jnp.max(need) > 0.0

    @pl.when(any_copies)
    def _():
        tri = (lax.broadcasted_iota(jnp.int32, (ch, ch), 1)
               <= lax.broadcasted_iota(jnp.int32, (ch, ch), 0)).astype(BF16)

        def write(c, seen):
            x = _chunk(s_ref, c, ch)
            eq = x == thr
            rank = seen + _dot(tri, jnp.where(eq, 1.0, 0.0).astype(BF16))
            sel = jnp.logical_or(x > thr, jnp.logical_and(eq, rank <= need))
            bias_ref[pl.ds(pl.multiple_of(c * ch, ch), ch), :] = jnp.where(sel, 0.0, NEG_INF)
            return rank[ch - 1:ch, :]

        lax.fori_loop(0, nk, write, jnp.zeros((1, w), F32))

    @pl.when(jnp.logical_not(any_copies))
    def _():
        def write(c, carry):
            sel = _chunk(s_ref, c, ch) > thr
            bias_ref[pl.ds(pl.multiple_of(c * ch, ch), ch), :] = jnp.where(sel, 0.0, NEG_INF)
            return carry

        lax.fori_loop(0, nk, write, 0)


def _prompt_attn_kernel(qbd_ref, qi_ref, wit_ref, kib_ref, kb_ref, vt_ref, o_ref, s_ref, bias_ref, acc_ref, lg_ref,
                        lg2_ref, *, topk):
    i = pl.program_id(1)
    nk = i + 1
    qi = qi_ref[...]
    wit = wit_ref[...]
    q_pos = lax.broadcasted_iota(jnp.int32, (TQ, TQ), 1) + i * TQ
    k_off = lax.broadcasted_iota(jnp.int32, (TQ, TQ), 0)

    bufs = (lg_ref, lg2_ref)
    n_pairs = (nk - 1) // 2
    odd_tail = (nk - 1) % 2 == 1

    def products(kc, h, buf):
        buf[h] = _dot_nt(kc, qi[:, h * IDX_DIM:(h + 1) * IDX_DIM])

    def accumulate(h, buf, s):
        return s + jnp.maximum(buf[h], 0.0) * wit[h:h + 1, :]

    def emit(c, s, stats):
        s = jnp.where(k_off + c * TQ <= q_pos, s, NEG_INF)
        s_ref[pl.ds(pl.multiple_of(c * TQ, TQ), TQ), :] = s
        return _score_stats_update(stats, s)

    def score_step(c, parity, stats):
        kc = _chunk(kib_ref, c, TQ)
        s = jnp.zeros((TQ, TQ), F32)
        for h in range(N_IDX_HEADS):
            products(kc, h, bufs[parity])
            s = accumulate(h, bufs[1 - parity], s)
        return emit(c - 1, s, stats)

    def score_last(c, parity, stats):
        s = jnp.zeros((TQ, TQ), F32)
        for h in range(N_IDX_HEADS):
            s = accumulate(h, bufs[parity], s)
        return emit(c, s, stats)

    kc0 = _chunk(kib_ref, 0, TQ)
    for h in range(N_IDX_HEADS):
        products(kc0, h, bufs[0])
    stats = lax.fori_loop(0, n_pairs, lambda p, st: score_step(2 * p + 2, 0, score_step(2 * p + 1, 1, st)),
                          _score_stats_init(TQ))
    stats = lax.cond(odd_tail, lambda: score_last(nk - 1, 1, score_step(nk - 1, 1, stats)),
                     lambda: score_last(nk - 1, 0, stats))
    _select_bias(s_ref, bias_ref, nk, TQ, topk, stats)

    acc_ref[...] = jnp.zeros(acc_ref.shape, F32)

    def logits(c, h, buf):
        rows = pl.ds(pl.multiple_of(c * TQ, TQ), TQ)
        lg = _dot_nt(kb_ref[rows, :], qbd_ref[:, h * KV_W:(h + 1) * KV_W]) + bias_ref[rows, :]
        buf[h] = lg
        return jnp.max(_fold_rows(lg, jnp.maximum), axis=0, keepdims=True)

    def weigh(c, h, buf, m_use, alpha):
        out = slice(h * VT_ROWS, (h + 1) * VT_ROWS)
        p = jnp.exp2(buf[h] - m_use[h:h + 1, :]).astype(BF16)
        vt = vt_ref[c, pl.ds((h // GROUP) * VT_ROWS, VT_ROWS), :]
        acc_ref[out, :] = alpha[h:h + 1, :] * acc_ref[out, :] + _dot(vt, p)

    def advance(m_old, chunk_max):
        m_new = jnp.maximum(m_old, jnp.concatenate(chunk_max, axis=0))
        m_use = jnp.where(m_new == NEG_INF, 0.0, m_new)
        return m_new, m_use, jnp.exp2(m_old - m_use)

    def step(c, parity, state):
        m, m_use, alpha = state
        chunk_max = []
        for h in range(N_HEADS):
            chunk_max.append(logits(c, h, bufs[parity]))
            weigh(c - 1, h, bufs[1 - parity], m_use, alpha)
        return advance(m, chunk_max)

    def finish(c, parity, state):
        for h in range(N_HEADS):
            weigh(c, h, bufs[parity], state[1], state[2])

    state = advance(jnp.full((N_HEADS, TQ), NEG_INF, F32), [logits(0, h, bufs[0]) for h in range(N_HEADS)])
    state = lax.fori_loop(0, n_pairs, lambda p, st: step(2 * p + 2, 0, step(2 * p + 1, 1, st)), state)

    @pl.when(odd_tail)
    def _():
        finish(nk - 1, 1, step(nk - 1, 1, state))

    @pl.when(jnp.logical_not(odd_tail))
    def _():
        finish(nk - 1, 0, state)
    outs = [acc_ref[h * VT_ROWS:h * VT_ROWS + HEAD_DIM, :] / acc_ref[h * VT_ROWS + HEAD_DIM:h * VT_ROWS + HEAD_DIM + 1, :]
            for h in range(N_HEADS)]
    o_ref[...] = jnp.concatenate(outs, axis=0).T.astype(o_ref.dtype)


def _prompt_attention(qbd, qi, wit, kib, kb, vt3, topk):
    b, t, _ = qbd.shape
    nblk = t // TQ
    qblk = lambda w: pl.BlockSpec((None, TQ, w), lambda bi, i: (bi, i, 0))
    full = lambda w: pl.BlockSpec((None, t, w), lambda bi, i: (bi, 0, 0))
    return pl.pallas_call(
        functools.partial(_prompt_attn_kernel, topk=topk),
        grid=(b, nblk),
        in_specs=[qblk(QBD_W), qblk(IQ_W), pl.BlockSpec((SUBLANES, TQ), lambda bi, i: (0, bi * nblk + i)),
                  full(IDX_DIM), full(KV_W), pl.BlockSpec((nblk, N_KV_HEADS * VT_ROWS, TQ), lambda bi, i: (bi, 0, 0))],
        out_specs=qblk(Q_W),
        out_shape=jax.ShapeDtypeStruct((b, t, Q_W), BF16),
        scratch_shapes=[pltpu.VMEM((t, TQ), F32), pltpu.VMEM((t, TQ), F32), pltpu.VMEM((N_HEADS * VT_ROWS, TQ), F32),
                        pltpu.VMEM((N_HEADS, TQ, TQ), F32), pltpu.VMEM((N_HEADS, TQ, TQ), F32)],
        compiler_params=_params(("parallel", "arbitrary")),
        name="prompt_attn",
    )(qbd, qi, wit, kib, kb, vt3)


def _group_spec(group, *block):
    nd = len(block)
    return pl.BlockSpec((group,) + block, lambda bi, pt: (bi,) + (0,) * nd)


def _new_key_block(q, k_new):
    qf = q.astype(F32)
    kf = k_new.astype(BF16).astype(F32)
    lane = lax.broadcasted_iota(jnp.int32, (q.shape[0], PAGE_SIZE), 1)
    blk = jnp.zeros((q.shape[0], PAGE_SIZE), F32)
    for j in range(k_new.shape[0]):
        blk = jnp.where(lane == j, jnp.sum(qf * kf[j:j + 1, :], axis=-1, keepdims=True), blk)
    return blk


def _sample_score_kernel(pt_ref, qi_ref, w_ref, new_ref, kidx_hbm, o_ref, buf, sem, *, n_pages):
    group = qi_ref.shape[0]
    slot = _prefetch_pages(*_page_fetcher(pt_ref, (kidx_hbm,), (buf,), sem, group * n_pages))
    t = qi_ref.shape[1] // N_IDX_HEADS
    for g in range(group):
        qi = qi_ref[g]
        w = w_ref[g]
        blocks = [_dot(qi, buf[slot, g * n_pages + j].astype(BF16)) for j in range(n_pages)]
        blocks.append(_new_key_block(qi, new_ref[g]))
        for j, s in enumerate(blocks):
            s = jnp.maximum(s, 0.0) * w
            o_ref[g * t:(g + 1) * t, j * PAGE_SIZE:(j + 1) * PAGE_SIZE] = jnp.sum(
                s.reshape(t, N_IDX_HEADS, PAGE_SIZE), axis=1)


def _sample_scores(page_table_flat, qi32, w32, kidx_t, ki_new_t, n_pages):
    nb, rows, _ = qi32.shape
    t = rows // N_IDX_HEADS
    lk = (n_pages + 1) * PAGE_SIZE
    g = SCORE_GROUP
    return pl.pallas_call(
        functools.partial(_sample_score_kernel, n_pages=n_pages),
        grid_spec=pltpu.PrefetchScalarGridSpec(
            num_scalar_prefetch=1,
            grid=(nb // g,),
            in_specs=[_group_spec(g, rows, IDX_DIM), _group_spec(g, rows, 1), _group_spec(g, t, IDX_DIM),
                      pl.BlockSpec(memory_space=pl.ANY)],
            out_specs=pl.BlockSpec((g * t, lk), lambda bi, pt: (bi, 0)),
            scratch_shapes=[pltpu.VMEM((2, g * n_pages, IDX_DIM, PAGE_SIZE), F32), pltpu.SemaphoreType.DMA((1, 2))],
        ),
        out_shape=jax.ShapeDtypeStruct((nb * t, lk), F32),
        compiler_params=_params(("arbitrary",)),
        name="sample_scores",
    )(page_table_flat, qi32, w32, ki_new_t, kidx_t)


def _sample_select_kernel(s_ref, bias_ref, sm_ref, bt_ref, *, past, t, topk):
    lk = s_ref.shape[1]
    nk = lk // PAGE_SIZE
    qcol = lax.broadcasted_iota(jnp.int32, (PAGE_SIZE, TQ_S), 1)
    qpos = past + (qcol & (t - 1))
    krow = lax.broadcasted_iota(jnp.int32, (PAGE_SIZE, TQ_S), 0)
    stats = _score_stats_init(TQ_S)
    for c in range(nk):
        cols = slice(c * PAGE_SIZE, (c + 1) * PAGE_SIZE)
        s = jnp.where(krow + c * PAGE_SIZE <= qpos, s_ref[:, cols].T, NEG_INF)
        sm_ref[cols, :] = s
        stats = _score_stats_update(stats, s)
    _select_bias(sm_ref, bt_ref, nk, PAGE_SIZE, topk, stats)
    for c in range(nk):
        cols = slice(c * PAGE_SIZE, (c + 1) * PAGE_SIZE)
        bias_ref[:, cols] = bt_ref[cols, :].T


def _sample_select(scores, past, t, topk):
    n, lk = scores.shape
    assert t & (t - 1) == 0 and TQ_S % t == 0, "token index is taken from the low bits of the query index"
    blk = pl.BlockSpec((TQ_S, lk), lambda i: (i, 0))
    return pl.pallas_call(
        functools.partial(_sample_select_kernel, past=past, t=t, topk=topk),
        grid=(n // TQ_S,),
        in_specs=[blk],
        out_specs=blk,
        out_shape=jax.ShapeDtypeStruct((n, lk), F32),
        scratch_shapes=[pltpu.VMEM((lk, TQ_S), F32), pltpu.VMEM((lk, TQ_S), F32)],
        compiler_params=_params(("parallel",)),
        name="sample_select",
    )(scores)


def _page_fetcher(pt_ref, srcs, bufs, sem, n_copy):
    def copy(a, slot, i, page):
        return pltpu.make_async_copy(srcs[a].at[page], bufs[a].at[slot, i], sem.at[a, slot])

    def start(step, slot):
        def body(i, carry):
            page = pt_ref[step * n_copy + i]
            for a in range(len(srcs)):
                copy(a, slot, i, page).start()
            return carry
        lax.fori_loop(0, n_copy, body, 0)

    def wait(slot):
        def body(i, carry):
            for a in range(len(srcs)):
                copy(a, slot, i, 0).wait()
            return carry
        lax.fori_loop(0, n_copy, body, 0)

    return start, wait


def _prefetch_pages(start, wait):
    s = pl.program_id(0)
    slot = s % 2

    @pl.when(s == 0)
    def _():
        start(0, 0)

    @pl.when(s + 1 < pl.num_programs(0))
    def _():
        start(s + 1, 1 - slot)

    wait(slot)
    return slot


def _sample_attn_kernel(pt_ref, q_ref, bias_ref, k_new, v_new, k_hbm, v_hbm, o_ref, kbuf, vbuf, sem, *, n_pages):
    group = q_ref.shape[0]
    slot = _prefetch_pages(*_page_fetcher(pt_ref, (k_hbm, v_hbm), (kbuf, vbuf), sem, group * n_pages))
    kt = lambda page: page.reshape(KV_W, PAGE_SIZE).astype(BF16)
    t = k_new.shape[1]
    for g in range(group):
        qb = q_ref[g]
        bias = jnp.concatenate([bias_ref[g * t:(g + 1) * t, :]] * (qb.shape[0] // t), axis=0)
        blocks = [_dot(qb, kt(kbuf[slot, g * n_pages + j])) for j in range(n_pages)]
        blocks.append(_new_key_block(qb, k_new[g]))
        lg = jnp.concatenate(blocks, axis=-1) + bias
        m = jnp.max(lg, axis=-1, keepdims=True)
        p = jnp.exp2(lg - m)
        l = jnp.sum(p, axis=-1, keepdims=True)
        pb = p.astype(BF16)
        acc = jnp.zeros(o_ref.shape[1:], F32)
        for j in range(n_pages):
            acc = acc + _dot_nt(pb[:, j * PAGE_SIZE:(j + 1) * PAGE_SIZE], kt(vbuf[slot, g * n_pages + j]))
        p_new = pb[:, n_pages * PAGE_SIZE:].astype(F32)
        v_rows = v_new[g].astype(BF16).astype(F32)
        for j in range(v_rows.shape[0]):
            acc = acc + p_new[:, j:j + 1] * v_rows[j:j + 1, :]
        o_ref[g] = acc / l


def _sample_attend(page_table_flat, q_bd, bias, k_t, v_t, k_new_t, v_new_t, n_pages):
    nb, rows, _ = q_bd.shape
    lk = bias.shape[1]
    t = bias.shape[0] // nb
    g = ATTN_GROUP
    page = (N_KV_HEADS, HEAD_DIM, PAGE_SIZE)
    hbm = pl.BlockSpec(memory_space=pl.ANY)
    page_buf = pltpu.VMEM((2, g * n_pages) + page, F32)
    return pl.pallas_call(
        functools.partial(_sample_attn_kernel, n_pages=n_pages),
        grid_spec=pltpu.PrefetchScalarGridSpec(
            num_scalar_prefetch=1,
            grid=(nb // g,),
            in_specs=[_group_spec(g, rows, KV_W), pl.BlockSpec((g * t, lk), lambda bi, pt: (bi, 0)),
                      _group_spec(g, t, KV_W), _group_spec(g, t, KV_W), hbm, hbm],
            out_specs=_group_spec(g, rows, KV_W),
            scratch_shapes=[page_buf, page_buf, pltpu.SemaphoreType.DMA((2, 2))],
        ),
        out_shape=jax.ShapeDtypeStruct((nb, rows, KV_W), F32),
        compiler_params=_params(("arbitrary",)),
        name="sample_attn",
    )(page_table_flat, q_bd, bias, k_new_t, v_new_t, k_t, v_t)


def _glu(xb, w_glu_ref):
    gi = _dot_nt(xb, w_glu_ref[...])
    return gi[:, :C_CONV] * _sigmoid(gi[:, C_CONV:])


def _gates(xb, attn_b, w_g_ref, w_ao_ref, cols=slice(0, D_MODEL)):
    gate_cols = slice(D_MODEL + cols.start, D_MODEL + cols.stop)
    a_term = _sigmoid(_dot_nt(xb, w_g_ref[cols, :])) * _dot(attn_b, w_ao_ref[:, cols])
    return a_term, _sigmoid(_dot_nt(xb, w_g_ref[gate_cols, :]))


def _mix_tail(x, c, a_term, c_gate, lncg_ref, lncb_ref, w_co_ref, w_out_ref, ln1g_ref, ln1b_ref):
    cn = _layer_norm(c, lncg_ref[...], lncb_ref[...])
    c_branch = _dot((cn * _sigmoid(cn)).astype(BF16), w_co_ref[...])
    merged = a_term + c_gate * c_branch
    h = ALPHA * x + _dot(merged.astype(BF16), w_out_ref[...])
    return _layer_norm(h, ln1g_ref[...], ln1b_ref[...])


def _mix_prompt_kernel(x_ref, attn_ref, w_glu_ref, w_g_ref, w_ao_ref, w_dw_ref, b_dw_ref, lncg_ref, lncb_ref,
                       w_co_ref, w_out_ref, ln1g_ref, ln1b_ref, x1_ref, tail_ref, xp_ref):
    tm = x_ref.shape[0]

    @pl.when(pl.program_id(1) == 0)
    def _():
        xp_ref[0:CONV_HALO, :] = jnp.zeros((CONV_HALO, C_CONV), F32)

    x = x_ref[...]
    xb = x.astype(BF16)
    glu = _glu(xb, w_glu_ref)
    xp_ref[CONV_HALO:CONV_HALO + tm, :] = glu
    tail_ref[...] = glu[tm - CONV_HALO:, :]
    attn_b = attn_ref[...]

    first = CONV_HALO - (CONV_WIDTH - 1)
    rb = 128
    n_blk = C_CONV // LANES
    cols, gate_parts = [], []
    for c0 in range(0, C_CONV, LANES):
        lanes = slice(c0, c0 + LANES)
        q = c0 // LANES
        gate_parts.append(_gates(xb, attn_b, w_g_ref, w_ao_ref, slice(q * D_MODEL // n_blk, (q + 1) * D_MODEL // n_blk)))
        blocks = []
        for r0 in range(0, tm, rb):
            y = jnp.broadcast_to(b_dw_ref[:, lanes], (rb, LANES))
            for b in range(SUBLANES):
                rows = rb + (SUBLANES if b else 0)
                part = None
                for j in range(CONV_WIDTH):
                    if (first + j) % SUBLANES == b:
                        a0 = first + j - b + r0
                        term = w_dw_ref[j:j + 1, lanes] * xp_ref[a0:a0 + rows, lanes]
                        part = term if part is None else part + term
                y = y + part[b:b + rb, :]
            blocks.append(y)
        cols.append(jnp.concatenate(blocks, axis=0))
    c = jnp.concatenate(cols, axis=-1)
    a_term = jnp.concatenate([p[0] for p in gate_parts], axis=-1)
    c_gate = jnp.concatenate([p[1] for p in gate_parts], axis=-1)
    xp_ref[0:CONV_HALO, :] = xp_ref[tm:tm + CONV_HALO, :]

    x1_ref[...] = _mix_tail(x, c, a_term, c_gate, lncg_ref, lncb_ref, w_co_ref, w_out_ref, ln1g_ref, ln1b_ref)


def _mix_sample_kernel(x_ref, attn_ref, hist_ref, w_glu_ref, w_g_ref, w_ao_ref, w_dw_ref, b_dw_ref, lncg_ref,
                       lncb_ref, w_co_ref, w_out_ref, ln1g_ref, ln1b_ref, x1_ref, glu_ref):
    nb = hist_ref.shape[1]
    t = x_ref.shape[0] // nb
    n_hist = CONV_WIDTH - 1
    x = x_ref[...]
    xb = x.astype(BF16)
    glu = _glu(xb, w_glu_ref)
    glu_ref[...] = glu

    def slab(m):
        return hist_ref[m] if m < n_hist else glu[(m - n_hist) * nb:(m - n_hist + 1) * nb, :]

    outs = []
    for ti in range(t):
        acc = jnp.broadcast_to(b_dw_ref[...], (nb, C_CONV))
        for j in range(CONV_WIDTH):
            acc = acc + w_dw_ref[j:j + 1, :] * slab(ti + j)
        outs.append(acc)
    c = jnp.concatenate(outs, axis=0)
    a_term, c_gate = _gates(xb, attn_ref[...], w_g_ref, w_ao_ref)
    x1_ref[...] = _mix_tail(x, c, a_term, c_gate, lncg_ref, lncb_ref, w_co_ref, w_out_ref, ln1g_ref, ln1b_ref)


def _mix_weight_specs():
    return [
        _const_spec((2 * C_CONV, D_MODEL)), _const_spec((2 * D_MODEL, D_MODEL)), _const_spec((Q_W, D_MODEL)),
        _const_spec((CONV_WIDTH, C_CONV)), _const_spec((1, C_CONV)), _const_spec((1, C_CONV)),
        _const_spec((1, C_CONV)), _const_spec((C_CONV, D_MODEL)), _const_spec((D_MODEL, D_MODEL)),
        _const_spec((1, D_MODEL)), _const_spec((1, D_MODEL)),
    ]


def _mix_prompt(x, attn, mix_w, tm):
    b, t, _ = x.shape
    rows = lambda w: pl.BlockSpec((None, tm, w), lambda bi, i: (bi, i, 0))
    return pl.pallas_call(
        _mix_prompt_kernel,
        grid=(b, t // tm),
        in_specs=[rows(D_MODEL), rows(Q_W)] + _mix_weight_specs(),
        out_specs=[rows(D_MODEL), pl.BlockSpec((None, CONV_HALO, C_CONV), lambda bi, i: (bi, 0, 0))],
        out_shape=[jax.ShapeDtypeStruct((b, t, D_MODEL), F32), jax.ShapeDtypeStruct((b, CONV_HALO, C_CONV), F32)],
        scratch_shapes=[pltpu.VMEM((CONV_HALO + tm, C_CONV), F32)],
        compiler_params=_params(("parallel", "arbitrary")),
        name="mix_prompt",
    )(x, attn, *mix_w)


def _mix_sample(x_tm, attn_tm, hist_tm, mix_w):
    n = x_tm.shape[0]
    return pl.pallas_call(
        _mix_sample_kernel,
        grid=(1,),
        in_specs=[_const_spec((n, D_MODEL)), _const_spec((n, Q_W)), _const_spec(hist_tm.shape)] + _mix_weight_specs(),
        out_specs=[_whole_spec((n, D_MODEL)), _whole_spec((n, C_CONV))],
        out_shape=[jax.ShapeDtypeStruct((n, D_MODEL), F32), jax.ShapeDtypeStruct((n, C_CONV), F32)],
        compiler_params=_params(("arbitrary",)),
        name="mix_sample",
    )(x_tm, attn_tm, hist_tm, *mix_w)


def _ffn_tail(x1, uc, gate, w_down_ref, ln2g_ref, ln2b_ref):
    f = _dot((jax.nn.gelu(uc) * gate).astype(BF16), w_down_ref[...])
    return _layer_norm(ALPHA * x1 + f, ln2g_ref[...], ln2b_ref[...])


def _ffn_prompt_kernel(x1_ref, w_up_ref, w_gate_ref, w_dw_ref, b_dw_ref, w_down_ref, ln2g_ref, ln2b_ref,
                       y_ref, tail_ref, up_ref):
    tm = x1_ref.shape[0]

    @pl.when(pl.program_id(1) == 0)
    def _():
        up_ref[0:FFN_HALO, :] = jnp.zeros((FFN_HALO, D_FF), F32)

    x1 = x1_ref[...]
    x1b = x1.astype(BF16)
    u = _dot(x1b, w_up_ref[...])
    gate = _dot(x1b, w_gate_ref[...])
    up_ref[FFN_HALO:FFN_HALO + tm, :] = u
    tail_ref[...] = u[tm - FFN_HALO:, :]
    first = FFN_HALO - (FFN_CONV_WIDTH - 1)
    uc = b_dw_ref[...]
    for j in range(FFN_CONV_WIDTH):
        uc = uc + w_dw_ref[j:j + 1, :] * up_ref[first + j:first + j + tm, :]
    up_ref[0:FFN_HALO, :] = up_ref[tm:tm + FFN_HALO, :]
    y_ref[...] = _ffn_tail(x1, uc, gate, w_down_ref, ln2g_ref, ln2b_ref)


def _ffn_sample_kernel(x1_ref, hist_ref, w_up_ref, w_gate_ref, w_dw_ref, b_dw_ref, w_down_ref, ln2g_ref, ln2b_ref,
                       y_ref, u_ref):
    nb = hist_ref.shape[1]
    t = x1_ref.shape[0] // nb
    n_hist = FFN_CONV_WIDTH - 1
    x1 = x1_ref[...]
    x1b = x1.astype(BF16)
    u = _dot(x1b, w_up_ref[...])
    gate = _dot(x1b, w_gate_ref[...])
    u_ref[...] = u

    def slab(m):
        return hist_ref[m] if m < n_hist else u[(m - n_hist) * nb:(m - n_hist + 1) * nb, :]

    outs = []
    for ti in range(t):
        acc = jnp.broadcast_to(b_dw_ref[...], (nb, D_FF))
        for j in range(FFN_CONV_WIDTH):
            acc = acc + w_dw_ref[j:j + 1, :] * slab(ti + j)
        outs.append(acc)
    uc = jnp.concatenate(outs, axis=0)
    y_ref[...] = _ffn_tail(x1, uc, gate, w_down_ref, ln2g_ref, ln2b_ref)


def _ffn_weight_specs():
    return [
        _const_spec((D_MODEL, D_FF)), _const_spec((D_MODEL, D_FF)), _const_spec((FFN_CONV_WIDTH, D_FF)),
        _const_spec((1, D_FF)), _const_spec((D_FF, D_MODEL)), _const_spec((1, D_MODEL)), _const_spec((1, D_MODEL)),
    ]


def _ffn_prompt(x1, ffn_w, tm):
    b, t, _ = x1.shape
    rows = pl.BlockSpec((None, tm, D_MODEL), lambda bi, i: (bi, i, 0))
    return pl.pallas_call(
        _ffn_prompt_kernel,
        grid=(b, t // tm),
        in_specs=[rows] + _ffn_weight_specs(),
        out_specs=[rows, pl.BlockSpec((None, FFN_HALO, D_FF), lambda bi, i: (bi, 0, 0))],
        out_shape=[jax.ShapeDtypeStruct((b, t, D_MODEL), F32), jax.ShapeDtypeStruct((b, FFN_HALO, D_FF), F32)],
        scratch_shapes=[pltpu.VMEM((FFN_HALO + tm, D_FF), F32)],
        compiler_params=_params(("parallel", "arbitrary")),
        name="ffn_prompt",
    )(x1, *ffn_w)


def _ffn_sample(x1_tm, hist_tm, ffn_w):
    n = x1_tm.shape[0]
    return pl.pallas_call(
        _ffn_sample_kernel,
        grid=(1,),
        in_specs=[_const_spec((n, D_MODEL)), _const_spec(hist_tm.shape)] + _ffn_weight_specs(),
        out_specs=[_whole_spec((n, D_MODEL)), _whole_spec((n, D_FF))],
        out_shape=[jax.ShapeDtypeStruct((n, D_MODEL), F32), jax.ShapeDtypeStruct((n, D_FF), F32)],
        compiler_params=_params(("arbitrary",)),
        name="ffn_sample",
    )(x1_tm, hist_tm, *ffn_w)


def _token_major(a):
    b, t, w = a.shape
    return jnp.transpose(a, (1, 0, 2)).reshape(t * b, w)


def _batch_major(a, b):
    tb, w = a.shape
    return jnp.transpose(a.reshape(tb // b, b, w), (1, 0, 2))


def _stage_weights(w_in):
    wt = w_in.T
    o = 0
    w_q = wt[o:o + Q_W]; o += Q_W
    w_k = wt[o:o + KV_W]; o += KV_W
    w_v = wt[o:o + KV_W]; o += KV_W
    w_qi = wt[o:o + IQ_W]; o += IQ_W
    w_ki = wt[o:o + IDX_DIM]; o += IDX_DIM
    w_wi = wt[o:o + N_IDX_HEADS]; o += N_IDX_HEADS
    w_glu = wt[o:o + 2 * C_CONV]; o += 2 * C_CONV
    w_g = wt[o:o + 2 * D_MODEL]
    head = lambda h: w_q[h * HEAD_DIM:(h + 1) * HEAD_DIM]
    q_pairs = [w for i in range(GROUP) for w in (head(i), head(GROUP + i))]
    zero = jnp.zeros((LANES - IDX_DIM, D_MODEL), w_in.dtype)
    w_att = jnp.concatenate(q_pairs + [w_k, w_qi, w_ki, zero], axis=0).astype(BF16)
    w_t = jnp.concatenate([w_k, w_v, w_ki, w_wi], axis=0).astype(BF16)
    return w_att, w_t, w_glu.astype(BF16), w_g.astype(BF16)


def kernel(x_prompt, x_sample, cache_k, cache_v, cache_kidx, state_conv, state_ffn, page_table, w_in, w_attn_o, w_conv_dw, b_conv_dw, ln_conv_g, ln_conv_b, w_conv_o, w_out, ln1_g, ln1_b, w_ffn_up, w_ffn_gate, w_ffn_dw, b_ffn_dw, w_ffn_down, ln2_g, ln2_b):
    bp, tp, _ = x_prompt.shape
    bs, ts, _ = x_sample.shape
    n_pages = page_table.shape[1]
    past = n_pages * PAGE_SIZE
    assert N_KV_HEADS == 2 and N_IDX_HEADS == SUBLANES

    w_att, w_t, w_glu, w_g = _stage_weights(w_in)
    row2 = lambda a: a.reshape(1, -1)
    mix_w = (w_glu, w_g, w_attn_o.astype(BF16), w_conv_dw, row2(b_conv_dw), row2(ln_conv_g), row2(ln_conv_b),
             w_conv_o.astype(BF16), w_out.astype(BF16), row2(ln1_g), row2(ln1_b))
    ffn_w = (w_ffn_up.astype(BF16), w_ffn_gate.astype(BF16), w_ffn_dw, row2(b_ffn_dw), w_ffn_down.astype(BF16),
             row2(ln2_g), row2(ln2_b))

    qbd, qi, kb, kib, vt3, k_t, v_t, ki_t, wit = _proj(x_prompt, w_att, w_t, 1024)
    seq = lambda a: a.reshape(bp, tp, a.shape[-1])
    attn_p = _prompt_attention(seq(qbd), seq(qi), wit, seq(kib), seq(kb), vt3, min(TOPK_MAX, tp // 4))
    x1_p, conv_tail = _mix_prompt(x_prompt, attn_p, mix_w, 512)
    y_p, ffn_tail = _ffn_prompt(x1_p, ffn_w, 512)
    heads_last = lambda a_t: jnp.transpose(a_t.reshape(bp, N_KV_HEADS, HEAD_DIM, tp), (0, 3, 1, 2))
    k_p = heads_last(k_t)
    v_p = heads_last(v_t)
    ki_p = jnp.transpose(ki_t, (0, 2, 1))
    conv_p = conv_tail[:, CONV_HALO - (CONV_WIDTH - 1):, :]
    ffn_p = ffn_tail[:, FFN_HALO - (FFN_CONV_WIDTH - 1):, :]

    n_s = bs * ts
    qbd_s, qis, _, _, _, ks_t, vs_t, kis_t, wit_s = _proj(x_sample.reshape(1, n_s, D_MODEL), w_att, w_t, n_s)
    ks_t, vs_t, kis_t = ks_t[0], vs_t[0], kis_t[0]
    pt_flat = page_table.reshape(-1).astype(jnp.int32)
    new_rows = lambda a_t: a_t.T.reshape(bs, ts, -1)
    qi32 = qis.reshape(bs, ts * N_IDX_HEADS, IDX_DIM)
    w32 = wit_s.T.reshape(bs, ts * N_IDX_HEADS, 1)
    scores = _sample_scores(pt_flat, qi32, w32, jnp.transpose(cache_kidx, (0, 2, 1)), new_rows(kis_t), n_pages)
    bias = _sample_select(scores, past, ts, min(TOPK_MAX, (past + ts) // 4))
    q_bd = jnp.transpose(qbd_s.reshape(bs, ts, N_HEADS, KV_W), (0, 2, 1, 3)).reshape(bs, N_HEADS * ts, KV_W)
    as_pages = lambda a: jnp.transpose(a, (0, 2, 3, 1))
    o_bd = _sample_attend(pt_flat, q_bd, bias, as_pages(cache_k), as_pages(cache_v),
                          new_rows(ks_t), new_rows(vs_t), n_pages)
    o6 = o_bd.reshape(bs, N_KV_HEADS, GROUP, ts, N_KV_HEADS, HEAD_DIM)
    o_sel = jnp.stack([o6[:, n, :, :, n, :] for n in range(N_KV_HEADS)], axis=1)
    attn_s = jnp.transpose(o_sel, (0, 3, 1, 2, 4)).reshape(bs, ts, Q_W)

    x1_s, glu_s = _mix_sample(_token_major(x_sample), _token_major(attn_s).astype(BF16),
                              jnp.transpose(state_conv, (1, 0, 2)), mix_w)
    y_s, u_s = _ffn_sample(x1_s, jnp.transpose(state_ffn, (1, 0, 2)), ffn_w)
    y_s = _batch_major(y_s, bs)
    conv_s = jnp.concatenate([state_conv, _batch_major(glu_s, bs)], axis=1)[:, -(CONV_WIDTH - 1):, :]
    ffn_s = jnp.concatenate([state_ffn, _batch_major(u_s, bs)], axis=1)[:, -(FFN_CONV_WIDTH - 1):, :]
    k_s = new_rows(ks_t).reshape(bs, ts, N_KV_HEADS, HEAD_DIM)
    v_s = new_rows(vs_t).reshape(bs, ts, N_KV_HEADS, HEAD_DIM)
    ki_s = new_rows(kis_t)

    return (y_p, y_s, k_p, v_p, ki_p, conv_p, ffn_p, k_s, v_s, ki_s, conv_s, ffn_s)
```

```python
import functools

import jax
import jax.numpy as jnp
from jax import lax
from jax.experimental import pallas as pl
from jax.experimental.pallas import tpu as pltpu

D_MODEL = 1024
N_HEADS = 8
HEAD_DIM = 64
N_KV_HEADS = 2
N_IDX_HEADS = 8
IDX_DIM = 64
TOPK_MAX = 256
C_CONV = D_MODEL // 2
CONV_WIDTH = 31
D_FF = 2816
FFN_CONV_WIDTH = 3
LN_EPS = 1e-5
DEPTH = 1
ALPHA = (2.0 * DEPTH) ** 0.25
PAGE_SIZE = 128

Q_W = N_HEADS * HEAD_DIM
KV_W = N_KV_HEADS * HEAD_DIM
IQ_W = N_IDX_HEADS * IDX_DIM
GROUP = N_HEADS // N_KV_HEADS

LANES = 128
SUBLANES = 8
QBD_W = N_HEADS * KV_W
PROJ_W = Q_W + KV_W + IQ_W + LANES
PROJ_T = 2 * KV_W + IDX_DIM + N_IDX_HEADS
TQ = 256
TQ_S = 256
SCORE_GROUP = 8
ATTN_GROUP = 4
CONV_HALO = 32
FFN_HALO = 8
N_BISECT = 20
VT_ROWS = HEAD_DIM + 16
LOG2E = 1.4426950408889634
VMEM_LIMIT = 56 * 1024 * 1024

F32 = jnp.float32
BF16 = jnp.bfloat16
NEG_INF = float("-inf")
POS_INF = float("inf")


def _dot(a, b):
    return jnp.dot(a, b, preferred_element_type=F32)


def _dot_nt(a, b):
    return lax.dot_general(a, b, (((1,), (1,)), ((), ())), preferred_element_type=F32)


def _sigmoid(x):
    return 0.5 * jnp.tanh(0.5 * x) + 0.5


def _layer_norm(x, g, b):
    mu = jnp.mean(x, axis=-1, keepdims=True)
    xc = x - mu
    var = jnp.mean(xc * xc, axis=-1, keepdims=True)
    return xc * lax.rsqrt(var + LN_EPS) * g + b


def _params(sem):
    return pltpu.CompilerParams(dimension_semantics=sem, vmem_limit_bytes=VMEM_LIMIT)


def _whole_spec(shape):
    nd = len(shape)
    return pl.BlockSpec(shape, lambda *_: (0,) * nd)


def _const_spec(shape):
    nd = len(shape)
    return pl.BlockSpec(shape, lambda *_: (0,) * nd, pipeline_mode=pl.Buffered(1))


def _proj_kernel(x_ref, w_ref, wt_ref, qbd_ref, qi_ref, kb_ref, kib_ref, vt_ref, kt_ref, vtf_ref, kit_ref, wit_ref):
    xb = x_ref[...].astype(BF16)
    o = 0
    qp = _dot_nt(xb, w_ref[o:o + Q_W, :]); o += Q_W
    k = _dot_nt(xb, w_ref[o:o + KV_W, :]); o += KV_W
    qi = _dot_nt(xb, w_ref[o:o + IQ_W, :]); o += IQ_W
    ki = _dot_nt(xb, w_ref[o:o + LANES, :])[:, :IDX_DIM]
    qp = (qp * (HEAD_DIM ** -0.5 * LOG2E)).astype(BF16)
    low = lax.broadcasted_iota(jnp.int32, (qp.shape[0], KV_W), 1) < HEAD_DIM
    for h in range(N_HEADS):
        pair = qp[:, (h % GROUP) * KV_W:(h % GROUP + 1) * KV_W]
        qbd_ref[:, h * KV_W:(h + 1) * KV_W] = jnp.where(low if h < GROUP else jnp.logical_not(low), pair, 0.0)
    qi_ref[...] = qi.astype(BF16)
    kb_ref[...] = k.astype(BF16)
    kib_ref[...] = ki.astype(BF16)
    t = _dot_nt(wt_ref[...], xb)
    kt_ref[...] = t[:KV_W, :]
    vtf_ref[...] = t[KV_W:2 * KV_W, :]
    kit_ref[...] = t[2 * KV_W:2 * KV_W + IDX_DIM, :]
    wit_ref[...] = t[2 * KV_W + IDX_DIM:, :] * ((IDX_DIM ** -0.5) * (N_IDX_HEADS ** -0.5))
    vt = t[KV_W:2 * KV_W, :].astype(BF16)
    ones = jnp.ones((VT_ROWS - HEAD_DIM, TQ), BF16)
    for j in range(vt_ref.shape[0]):
        for n in range(N_KV_HEADS):
            vt_ref[j, n * VT_ROWS:n * VT_ROWS + HEAD_DIM, :] = vt[n * HEAD_DIM:(n + 1) * HEAD_DIM, j * TQ:(j + 1) * TQ]
            vt_ref[j, n * VT_ROWS + HEAD_DIM:(n + 1) * VT_ROWS, :] = ones


def _proj(x, w_att, w_t, tm):
    b, t, _ = x.shape
    n = b * t
    tiles = t // tm
    row = lambda w: pl.BlockSpec((tm, w), lambda i: (i, 0))
    col = lambda r: pl.BlockSpec((None, r, tm), lambda i: (i // tiles, 0, i % tiles))
    outs = [(QBD_W, BF16), (IQ_W, BF16), (KV_W, BF16), (IDX_DIM, BF16)]
    return pl.pallas_call(
        _proj_kernel,
        grid=(n // tm,),
        in_specs=[row(D_MODEL), _const_spec((PROJ_W, D_MODEL)), _const_spec((PROJ_T, D_MODEL))],
        out_specs=[row(w) for w, _ in outs]
        + [pl.BlockSpec((tm // TQ, N_KV_HEADS * VT_ROWS, TQ), lambda i: (i, 0, 0)), col(KV_W), col(KV_W), col(IDX_DIM),
           pl.BlockSpec((SUBLANES, tm), lambda i: (0, i))],
        out_shape=[jax.ShapeDtypeStruct((n, w), dt) for w, dt in outs]
        + [jax.ShapeDtypeStruct((n // TQ, N_KV_HEADS * VT_ROWS, TQ), BF16), jax.ShapeDtypeStruct((b, KV_W, t), F32),
           jax.ShapeDtypeStruct((b, KV_W, t), F32), jax.ShapeDtypeStruct((b, IDX_DIM, t), F32),
           jax.ShapeDtypeStruct((SUBLANES, n), F32)],
        compiler_params=_params(("parallel",)),
        name="proj",
    )(x.reshape(n, D_MODEL), w_att, w_t)


def _chunk(ref, c, ch):
    return ref[pl.ds(pl.multiple_of(c * ch, ch), ch), :]


def _fold_rows(x, op):
    parts = [x[r:r + SUBLANES, :] for r in range(0, x.shape[0], SUBLANES)]
    accs = parts[:4]
    for i, part in enumerate(parts[4:]):
        accs[i % 4] = op(accs[i % 4], part)
    while len(accs) > 1:
        accs = [op(accs[a], accs[a + 1]) for a in range(0, len(accs) - 1, 2)] + ([accs[-1]] if len(accs) % 2 else [])
    return accs[0]


def _reduce_keys(s_ref, nk, ch, init, f, op, red):
    w = s_ref.shape[1]

    def body(c, acc):
        return op(acc, _fold_rows(f(_chunk(s_ref, c, ch), c), op))

    acc = lax.fori_loop(0, nk, body, jnp.full((SUBLANES, w), init, F32))
    return red(acc, axis=0, keepdims=True)


def _score_stats_init(w):
    return (jnp.full((SUBLANES, w), NEG_INF, F32), jnp.full((SUBLANES, w), POS_INF, F32),
            jnp.zeros((SUBLANES, w), F32))


def _score_stats_update(stats, x):
    mx, mn, cnt = stats
    live = x > NEG_INF
    return (jnp.maximum(mx, _fold_rows(x, jnp.maximum)),
            jnp.minimum(mn, _fold_rows(jnp.where(live, x, POS_INF), jnp.minimum)),
            cnt + _fold_rows(jnp.where(live, 1.0, 0.0), jnp.add))


def _select_bias(s_ref, bias_ref, nk, ch, topk, stats):
    w = s_ref.shape[1]
    kf = float(topk)
    def count(pred):
        def body(c, accs):
            hit = pred(_chunk(s_ref, c, ch), c)
            accs = list(accs)
            for i, r in enumerate(range(0, ch, SUBLANES)):
                k = i % len(accs)
                accs[k] = jnp.where(hit[r:r + SUBLANES, :], accs[k] + 1.0, accs[k])
            return tuple(accs)

        accs = lax.fori_loop(0, nk, body, tuple(jnp.zeros((SUBLANES, w), F32) for _ in range(4)))
        return jnp.sum((accs[0] + accs[1]) + (accs[2] + accs[3]), axis=0, keepdims=True)
    min_above = lambda t: _reduce_keys(s_ref, nk, ch, POS_INF, lambda x, c: jnp.where(x > t, x, POS_INF),
                                       jnp.minimum, jnp.min)

    neg = jnp.full((1, w), NEG_INF, F32)

    hi = jnp.max(stats[0], axis=0, keepdims=True)
    lo_fin = jnp.min(stats[1], axis=0, keepdims=True)
    n_adm = jnp.sum(stats[2], axis=0, keepdims=True)

    def bisect(_, st):
        lo, lo_fin, hi, n_lo = st
        mid = 0.5 * lo_fin + 0.5 * hi
        c = count(lambda x, _: x > mid)
        ok = c >= kf
        return jnp.where(ok, mid, lo), jnp.where(ok, mid, lo_fin), jnp.where(ok, hi, mid), jnp.where(ok, c, n_lo)

    lo, _, _, n_lo = lax.fori_loop(0, N_BISECT, bisect, (neg, lo_fin, hi, n_adm))

    def peel(st):
        lo, thr, n_gt, done = st
        v = min_above(lo)
        c = count(lambda x, _: x > v)
        found = c < kf
        newly = jnp.logical_and(done < 0.5, found)
        return (jnp.where(jnp.logical_or(found, done > 0.5), lo, v), jnp.where(newly, v, thr),
                jnp.where(newly, c, n_gt), jnp.where(found, 1.0, done))

    _, thr, n_gt, _ = lax.while_loop(lambda st: jnp.min(st[3]) < 0.5, peel,
                                     (lo, lo, n_lo, jnp.where(n_lo <= kf, 1.0, 0.0)))

    need = jnp.where(thr == neg, 0.0, kf - n_gt)
    any_copies = jnp.max(need) > 0.0

    @pl.when(any_copies)
    def _():
        tri = (lax.broadcasted_iota(jnp.int32, (ch, ch), 1)
               <= lax.broadcasted_iota(jnp.int32, (ch, ch), 0)).astype(BF16)

        def write(c, seen):
            x = _chunk(s_ref, c, ch)
            eq = x == thr
            rank = seen + _dot(tri, jnp.where(eq, 1.0, 0.0).astype(BF16))
            sel = jnp.logical_or(x > thr, jnp.logical_and(eq, rank <= need))
            bias_ref[pl.ds(pl.multiple_of(c * ch, ch), ch), :] = jnp.where(sel, 0.0, NEG_INF)
            return rank[ch - 1:ch, :]

        lax.fori_loop(0, nk, write, jnp.zeros((1, w), F32))

    @pl.when(jnp.logical_not(any_copies))
    def _():
        def write(c, carry):
            sel = _chunk(s_ref, c, ch) > thr
            bias_ref[pl.ds(pl.multiple_of(c * ch, ch), ch), :] = jnp.where(sel, 0.0, NEG_INF)
            return carry

        lax.fori_loop(0, nk, write, 0)


def _prompt_attn_kernel(qbd_ref, qi_ref, wit_ref, kib_ref, kb_ref, vt_ref, o_ref, s_ref, bias_ref, acc_ref, lg_ref,
                        lg2_ref, *, topk):
    i = pl.program_id(1)
    nk = i + 1
    qi = qi_ref[...]
    wit = wit_ref[...]
    q_pos = lax.broadcasted_iota(jnp.int32, (TQ, TQ), 1) + i * TQ
    k_off = lax.broadcasted_iota(jnp.int32, (TQ, TQ), 0)

    bufs = (lg_ref, lg2_ref)
    n_pairs = (nk - 1) // 2
    odd_tail = (nk - 1) % 2 == 1

    def products(kc, h, buf):
        buf[h] = _dot_nt(kc, qi[:, h * IDX_DIM:(h + 1) * IDX_DIM])

    def accumulate(h, buf, s):
        return s + jnp.maximum(buf[h], 0.0) * wit[h:h + 1, :]

    def emit(c, s, stats):
        s = jnp.where(k_off + c * TQ <= q_pos, s, NEG_INF)
        s_ref[pl.ds(pl.multiple_of(c * TQ, TQ), TQ), :] = s
        return _score_stats_update(stats, s)

    def score_step(c, parity, stats):
        kc = _chunk(kib_ref, c, TQ)
        s = jnp.zeros((TQ, TQ), F32)
        for h in range(N_IDX_HEADS):
            products(kc, h, bufs[parity])
            s = accumulate(h, bufs[1 - parity], s)
        return emit(c - 1, s, stats)

    def score_last(c, parity, stats):
        s = jnp.zeros((TQ, TQ), F32)
        for h in range(N_IDX_HEADS):
            s = accumulate(h, bufs[parity], s)
        return emit(c, s, stats)

    kc0 = _chunk(kib_ref, 0, TQ)
    for h in range(N_IDX_HEADS):
        products(kc0, h, bufs[0])
    stats = lax.fori_loop(0, n_pairs, lambda p, st: score_step(2 * p + 2, 0, score_step(2 * p + 1, 1, st)),
                          _score_stats_init(TQ))
    stats = lax.cond(odd_tail, lambda: score_last(nk - 1, 1, score_step(nk - 1, 1, stats)),
                     lambda: score_last(nk - 1, 0, stats))
    _select_bias(s_ref, bias_ref, nk, TQ, topk, stats)

    acc_ref[...] = jnp.zeros(acc_ref.shape, F32)

    def logits(c, h, buf):
        rows = pl.ds(pl.multiple_of(c * TQ, TQ), TQ)
        lg = _dot_nt(kb_ref[rows, :], qbd_ref[:, h * KV_W:(h + 1) * KV_W]) + bias_ref[rows, :]
        buf[h] = lg
        return jnp.max(_fold_rows(lg, jnp.maximum), axis=0, keepdims=True)

    def weigh(c, h, buf, m_use, alpha):
        out = slice(h * VT_ROWS, (h + 1) * VT_ROWS)
        p = jnp.exp2(buf[h] - m_use[h:h + 1, :]).astype(BF16)
        vt = vt_ref[c, pl.ds((h // GROUP) * VT_ROWS, VT_ROWS), :]
        acc_ref[out, :] = alpha[h:h + 1, :] * acc_ref[out, :] + _dot(vt, p)

    def advance(m_old, chunk_max):
        m_new = jnp.maximum(m_old, jnp.concatenate(chunk_max, axis=0))
        m_use = jnp.where(m_new == NEG_INF, 0.0, m_new)
        return m_new, m_use, jnp.exp2(m_old - m_use)

    def step(c, parity, state):
        m, m_use, alpha = state
        chunk_max = []
        for h in range(N_HEADS):
            chunk_max.append(logits(c, h, bufs[parity]))
            weigh(c - 1, h, bufs[1 - parity], m_use, alpha)
        return advance(m, chunk_max)

    def finish(c, parity, state):
        for h in range(N_HEADS):
            weigh(c, h, bufs[parity], state[1], state[2])

    state = advance(jnp.full((N_HEADS, TQ), NEG_INF, F32), [logits(0, h, bufs[0]) for h in range(N_HEADS)])
    state = lax.fori_loop(0, n_pairs, lambda p, st: step(2 * p + 2, 0, step(2 * p + 1, 1, st)), state)

    @pl.when(odd_tail)
    def _():
        finish(nk - 1, 1, step(nk - 1, 1, state))

    @pl.when(jnp.logical_not(odd_tail))
    def _():
        finish(nk - 1, 0, state)
    outs = [acc_ref[h * VT_ROWS:h * VT_ROWS + HEAD_DIM, :] / acc_ref[h * VT_ROWS + HEAD_DIM:h * VT_ROWS + HEAD_DIM + 1, :]
            for h in range(N_HEADS)]
    o_ref[...] = jnp.concatenate(outs, axis=0).T.astype(o_ref.dtype)


def _prompt_attention(qbd, qi, wit, kib, kb, vt3, topk):
    b, t, _ = qbd.shape
    nblk = t // TQ
    qblk = lambda w: pl.BlockSpec((None, TQ, w), lambda bi, i: (bi, i, 0))
    full = lambda w: pl.BlockSpec((None, t, w), lambda bi, i: (bi, 0, 0))
    return pl.pallas_call(
        functools.partial(_prompt_attn_kernel, topk=topk),
        grid=(b, nblk),
        in_specs=[qblk(QBD_W), qblk(IQ_W), pl.BlockSpec((SUBLANES, TQ), lambda bi, i: (0, bi * nblk + i)),
                  full(IDX_DIM), full(KV_W), pl.BlockSpec((nblk, N_KV_HEADS * VT_ROWS, TQ), lambda bi, i: (bi, 0, 0))],
        out_specs=qblk(Q_W),
        out_shape=jax.ShapeDtypeStruct((b, t, Q_W), BF16),
        scratch_shapes=[pltpu.VMEM((t, TQ), F32), pltpu.VMEM((t, TQ), F32), pltpu.VMEM((N_HEADS * VT_ROWS, TQ), F32),
                        pltpu.VMEM((N_HEADS, TQ, TQ), F32), pltpu.VMEM((N_HEADS, TQ, TQ), F32)],
        compiler_params=_params(("parallel", "arbitrary")),
        name="prompt_attn",
    )(qbd, qi, wit, kib, kb, vt3)


def _group_spec(group, *block):
    nd = len(block)
    return pl.BlockSpec((group,) + block, lambda bi, pt: (bi,) + (0,) * nd)


def _new_key_block(q, k_new):
    qf = q.astype(F32)
    kf = k_new.astype(BF16).astype(F32)
    lane = lax.broadcasted_iota(jnp.int32, (q.shape[0], PAGE_SIZE), 1)
    blk = jnp.zeros((q.shape[0], PAGE_SIZE), F32)
    for j in range(k_new.shape[0]):
        blk = jnp.where(lane == j, jnp.sum(qf * kf[j:j + 1, :], axis=-1, keepdims=True), blk)
    return blk


def _sample_score_kernel(pt_ref, qi_ref, w_ref, new_ref, kidx_hbm, o_ref, buf, sem, *, n_pages):
    group = qi_ref.shape[0]
    slot = _prefetch_pages(*_page_fetcher(pt_ref, (kidx_hbm,), (buf,), sem, group * n_pages))
    t = qi_ref.shape[1] // N_IDX_HEADS
    for g in range(group):
        qi = qi_ref[g]
        w = w_ref[g]
        blocks = [_dot(qi, buf[slot, g * n_pages + j].astype(BF16)) for j in range(n_pages)]
        blocks.append(_new_key_block(qi, new_ref[g]))
        for j, s in enumerate(blocks):
            s = jnp.maximum(s, 0.0) * w
            o_ref[g * t:(g + 1) * t, j * PAGE_SIZE:(j + 1) * PAGE_SIZE] = jnp.sum(
                s.reshape(t, N_IDX_HEADS, PAGE_SIZE), axis=1)


def _sample_scores(page_table_flat, qi32, w32, kidx_t, ki_new_t, n_pages):
    nb, rows, _ = qi32.shape
    t = rows // N_IDX_HEADS
    lk = (n_pages + 1) * PAGE_SIZE
    g = SCORE_GROUP
    return pl.pallas_call(
        functools.partial(_sample_score_kernel, n_pages=n_pages),
        grid_spec=pltpu.PrefetchScalarGridSpec(
            num_scalar_prefetch=1,
            grid=(nb // g,),
            in_specs=[_group_spec(g, rows, IDX_DIM), _group_spec(g, rows, 1), _group_spec(g, t, IDX_DIM),
                      pl.BlockSpec(memory_space=pl.ANY)],
            out_specs=pl.BlockSpec((g * t, lk), lambda bi, pt: (bi, 0)),
            scratch_shapes=[pltpu.VMEM((2, g * n_pages, IDX_DIM, PAGE_SIZE), F32), pltpu.SemaphoreType.DMA((1, 2))],
        ),
        out_shape=jax.ShapeDtypeStruct((nb * t, lk), F32),
        compiler_params=_params(("arbitrary",)),
        name="sample_scores",
    )(page_table_flat, qi32, w32, ki_new_t, kidx_t)


def _sample_select_kernel(s_ref, bias_ref, sm_ref, bt_ref, *, past, t, topk):
    lk = s_ref.shape[1]
    nk = lk // PAGE_SIZE
    qcol = lax.broadcasted_iota(jnp.int32, (PAGE_SIZE, TQ_S), 1)
    qpos = past + (qcol & (t - 1))
    krow = lax.broadcasted_iota(jnp.int32, (PAGE_SIZE, TQ_S), 0)
    stats = _score_stats_init(TQ_S)
    for c in range(nk):
        cols = slice(c * PAGE_SIZE, (c + 1) * PAGE_SIZE)
        s = jnp.where(krow + c * PAGE_SIZE <= qpos, s_ref[:, cols].T, NEG_INF)
        sm_ref[cols, :] = s
        stats = _score_stats_update(stats, s)
    _select_bias(sm_ref, bt_ref, nk, PAGE_SIZE, topk, stats)
    for c in range(nk):
        cols = slice(c * PAGE_SIZE, (c + 1) * PAGE_SIZE)
        bias_ref[:, cols] = bt_ref[cols, :].T


def _sample_select(scores, past, t, topk):
    n, lk = scores.shape
    assert t & (t - 1) == 0 and TQ_S % t == 0, "token index is taken from the low bits of the query index"
    blk = pl.BlockSpec((TQ_S, lk), lambda i: (i, 0))
    return pl.pallas_call(
        functools.partial(_sample_select_kernel, past=past, t=t, topk=topk),
        grid=(n // TQ_S,),
        in_specs=[blk],
        out_specs=blk,
        out_shape=jax.ShapeDtypeStruct((n, lk), F32),
        scratch_shapes=[pltpu.VMEM((lk, TQ_S), F32), pltpu.VMEM((lk, TQ_S), F32)],
        compiler_params=_params(("parallel",)),
        name="sample_select",
    )(scores)


def _page_fetcher(pt_ref, srcs, bufs, sem, n_copy):
    def copy(a, slot, i, page):
        return pltpu.make_async_copy(srcs[a].at[page], bufs[a].at[slot, i], sem.at[a, slot])

    def start(step, slot):
        def body(i, carry):
            page = pt_ref[step * n_copy + i]
            for a in range(len(srcs)):
                copy(a, slot, i, page).start()
            return carry
        lax.fori_loop(0, n_copy, body, 0)

    def wait(slot):
        def body(i, carry):
            for a in range(len(srcs)):
                copy(a, slot, i, 0).wait()
            return carry
        lax.fori_loop(0, n_copy, body, 0)

    return start, wait


def _prefetch_pages(start, wait):
    s = pl.program_id(0)
    slot = s % 2

    @pl.when(s == 0)
    def _():
        start(0, 0)

    @pl.when(s + 1 < pl.num_programs(0))
    def _():
        start(s + 1, 1 - slot)

    wait(slot)
    return slot


def _sample_attn_kernel(pt_ref, q_ref, bias_ref, k_new, v_new, k_hbm, v_hbm, o_ref, kbuf, vbuf, sem, *, n_pages):
    group = q_ref.shape[0]
    slot = _prefetch_pages(*_page_fetcher(pt_ref, (k_hbm, v_hbm), (kbuf, vbuf), sem, group * n_pages))
    kt = lambda page: page.reshape(KV_W, PAGE_SIZE).astype(BF16)
    t = k_new.shape[1]
    for g in range(group):
        qb = q_ref[g]
        bias = jnp.concatenate([bias_ref[g * t:(g + 1) * t, :]] * (qb.shape[0] // t), axis=0)
        blocks = [_dot(qb, kt(kbuf[slot, g * n_pages + j])) for j in range(n_pages)]
        blocks.append(_new_key_block(qb, k_new[g]))
        lg = jnp.concatenate(blocks, axis=-1) + bias
        m = jnp.max(lg, axis=-1, keepdims=True)
        p = jnp.exp2(lg - m)
        l = jnp.sum(p, axis=-1, keepdims=True)
        pb = p.astype(BF16)
        acc = jnp.zeros(o_ref.shape[1:], F32)
        for j in range(n_pages):
            acc = acc + _dot_nt(pb[:, j * PAGE_SIZE:(j + 1) * PAGE_SIZE], kt(vbuf[slot, g * n_pages + j]))
        p_new = pb[:, n_pages * PAGE_SIZE:].astype(F32)
        v_rows = v_new[g].astype(BF16).astype(F32)
        for j in range(v_rows.shape[0]):
            acc = acc + p_new[:, j:j + 1] * v_rows[j:j + 1, :]
        o_ref[g] = acc / l


def _sample_attend(page_table_flat, q_bd, bias, k_t, v_t, k_new_t, v_new_t, n_pages):
    nb, rows, _ = q_bd.shape
    lk = bias.shape[1]
    t = bias.shape[0] // nb
    g = ATTN_GROUP
    page = (N_KV_HEADS, HEAD_DIM, PAGE_SIZE)
    hbm = pl.BlockSpec(memory_space=pl.ANY)
    page_buf = pltpu.VMEM((2, g * n_pages) + page, F32)
    return pl.pallas_call(
        functools.partial(_sample_attn_kernel, n_pages=n_pages),
        grid_spec=pltpu.PrefetchScalarGridSpec(
            num_scalar_prefetch=1,
            grid=(nb // g,),
            in_specs=[_group_spec(g, rows, KV_W), pl.BlockSpec((g * t, lk), lambda bi, pt: (bi, 0)),
                      _group_spec(g, t, KV_W), _group_spec(g, t, KV_W), hbm, hbm],
            out_specs=_group_spec(g, rows, KV_W),
            scratch_shapes=[page_buf, page_buf, pltpu.SemaphoreType.DMA((2, 2))],
        ),
        out_shape=jax.ShapeDtypeStruct((nb, rows, KV_W), F32),
        compiler_params=_params(("arbitrary",)),
        name="sample_attn",
    )(page_table_flat, q_bd, bias, k_new_t, v_new_t, k_t, v_t)


def _glu(xb, w_glu_ref):
    gi = _dot_nt(xb, w_glu_ref[...])
    return gi[:, :C_CONV] * _sigmoid(gi[:, C_CONV:])


def _gates(xb, attn_b, w_g_ref, w_ao_ref, cols=slice(0, D_MODEL)):
    gate_cols = slice(D_MODEL + cols.start, D_MODEL + cols.stop)
    a_term = _sigmoid(_dot_nt(xb, w_g_ref[cols, :])) * _dot(attn_b, w_ao_ref[:, cols])
    return a_term, _sigmoid(_dot_nt(xb, w_g_ref[gate_cols, :]))


def _mix_tail(x, c, a_term, c_gate, lncg_ref, lncb_ref, w_co_ref, w_out_ref, ln1g_ref, ln1b_ref):
    cn = _layer_norm(c, lncg_ref[...], lncb_ref[...])
    c_branch = _dot((cn * _sigmoid(cn)).astype(BF16), w_co_ref[...])
    merged = a_term + c_gate * c_branch
    h = ALPHA * x + _dot(merged.astype(BF16), w_out_ref[...])
    return _layer_norm(h, ln1g_ref[...], ln1b_ref[...])


def _mix_prompt_kernel(x_ref, attn_ref, w_glu_ref, w_g_ref, w_ao_ref, w_dw_ref, b_dw_ref, lncg_ref, lncb_ref,
                       w_co_ref, w_out_ref, ln1g_ref, ln1b_ref, x1_ref, tail_ref, xp_ref):
    tm = x_ref.shape[0]

    @pl.when(pl.program_id(1) == 0)
    def _():
        xp_ref[0:CONV_HALO, :] = jnp.zeros((CONV_HALO, C_CONV), F32)

    x = x_ref[...]
    xb = x.astype(BF16)
    glu = _glu(xb, w_glu_ref)
    xp_ref[CONV_HALO:CONV_HALO + tm, :] = glu
    tail_ref[...] = glu[tm - CONV_HALO:, :]
    attn_b = attn_ref[...]

    first = CONV_HALO - (CONV_WIDTH - 1)
    rb = 128
    n_blk = C_CONV // LANES
    cols, gate_parts = [], []
    for c0 in range(0, C_CONV, LANES):
        lanes = slice(c0, c0 + LANES)
        q = c0 // LANES
        gate_parts.append(_gates(xb, attn_b, w_g_ref, w_ao_ref, slice(q * D_MODEL // n_blk, (q + 1) * D_MODEL // n_blk)))
        blocks = []
        for r0 in range(0, tm, rb):
            y = jnp.broadcast_to(b_dw_ref[:, lanes], (rb, LANES))
            for b in range(SUBLANES):
                rows = rb + (SUBLANES if b else 0)
                part = None
                for j in range(CONV_WIDTH):
                    if (first + j) % SUBLANES == b:
                        a0 = first + j - b + r0
                        term = w_dw_ref[j:j + 1, lanes] * xp_ref[a0:a0 + rows, lanes]
                        part = term if part is None else part + term
                y = y + part[b:b + rb, :]
            blocks.append(y)
        cols.append(jnp.concatenate(blocks, axis=0))
    c = jnp.concatenate(cols, axis=-1)
    a_term = jnp.concatenate([p[0] for p in gate_parts], axis=-1)
    c_gate = jnp.concatenate([p[1] for p in gate_parts], axis=-1)
    xp_ref[0:CONV_HALO, :] = xp_ref[tm:tm + CONV_HALO, :]

    x1_ref[...] = _mix_tail(x, c, a_term, c_gate, lncg_ref, lncb_ref, w_co_ref, w_out_ref, ln1g_ref, ln1b_ref)


def _mix_sample_kernel(x_ref, attn_ref, hist_ref, w_glu_ref, w_g_ref, w_ao_ref, w_dw_ref, b_dw_ref, lncg_ref,
                       lncb_ref, w_co_ref, w_out_ref, ln1g_ref, ln1b_ref, x1_ref, glu_ref):
    nb = hist_ref.shape[1]
    t = x_ref.shape[0] // nb
    n_hist = CONV_WIDTH - 1
    x = x_ref[...]
    xb = x.astype(BF16)
    glu = _glu(xb, w_glu_ref)
    glu_ref[...] = glu

    def slab(m):
        return hist_ref[m] if m < n_hist else glu[(m - n_hist) * nb:(m - n_hist + 1) * nb, :]

    outs = []
    for ti in range(t):
        acc = jnp.broadcast_to(b_dw_ref[...], (nb, C_CONV))
        for j in range(CONV_WIDTH):
            acc = acc + w_dw_ref[j:j + 1, :] * slab(ti + j)
        outs.append(acc)
    c = jnp.concatenate(outs, axis=0)
    a_term, c_gate = _gates(xb, attn_ref[...], w_g_ref, w_ao_ref)
    x1_ref[...] = _mix_tail(x, c, a_term, c_gate, lncg_ref, lncb_ref, w_co_ref, w_out_ref, ln1g_ref, ln1b_ref)


def _mix_weight_specs():
    return [
        _const_spec((2 * C_CONV, D_MODEL)), _const_spec((2 * D_MODEL, D_MODEL)), _const_spec((Q_W, D_MODEL)),
        _const_spec((CONV_WIDTH, C_CONV)), _const_spec((1, C_CONV)), _const_spec((1, C_CONV)),
        _const_spec((1, C_CONV)), _const_spec((C_CONV, D_MODEL)), _const_spec((D_MODEL, D_MODEL)),
        _const_spec((1, D_MODEL)), _const_spec((1, D_MODEL)),
    ]


def _mix_prompt(x, attn, mix_w, tm):
    b, t, _ = x.shape
    rows = lambda w: pl.BlockSpec((None, tm, w), lambda bi, i: (bi, i, 0))
    return pl.pallas_call(
        _mix_prompt_kernel,
        grid=(b, t // tm),
        in_specs=[rows(D_MODEL), rows(Q_W)] + _mix_weight_specs(),
        out_specs=[rows(D_MODEL), pl.BlockSpec((None, CONV_HALO, C_CONV), lambda bi, i: (bi, 0, 0))],
        out_shape=[jax.ShapeDtypeStruct((b, t, D_MODEL), F32), jax.ShapeDtypeStruct((b, CONV_HALO, C_CONV), F32)],
        scratch_shapes=[pltpu.VMEM((CONV_HALO + tm, C_CONV), F32)],
        compiler_params=_params(("parallel", "arbitrary")),
        name="mix_prompt",
    )(x, attn, *mix_w)


def _mix_sample(x_tm, attn_tm, hist_tm, mix_w):
    n = x_tm.shape[0]
    return pl.pallas_call(
        _mix_sample_kernel,
        grid=(1,),
        in_specs=[_const_spec((n, D_MODEL)), _const_spec((n, Q_W)), _const_spec(hist_tm.shape)] + _mix_weight_specs(),
        out_specs=[_whole_spec((n, D_MODEL)), _whole_spec((n, C_CONV))],
        out_shape=[jax.ShapeDtypeStruct((n, D_MODEL), F32), jax.ShapeDtypeStruct((n, C_CONV), F32)],
        compiler_params=_params(("arbitrary",)),
        name="mix_sample",
    )(x_tm, attn_tm, hist_tm, *mix_w)


def _ffn_tail(x1, uc, gate, w_down_ref, ln2g_ref, ln2b_ref):
    f = _dot((jax.nn.gelu(uc) * gate).astype(BF16), w_down_ref[...])
    return _layer_norm(ALPHA * x1 + f, ln2g_ref[...], ln2b_ref[...])


def _ffn_prompt_kernel(x1_ref, w_up_ref, w_gate_ref, w_dw_ref, b_dw_ref, w_down_ref, ln2g_ref, ln2b_ref,
                       y_ref, tail_ref, up_ref):
    tm = x1_ref.shape[0]

    @pl.when(pl.program_id(1) == 0)
    def _():
        up_ref[0:FFN_HALO, :] = jnp.zeros((FFN_HALO, D_FF), F32)

    x1 = x1_ref[...]
    x1b = x1.astype(BF16)
    u = _dot(x1b, w_up_ref[...])
    gate = _dot(x1b, w_gate_ref[...])
    up_ref[FFN_HALO:FFN_HALO + tm, :] = u
    tail_ref[...] = u[tm - FFN_HALO:, :]
    first = FFN_HALO - (FFN_CONV_WIDTH - 1)
    uc = b_dw_ref[...]
    for j in range(FFN_CONV_WIDTH):
        uc = uc + w_dw_ref[j:j + 1, :] * up_ref[first + j:first + j + tm, :]
    up_ref[0:FFN_HALO, :] = up_ref[tm:tm + FFN_HALO, :]
    y_ref[...] = _ffn_tail(x1, uc, gate, w_down_ref, ln2g_ref, ln2b_ref)


def _ffn_sample_kernel(x1_ref, hist_ref, w_up_ref, w_gate_ref, w_dw_ref, b_dw_ref, w_down_ref, ln2g_ref, ln2b_ref,
                       y_ref, u_ref):
    nb = hist_ref.shape[1]
    t = x1_ref.shape[0] // nb
    n_hist = FFN_CONV_WIDTH - 1
    x1 = x1_ref[...]
    x1b = x1.astype(BF16)
    u = _dot(x1b, w_up_ref[...])
    gate = _dot(x1b, w_gate_ref[...])
    u_ref[...] = u

    def slab(m):
        return hist_ref[m] if m < n_hist else u[(m - n_hist) * nb:(m - n_hist + 1) * nb, :]

    outs = []
    for ti in range(t):
        acc = jnp.broadcast_to(b_dw_ref[...], (nb, D_FF))
        for j in range(FFN_CONV_WIDTH):
            acc = acc + w_dw_ref[j:j + 1, :] * slab(ti + j)
        outs.append(acc)
    uc = jnp.concatenate(outs, axis=0)
    y_ref[...] = _ffn_tail(x1, uc, gate, w_down_ref, ln2g_ref, ln2b_ref)


def _ffn_weight_specs():
    return [
        _const_spec((D_MODEL, D_FF)), _const_spec((D_MODEL, D_FF)), _const_spec((FFN_CONV_WIDTH, D_FF)),
        _const_spec((1, D_FF)), _const_spec((D_FF, D_MODEL)), _const_spec((1, D_MODEL)), _const_spec((1, D_MODEL)),
    ]


def _ffn_prompt(x1, ffn_w, tm):
    b, t, _ = x1.shape
    rows = pl.BlockSpec((None, tm, D_MODEL), lambda bi, i: (bi, i, 0))
    return pl.pallas_call(
        _ffn_prompt_kernel,
        grid=(b, t // tm),
        in_specs=[rows] + _ffn_weight_specs(),
        out_specs=[rows, pl.BlockSpec((None, FFN_HALO, D_FF), lambda bi, i: (bi, 0, 0))],
        out_shape=[jax.ShapeDtypeStruct((b, t, D_MODEL), F32), jax.ShapeDtypeStruct((b, FFN_HALO, D_FF), F32)],
        scratch_shapes=[pltpu.VMEM((FFN_HALO + tm, D_FF), F32)],
        compiler_params=_params(("parallel", "arbitrary")),
        name="ffn_prompt",
    )(x1, *ffn_w)


def _ffn_sample(x1_tm, hist_tm, ffn_w):
    n = x1_tm.shape[0]
    return pl.pallas_call(
        _ffn_sample_kernel,
        grid=(1,),
        in_specs=[_const_spec((n, D_MODEL)), _const_spec(hist_tm.shape)] + _ffn_weight_specs(),
        out_specs=[_whole_spec((n, D_MODEL)), _whole_spec((n, D_FF))],
        out_shape=[jax.ShapeDtypeStruct((n, D_MODEL), F32), jax.ShapeDtypeStruct((n, D_FF), F32)],
        compiler_params=_params(("arbitrary",)),
        name="ffn_sample",
    )(x1_tm, hist_tm, *ffn_w)


def _token_major(a):
    b, t, w = a.shape
    return jnp.transpose(a, (1, 0, 2)).reshape(t * b, w)


def _batch_major(a, b):
    tb, w = a.shape
    return jnp.transpose(a.reshape(tb // b, b, w), (1, 0, 2))


def _stage_weights(w_in):
    wt = w_in.T
    o = 0
    w_q = wt[o:o + Q_W]; o += Q_W
    w_k = wt[o:o + KV_W]; o += KV_W
    w_v = wt[o:o + KV_W]; o += KV_W
    w_qi = wt[o:o + IQ_W]; o += IQ_W
    w_ki = wt[o:o + IDX_DIM]; o += IDX_DIM
    w_wi = wt[o:o + N_IDX_HEADS]; o += N_IDX_HEADS
    w_glu = wt[o:o + 2 * C_CONV]; o += 2 * C_CONV
    w_g = wt[o:o + 2 * D_MODEL]
    head = lambda h: w_q[h * HEAD_DIM:(h + 1) * HEAD_DIM]
    q_pairs = [w for i in range(GROUP) for w in (head(i), head(GROUP + i))]
    zero = jnp.zeros((LANES - IDX_DIM, D_MODEL), w_in.dtype)
    w_att = jnp.concatenate(q_pairs + [w_k, w_qi, w_ki, zero], axis=0).astype(BF16)
    w_t = jnp.concatenate([w_k, w_v, w_ki, w_wi], axis=0).astype(BF16)
    return w_att, w_t, w_glu.astype(BF16), w_g.astype(BF16)


def kernel(x_prompt, x_sample, cache_k, cache_v, cache_kidx, state_conv, state_ffn, page_table, w_in, w_attn_o, w_conv_dw, b_conv_dw, ln_conv_g, ln_conv_b, w_conv_o, w_out, ln1_g, ln1_b, w_ffn_up, w_ffn_gate, w_ffn_dw, b_ffn_dw, w_ffn_down, ln2_g, ln2_b):
    bp, tp, _ = x_prompt.shape
    bs, ts, _ = x_sample.shape
    n_pages = page_table.shape[1]
    past = n_pages * PAGE_SIZE
    assert N_KV_HEADS == 2 and N_IDX_HEADS == SUBLANES

    w_att, w_t, w_glu, w_g = _stage_weights(w_in)
    row2 = lambda a: a.reshape(1, -1)
    mix_w = (w_glu, w_g, w_attn_o.astype(BF16), w_conv_dw, row2(b_conv_dw), row2(ln_conv_g), row2(ln_conv_b),
             w_conv_o.astype(BF16), w_out.astype(BF16), row2(ln1_g), row2(ln1_b))
    ffn_w = (w_ffn_up.astype(BF16), w_ffn_gate.astype(BF16), w_ffn_dw, row2(b_ffn_dw), w_ffn_down.astype(BF16),
             row2(ln2_g), row2(ln2_b))

    qbd, qi, kb, kib, vt3, k_t, v_t, ki_t, wit = _proj(x_prompt, w_att, w_t, 1024)
    seq = lambda a: a.reshape(bp, tp, a.shape[-1])
    attn_p = _prompt_attention(seq(qbd), seq(qi), wit, seq(kib), seq(kb), vt3, min(TOPK_MAX, tp // 4))
    x1_p, conv_tail = _mix_prompt(x_prompt, attn_p, mix_w, 512)
    y_p, ffn_tail = _ffn_prompt(x1_p, ffn_w, 512)
    heads_last = lambda a_t: jnp.transpose(a_t.reshape(bp, N_KV_HEADS, HEAD_DIM, tp), (0, 3, 1, 2))
    k_p = heads_last(k_t)
    v_p = heads_last(v_t)
    ki_p = jnp.transpose(ki_t, (0, 2, 1))
    conv_p = conv_tail[:, CONV_HALO - (CONV_WIDTH - 1):, :]
    ffn_p = ffn_tail[:, FFN_HALO - (FFN_CONV_WIDTH - 1):, :]

    n_s = bs * ts
    qbd_s, qis, _, _, _, ks_t, vs_t, kis_t, wit_s = _proj(x_sample.reshape(1, n_s, D_MODEL), w_att, w_t, n_s)
    ks_t, vs_t, kis_t = ks_t[0], vs_t[0], kis_t[0]
    pt_flat = page_table.reshape(-1).astype(jnp.int32)
    new_rows = lambda a_t: a_t.T.reshape(bs, ts, -1)
    qi32 = qis.reshape(bs, ts * N_IDX_HEADS, IDX_DIM)
    w32 = wit_s.T.reshape(bs, ts * N_IDX_HEADS, 1)
    scores = _sample_scores(pt_flat, qi32, w32, jnp.transpose(cache_kidx, (0, 2, 1)), new_rows(kis_t), n_pages)
    bias = _sample_select(scores, past, ts, min(TOPK_MAX, (past + ts) // 4))
    q_bd = jnp.transpose(qbd_s.reshape(bs, ts, N_HEADS, KV_W), (0, 2, 1, 3)).reshape(bs, N_HEADS * ts, KV_W)
    as_pages = lambda a: jnp.transpose(a, (0, 2, 3, 1))
    o_bd = _sample_attend(pt_flat, q_bd, bias, as_pages(cache_k), as_pages(cache_v),
                          new_rows(ks_t), new_rows(vs_t), n_pages)
    o6 = o_bd.reshape(bs, N_KV_HEADS, GROUP, ts, N_KV_HEADS, HEAD_DIM)
    o_sel = jnp.stack([o6[:, n, :, :, n, :] for n in range(N_KV_HEADS)], axis=1)
    attn_s = jnp.transpose(o_sel, (0, 3, 1, 2, 4)).reshape(bs, ts, Q_W)

    x1_s, glu_s = _mix_sample(_token_major(x_sample), _token_major(attn_s).astype(BF16),
                              jnp.transpose(state_conv, (1, 0, 2)), mix_w)
    y_s, u_s = _ffn_sample(x1_s, jnp.transpose(state_ffn, (1, 0, 2)), ffn_w)
    y_s = _batch_major(y_s, bs)
    conv_s = jnp.concatenate([state_conv, _batch_major(glu_s, bs)], axis=1)[:, -(CONV_WIDTH - 1):, :]
    ffn_s = jnp.concatenate([state_ffn, _batch_major(u_s, bs)], axis=1)[:, -(FFN_CONV_WIDTH - 1):, :]
    k_s = new_rows(ks_t).reshape(bs, ts, N_KV_HEADS, HEAD_DIM)
    v_s = new_rows(vs_t).reshape(bs, ts, N_KV_HEADS, HEAD_DIM)
    ki_s = new_rows(kis_t)

    return (y_p, y_s, k_p, v_p, ki_p, conv_p, ffn_p, k_s, v_s, ki_s, conv_s, ffn_s)
```

```python
import functools

import jax
import jax.numpy as jnp
from jax import lax
from jax.experimental import pallas as pl
from jax.experimental.pallas import tpu as pltpu

D_MODEL = 1024
N_HEADS = 8
HEAD_DIM = 64
N_KV_HEADS = 2
N_IDX_HEADS = 8
IDX_DIM = 64
TOPK_MAX = 256
C_CONV = D_MODEL // 2
CONV_WIDTH = 31
D_FF = 2816
FFN_CONV_WIDTH = 3
LN_EPS = 1e-5
DEPTH = 1
ALPHA = (2.0 * DEPTH) ** 0.25
PAGE_SIZE = 128

Q_W = N_HEADS * HEAD_DIM
KV_W = N_KV_HEADS * HEAD_DIM
IQ_W = N_IDX_HEADS * IDX_DIM
GROUP = N_HEADS // N_KV_HEADS

LANES = 128
SUBLANES = 8
QBD_W = N_HEADS * KV_W
PROJ_W = Q_W + KV_W + IQ_W + LANES
PROJ_T = 2 * KV_W + IDX_DIM + N_IDX_HEADS
TQ = 256
TQ_S = 256
SCORE_GROUP = 8
ATTN_GROUP = 4
CONV_HALO = 32
FFN_HALO = 8
N_BISECT = 19
VT_ROWS = HEAD_DIM + 16
LOG2E = 1.4426950408889634
VMEM_LIMIT = 56 * 1024 * 1024

F32 = jnp.float32
BF16 = jnp.bfloat16
NEG_INF = float("-inf")
POS_INF = float("inf")


def _dot(a, b):
    return jnp.dot(a, b, preferred_element_type=F32)


def _dot_nt(a, b):
    return lax.dot_general(a, b, (((1,), (1,)), ((), ())), preferred_element_type=F32)


def _sigmoid(x):
    return 0.5 * jnp.tanh(0.5 * x) + 0.5


def _layer_norm(x, g, b):
    mu = jnp.mean(x, axis=-1, keepdims=True)
    xc = x - mu
    var = jnp.mean(xc * xc, axis=-1, keepdims=True)
    return xc * lax.rsqrt(var + LN_EPS) * g + b


def _params(sem):
    return pltpu.CompilerParams(dimension_semantics=sem, vmem_limit_bytes=VMEM_LIMIT)


def _whole_spec(shape):
    nd = len(shape)
    return pl.BlockSpec(shape, lambda *_: (0,) * nd)


def _const_spec(shape):
    nd = len(shape)
    return pl.BlockSpec(shape, lambda *_: (0,) * nd, pipeline_mode=pl.Buffered(1))


def _proj_kernel(x_ref, w_ref, wt_ref, qbd_ref, qi_ref, kb_ref, kib_ref, vt_ref, kt_ref, vtf_ref, kit_ref, wit_ref):
    xb = x_ref[...].astype(BF16)
    o = 0
    qp = _dot_nt(xb, w_ref[o:o + Q_W, :]); o += Q_W
    k = _dot_nt(xb, w_ref[o:o + KV_W, :]); o += KV_W
    qi = _dot_nt(xb, w_ref[o:o + IQ_W, :]); o += IQ_W
    ki = _dot_nt(xb, w_ref[o:o + LANES, :])[:, :IDX_DIM]
    qp = (qp * (HEAD_DIM ** -0.5 * LOG2E)).astype(BF16)
    low = lax.broadcasted_iota(jnp.int32, (qp.shape[0], KV_W), 1) < HEAD_DIM
    for h in range(N_HEADS):
        pair = qp[:, (h % GROUP) * KV_W:(h % GROUP + 1) * KV_W]
        qbd_ref[:, h * KV_W:(h + 1) * KV_W] = jnp.where(low if h < GROUP else jnp.logical_not(low), pair, 0.0)
    qi_ref[...] = qi.astype(BF16)
    kb_ref[...] = k.astype(BF16)
    kib_ref[...] = ki.astype(BF16)
    t = _dot_nt(wt_ref[...], xb)
    kt_ref[...] = t[:KV_W, :]
    vtf_ref[...] = t[KV_W:2 * KV_W, :]
    kit_ref[...] = t[2 * KV_W:2 * KV_W + IDX_DIM, :]
    wit_ref[...] = t[2 * KV_W + IDX_DIM:, :] * ((IDX_DIM ** -0.5) * (N_IDX_HEADS ** -0.5))
    vt = t[KV_W:2 * KV_W, :].astype(BF16)
    ones = jnp.ones((VT_ROWS - HEAD_DIM, TQ), BF16)
    for j in range(vt_ref.shape[0]):
        for n in range(N_KV_HEADS):
            vt_ref[j, n * VT_ROWS:n * VT_ROWS + HEAD_DIM, :] = vt[n * HEAD_DIM:(n + 1) * HEAD_DIM, j * TQ:(j + 1) * TQ]
            vt_ref[j, n * VT_ROWS + HEAD_DIM:(n + 1) * VT_ROWS, :] = ones


def _proj(x, w_att, w_t, tm):
    b, t, _ = x.shape
    n = b * t
    tiles = t // tm
    row = lambda w: pl.BlockSpec((tm, w), lambda i: (i, 0))
    col = lambda r: pl.BlockSpec((None, r, tm), lambda i: (i // tiles, 0, i % tiles))
    outs = [(QBD_W, BF16), (IQ_W, BF16), (KV_W, BF16), (IDX_DIM, BF16)]
    return pl.pallas_call(
        _proj_kernel,
        grid=(n // tm,),
        in_specs=[row(D_MODEL), _const_spec((PROJ_W, D_MODEL)), _const_spec((PROJ_T, D_MODEL))],
        out_specs=[row(w) for w, _ in outs]
        + [pl.BlockSpec((tm // TQ, N_KV_HEADS * VT_ROWS, TQ), lambda i: (i, 0, 0)), col(KV_W), col(KV_W), col(IDX_DIM),
           pl.BlockSpec((SUBLANES, tm), lambda i: (0, i))],
        out_shape=[jax.ShapeDtypeStruct((n, w), dt) for w, dt in outs]
        + [jax.ShapeDtypeStruct((n // TQ, N_KV_HEADS * VT_ROWS, TQ), BF16), jax.ShapeDtypeStruct((b, KV_W, t), F32),
           jax.ShapeDtypeStruct((b, KV_W, t), F32), jax.ShapeDtypeStruct((b, IDX_DIM, t), F32),
           jax.ShapeDtypeStruct((SUBLANES, n), F32)],
        compiler_params=_params(("parallel",)),
        name="proj",
    )(x.reshape(n, D_MODEL), w_att, w_t)


def _chunk(ref, c, ch):
    return ref[pl.ds(pl.multiple_of(c * ch, ch), ch), :]


def _fold_rows(x, op):
    parts = [x[r:r + SUBLANES, :] for r in range(0, x.shape[0], SUBLANES)]
    accs = parts[:4]
    for i, part in enumerate(parts[4:]):
        accs[i % 4] = op(accs[i % 4], part)
    while len(accs) > 1:
        accs = [op(accs[a], accs[a + 1]) for a in range(0, len(accs) - 1, 2)] + ([accs[-1]] if len(accs) % 2 else [])
    return accs[0]


def _reduce_keys(s_ref, nk, ch, init, f, op, red):
    w = s_ref.shape[1]

    def body(c, acc):
        return op(acc, _fold_rows(f(_chunk(s_ref, c, ch), c), op))

    acc = lax.fori_loop(0, nk, body, jnp.full((SUBLANES, w), init, F32))
    return red(acc, axis=0, keepdims=True)


def _score_stats_init(w):
    return (jnp.full((SUBLANES, w), NEG_INF, F32), jnp.full((SUBLANES, w), POS_INF, F32),
            jnp.zeros((SUBLANES, w), F32))


def _score_stats_update(stats, x):
    mx, mn, cnt = stats
    live = x > NEG_INF
    return (jnp.maximum(mx, _fold_rows(x, jnp.maximum)),
            jnp.minimum(mn, _fold_rows(jnp.where(live, x, POS_INF), jnp.minimum)),
            cnt + _fold_rows(jnp.where(live, 1.0, 0.0), jnp.add))


def _select_bias(s_ref, bias_ref, nk, ch, topk, stats):
    w = s_ref.shape[1]
    kf = float(topk)
    def count(pred):
        def body(c, accs):
            hit = pred(_chunk(s_ref, c, ch), c)
            accs = list(accs)
            for i, r in enumerate(range(0, ch, SUBLANES)):
                k = i % len(accs)
                accs[k] = jnp.where(hit[r:r + SUBLANES, :], accs[k] + 1.0, accs[k])
            return tuple(accs)

        accs = lax.fori_loop(0, nk, body, tuple(jnp.zeros((SUBLANES, w), F32) for _ in range(4)))
        return jnp.sum((accs[0] + accs[1]) + (accs[2] + accs[3]), axis=0, keepdims=True)
    min_above = lambda t: _reduce_keys(s_ref, nk, ch, POS_INF, lambda x, c: jnp.where(x > t, x, POS_INF),
                                       jnp.minimum, jnp.min)

    neg = jnp.full((1, w), NEG_INF, F32)

    hi = jnp.max(stats[0], axis=0, keepdims=True)
    lo_fin = jnp.min(stats[1], axis=0, keepdims=True)
    n_adm = jnp.sum(stats[2], axis=0, keepdims=True)

    def bisect(_, st):
        lo, lo_fin, hi, n_lo = st
        mid = 0.5 * lo_fin + 0.5 * hi
        c = count(lambda x, _: x > mid)
        ok = c >= kf
        return jnp.where(ok, mid, lo), jnp.where(ok, mid, lo_fin), jnp.where(ok, hi, mid), jnp.where(ok, c, n_lo)

    lo, _, _, n_lo = lax.fori_loop(0, N_BISECT, bisect, (neg, lo_fin, hi, n_adm))

    def peel(st):
        lo, thr, n_gt, done = st
        v = min_above(lo)
        c = count(lambda x, _: x > v)
        found = c < kf
        newly = jnp.logical_and(done < 0.5, found)
        return (jnp.where(jnp.logical_or(found, done > 0.5), lo, v), jnp.where(newly, v, thr),
                jnp.where(newly, c, n_gt), jnp.where(found, 1.0, done))

    _, thr, n_gt, _ = lax.while_loop(lambda st: jnp.min(st[3]) < 0.5, peel,
                                     (lo, lo, n_lo, jnp.where(n_lo <= kf, 1.0, 0.0)))

    need = jnp.where(thr == neg, 0.0, kf - n_gt)
    any_copies = jnp.max(need) > 0.0

    @pl.when(any_copies)
    def _():
        tri = (lax.broadcasted_iota(jnp.int32, (ch, ch), 1)
               <= lax.broadcasted_iota(jnp.int32, (ch, ch), 0)).astype(BF16)

        def write(c, seen):
            x = _chunk(s_ref, c, ch)
            eq = x == thr
            rank = seen + _dot(tri, jnp.where(eq, 1.0, 0.0).astype(BF16))
            sel = jnp.logical_or(x > thr, jnp.logical_and(eq, rank <= need))
            bias_ref[pl.ds(pl.multiple_of(c * ch, ch), ch), :] = jnp.where(sel, 0.0, NEG_INF)
            return rank[ch - 1:ch, :]

        lax.fori_loop(0, nk, write, jnp.zeros((1, w), F32))

    @pl.when(jnp.logical_not(any_copies))
    def _():
        def write(c, carry):
            sel = _chunk(s_ref, c, ch) > thr
            bias_ref[pl.ds(pl.multiple_of(c * ch, ch), ch), :] = jnp.where(sel, 0.0, NEG_INF)
            return carry

        lax.fori_loop(0, nk, write, 0)


def _prompt_attn_kernel(qbd_ref, qi_ref, wit_ref, kib_ref, kb_ref, vt_ref, o_ref, s_ref, bias_ref, acc_ref, lg_ref,
                        lg2_ref, *, topk):
    i = pl.program_id(1)
    nk = i + 1
    qi = qi_ref[...]
    wit = wit_ref[...]
    q_pos = lax.broadcasted_iota(jnp.int32, (TQ, TQ), 1) + i * TQ
    k_off = lax.broadcasted_iota(jnp.int32, (TQ, TQ), 0)

    bufs = (lg_ref, lg2_ref)
    n_pairs = (nk - 1) // 2
    odd_tail = (nk - 1) % 2 == 1

    def products(kc, h, buf):
        buf[h] = _dot_nt(kc, qi[:, h * IDX_DIM:(h + 1) * IDX_DIM])

    def accumulate(h, buf, s):
        return s + jnp.maximum(buf[h], 0.0) * wit[h:h + 1, :]

    def emit(c, s, stats):
        s = jnp.where(k_off + c * TQ <= q_pos, s, NEG_INF)
        s_ref[pl.ds(pl.multiple_of(c * TQ, TQ), TQ), :] = s
        return _score_stats_update(stats, s)

    def score_step(c, parity, stats):
        kc = _chunk(kib_ref, c, TQ)
        s = jnp.zeros((TQ, TQ), F32)
        for h in range(N_IDX_HEADS):
            products(kc, h, bufs[parity])
            s = accumulate(h, bufs[1 - parity], s)
        return emit(c - 1, s, stats)

    def score_last(c, parity, stats):
        s = jnp.zeros((TQ, TQ), F32)
        for h in range(N_IDX_HEADS):
            s = accumulate(h, bufs[parity], s)
        return emit(c, s, stats)

    kc0 = _chunk(kib_ref, 0, TQ)
    for h in range(N_IDX_HEADS):
        products(kc0, h, bufs[0])
    stats = lax.fori_loop(0, n_pairs, lambda p, st: score_step(2 * p + 2, 0, score_step(2 * p + 1, 1, st)),
                          _score_stats_init(TQ))
    stats = lax.cond(odd_tail, lambda: score_last(nk - 1, 1, score_step(nk - 1, 1, stats)),
                     lambda: score_last(nk - 1, 0, stats))
    _select_bias(s_ref, bias_ref, nk, TQ, topk, stats)

    acc_ref[...] = jnp.zeros(acc_ref.shape, F32)

    def logits(c, h, buf):
        rows = pl.ds(pl.multiple_of(c * TQ, TQ), TQ)
        lg = _dot_nt(kb_ref[rows, :], qbd_ref[:, h * KV_W:(h + 1) * KV_W]) + bias_ref[rows, :]
        buf[h] = lg
        return jnp.max(_fold_rows(lg, jnp.maximum), axis=0, keepdims=True)

    def weigh(c, h, buf, m_use, alpha):
        out = slice(h * VT_ROWS, (h + 1) * VT_ROWS)
        p = jnp.exp2(buf[h] - m_use[h:h + 1, :]).astype(BF16)
        vt = vt_ref[c, pl.ds((h // GROUP) * VT_ROWS, VT_ROWS), :]
        acc_ref[out, :] = alpha[h:h + 1, :] * acc_ref[out, :] + _dot(vt, p)

    def advance(m_old, chunk_max):
        m_new = jnp.maximum(m_old, jnp.concatenate(chunk_max, axis=0))
        m_use = jnp.where(m_new == NEG_INF, 0.0, m_new)
        return m_new, m_use, jnp.exp2(m_old - m_use)

    def step(c, parity, state):
        m, m_use, alpha = state
        chunk_max = []
        for h in range(N_HEADS):
            chunk_max.append(logits(c, h, bufs[parity]))
            weigh(c - 1, h, bufs[1 - parity], m_use, alpha)
        return advance(m, chunk_max)

    def finish(c, parity, state):
        for h in range(N_HEADS):
            weigh(c, h, bufs[parity], state[1], state[2])

    state = advance(jnp.full((N_HEADS, TQ), NEG_INF, F32), [logits(0, h, bufs[0]) for h in range(N_HEADS)])
    state = lax.fori_loop(0, n_pairs, lambda p, st: step(2 * p + 2, 0, step(2 * p + 1, 1, st)), state)

    @pl.when(odd_tail)
    def _():
        finish(nk - 1, 1, step(nk - 1, 1, state))

    @pl.when(jnp.logical_not(odd_tail))
    def _():
        finish(nk - 1, 0, state)
    outs = [acc_ref[h * VT_ROWS:h * VT_ROWS + HEAD_DIM, :] / acc_ref[h * VT_ROWS + HEAD_DIM:h * VT_ROWS + HEAD_DIM + 1, :]
            for h in range(N_HEADS)]
    o_ref[...] = jnp.concatenate(outs, axis=0).T.astype(o_ref.dtype)


def _prompt_attention(qbd, qi, wit, kib, kb, vt3, topk):
    b, t, _ = qbd.shape
    nblk = t // TQ
    qblk = lambda w: pl.BlockSpec((None, TQ, w), lambda bi, i: (bi, i, 0))
    full = lambda w: pl.BlockSpec((None, t, w), lambda bi, i: (bi, 0, 0))
    return pl.pallas_call(
        functools.partial(_prompt_attn_kernel, topk=topk),
        grid=(b, nblk),
        in_specs=[qblk(QBD_W), qblk(IQ_W), pl.BlockSpec((SUBLANES, TQ), lambda bi, i: (0, bi * nblk + i)),
                  full(IDX_DIM), full(KV_W), pl.BlockSpec((nblk, N_KV_HEADS * VT_ROWS, TQ), lambda bi, i: (bi, 0, 0))],
        out_specs=qblk(Q_W),
        out_shape=jax.ShapeDtypeStruct((b, t, Q_W), BF16),
        scratch_shapes=[pltpu.VMEM((t, TQ), F32), pltpu.VMEM((t, TQ), F32), pltpu.VMEM((N_HEADS * VT_ROWS, TQ), F32),
                        pltpu.VMEM((N_HEADS, TQ, TQ), F32), pltpu.VMEM((N_HEADS, TQ, TQ), F32)],
        compiler_params=_params(("parallel", "arbitrary")),
        name="prompt_attn",
    )(qbd, qi, wit, kib, kb, vt3)


def _group_spec(group, *block):
    nd = len(block)
    return pl.BlockSpec((group,) + block, lambda bi, pt: (bi,) + (0,) * nd)


def _new_key_block(q, k_new):
    qf = q.astype(F32)
    kf = k_new.astype(BF16).astype(F32)
    lane = lax.broadcasted_iota(jnp.int32, (q.shape[0], PAGE_SIZE), 1)
    blk = jnp.zeros((q.shape[0], PAGE_SIZE), F32)
    for j in range(k_new.shape[0]):
        blk = jnp.where(lane == j, jnp.sum(qf * kf[j:j + 1, :], axis=-1, keepdims=True), blk)
    return blk


def _sample_score_kernel(pt_ref, qi_ref, w_ref, new_ref, kidx_hbm, o_ref, buf, sem, *, n_pages):
    group = qi_ref.shape[0]
    slot = _prefetch_pages(*_page_fetcher(pt_ref, (kidx_hbm,), (buf,), sem, group * n_pages))
    t = qi_ref.shape[1] // N_IDX_HEADS
    for g in range(group):
        qi = qi_ref[g]
        w = w_ref[g]
        blocks = [_dot(qi, buf[slot, g * n_pages + j].astype(BF16)) for j in range(n_pages)]
        blocks.append(_new_key_block(qi, new_ref[g]))
        for j, s in enumerate(blocks):
            s = jnp.maximum(s, 0.0) * w
            o_ref[g * t:(g + 1) * t, j * PAGE_SIZE:(j + 1) * PAGE_SIZE] = jnp.sum(
                s.reshape(t, N_IDX_HEADS, PAGE_SIZE), axis=1)


def _sample_scores(page_table_flat, qi32, w32, kidx_t, ki_new_t, n_pages):
    nb, rows, _ = qi32.shape
    t = rows // N_IDX_HEADS
    lk = (n_pages + 1) * PAGE_SIZE
    g = SCORE_GROUP
    return pl.pallas_call(
        functools.partial(_sample_score_kernel, n_pages=n_pages),
        grid_spec=pltpu.PrefetchScalarGridSpec(
            num_scalar_prefetch=1,
            grid=(nb // g,),
            in_specs=[_group_spec(g, rows, IDX_DIM), _group_spec(g, rows, 1), _group_spec(g, t, IDX_DIM),
                      pl.BlockSpec(memory_space=pl.ANY)],
            out_specs=pl.BlockSpec((g * t, lk), lambda bi, pt: (bi, 0)),
            scratch_shapes=[pltpu.VMEM((2, g * n_pages, IDX_DIM, PAGE_SIZE), F32), pltpu.SemaphoreType.DMA((1, 2))],
        ),
        out_shape=jax.ShapeDtypeStruct((nb * t, lk), F32),
        compiler_params=_params(("arbitrary",)),
        name="sample_scores",
    )(page_table_flat, qi32, w32, ki_new_t, kidx_t)


def _sample_select_kernel(s_ref, bias_ref, sm_ref, bt_ref, *, past, t, topk):
    lk = s_ref.shape[1]
    nk = lk // PAGE_SIZE
    qcol = lax.broadcasted_iota(jnp.int32, (PAGE_SIZE, TQ_S), 1)
    qpos = past + (qcol & (t - 1))
    krow = lax.broadcasted_iota(jnp.int32, (PAGE_SIZE, TQ_S), 0)
    stats = _score_stats_init(TQ_S)
    for c in range(nk):
        cols = slice(c * PAGE_SIZE, (c + 1) * PAGE_SIZE)
        s = jnp.where(krow + c * PAGE_SIZE <= qpos, s_ref[:, cols].T, NEG_INF)
        sm_ref[cols, :] = s
        stats = _score_stats_update(stats, s)
    _select_bias(sm_ref, bt_ref, nk, PAGE_SIZE, topk, stats)
    for c in range(nk):
        cols = slice(c * PAGE_SIZE, (c + 1) * PAGE_SIZE)
        bias_ref[:, cols] = bt_ref[cols, :].T


def _sample_select(scores, past, t, topk):
    n, lk = scores.shape
    assert t & (t - 1) == 0 and TQ_S % t == 0, "token index is taken from the low bits of the query index"
    blk = pl.BlockSpec((TQ_S, lk), lambda i: (i, 0))
    return pl.pallas_call(
        functools.partial(_sample_select_kernel, past=past, t=t, topk=topk),
        grid=(n // TQ_S,),
        in_specs=[blk],
        out_specs=blk,
        out_shape=jax.ShapeDtypeStruct((n, lk), F32),
        scratch_shapes=[pltpu.VMEM((lk, TQ_S), F32), pltpu.VMEM((lk, TQ_S), F32)],
        compiler_params=_params(("parallel",)),
        name="sample_select",
    )(scores)


def _page_fetcher(pt_ref, srcs, bufs, sem, n_copy):
    def copy(a, slot, i, page):
        return pltpu.make_async_copy(srcs[a].at[page], bufs[a].at[slot, i], sem.at[a, slot])

    def start(step, slot):
        def body(i, carry):
            page = pt_ref[step * n_copy + i]
            for a in range(len(srcs)):
                copy(a, slot, i, page).start()
            return carry
        lax.fori_loop(0, n_copy, body, 0)

    def wait(slot):
        def body(i, carry):
            for a in range(len(srcs)):
                copy(a, slot, i, 0).wait()
            return carry
        lax.fori_loop(0, n_copy, body, 0)

    return start, wait


def _prefetch_pages(start, wait):
    s = pl.program_id(0)
    slot = s % 2

    @pl.when(s == 0)
    def _():
        start(0, 0)

    @pl.when(s + 1 < pl.num_programs(0))
    def _():
        start(s + 1, 1 - slot)

    wait(slot)
    return slot


def _sample_attn_kernel(pt_ref, q_ref, bias_ref, k_new, v_new, k_hbm, v_hbm, o_ref, kbuf, vbuf, sem, *, n_pages):
    group = q_ref.shape[0]
    slot = _prefetch_pages(*_page_fetcher(pt_ref, (k_hbm, v_hbm), (kbuf, vbuf), sem, group * n_pages))
    kt = lambda page: page.reshape(KV_W, PAGE_SIZE).astype(BF16)
    t = k_new.shape[1]
    for g in range(group):
        qb = q_ref[g]
        bias = jnp.concatenate([bias_ref[g * t:(g + 1) * t, :]] * (qb.shape[0] // t), axis=0)
        blocks = [_dot(qb, kt(kbuf[slot, g * n_pages + j])) for j in range(n_pages)]
        blocks.append(_new_key_block(qb, k_new[g]))
        lg = jnp.concatenate(blocks, axis=-1) + bias
        m = jnp.max(lg, axis=-1, keepdims=True)
        p = jnp.exp2(lg - m)
        l = jnp.sum(p, axis=-1, keepdims=True)
        pb = p.astype(BF16)
        acc = jnp.zeros(o_ref.shape[1:], F32)
        for j in range(n_pages):
            acc = acc + _dot_nt(pb[:, j * PAGE_SIZE:(j + 1) * PAGE_SIZE], kt(vbuf[slot, g * n_pages + j]))
        p_new = pb[:, n_pages * PAGE_SIZE:].astype(F32)
        v_rows = v_new[g].astype(BF16).astype(F32)
        for j in range(v_rows.shape[0]):
            acc = acc + p_new[:, j:j + 1] * v_rows[j:j + 1, :]
        o_ref[g] = acc / l


def _sample_attend(page_table_flat, q_bd, bias, k_t, v_t, k_new_t, v_new_t, n_pages):
    nb, rows, _ = q_bd.shape
    lk = bias.shape[1]
    t = bias.shape[0] // nb
    g = ATTN_GROUP
    page = (N_KV_HEADS, HEAD_DIM, PAGE_SIZE)
    hbm = pl.BlockSpec(memory_space=pl.ANY)
    page_buf = pltpu.VMEM((2, g * n_pages) + page, F32)
    return pl.pallas_call(
        functools.partial(_sample_attn_kernel, n_pages=n_pages),
        grid_spec=pltpu.PrefetchScalarGridSpec(
            num_scalar_prefetch=1,
            grid=(nb // g,),
            in_specs=[_group_spec(g, rows, KV_W), pl.BlockSpec((g * t, lk), lambda bi, pt: (bi, 0)),
                      _group_spec(g, t, KV_W), _group_spec(g, t, KV_W), hbm, hbm],
            out_specs=_group_spec(g, rows, KV_W),
            scratch_shapes=[page_buf, page_buf, pltpu.SemaphoreType.DMA((2, 2))],
        ),
        out_shape=jax.ShapeDtypeStruct((nb, rows, KV_W), F32),
        compiler_params=_params(("arbitrary",)),
        name="sample_attn",
    )(page_table_flat, q_bd, bias, k_new_t, v_new_t, k_t, v_t)


def _glu(xb, w_glu_ref):
    gi = _dot_nt(xb, w_glu_ref[...])
    return gi[:, :C_CONV] * _sigmoid(gi[:, C_CONV:])


def _gates(xb, attn_b, w_g_ref, w_ao_ref, cols=slice(0, D_MODEL)):
    gate_cols = slice(D_MODEL + cols.start, D_MODEL + cols.stop)
    a_term = _sigmoid(_dot_nt(xb, w_g_ref[cols, :])) * _dot(attn_b, w_ao_ref[:, cols])
    return a_term, _sigmoid(_dot_nt(xb, w_g_ref[gate_cols, :]))


def _mix_tail(x, c, a_term, c_gate, lncg_ref, lncb_ref, w_co_ref, w_out_ref, ln1g_ref, ln1b_ref):
    def swish_ln(rows):
        cn = _layer_norm(rows, lncg_ref[...], lncb_ref[...])
        return (cn * _sigmoid(cn)).astype(BF16)

    half = c.shape[0] // 2
    c_branch = _dot(jnp.concatenate([swish_ln(c[:half]), swish_ln(c[half:])], axis=0), w_co_ref[...])
    merged = a_term + c_gate * c_branch
    h = ALPHA * x + _dot(merged.astype(BF16), w_out_ref[...])
    return _layer_norm(h, ln1g_ref[...], ln1b_ref[...])


def _mix_prompt_kernel(x_ref, attn_ref, w_glu_ref, w_g_ref, w_ao_ref, w_dw_ref, b_dw_ref, lncg_ref, lncb_ref,
                       w_co_ref, w_out_ref, ln1g_ref, ln1b_ref, x1_ref, tail_ref, xp_ref):
    tm = x_ref.shape[0]

    @pl.when(pl.program_id(1) == 0)
    def _():
        xp_ref[0:CONV_HALO, :] = jnp.zeros((CONV_HALO, C_CONV), F32)

    x = x_ref[...]
    xb = x.astype(BF16)
    glu = _glu(xb, w_glu_ref)
    xp_ref[CONV_HALO:CONV_HALO + tm, :] = glu
    tail_ref[...] = glu[tm - CONV_HALO:, :]
    attn_b = attn_ref[...]

    first = CONV_HALO - (CONV_WIDTH - 1)
    rb = 128
    n_blk = C_CONV // LANES
    cols, gate_parts = [], []
    for c0 in range(0, C_CONV, LANES):
        lanes = slice(c0, c0 + LANES)
        q = c0 // LANES
        gate_parts.append(_gates(xb, attn_b, w_g_ref, w_ao_ref, slice(q * D_MODEL // n_blk, (q + 1) * D_MODEL // n_blk)))
        blocks = []
        for r0 in range(0, tm, rb):
            y = jnp.broadcast_to(b_dw_ref[:, lanes], (rb, LANES))
            for b in range(SUBLANES):
                rows = rb + (SUBLANES if b else 0)
                part = None
                for j in range(CONV_WIDTH):
                    if (first + j) % SUBLANES == b:
                        a0 = first + j - b + r0
                        term = w_dw_ref[j:j + 1, lanes] * xp_ref[a0:a0 + rows, lanes]
                        part = term if part is None else part + term
                y = y + part[b:b + rb, :]
            blocks.append(y)
        cols.append(jnp.concatenate(blocks, axis=0))
    c = jnp.concatenate(cols, axis=-1)
    a_term = jnp.concatenate([p[0] for p in gate_parts], axis=-1)
    c_gate = jnp.concatenate([p[1] for p in gate_parts], axis=-1)
    xp_ref[0:CONV_HALO, :] = xp_ref[tm:tm + CONV_HALO, :]

    x1_ref[...] = _mix_tail(x, c, a_term, c_gate, lncg_ref, lncb_ref, w_co_ref, w_out_ref, ln1g_ref, ln1b_ref)


def _mix_sample_kernel(x_ref, attn_ref, hist_ref, w_glu_ref, w_g_ref, w_ao_ref, w_dw_ref, b_dw_ref, lncg_ref,
                       lncb_ref, w_co_ref, w_out_ref, ln1g_ref, ln1b_ref, x1_ref, glu_ref):
    nb = hist_ref.shape[1]
    t = x_ref.shape[0] // nb
    n_hist = CONV_WIDTH - 1
    x = x_ref[...]
    xb = x.astype(BF16)
    glu = _glu(xb, w_glu_ref)
    glu_ref[...] = glu

    def slab(m):
        return hist_ref[m] if m < n_hist else glu[(m - n_hist) * nb:(m - n_hist + 1) * nb, :]

    outs = []
    for ti in range(t):
        acc = jnp.broadcast_to(b_dw_ref[...], (nb, C_CONV))
        for j in range(CONV_WIDTH):
            acc = acc + w_dw_ref[j:j + 1, :] * slab(ti + j)
        outs.append(acc)
    c = jnp.concatenate(outs, axis=0)
    a_term, c_gate = _gates(xb, attn_ref[...], w_g_ref, w_ao_ref)
    x1_ref[...] = _mix_tail(x, c, a_term, c_gate, lncg_ref, lncb_ref, w_co_ref, w_out_ref, ln1g_ref, ln1b_ref)


def _mix_weight_specs():
    return [
        _const_spec((2 * C_CONV, D_MODEL)), _const_spec((2 * D_MODEL, D_MODEL)), _const_spec((Q_W, D_MODEL)),
        _const_spec((CONV_WIDTH, C_CONV)), _const_spec((1, C_CONV)), _const_spec((1, C_CONV)),
        _const_spec((1, C_CONV)), _const_spec((C_CONV, D_MODEL)), _const_spec((D_MODEL, D_MODEL)),
        _const_spec((1, D_MODEL)), _const_spec((1, D_MODEL)),
    ]


def _mix_prompt(x, attn, mix_w, tm):
    b, t, _ = x.shape
    rows = lambda w: pl.BlockSpec((None, tm, w), lambda bi, i: (bi, i, 0))
    return pl.pallas_call(
        _mix_prompt_kernel,
        grid=(b, t // tm),
        in_specs=[rows(D_MODEL), rows(Q_W)] + _mix_weight_specs(),
        out_specs=[rows(D_MODEL), pl.BlockSpec((None, CONV_HALO, C_CONV), lambda bi, i: (bi, 0, 0))],
        out_shape=[jax.ShapeDtypeStruct((b, t, D_MODEL), F32), jax.ShapeDtypeStruct((b, CONV_HALO, C_CONV), F32)],
        scratch_shapes=[pltpu.VMEM((CONV_HALO + tm, C_CONV), F32)],
        compiler_params=_params(("parallel", "arbitrary")),
        name="mix_prompt",
    )(x, attn, *mix_w)


def _mix_sample(x_tm, attn_tm, hist_tm, mix_w):
    n = x_tm.shape[0]
    return pl.pallas_call(
        _mix_sample_kernel,
        grid=(1,),
        in_specs=[_const_spec((n, D_MODEL)), _const_spec((n, Q_W)), _const_spec(hist_tm.shape)] + _mix_weight_specs(),
        out_specs=[_whole_spec((n, D_MODEL)), _whole_spec((n, C_CONV))],
        out_shape=[jax.ShapeDtypeStruct((n, D_MODEL), F32), jax.ShapeDtypeStruct((n, C_CONV), F32)],
        compiler_params=_params(("arbitrary",)),
        name="mix_sample",
    )(x_tm, attn_tm, hist_tm, *mix_w)


def _ffn_tail(x1, uc, gate, w_down_ref, ln2g_ref, ln2b_ref):
    f = _dot((jax.nn.gelu(uc) * gate).astype(BF16), w_down_ref[...])
    return _layer_norm(ALPHA * x1 + f, ln2g_ref[...], ln2b_ref[...])


def _ffn_prompt_kernel(x1_ref, w_up_ref, w_gate_ref, w_dw_ref, b_dw_ref, w_down_ref, ln2g_ref, ln2b_ref,
                       y_ref, tail_ref, up_ref):
    tm = x1_ref.shape[0]

    @pl.when(pl.program_id(1) == 0)
    def _():
        up_ref[0:FFN_HALO, :] = jnp.zeros((FFN_HALO, D_FF), F32)

    x1 = x1_ref[...]
    x1b = x1.astype(BF16)
    u = _dot(x1b, w_up_ref[...])
    gate = _dot(x1b, w_gate_ref[...])
    up_ref[FFN_HALO:FFN_HALO + tm, :] = u
    tail_ref[...] = u[tm - FFN_HALO:, :]
    first = FFN_HALO - (FFN_CONV_WIDTH - 1)
    uc = b_dw_ref[...]
    for j in range(FFN_CONV_WIDTH):
        uc = uc + w_dw_ref[j:j + 1, :] * up_ref[first + j:first + j + tm, :]
    up_ref[0:FFN_HALO, :] = up_ref[tm:tm + FFN_HALO, :]
    y_ref[...] = _ffn_tail(x1, uc, gate, w_down_ref, ln2g_ref, ln2b_ref)


def _ffn_sample_kernel(x1_ref, hist_ref, w_up_ref, w_gate_ref, w_dw_ref, b_dw_ref, w_down_ref, ln2g_ref, ln2b_ref,
                       y_ref, u_ref):
    nb = hist_ref.shape[1]
    t = x1_ref.shape[0] // nb
    n_hist = FFN_CONV_WIDTH - 1
    x1 = x1_ref[...]
    x1b = x1.astype(BF16)
    u = _dot(x1b, w_up_ref[...])
    gate = _dot(x1b, w_gate_ref[...])
    u_ref[...] = u

    def slab(m):
        return hist_ref[m] if m < n_hist else u[(m - n_hist) * nb:(m - n_hist + 1) * nb, :]

    outs = []
    for ti in range(t):
        acc = jnp.broadcast_to(b_dw_ref[...], (nb, D_FF))
        for j in range(FFN_CONV_WIDTH):
            acc = acc + w_dw_ref[j:j + 1, :] * slab(ti + j)
        outs.append(acc)
    uc = jnp.concatenate(outs, axis=0)
    y_ref[...] = _ffn_tail(x1, uc, gate, w_down_ref, ln2g_ref, ln2b_ref)


def _ffn_weight_specs():
    return [
        _const_spec((D_MODEL, D_FF)), _const_spec((D_MODEL, D_FF)), _const_spec((FFN_CONV_WIDTH, D_FF)),
        _const_spec((1, D_FF)), _const_spec((D_FF, D_MODEL)), _const_spec((1, D_MODEL)), _const_spec((1, D_MODEL)),
    ]


def _ffn_prompt(x1, ffn_w, tm):
    b, t, _ = x1.shape
    rows = pl.BlockSpec((None, tm, D_MODEL), lambda bi, i: (bi, i, 0))
    return pl.pallas_call(
        _ffn_prompt_kernel,
        grid=(b, t // tm),
        in_specs=[rows] + _ffn_weight_specs(),
        out_specs=[rows, pl.BlockSpec((None, FFN_HALO, D_FF), lambda bi, i: (bi, 0, 0))],
        out_shape=[jax.ShapeDtypeStruct((b, t, D_MODEL), F32), jax.ShapeDtypeStruct((b, FFN_HALO, D_FF), F32)],
        scratch_shapes=[pltpu.VMEM((FFN_HALO + tm, D_FF), F32)],
        compiler_params=_params(("parallel", "arbitrary")),
        name="ffn_prompt",
    )(x1, *ffn_w)


def _ffn_sample(x1_tm, hist_tm, ffn_w):
    n = x1_tm.shape[0]
    return pl.pallas_call(
        _ffn_sample_kernel,
        grid=(1,),
        in_specs=[_const_spec((n, D_MODEL)), _const_spec(hist_tm.shape)] + _ffn_weight_specs(),
        out_specs=[_whole_spec((n, D_MODEL)), _whole_spec((n, D_FF))],
        out_shape=[jax.ShapeDtypeStruct((n, D_MODEL), F32), jax.ShapeDtypeStruct((n, D_FF), F32)],
        compiler_params=_params(("arbitrary",)),
        name="ffn_sample",
    )(x1_tm, hist_tm, *ffn_w)


def _token_major(a):
    b, t, w = a.shape
    return jnp.transpose(a, (1, 0, 2)).reshape(t * b, w)


def _batch_major(a, b):
    tb, w = a.shape
    return jnp.transpose(a.reshape(tb // b, b, w), (1, 0, 2))


def _stage_weights(w_in):
    wt = w_in.T
    o = 0
    w_q = wt[o:o + Q_W]; o += Q_W
    w_k = wt[o:o + KV_W]; o += KV_W
    w_v = wt[o:o + KV_W]; o += KV_W
    w_qi = wt[o:o + IQ_W]; o += IQ_W
    w_ki = wt[o:o + IDX_DIM]; o += IDX_DIM
    w_wi = wt[o:o + N_IDX_HEADS]; o += N_IDX_HEADS
    w_glu = wt[o:o + 2 * C_CONV]; o += 2 * C_CONV
    w_g = wt[o:o + 2 * D_MODEL]
    head = lambda h: w_q[h * HEAD_DIM:(h + 1) * HEAD_DIM]
    q_pairs = [w for i in range(GROUP) for w in (head(i), head(GROUP + i))]
    zero = jnp.zeros((LANES - IDX_DIM, D_MODEL), w_in.dtype)
    w_att = jnp.concatenate(q_pairs + [w_k, w_qi, w_ki, zero], axis=0).astype(BF16)
    w_t = jnp.concatenate([w_k, w_v, w_ki, w_wi], axis=0).astype(BF16)
    return w_att, w_t, w_glu.astype(BF16), w_g.astype(BF16)


def kernel(x_prompt, x_sample, cache_k, cache_v, cache_kidx, state_conv, state_ffn, page_table, w_in, w_attn_o, w_conv_dw, b_conv_dw, ln_conv_g, ln_conv_b, w_conv_o, w_out, ln1_g, ln1_b, w_ffn_up, w_ffn_gate, w_ffn_dw, b_ffn_dw, w_ffn_down, ln2_g, ln2_b):
    bp, tp, _ = x_prompt.shape
    bs, ts, _ = x_sample.shape
    n_pages = page_table.shape[1]
    past = n_pages * PAGE_SIZE
    assert N_KV_HEADS == 2 and N_IDX_HEADS == SUBLANES

    w_att, w_t, w_glu, w_g = _stage_weights(w_in)
    row2 = lambda a: a.reshape(1, -1)
    mix_w = (w_glu, w_g, w_attn_o.astype(BF16), w_conv_dw, row2(b_conv_dw), row2(ln_conv_g), row2(ln_conv_b),
             w_conv_o.astype(BF16), w_out.astype(BF16), row2(ln1_g), row2(ln1_b))
    ffn_w = (w_ffn_up.astype(BF16), w_ffn_gate.astype(BF16), w_ffn_dw, row2(b_ffn_dw), w_ffn_down.astype(BF16),
             row2(ln2_g), row2(ln2_b))

    qbd, qi, kb, kib, vt3, k_t, v_t, ki_t, wit = _proj(x_prompt, w_att, w_t, 1024)
    seq = lambda a: a.reshape(bp, tp, a.shape[-1])
    attn_p = _prompt_attention(seq(qbd), seq(qi), wit, seq(kib), seq(kb), vt3, min(TOPK_MAX, tp // 4))
    x1_p, conv_tail = _mix_prompt(x_prompt, attn_p, mix_w, 512)
    y_p, ffn_tail = _ffn_prompt(x1_p, ffn_w, 512)
    heads_last = lambda a_t: jnp.transpose(a_t.reshape(bp, N_KV_HEADS, HEAD_DIM, tp), (0, 3, 1, 2))
    k_p = heads_last(k_t)
    v_p = heads_last(v_t)
    ki_p = jnp.transpose(ki_t, (0, 2, 1))
    conv_p = conv_tail[:, CONV_HALO - (CONV_WIDTH - 1):, :]
    ffn_p = ffn_tail[:, FFN_HALO - (FFN_CONV_WIDTH - 1):, :]

    n_s = bs * ts
    qbd_s, qis, _, _, _, ks_t, vs_t, kis_t, wit_s = _proj(x_sample.reshape(1, n_s, D_MODEL), w_att, w_t, n_s)
    ks_t, vs_t, kis_t = ks_t[0], vs_t[0], kis_t[0]
    pt_flat = page_table.reshape(-1).astype(jnp.int32)
    new_rows = lambda a_t: a_t.T.reshape(bs, ts, -1)
    qi32 = qis.reshape(bs, ts * N_IDX_HEADS, IDX_DIM)
    w32 = wit_s.T.reshape(bs, ts * N_IDX_HEADS, 1)
    scores = _sample_scores(pt_flat, qi32, w32, jnp.transpose(cache_kidx, (0, 2, 1)), new_rows(kis_t), n_pages)
    bias = _sample_select(scores, past, ts, min(TOPK_MAX, (past + ts) // 4))
    q_bd = jnp.transpose(qbd_s.reshape(bs, ts, N_HEADS, KV_W), (0, 2, 1, 3)).reshape(bs, N_HEADS * ts, KV_W)
    as_pages = lambda a: jnp.transpose(a, (0, 2, 3, 1))
    o_bd = _sample_attend(pt_flat, q_bd, bias, as_pages(cache_k), as_pages(cache_v),
                          new_rows(ks_t), new_rows(vs_t), n_pages)
    o6 = o_bd.reshape(bs, N_KV_HEADS, GROUP, ts, N_KV_HEADS, HEAD_DIM)
    o_sel = jnp.stack([o6[:, n, :, :, n, :] for n in range(N_KV_HEADS)], axis=1)
    attn_s = jnp.transpose(o_sel, (0, 3, 1, 2, 4)).reshape(bs, ts, Q_W)

    x1_s, glu_s = _mix_sample(_token_major(x_sample), _token_major(attn_s).astype(BF16),
                              jnp.transpose(state_conv, (1, 0, 2)), mix_w)
    y_s, u_s = _ffn_sample(x1_s, jnp.transpose(state_ffn, (1, 0, 2)), ffn_w)
    y_s = _batch_major(y_s, bs)
    conv_s = jnp.concatenate([state_conv, _batch_major(glu_s, bs)], axis=1)[:, -(CONV_WIDTH - 1):, :]
    ffn_s = jnp.concatenate([state_ffn, _batch_major(u_s, bs)], axis=1)[:, -(FFN_CONV_WIDTH - 1):, :]
    k_s = new_rows(ks_t).reshape(bs, ts, N_KV_HEADS, HEAD_DIM)
    v_s = new_rows(vs_t).reshape(bs, ts, N_KV_HEADS, HEAD_DIM)
    ki_s = new_rows(kis_t)

    return (y_p, y_s, k_p, v_p, ki_p, conv_p, ffn_p, k_s, v_s, ki_s, conv_s, ffn_s)
```
